```python
import math
import jax
import jax.numpy as jnp
from jax import lax
import numpy as np

D_MODEL = 1024
BATCH = 4
SEQ = 8192
DEPTH = 2
DEC_BATCH = 128
DEC_SEQ = 1
PAST_LEN = 16384
PAGE_SIZE = 128

D_A = D_MODEL // 2
HEAD_A = 64
H_A = D_A // HEAD_A
W_LORA = 32
A_LORA = 64
G_LORA = 96
RWKV_COLS = 3 * D_A + W_LORA + A_LORA + G_LORA
D_B = D_MODEL - D_A
HEAD_B = 64
H_B = D_B // HEAD_B
KV_HEADS = 2
Q_PER_KV = H_B // KV_HEADS
SWA_COLS = D_B + 2 * KV_HEADS * HEAD_B
PROJ_COLS = RWKV_COLS + SWA_COLS
WINDOW = 128
N_BUCKETS = 32
MAX_EXACT = 16
REL_MAX_DIST = 128
POOL_WINDOWS = (2, 4, 8, 16)
N_POOL_GROUPS = len(POOL_WINDOWS)
POOL_GROUP = D_MODEL // N_POOL_GROUPS
POOL_KEEP = max(POOL_WINDOWS) - 1
N_EXPERTS = 16
N_GROUPS = 4
EXPERTS_PER_GROUP = N_EXPERTS // N_GROUPS
TOP_K = 2
D_FF = 256
N_EVEN = (DEPTH + 1) // 2
N_ODD = DEPTH // 2
ALPHA = (2.0 * DEPTH) ** 0.25
BETA = (8.0 * DEPTH) ** -0.25
LN_EPS = 1e-5
LNX_EPS = 64e-5
NEG_INF = -1e30

kernel_name = 'hybrid_rwkv7_swa_pool_moe_step'


def layer_norm(x, g, b, eps=LN_EPS):
    xf = x.astype(jnp.float32)
    mu = xf.mean(-1, keepdims=True)
    var = jnp.square(xf - mu).mean(-1, keepdims=True)
    return ((xf - mu) * lax.rsqrt(var + eps) * g + b).astype(x.dtype)


def wkv7_scan(r, decay, k, v, a, b, S0):
    xs = tuple(jnp.moveaxis(t, 1, 0) for t in (r, decay, k, v, a, b))

    def step(S, inp):
        r_t, w_t, k_t, v_t, a_t, b_t = inp
        sa = jnp.einsum('bhij,bhj->bhi', S, a_t)
        S = S * w_t[:, :, None, :] + sa[..., None] * b_t[:, :, None, :] + v_t[..., None] * k_t[:, :, None, :]
        return S, jnp.einsum('bhij,bhj->bhi', S, r_t)

    S, ys = lax.scan(step, S0, xs)
    return jnp.moveaxis(ys, 0, 1), S


def rwkv7_mixer(m, S0, w0, w_w2, a0, w_a2, w_g2, k_k, k_a, r_k, lnx_g, lnx_b):
    Bn, T, _ = m.shape
    m = m.astype(jnp.float32)
    r, k, v, wd, ad, gd = jnp.split(m, [D_A, 2 * D_A, 3 * D_A, 3 * D_A + W_LORA, 3 * D_A + W_LORA + A_LORA], axis=-1)
    w_log = -jax.nn.softplus(-(w0 + jnp.tanh(wd) @ w_w2)) - 0.5
    decay = jnp.exp(-jnp.exp(w_log))
    a = jax.nn.sigmoid(a0 + ad @ w_a2)
    g = jax.nn.sigmoid(gd) @ w_g2
    heads = lambda t: t.reshape(Bn, T, H_A, HEAD_A)
    kk = heads(k * k_k)
    kk = kk / jnp.maximum(jnp.linalg.norm(kk, axis=-1, keepdims=True), 1e-12)
    k = k * (1.0 + (a - 1.0) * k_a)
    r, k, v, a, decay = heads(r), heads(k), heads(v), heads(a), heads(decay)
    y, S = wkv7_scan(r, decay, k, v, -kk, kk * a, S0.astype(jnp.float32))
    mu = y.mean(-1, keepdims=True)
    var = jnp.square(y - mu).mean(-1, keepdims=True)
    y = ((y - mu) * lax.rsqrt(var + LNX_EPS)).reshape(Bn, T, D_A) * lnx_g + lnx_b
    bonus = jnp.sum(r * k * r_k, axis=-1, keepdims=True) * v
    return (y + bonus.reshape(Bn, T, D_A)) * g, S


def t5_bucket(dist):
    n = jnp.maximum(dist, 0)
    scaled = jnp.log(jnp.maximum(n, MAX_EXACT).astype(jnp.float32) / MAX_EXACT) / math.log(REL_MAX_DIST / MAX_EXACT)
    large = jnp.minimum(MAX_EXACT + (scaled * (N_BUCKETS - MAX_EXACT)).astype(jnp.int32), N_BUCKETS - 1)
    return jnp.where(n < MAX_EXACT, n, large)


def window_attention(qb, kb, vb, q_pos, k_pos, sinks, rel_bias):
    s = jnp.einsum('bnqhgd,bnkhd->bnhgqk', qb, kb).astype(jnp.float32) * (HEAD_B ** -0.5)
    dist = q_pos[:, :, None] - k_pos[:, None, :]
    valid = (dist >= 0) & (dist <= WINDOW) & (k_pos[:, None, :] >= 0)
    nb, nq, nk = dist.shape
    bias = jnp.moveaxis(rel_bias.astype(jnp.float32)[t5_bucket(dist)], -1, 1).reshape(nb, KV_HEADS, Q_PER_KV, nq, nk)
    s = jnp.where(valid[None, :, None, None], s + bias[None], NEG_INF)
    sink = sinks.astype(jnp.float32).reshape(1, 1, KV_HEADS, Q_PER_KV, 1)
    m = jnp.maximum(s.max(-1), sink)
    p = jnp.exp(s - m[..., None])
    p = p / (p.sum(-1) + jnp.exp(sink - m))[..., None]
    return jnp.einsum('bnhgqk,bnkhd->bnqhgd', p, vb.astype(jnp.float32))


def swa_mixer(q, k, v, k_buf, v_buf, start_pos, sinks, rel_bias):
    Bn, T, _ = q.shape
    q = q.reshape(Bn, T, KV_HEADS, Q_PER_KV, HEAD_B)
    k = k.reshape(Bn, T, KV_HEADS, HEAD_B)
    v = v.reshape(Bn, T, KV_HEADS, HEAD_B)
    if k_buf is None:
        nb = T // WINDOW

        def band(t):
            cur = t.reshape(Bn, nb, WINDOW, KV_HEADS, HEAD_B)
            prev = jnp.concatenate([jnp.zeros_like(cur[:, :1]), cur[:, :-1]], axis=1)
            return jnp.concatenate([prev, cur], axis=2)

        qb = q.reshape(Bn, nb, WINDOW, KV_HEADS, Q_PER_KV, HEAD_B)
        kb, vb = band(k), band(v)
        q_pos = start_pos + jnp.arange(T).reshape(nb, WINDOW)
        k_pos = jnp.concatenate([q_pos - WINDOW, q_pos], axis=1)
        k_all, v_all = k, v
    else:
        w_buf = k_buf.shape[1]
        k_all = jnp.concatenate([k_buf.astype(k.dtype), k], axis=1)
        v_all = jnp.concatenate([v_buf.astype(v.dtype), v], axis=1)
        qb, kb, vb = q[:, None], k_all[:, None], v_all[:, None]
        q_pos = (start_pos + jnp.arange(T))[None]
        k_pos = (start_pos - w_buf + jnp.arange(w_buf + T))[None]
    o = window_attention(qb, kb, vb, q_pos, k_pos, sinks, rel_bias)
    return o.reshape(Bn, T, D_B), k_all[:, -WINDOW:], v_all[:, -WINDOW:]


def even_mixer(x, prev_row, S0, k_buf, v_buf, start_pos, prm, i):
    w_in = prm['w_in'][i]
    proj = x @ w_in
    pr = proj[..., :RWKV_COLS]
    prev = (prev_row.astype(x.dtype) @ w_in[:, :RWKV_COLS])[:, None]
    shifted = jnp.concatenate([prev, pr[:, :-1]], axis=1)
    mixed = pr + prm['tshift_mu'][i] * (shifted - pr)
    y_a, S = rwkv7_mixer(mixed, S0, prm['decay_w0'][i], prm['decay_w2'][i], prm['iclr_a0'][i], prm['iclr_a2'][i],
                         prm['gate_w2'][i], prm['k_k'][i], prm['k_a'][i], prm['r_k'][i], prm['lnx_g'][i], prm['lnx_b'][i])
    q, ks, vs = jnp.split(proj[..., RWKV_COLS:], [D_B, D_B + KV_HEADS * HEAD_B], axis=-1)
    y_b, k_new, v_new = swa_mixer(q, ks, vs, k_buf, v_buf, start_pos, prm['attn_sinks'][i], prm['rel_bias'])
    out = jnp.concatenate([y_a.astype(x.dtype), y_b.astype(x.dtype)], axis=-1) @ prm['w_out'][i]
    return out, x[:, -1], S, k_new, v_new


def pool_mixer(x, buf, start_pos, w_pool, pool_scale):
    Bn, T, D = x.shape
    xcat = x if buf is None else jnp.concatenate([buf.astype(x.dtype), x], axis=1)
    L = xcat.shape[1]
    base = start_pos - (L - T)
    cs = jnp.concatenate([jnp.zeros((Bn, 1, D), jnp.float32), jnp.cumsum(xcat.astype(jnp.float32), axis=1)], axis=1)
    t = jnp.arange(L - T, L)
    pos = base + t
    hi = cs[:, L - T + 1:]
    means = []
    for gi, w in enumerate(POOL_WINDOWS):
        cols = slice(gi * POOL_GROUP, (gi + 1) * POOL_GROUP)
        lo = jnp.take(cs[..., cols], jnp.maximum(t + 1 - w, 0), axis=1)
        cnt = jnp.minimum(w, pos + 1).astype(jnp.float32)[None, :, None]
        means.append((hi[..., cols] - lo) / cnt)
    pooled = jnp.concatenate(means, axis=-1) - x.astype(jnp.float32)
    mixed = jnp.einsum('btgc,gcd->btgd', pooled.reshape(Bn, T, N_POOL_GROUPS, POOL_GROUP),
                       w_pool.astype(jnp.float32)).reshape(Bn, T, D)
    return (mixed * pool_scale).astype(x.dtype), xcat[:, -POOL_KEEP:]


def moe_ffn(x, w_router, b_router, w_gate, w_up, w_down):
    Bn, T, D = x.shape
    xt = x.reshape(Bn * T, D)
    probs = jax.nn.softmax((xt @ w_router).astype(jnp.float32) + b_router, axis=-1)
    group_score = lax.top_k(probs.reshape(-1, N_GROUPS, EXPERTS_PER_GROUP), TOP_K)[0].sum(-1)
    best_group = jnp.argmax(group_score, axis=-1)
    in_group = (jnp.arange(N_EXPERTS) // EXPERTS_PER_GROUP)[None, :] == best_group[:, None]
    vals, idx = lax.top_k(jnp.where(in_group, probs, -1.0), TOP_K)
    gates = vals / vals.sum(-1, keepdims=True)
    dense_gate = jnp.sum(jax.nn.one_hot(idx, N_EXPERTS, dtype=jnp.float32) * gates[..., None], axis=1)
    y = jnp.zeros(xt.shape, jnp.float32)
    for e in range(N_EXPERTS):
        h = jax.nn.silu(xt @ w_gate[e]) * (xt @ w_up[e])
        y = y + dense_gate[:, e:e + 1] * (h @ w_down[e]).astype(jnp.float32)
    return y.astype(x.dtype).reshape(Bn, T, D)


def trunk(x, start_pos, st_rwkv, st_shift, c_k, c_v, st_pool, prm):
    Bn = x.shape[0]
    fresh = st_rwkv is None
    new_S, new_shift, new_k, new_v, new_pool = [], [], [], [], []
    for layer in range(DEPTH):
        i = layer // 2
        if layer % 2 == 0:
            S0 = jnp.zeros((Bn, H_A, HEAD_A, HEAD_A), jnp.float32) if fresh else st_rwkv[i]
            prev = jnp.zeros((Bn, D_MODEL), x.dtype) if fresh else st_shift[i]
            kb = None if fresh else c_k[i]
            vb = None if fresh else c_v[i]
            h, last, S, kn, vn = even_mixer(x, prev, S0, kb, vb, start_pos, prm, i)
            new_S.append(S)
            new_shift.append(last)
            new_k.append(kn)
            new_v.append(vn)
        else:
            h, pn = pool_mixer(x, None if fresh else st_pool[i], start_pos, prm['w_pool'][i], prm['pool_scale'][i])
            new_pool.append(pn)
        x = layer_norm(ALPHA * x + h, prm['ln_mix_g'][layer], prm['ln_mix_b'][layer])
        f = moe_ffn(x, prm['w_router'], prm['b_router'], prm['w_ex_gate'][layer], prm['w_ex_up'][layer], prm['w_ex_down'][layer])
        x = layer_norm(ALPHA * x + f, prm['ln_ffn_g'][layer], prm['ln_ffn_b'][layer])
    return x, jnp.stack(new_S), jnp.stack(new_shift), jnp.stack(new_k), jnp.stack(new_v), jnp.stack(new_pool)


def setup_inputs(seed: int = 0) -> dict:
    key = jax.random.key(seed)
    k = jax.random.split(key, 33)
    f32 = jnp.float32
    nrm = lambda kk, shape, s=1.0: s * jax.random.normal(kk, shape, f32)
    win_buf = min(WINDOW, PAST_LEN)
    head_profile = jnp.linspace(-6.0, 0.0, HEAD_A, dtype=f32)
    return {
        'x_prompt': nrm(k[0], (BATCH, SEQ, D_MODEL)),
        'x_sample': nrm(k[1], (DEC_BATCH, DEC_SEQ, D_MODEL)),
        'state_rwkv': nrm(k[2], (N_EVEN, DEC_BATCH, H_A, HEAD_A, HEAD_A), 0.3),
        'state_shift': nrm(k[3], (N_EVEN, DEC_BATCH, D_MODEL)),
        'cache_swa_k': nrm(k[4], (N_EVEN, DEC_BATCH, win_buf, KV_HEADS, HEAD_B)),
        'cache_swa_v': nrm(k[5], (N_EVEN, DEC_BATCH, win_buf, KV_HEADS, HEAD_B)),
        'state_pool': nrm(k[6], (N_ODD, DEC_BATCH, POOL_KEEP, D_MODEL)),
        'w_in': nrm(k[7], (N_EVEN, D_MODEL, PROJ_COLS), D_MODEL ** -0.5),
        'tshift_mu': jax.random.uniform(k[8], (N_EVEN, RWKV_COLS), f32),
        'decay_w0': jnp.tile(head_profile, (N_EVEN, H_A)) + nrm(k[9], (N_EVEN, D_A), 0.1),
        'decay_w2': nrm(k[10], (N_EVEN, W_LORA, D_A), 0.1),
        'iclr_a0': nrm(k[11], (N_EVEN, D_A), 0.2),
        'iclr_a2': nrm(k[12], (N_EVEN, A_LORA, D_A), A_LORA ** -0.5),
        'gate_w2': nrm(k[13], (N_EVEN, G_LORA, D_A), G_LORA ** -0.5),
        'k_k': 0.85 + nrm(k[14], (N_EVEN, D_A), 0.05),
        'k_a': 1.0 + nrm(k[15], (N_EVEN, D_A), 0.05),
        'r_k': nrm(k[16], (N_EVEN, H_A, HEAD_A), 0.1),
        'lnx_g': 1.0 + nrm(k[17], (N_EVEN, D_A), 0.1),
        'lnx_b': nrm(k[18], (N_EVEN, D_A), 0.1),
        'attn_sinks': nrm(k[19], (N_EVEN, H_B), 0.5),
        'rel_bias': nrm(k[20], (N_BUCKETS, H_B), 0.5),
        'w_out': nrm(k[21], (N_EVEN, D_MODEL, D_MODEL), D_MODEL ** -0.5 * BETA),
        'w_pool': nrm(k[22], (N_ODD, N_POOL_GROUPS, POOL_GROUP, POOL_GROUP), POOL_GROUP ** -0.5 * BETA),
        'pool_scale': 1.0 + nrm(k[23], (N_ODD, D_MODEL), 0.1),
        'ln_mix_g': 1.0 + nrm(k[24], (DEPTH, D_MODEL), 0.1),
        'ln_mix_b': nrm(k[25], (DEPTH, D_MODEL), 0.1),
        'ln_ffn_g': 1.0 + nrm(k[26], (DEPTH, D_MODEL), 0.1),
        'ln_ffn_b': nrm(k[27], (DEPTH, D_MODEL), 0.1),
        'w_router': nrm(k[28], (D_MODEL, N_EXPERTS), D_MODEL ** -0.5),
        'b_router': nrm(k[29], (N_EXPERTS,), 0.01),
        'w_ex_gate': nrm(k[30], (DEPTH, N_EXPERTS, D_MODEL, D_FF), D_MODEL ** -0.5),
        'w_ex_up': nrm(k[31], (DEPTH, N_EXPERTS, D_MODEL, D_FF), D_MODEL ** -0.5),
        'w_ex_down': nrm(k[32], (DEPTH, N_EXPERTS, D_FF, D_MODEL), D_FF ** -0.5 * BETA),
    }


def reference(x_prompt, x_sample, state_rwkv, state_shift, cache_swa_k, cache_swa_v, state_pool,
              w_in, tshift_mu, decay_w0, decay_w2, iclr_a0, iclr_a2, gate_w2, k_k, k_a, r_k, lnx_g, lnx_b,
              attn_sinks, rel_bias, w_out, w_pool, pool_scale, ln_mix_g, ln_mix_b, ln_ffn_g, ln_ffn_b,
              w_router, b_router, w_ex_gate, w_ex_up, w_ex_down):
    prm = dict(w_in=w_in, tshift_mu=tshift_mu, decay_w0=decay_w0, decay_w2=decay_w2, iclr_a0=iclr_a0,
               iclr_a2=iclr_a2, gate_w2=gate_w2, k_k=k_k, k_a=k_a, r_k=r_k, lnx_g=lnx_g, lnx_b=lnx_b,
               attn_sinks=attn_sinks, rel_bias=rel_bias, w_out=w_out, w_pool=w_pool, pool_scale=pool_scale,
               ln_mix_g=ln_mix_g, ln_mix_b=ln_mix_b, ln_ffn_g=ln_ffn_g, ln_ffn_b=ln_ffn_b,
               w_router=w_router, b_router=b_router, w_ex_gate=w_ex_gate, w_ex_up=w_ex_up, w_ex_down=w_ex_down)
    y_prompt, rwkv_p, shift_p, k_p, v_p, pool_p = trunk(x_prompt, 0, None, None, None, None, None, prm)
    y_sample, rwkv_s, shift_s, k_s, v_s, pool_s = trunk(x_sample, PAST_LEN, state_rwkv, state_shift,
                                                        cache_swa_k, cache_swa_v, state_pool, prm)
    return (y_prompt, y_sample, rwkv_p, rwkv_s, shift_p, shift_s, k_p, k_s, v_p, v_s, pool_p, pool_s)
```

```python
import functools
import math

import numpy as np
import jax
import jax.numpy as jnp
from jax import lax
from jax.experimental import pallas as pl
from jax.experimental.pallas import tpu as pltpu

F32 = jnp.float32
BF16 = jnp.bfloat16

D_MODEL = 1024
DEPTH = 2
PAST_LEN = 16384
D_A = 512
HEAD = 64
N_HEADS = 8
W_LORA, A_LORA, G_LORA = 32, 64, 96
RWKV_COLS = 3 * D_A + W_LORA + A_LORA + G_LORA
D_B = 512
KV_HEADS = 2
Q_PER_KV = 4
KV_COLS = KV_HEADS * HEAD
SWA_COLS = D_B + 2 * KV_COLS
WINDOW = 128
N_BUCKETS = 32
MAX_EXACT = 16
REL_MAX_DIST = 128
POOL_WINDOWS = (2, 4, 8, 16)
POOL_GROUP = 256
POOL_KEEP = 15
N_EXPERTS = 16
N_GROUPS = 4
EXPERTS_PER_GROUP = 4
D_FF = 256
ALPHA = (2.0 * DEPTH) ** 0.25
LN_EPS = 1e-5
LNX_EPS = 64e-5
NEG_INF = -1e30

LANES = 128
SUBLANES = 8
VMEM_LIMIT_BYTES = 56 * 1024 * 1024

LORA_PAD = LANES
RWKV_PAD_COLS = 3 * D_A + 3 * LORA_PAD
CHUNK = 64

NN = ((1,), (0,))
NT = ((1,), (1,))
TN = ((0,), (0,))


def _dg(a, b, dims=NN):
    return lax.dot_general(a, b, (dims, ((), ())), preferred_element_type=F32)


def _bdot(a, b, dims=NN):
    return _dg(a.astype(BF16), b.astype(BF16), dims)


def _split(x, n):
    parts, rem = [], x
    for i in range(n):
        p = rem.astype(BF16)
        parts.append(p)
        if i + 1 < n:
            rem = rem - p.astype(F32)
    return parts


def _mm(a, b, dims=NN, passes=3):
    if passes == 1:
        return _bdot(a, b, dims)
    ah, al = _split(a, 2)
    bh, bl = _split(b, 2)
    return _dg(ah, bh, dims) + (_dg(ah, bl, dims) + _dg(al, bh, dims))


def _dot_exact_rhs(x, ones_bf16, n=3):
    out = None
    for p in _split(x, n):
        t = _dg(p, ones_bf16)
        out = t if out is None else out + t
    return out


def _dot_exact_lhs(ones_bf16, x, n=3):
    out = None
    for p in _split(x, n):
        t = _dg(ones_bf16, p)
        out = t if out is None else out + t
    return out


def _sigmoid(x):
    return 1.0 / (1.0 + jnp.exp(-x))


def _layer_norm(z, g, b):
    mu = jnp.mean(z, axis=-1, keepdims=True)
    d = z - mu
    var = jnp.mean(d * d, axis=-1, keepdims=True)
    return d * lax.rsqrt(var + LN_EPS) * g + b


def _params(sem):
    return pltpu.CompilerParams(dimension_semantics=sem, vmem_limit_bytes=VMEM_LIMIT_BYTES)


def _full(shape):
    nd = len(shape)
    return pl.BlockSpec(shape, lambda *_: (0,) * nd)


def _inproj_prep_body(*refs, tm, tiles_per_seq, chunk, has_prev):
    if has_prev:
        x_ref, xp_ref = refs[0], refs[1]
        refs = refs[2:]
    else:
        x_ref, xp_ref = refs[0], None
        refs = refs[1:]
    (wr_ref, ws_ref, mu_ref, w0_ref, ww2_ref, a0_ref, wa2_ref, wg2_ref, kk_ref, ka_ref, rk_ref,
     ones_ref, tri_ref,
     rt_ref, kt_ref, bt_ref, at_ref, v_ref, bonus_ref, g_ref, gl_ref, swa_ref, carry_ref) = refs

    i = pl.program_id(0)
    xb = x_ref[...].astype(BF16)
    pr = _dg(xb, wr_ref[...])
    swa_ref[...] = _dg(xb, ws_ref[...])

    if has_prev:
        shifted = _dg(xp_ref[...].astype(BF16), wr_ref[...])
    else:
        @pl.when(i == 0)
        def _():
            carry_ref[...] = jnp.zeros_like(carry_ref)

        first = (i % tiles_per_seq) == 0
        prev_last = jnp.where(first, 0.0, carry_ref[0:1, :])
        rolled = pltpu.roll(pr, 1, axis=0)
        row = lax.broadcasted_iota(jnp.int32, (tm, 1), 0)
        shifted = jnp.where(row == 0, prev_last, rolled)
        carry_ref[0:1, :] = pr[tm - 1:tm, :]
    m = pr + mu_ref[...] * (shifted - pr)

    r = m[:, 0:D_A]
    k = m[:, D_A:2 * D_A]
    v = m[:, 2 * D_A:3 * D_A]
    wd = m[:, 3 * D_A:3 * D_A + LORA_PAD]
    ad = m[:, 3 * D_A + LORA_PAD:3 * D_A + 2 * LORA_PAD]
    gd = m[:, 3 * D_A + 2 * LORA_PAD:3 * D_A + 3 * LORA_PAD]

    nz = -(w0_ref[...] + _bdot(jnp.tanh(wd), ww2_ref[...]))
    softplus = jnp.maximum(nz, 0.0) + jnp.log1p(jnp.exp(-jnp.abs(nz)))
    w_log = -softplus - 0.5
    logw = -jnp.exp(w_log)
    a = _sigmoid(a0_ref[...] + _bdot(ad, wa2_ref[...]))
    g = _bdot(_sigmoid(gd), wg2_ref[...])

    ones_bd = ones_ref[...]
    kkr = k * kk_ref[...]
    nrm = jnp.sqrt(_dot_exact_rhs(kkr * kkr, ones_bd))
    kk = kkr / jnp.maximum(nrm, 1e-12)
    k2 = k * (1.0 + (a - 1.0) * ka_ref[...])
    bonus_ref[...] = _dot_exact_rhs(r * k2 * rk_ref[...], ones_bd) * v

    if chunk > 1:
        cum = _dot_exact_lhs(tri_ref[...], logw)
    else:
        cum = logw
    gam = jnp.exp(cum)
    inv = jnp.exp(-cum)
    rt_ref[...] = r * gam
    kt_ref[...] = k2 * inv
    bt_ref[...] = kk * a * inv
    at_ref[...] = -kk * jnp.exp(cum - logw)
    v_ref[...] = v
    g_ref[...] = g
    if chunk > 1:
        for c in range(tm // chunk):
            gl_ref[c] = gam[(c + 1) * chunk - 1:(c + 1) * chunk, :]
    else:
        gl_ref[...] = gam


def _inproj_prep(x2d, x_prev, wts, *, seq_len, chunk, tm):
    n = x2d.shape[0]
    assert n % tm == 0 and (x_prev is not None or seq_len % tm == 0)
    assert tm % chunk == 0
    has_prev = x_prev is not None
    row = lambda w: pl.BlockSpec((tm, w), lambda i: (i, 0))
    ins = [x2d] + ([x_prev] if has_prev else [])
    in_specs = [row(D_MODEL)] + ([row(D_MODEL)] if has_prev else [])
    consts = [wts['wr'], wts['ws'], wts['mu'], wts['w0'], wts['ww2'], wts['a0'], wts['wa2'], wts['wg2'],
              wts['k_k'], wts['k_a'], wts['r_k'], _ones_block_diag(D_A, HEAD), _tri_block_diag(tm, chunk)]
    ins += consts
    in_specs += [_full(c.shape) for c in consts]
    if chunk > 1:
        gl_shape = jax.ShapeDtypeStruct((n // chunk, 1, D_A), F32)
        gl_spec = pl.BlockSpec((tm // chunk, 1, D_A), lambda i: (i, 0, 0))
    else:
        gl_shape = jax.ShapeDtypeStruct((n, D_A), F32)
        gl_spec = row(D_A)
    out_shape = [jax.ShapeDtypeStruct((n, D_A), F32)] * 7 + [gl_shape, jax.ShapeDtypeStruct((n, SWA_COLS), F32)]
    out_specs = [row(D_A)] * 7 + [gl_spec, row(SWA_COLS)]
    body = functools.partial(_inproj_prep_body, tm=tm, tiles_per_seq=max(seq_len // tm, 1), chunk=chunk,
                             has_prev=has_prev)
    return pl.pallas_call(
        body, grid=(n // tm,), in_specs=in_specs, out_specs=out_specs, out_shape=out_shape,
        scratch_shapes=[pltpu.VMEM((SUBLANES, RWKV_PAD_COLS), F32)],
        compiler_params=_params(("arbitrary",)), name="inproj_prep",
    )(*ins)


def _ones_block_diag(n, blk):
    idx = np.arange(n) // blk
    return jnp.asarray(idx[:, None] == idx[None, :], dtype=BF16)


def _tri_block_diag(n, blk):
    idx = np.arange(n)
    same = (idx[:, None] // blk) == (idx[None, :] // blk)
    return jnp.asarray(same & (idx[:, None] >= idx[None, :]), dtype=BF16)


def _scan_body(rt_ref, kt_ref, bt_ref, at_ref, v_ref, bonus_ref, g_ref, gl_ref, lng_ref, lnb_ref, s0_ref,
               y_ref, sout_ref, S_ref, *, passes):
    c = pl.program_id(1)

    @pl.when(c == 0)
    def _():
        S_ref[...] = s0_ref[0]

    C = CHUNK
    row = lax.broadcasted_iota(jnp.int32, (C, C), 0)
    col = lax.broadcasted_iota(jnp.int32, (C, C), 1)
    strict = row > col
    incl = row >= col
    eye = (row == col).astype(F32)
    mm = functools.partial(_mm, passes=passes)

    outs = []
    for h in range(N_HEADS):
        sl = slice(HEAD * h, HEAD * (h + 1))
        R, K, Bm, A, V = rt_ref[:, sl], kt_ref[:, sl], bt_ref[:, sl], at_ref[:, sl], v_ref[:, sl]
        g_c = gl_ref[0, :, sl]
        S0 = S_ref[h]
        big = mm(jnp.concatenate([A, R], axis=0), jnp.concatenate([Bm, K], axis=0), NT)
        a_ab = jnp.where(strict, big[:C, :C], 0.0)
        a_ak = jnp.where(strict, big[:C, C:], 0.0)
        l_rb = jnp.where(incl, big[C:, :C], 0.0)
        l_rk = jnp.where(incl, big[C:, C:], 0.0)
        X = a_ab
        T = eye + X
        for _ in range(int(math.log2(C)) - 1):
            X = mm(X, X)
            T = T + mm(T, X)
        pq = mm(T, jnp.concatenate([A, mm(a_ak, V)], axis=1))
        wz = mm(l_rb, pq)
        W = R + wz[:, :HEAD]
        Z = wz[:, HEAD:] + mm(l_rk, V)
        pq_b = mm(pq, Bm, TN)
        M = (eye + pq_b[:HEAD]) * g_c
        G = (pq_b[HEAD:] + mm(V, K, TN)) * g_c
        Y = mm(W, S0, NT) + Z
        S_ref[h] = mm(S0, M) + G

        mu = jnp.mean(Y, axis=-1, keepdims=True)
        d = Y - mu
        var = jnp.mean(d * d, axis=-1, keepdims=True)
        yn = d * lax.rsqrt(var + LNX_EPS) * lng_ref[:, sl] + lnb_ref[:, sl]
        outs.append((yn + bonus_ref[:, sl]) * g_ref[:, sl])
    y_ref[...] = jnp.concatenate(outs, axis=1)

    @pl.when(c == pl.num_programs(1) - 1)
    def _():
        sout_ref[0] = S_ref[...]


def _rwkv_scan(rt, kt, bt, at, v, bonus, g, gl, lnx_g, lnx_b, s0, *, n_seq, seq_len, passes):
    ncb = seq_len // CHUNK
    row = pl.BlockSpec((CHUNK, D_A), lambda b, c: (b * ncb + c, 0))
    st = pl.BlockSpec((1, N_HEADS, HEAD, HEAD), lambda b, c: (b, 0, 0, 0))
    vec = pl.BlockSpec((1, D_A), lambda b, c: (0, 0))
    return pl.pallas_call(
        functools.partial(_scan_body, passes=passes),
        grid=(n_seq, ncb),
        in_specs=[row] * 7 + [pl.BlockSpec((1, 1, D_A), lambda b, c: (b * ncb + c, 0, 0)), vec, vec, st],
        out_specs=[row, st],
        out_shape=[jax.ShapeDtypeStruct((n_seq * seq_len, D_A), F32),
                   jax.ShapeDtypeStruct((n_seq, N_HEADS, HEAD, HEAD), F32)],
        scratch_shapes=[pltpu.VMEM((N_HEADS, HEAD, HEAD), F32)],
        compiler_params=_params(("arbitrary", "arbitrary")), name="rwkv_scan",
    )(rt, kt, bt, at, v, bonus, g, gl, lnx_g, lnx_b, s0)


def _bucket_ranges():
    d = np.arange(WINDOW + 1)
    scaled = np.log(np.maximum(d, MAX_EXACT).astype(np.float32) / MAX_EXACT) / math.log(REL_MAX_DIST / MAX_EXACT)
    large = np.minimum(MAX_EXACT + (scaled * (N_BUCKETS - MAX_EXACT)).astype(np.int32), N_BUCKETS - 1)
    bucket = np.where(d < MAX_EXACT, d, large)
    frac = scaled.astype(np.float64) * (N_BUCKETS - MAX_EXACT)
    near = np.abs(frac - np.round(frac)) < 1e-3
    assert all(int(x) in (MAX_EXACT, REL_MAX_DIST) for x in d[(d >= MAX_EXACT) & near])
    ranges = []
    for b in range(N_BUCKETS):
        hit = d[bucket == b]
        ranges.append((int(hit.min()), int(hit.max())) if hit.size else None)
    return tuple(ranges)


def _swa_body(rb_ref, sink_ref, cur_ref, prev_ref, y_ref, bias_ref, *, blk_start, ranges):
    W = WINDOW
    i = pl.program_id(1) + blk_start

    @pl.when((pl.program_id(0) == 0) & (pl.program_id(1) == 0))
    def _():
        qi = lax.broadcasted_iota(jnp.int32, (W, 2 * W), 0)
        kj = lax.broadcasted_iota(jnp.int32, (W, 2 * W), 1)
        d = qi + W - kj
        valid = (d >= 0) & (d <= W)
        for hd in range(N_HEADS):
            t = jnp.zeros((W, 2 * W), F32)
            for bkt, rng in enumerate(ranges):
                if rng is not None:
                    t = jnp.where((d >= rng[0]) & (d <= rng[1]), rb_ref[bkt * N_HEADS + hd], t)
            grp = hd % Q_PER_KV
            bias_ref[hd // Q_PER_KV, grp * W:(grp + 1) * W, :] = jnp.where(valid, t, NEG_INF)

    colk = lax.broadcasted_iota(jnp.int32, (Q_PER_KV * W, 2 * W), 1)
    kill = colk < jnp.where(i == 0, W, 0)
    outs = [None] * N_HEADS
    for kvh in range(KV_HEADS):
        ko, vo = D_B + kvh * HEAD, D_B + KV_COLS + kvh * HEAD
        kb = jnp.concatenate([prev_ref[:, kvh * HEAD:(kvh + 1) * HEAD], cur_ref[:, ko:ko + HEAD]], axis=0)
        vb = jnp.concatenate([prev_ref[:, KV_COLS + kvh * HEAD:KV_COLS + (kvh + 1) * HEAD],
                              cur_ref[:, vo:vo + HEAD]], axis=0)
        q4 = jnp.concatenate([cur_ref[:, (kvh * Q_PER_KV + gq) * HEAD:(kvh * Q_PER_KV + gq + 1) * HEAD]
                              for gq in range(Q_PER_KV)], axis=0)
        s = _bdot(q4, kb, NT) * (HEAD ** -0.5) + bias_ref[kvh]
        s = jnp.where(kill, NEG_INF, s)
        sink = jnp.concatenate([jnp.full((W, 1), sink_ref[kvh * Q_PER_KV + gq], F32) for gq in range(Q_PER_KV)],
                               axis=0)
        m = jnp.maximum(jnp.max(s, axis=-1, keepdims=True), sink)
        p = jnp.exp(s - m)
        den = jnp.sum(p, axis=-1, keepdims=True) + jnp.exp(sink - m)
        o = _bdot(p, vb) / den
        for gq in range(Q_PER_KV):
            outs[kvh * Q_PER_KV + gq] = o[gq * W:(gq + 1) * W]
    y_ref[...] = jnp.concatenate(outs, axis=1)


def _swa(swa2d, rel_bias, sinks, *, n_seq, n_blk, blk_start):
    W = WINDOW
    nout = n_blk - blk_start
    smem = pl.BlockSpec(memory_space=pltpu.SMEM)
    return pl.pallas_call(
        functools.partial(_swa_body, blk_start=blk_start, ranges=_bucket_ranges()),
        grid=(n_seq, nout),
        in_specs=[smem, smem,
                  pl.BlockSpec((W, SWA_COLS), lambda b, j: (b * n_blk + j + blk_start, 0)),
                  pl.BlockSpec((W, 2 * KV_COLS), lambda b, j: (b * n_blk + jnp.maximum(j + blk_start - 1, 0), 2))],
        out_specs=pl.BlockSpec((W, D_B), lambda b, j: (b * nout + j, 0)),
        out_shape=jax.ShapeDtypeStruct((n_seq * nout * W, D_B), F32),
        scratch_shapes=[pltpu.VMEM((KV_HEADS, Q_PER_KV * W, 2 * W), F32)],
        compiler_params=_params(("arbitrary", "arbitrary")), name="swa",
    )(rel_bias.reshape(-1), sinks, swa2d, swa2d)


def _outproj_ln_body(ya_ref, yb_ref, x_ref, wa_ref, wb_ref, g_ref, b_ref, o_ref):
    h = _dg(ya_ref[...].astype(BF16), wa_ref[...]) + _dg(yb_ref[...].astype(BF16), wb_ref[...])
    o_ref[...] = _layer_norm(ALPHA * x_ref[...] + h, g_ref[...], b_ref[...])


def _outproj_ln(ya, yb, x2d, w_out, ln_g, ln_b, *, tm):
    n = x2d.shape[0]
    half = pl.BlockSpec((tm, D_A), lambda i: (i, 0))
    full = pl.BlockSpec((tm, D_MODEL), lambda i: (i, 0))
    wa, wb = w_out[:D_A].astype(BF16), w_out[D_A:].astype(BF16)
    return pl.pallas_call(
        _outproj_ln_body, grid=(n // tm,),
        in_specs=[half, half, full, _full(wa.shape), _full(wb.shape), _full((1, D_MODEL)), _full((1, D_MODEL))],
        out_specs=full, out_shape=jax.ShapeDtypeStruct((n, D_MODEL), F32),
        compiler_params=_params(("arbitrary",)), name="outproj_ln",
    )(ya, yb, x2d, wa, wb, ln_g.reshape(1, -1), ln_b.reshape(1, -1))


def _router_body(x_ref, wrt_ref, br_ref, eye_ref, gates_ref):
    logits = _dg(wrt_ref[...], x_ref[...].astype(BF16), NT) + br_ref[...]
    z = jnp.exp(logits - jnp.max(logits, axis=0, keepdims=True))
    probs = z / jnp.sum(z, axis=0, keepdims=True)
    p = [probs[e:e + 1, :] for e in range(N_EXPERTS)]

    def top2_mask(vals):
        out = []
        for a_ in range(len(vals)):
            rank = None
            for j in range(len(vals)):
                if j == a_:
                    continue
                ahead = (vals[j] >= vals[a_]) if j < a_ else (vals[j] > vals[a_])
                ahead = ahead.astype(jnp.int32)
                rank = ahead if rank is None else rank + ahead
            out.append(rank < 2)
        return out

    sel, score = [], []
    for gi in range(N_GROUPS):
        grp = p[gi * EXPERTS_PER_GROUP:(gi + 1) * EXPERTS_PER_GROUP]
        msk = top2_mask(grp)
        sel += msk
        sc = None
        for a_ in range(EXPERTS_PER_GROUP):
            t = jnp.where(msk[a_], grp[a_], 0.0)
            sc = t if sc is None else sc + t
        score.append(sc)
    best = []
    for gi in range(N_GROUPS):
        ok = None
        for j in range(N_GROUPS):
            if j == gi:
                continue
            c_ = (score[gi] > score[j]) if j < gi else (score[gi] >= score[j])
            ok = c_ if ok is None else (ok & c_)
        best.append(ok)
    active = [sel[e] & best[e // EXPERTS_PER_GROUP] for e in range(N_EXPERTS)]
    tot = None
    for e in range(N_EXPERTS):
        t = jnp.where(active[e], p[e], 0.0)
        tot = t if tot is None else tot + t
    gates_t = jnp.concatenate([jnp.where(active[e], p[e] / tot, 0.0) for e in range(N_EXPERTS)], axis=0)
    out = None
    for part in _split(gates_t, 3):
        t = _dg(eye_ref[...], part, NT)
        out = t if out is None else out + t
    gates_ref[...] = out


def _router(x2d, w_router, b_router, *, tm):
    n = x2d.shape[0]
    eye = jnp.asarray(np.eye(tm), dtype=BF16)
    return pl.pallas_call(
        _router_body, grid=(n // tm,),
        in_specs=[pl.BlockSpec((tm, D_MODEL), lambda i: (i, 0)), _full((N_EXPERTS, D_MODEL)),
                  _full((N_EXPERTS, 1)), _full((tm, tm))],
        out_specs=pl.BlockSpec((tm, N_EXPERTS), lambda i: (i, 0)),
        out_shape=jax.ShapeDtypeStruct((n, N_EXPERTS), F32),
        compiler_params=_params(("arbitrary",)), name="router",
    )(x2d, w_router.T.astype(BF16), b_router.reshape(-1, 1), eye)


def _moe_ln_body(x_ref, gates_ref, wg_ref, wu_ref, wd_ref, g_ref, b_ref, o_ref, acc_ref, xb_ref):
    e = pl.program_id(1)

    @pl.when(e == 0)
    def _():
        xb_ref[...] = x_ref[...].astype(BF16)
        acc_ref[...] = jnp.zeros_like(acc_ref)

    xb = xb_ref[...]
    hg = _dg(xb, wg_ref[0])
    hu = _dg(xb, wu_ref[0])
    h = hg * _sigmoid(hg) * hu
    ye = _dg(h.astype(BF16), wd_ref[0])
    lane = lax.broadcasted_iota(jnp.int32, gates_ref.shape, 1)
    gate = jnp.sum(jnp.where(lane == e, gates_ref[...], 0.0), axis=1, keepdims=True)
    acc_ref[...] += gate * ye

    @pl.when(e == N_EXPERTS - 1)
    def _():
        o_ref[...] = _layer_norm(ALPHA * x_ref[...] + acc_ref[...], g_ref[...], b_ref[...])


def _moe_ln(x2d, gates, wg, wu, wd, ln_g, ln_b, *, tm):
    n = x2d.shape[0]
    tok = pl.BlockSpec((tm, D_MODEL), lambda i, e: (i, 0))
    return pl.pallas_call(
        _moe_ln_body, grid=(n // tm, N_EXPERTS),
        in_specs=[tok, pl.BlockSpec((tm, N_EXPERTS), lambda i, e: (i, 0)),
                  pl.BlockSpec((1, D_MODEL, D_FF), lambda i, e: (e, 0, 0)),
                  pl.BlockSpec((1, D_MODEL, D_FF), lambda i, e: (e, 0, 0)),
                  pl.BlockSpec((1, D_FF, D_MODEL), lambda i, e: (e, 0, 0)),
                  pl.BlockSpec((1, D_MODEL), lambda i, e: (0, 0)), pl.BlockSpec((1, D_MODEL), lambda i, e: (0, 0))],
        out_specs=tok, out_shape=jax.ShapeDtypeStruct((n, D_MODEL), F32),
        scratch_shapes=[pltpu.VMEM((tm, D_MODEL), F32), pltpu.VMEM((tm, D_MODEL), BF16)],
        compiler_params=_params(("arbitrary", "arbitrary")), name="moe_ln",
    )(x2d, gates, wg, wu, wd, ln_g.reshape(1, -1), ln_b.reshape(1, -1))


HALO = 16


def _pool_ln_body(x_ref, halo_ref, wp_ref, ps_ref, g_ref, b_ref, o_ref, *, tm, seq_len, start_pos):
    i = pl.program_id(0)
    x = x_ref[...]
    idx = lax.broadcasted_iota(jnp.int32, (tm + HALO, 1), 0)
    pos_e = (i * tm - HALO + idx) & (seq_len - 1)
    cnt_pos = (pos_e[HALO:] + (start_pos + 1)).astype(F32)
    mixed = []
    for gi, w in enumerate(POOL_WINDOWS):
        cols = slice(gi * POOL_GROUP, (gi + 1) * POOL_GROUP)
        xg = x[:, cols]
        s = jnp.concatenate([halo_ref[:, cols], xg], axis=0)
        step = 1
        while step < w:
            s = s + jnp.where(pos_e >= step, pltpu.roll(s, step, axis=0), 0.0)
            step *= 2
        pooled = s[HALO:] / jnp.minimum(float(w), cnt_pos) - xg
        mixed.append(_bdot(pooled, wp_ref[gi]))
    h = jnp.concatenate(mixed, axis=1) * ps_ref[...]
    o_ref[...] = _layer_norm(ALPHA * x + h, g_ref[...], b_ref[...])


def _pool_ln(x2d, w_pool, pool_scale, ln_g, ln_b, *, tm, seq_len, start_pos):
    n = x2d.shape[0]
    assert seq_len & (seq_len - 1) == 0 and seq_len >= HALO and tm % HALO == 0
    tok = pl.BlockSpec((tm, D_MODEL), lambda i: (i, 0))
    return pl.pallas_call(
        functools.partial(_pool_ln_body, tm=tm, seq_len=seq_len, start_pos=start_pos), grid=(n // tm,),
        in_specs=[tok, pl.BlockSpec((HALO, D_MODEL), lambda i: (jnp.maximum(i * (tm // HALO) - 1, 0), 0)),
                  _full(w_pool.shape), _full((1, D_MODEL)), _full((1, D_MODEL)), _full((1, D_MODEL))],
        out_specs=tok, out_shape=jax.ShapeDtypeStruct((n, D_MODEL), F32),
        compiler_params=_params(("arbitrary",)), name="pool_ln",
    )(x2d, x2d, w_pool.astype(BF16), pool_scale.reshape(1, -1), ln_g.reshape(1, -1), ln_b.reshape(1, -1))


def _pad_lora_cols(w, width):
    return jnp.pad(w, ((0, 0), (0, LORA_PAD - width)))


def _even_layer_weights(prm, i):
    w_in = prm['w_in'][i]
    o = 3 * D_A
    cols = lambda w: jnp.concatenate(
        [w[:, :o], _pad_lora_cols(w[:, o:o + W_LORA], W_LORA),
         _pad_lora_cols(w[:, o + W_LORA:o + W_LORA + A_LORA], A_LORA),
         _pad_lora_cols(w[:, o + W_LORA + A_LORA:RWKV_COLS], G_LORA)], axis=1)
    pad_rows = lambda w: jnp.pad(w, ((0, LORA_PAD - w.shape[0]), (0, 0))).astype(BF16)
    vec = lambda a: a.reshape(1, -1)
    return dict(
        wr=cols(w_in[:, :RWKV_COLS]).astype(BF16), ws=w_in[:, RWKV_COLS:].astype(BF16),
        mu=cols(prm['tshift_mu'][i].reshape(1, -1)),
        w0=vec(prm['decay_w0'][i]), ww2=pad_rows(prm['decay_w2'][i]),
        a0=vec(prm['iclr_a0'][i]), wa2=pad_rows(prm['iclr_a2'][i]), wg2=pad_rows(prm['gate_w2'][i]),
        k_k=vec(prm['k_k'][i]), k_a=vec(prm['k_a'][i]), r_k=vec(prm['r_k'][i]),
    )


def _moe_block(x2d, prm, layer, *, tm):
    gates = _router(x2d, prm['w_router'], prm['b_router'], tm=min(tm, 512))
    return _moe_ln(x2d, gates, prm['wg'][layer], prm['wu'][layer], prm['wd'][layer],
                   prm['ln_ffn_g'][layer], prm['ln_ffn_b'][layer], tm=tm)


RWKV_PASSES = 3


def _prompt_trunk(x, prm):
    bsz, t, _ = x.shape
    n = bsz * t
    x0 = x.reshape(n, D_MODEL)
    wts = _even_layer_weights(prm, 0)
    rt, kt, bt, at, v, bonus, g, gl, swa = _inproj_prep(x0, None, wts, seq_len=t, chunk=CHUNK, tm=256)
    s0 = jnp.zeros((bsz, N_HEADS, HEAD, HEAD), F32)
    ya, s_new = _rwkv_scan(rt, kt, bt, at, v, bonus, g, gl, prm['lnx_g'][0].reshape(1, -1),
                           prm['lnx_b'][0].reshape(1, -1), s0, n_seq=bsz, seq_len=t, passes=RWKV_PASSES)
    yb = _swa(swa, prm['rel_bias'], prm['attn_sinks'][0], n_seq=bsz, n_blk=t // WINDOW, blk_start=0)
    x1 = _outproj_ln(ya, yb, x0, prm['w_out'][0], prm['ln_mix_g'][0], prm['ln_mix_b'][0], tm=min(512, n))
    x2 = _moe_block(x1, prm, 0, tm=min(1024, n))
    x3 = _pool_ln(x2, prm['w_pool'][0], prm['pool_scale'][0], prm['ln_mix_g'][1], prm['ln_mix_b'][1],
                  tm=min(512, n), seq_len=t, start_pos=0)
    x4 = _moe_block(x3, prm, 1, tm=min(1024, n))
    swa3 = swa.reshape(bsz, t, SWA_COLS)
    k_new = swa3[:, t - WINDOW:, D_B:D_B + KV_COLS].reshape(bsz, WINDOW, KV_HEADS, HEAD)
    v_new = swa3[:, t - WINDOW:, D_B + KV_COLS:].reshape(bsz, WINDOW, KV_HEADS, HEAD)
    pool_new = x2.reshape(bsz, t, D_MODEL)[:, t - POOL_KEEP:]
    return (x4.reshape(bsz, t, D_MODEL), s_new[None], x[:, -1][None], k_new[None], v_new[None], pool_new[None])


def _sample_trunk(x, st_rwkv, st_shift, c_k, c_v, st_pool, prm):
    bsz = x.shape[0]
    x0 = x.reshape(bsz, D_MODEL)
    wts = _even_layer_weights(prm, 0)
    outs = _inproj_prep(x0, st_shift[0], wts, seq_len=1, chunk=1, tm=bsz)
    gl, swa = outs[7], outs[8]
    padded = [jnp.zeros((bsz, CHUNK, D_A), F32).at[:, 0].set(a).reshape(bsz * CHUNK, D_A) for a in outs[:7]]
    ya, s_new = _rwkv_scan(*padded, gl.reshape(bsz, 1, D_A), prm['lnx_g'][0].reshape(1, -1),
                           prm['lnx_b'][0].reshape(1, -1), st_rwkv[0], n_seq=bsz, seq_len=CHUNK,
                           passes=RWKV_PASSES)
    ya = ya.reshape(bsz, CHUNK, D_A)[:, 0]
    kv_prev = jnp.concatenate([jnp.zeros((bsz, WINDOW, D_B), F32), c_k[0].reshape(bsz, WINDOW, KV_COLS),
                               c_v[0].reshape(bsz, WINDOW, KV_COLS)], axis=-1)
    cur = jnp.zeros((bsz, WINDOW, SWA_COLS), F32).at[:, 0].set(swa)
    seqs = jnp.concatenate([kv_prev, cur], axis=1).reshape(bsz * 2 * WINDOW, SWA_COLS)
    yb = _swa(seqs, prm['rel_bias'], prm['attn_sinks'][0], n_seq=bsz, n_blk=2, blk_start=1)
    yb = yb.reshape(bsz, WINDOW, D_B)[:, 0]
    x1 = _outproj_ln(ya, yb, x0, prm['w_out'][0], prm['ln_mix_g'][0], prm['ln_mix_b'][0], tm=bsz)
    x2 = _moe_block(x1, prm, 0, tm=bsz)
    xcat = jnp.concatenate([st_pool[0], x2[:, None]], axis=1)
    x3 = _pool_ln(xcat.reshape(bsz * HALO, D_MODEL), prm['w_pool'][0], prm['pool_scale'][0],
                  prm['ln_mix_g'][1], prm['ln_mix_b'][1], tm=min(512, bsz * HALO), seq_len=HALO,
                  start_pos=PAST_LEN - POOL_KEEP)
    x3 = x3.reshape(bsz, HALO, D_MODEL)[:, -1]
    x4 = _moe_block(x3, prm, 1, tm=bsz)
    k_new = jnp.concatenate([c_k[0][:, 1:], swa[:, D_B:D_B + KV_COLS].reshape(bsz, 1, KV_HEADS, HEAD)], axis=1)
    v_new = jnp.concatenate([c_v[0][:, 1:], swa[:, D_B + KV_COLS:].reshape(bsz, 1, KV_HEADS, HEAD)], axis=1)
    return (x4.reshape(bsz, 1, D_MODEL), s_new[None], x[:, -1][None], k_new[None], v_new[None],
            xcat[:, 1:][None])


def kernel(x_prompt, x_sample, state_rwkv, state_shift, cache_swa_k, cache_swa_v, state_pool, w_in, tshift_mu,
           decay_w0, decay_w2, iclr_a0, iclr_a2, gate_w2, k_k, k_a, r_k, lnx_g, lnx_b, attn_sinks, rel_bias, w_out,
           w_pool, pool_scale, ln_mix_g, ln_mix_b, ln_ffn_g, ln_ffn_b, w_router, b_router, w_ex_gate, w_ex_up,
           w_ex_down):
    prm = dict(w_in=w_in, tshift_mu=tshift_mu, decay_w0=decay_w0, decay_w2=decay_w2, iclr_a0=iclr_a0,
               iclr_a2=iclr_a2, gate_w2=gate_w2, k_k=k_k, k_a=k_a, r_k=r_k.reshape(r_k.shape[0], -1),
               lnx_g=lnx_g, lnx_b=lnx_b, attn_sinks=attn_sinks, rel_bias=rel_bias, w_out=w_out, w_pool=w_pool,
               pool_scale=pool_scale, ln_mix_g=ln_mix_g, ln_mix_b=ln_mix_b, ln_ffn_g=ln_ffn_g, ln_ffn_b=ln_ffn_b,
               w_router=w_router, b_router=b_router,
               wg=w_ex_gate.astype(BF16), wu=w_ex_up.astype(BF16), wd=w_ex_down.astype(BF16))
    y_p, rwkv_p, shift_p, k_p, v_p, pool_p = _prompt_trunk(x_prompt, prm)
    y_s, rwkv_s, shift_s, k_s, v_s, pool_s = _sample_trunk(x_sample, state_rwkv, state_shift, cache_swa_k,
                                                            cache_swa_v, state_pool, prm)
    return (y_p, y_s, rwkv_p, rwkv_s, shift_p, shift_s, k_p, k_s, v_p, v_s, pool_p, pool_s)
```

```python
import functools
import math

import numpy as np
import jax
import jax.numpy as jnp
from jax import lax
from jax.experimental import pallas as pl
from jax.experimental.pallas import tpu as pltpu

F32 = jnp.float32
BF16 = jnp.bfloat16

D_MODEL = 1024
DEPTH = 2
PAST_LEN = 16384
D_A = 512
HEAD = 64
N_HEADS = 8
W_LORA, A_LORA, G_LORA = 32, 64, 96
RWKV_COLS = 3 * D_A + W_LORA + A_LORA + G_LORA
D_B = 512
KV_HEADS = 2
Q_PER_KV = 4
KV_COLS = KV_HEADS * HEAD
SWA_COLS = D_B + 2 * KV_COLS
WINDOW = 128
N_BUCKETS = 32
MAX_EXACT = 16
REL_MAX_DIST = 128
POOL_WINDOWS = (2, 4, 8, 16)
POOL_GROUP = 256
POOL_KEEP = 15
N_EXPERTS = 16
N_GROUPS = 4
EXPERTS_PER_GROUP = 4
D_FF = 256
ALPHA = (2.0 * DEPTH) ** 0.25
LN_EPS = 1e-5
LNX_EPS = 64e-5
NEG_INF = -1e30

LANES = 128
SUBLANES = 8
VMEM_LIMIT_BYTES = 56 * 1024 * 1024

LORA_PAD = LANES
RWKV_PAD_COLS = 3 * D_A + 3 * LORA_PAD
CHUNK = 64

NN = ((1,), (0,))
NT = ((1,), (1,))
TN = ((0,), (0,))


def _dg(a, b, dims=NN):
    return lax.dot_general(a, b, (dims, ((), ())), preferred_element_type=F32)


def _bdot(a, b, dims=NN):
    return _dg(a.astype(BF16), b.astype(BF16), dims)


def _split(x, n):
    parts, rem = [], x
    for i in range(n):
        p = rem.astype(BF16)
        parts.append(p)
        if i + 1 < n:
            rem = rem - p.astype(F32)
    return parts


def _mm(a, b, dims=NN, passes=3):
    if passes == 1:
        return _bdot(a, b, dims)
    ah, al = _split(a, 2)
    bh, bl = _split(b, 2)
    return _dg(ah, bh, dims) + (_dg(ah, bl, dims) + _dg(al, bh, dims))


def _dot_exact_rhs(x, ones_bf16, n=3):
    out = None
    for p in _split(x, n):
        t = _dg(p, ones_bf16)
        out = t if out is None else out + t
    return out


def _dot_exact_lhs(ones_bf16, x, n=3):
    out = None
    for p in _split(x, n):
        t = _dg(ones_bf16, p)
        out = t if out is None else out + t
    return out


def _sigmoid(x):
    return 1.0 / (1.0 + jnp.exp(-x))


def _layer_norm(z, g, b):
    mu = jnp.mean(z, axis=-1, keepdims=True)
    d = z - mu
    var = jnp.mean(d * d, axis=-1, keepdims=True)
    return d * lax.rsqrt(var + LN_EPS) * g + b


def _params(sem):
    return pltpu.CompilerParams(dimension_semantics=sem, vmem_limit_bytes=VMEM_LIMIT_BYTES)


def _full(shape):
    nd = len(shape)
    return pl.BlockSpec(shape, lambda *_: (0,) * nd)


def _inproj_prep_body(*refs, tm, tiles_per_seq, chunk, has_prev):
    if has_prev:
        x_ref, xp_ref = refs[0], refs[1]
        refs = refs[2:]
    else:
        x_ref, xp_ref = refs[0], None
        refs = refs[1:]
    (wr_ref, ws_ref, mu_ref, w0_ref, ww2_ref, a0_ref, wa2_ref, wg2_ref, kk_ref, ka_ref, rk_ref,
     ones_ref, tri_ref,
     rt_ref, kt_ref, bt_ref, at_ref, v_ref, bonus_ref, g_ref, gl_ref, swa_ref, carry_ref) = refs

    i = pl.program_id(0)
    xb = x_ref[...].astype(BF16)
    pr = _dg(xb, wr_ref[...])
    swa_ref[...] = _dg(xb, ws_ref[...])

    if has_prev:
        shifted = _dg(xp_ref[...].astype(BF16), wr_ref[...])
    else:
        @pl.when(i == 0)
        def _():
            carry_ref[...] = jnp.zeros_like(carry_ref)

        first = (i % tiles_per_seq) == 0
        prev_last = jnp.where(first, 0.0, carry_ref[0:1, :])
        rolled = pltpu.roll(pr, 1, axis=0)
        row = lax.broadcasted_iota(jnp.int32, (tm, 1), 0)
        shifted = jnp.where(row == 0, prev_last, rolled)
        carry_ref[0:1, :] = pr[tm - 1:tm, :]
    m = pr + mu_ref[...] * (shifted - pr)

    r = m[:, 0:D_A]
    k = m[:, D_A:2 * D_A]
    v = m[:, 2 * D_A:3 * D_A]
    wd = m[:, 3 * D_A:3 * D_A + LORA_PAD]
    ad = m[:, 3 * D_A + LORA_PAD:3 * D_A + 2 * LORA_PAD]
    gd = m[:, 3 * D_A + 2 * LORA_PAD:3 * D_A + 3 * LORA_PAD]

    nz = -(w0_ref[...] + _bdot(jnp.tanh(wd), ww2_ref[...]))
    softplus = jnp.maximum(nz, 0.0) + jnp.log1p(jnp.exp(-jnp.abs(nz)))
    w_log = -softplus - 0.5
    logw = -jnp.exp(w_log)
    a = _sigmoid(a0_ref[...] + _bdot(ad, wa2_ref[...]))
    g = _bdot(_sigmoid(gd), wg2_ref[...])

    ones_bd = ones_ref[...]
    kkr = k * kk_ref[...]
    nrm = jnp.sqrt(_dot_exact_rhs(kkr * kkr, ones_bd))
    kk = kkr / jnp.maximum(nrm, 1e-12)
    k2 = k * (1.0 + (a - 1.0) * ka_ref[...])
    bonus_ref[...] = _dot_exact_rhs(r * k2 * rk_ref[...], ones_bd) * v

    if chunk > 1:
        cum = _dot_exact_lhs(tri_ref[...], logw)
    else:
        cum = logw
    gam = jnp.exp(cum)
    inv = jnp.exp(-cum)
    rt_ref[...] = r * gam
    kt_ref[...] = k2 * inv
    bt_ref[...] = kk * a * inv
    at_ref[...] = -kk * jnp.exp(cum - logw)
    v_ref[...] = v
    g_ref[...] = g
    if chunk > 1:
        for c in range(tm // chunk):
            gl_ref[c] = gam[(c + 1) * chunk - 1:(c + 1) * chunk, :]
    else:
        gl_ref[...] = gam


def _inproj_prep(x2d, x_prev, wts, *, seq_len, chunk, tm):
    n = x2d.shape[0]
    assert n % tm == 0 and (x_prev is not None or seq_len % tm == 0)
    assert tm % chunk == 0
    has_prev = x_prev is not None
    row = lambda w: pl.BlockSpec((tm, w), lambda i: (i, 0))
    ins = [x2d] + ([x_prev] if has_prev else [])
    in_specs = [row(D_MODEL)] + ([row(D_MODEL)] if has_prev else [])
    consts = [wts['wr'], wts['ws'], wts['mu'], wts['w0'], wts['ww2'], wts['a0'], wts['wa2'], wts['wg2'],
              wts['k_k'], wts['k_a'], wts['r_k'], _ones_block_diag(D_A, HEAD), _tri_block_diag(tm, chunk)]
    ins += consts
    in_specs += [_full(c.shape) for c in consts]
    if chunk > 1:
        gl_shape = jax.ShapeDtypeStruct((n // chunk, 1, D_A), F32)
        gl_spec = pl.BlockSpec((tm // chunk, 1, D_A), lambda i: (i, 0, 0))
    else:
        gl_shape = jax.ShapeDtypeStruct((n, D_A), F32)
        gl_spec = row(D_A)
    out_shape = [jax.ShapeDtypeStruct((n, D_A), F32)] * 7 + [gl_shape, jax.ShapeDtypeStruct((n, SWA_COLS), F32)]
    out_specs = [row(D_A)] * 7 + [gl_spec, row(SWA_COLS)]
    body = functools.partial(_inproj_prep_body, tm=tm, tiles_per_seq=max(seq_len // tm, 1), chunk=chunk,
                             has_prev=has_prev)
    return pl.pallas_call(
        body, grid=(n // tm,), in_specs=in_specs, out_specs=out_specs, out_shape=out_shape,
        scratch_shapes=[pltpu.VMEM((SUBLANES, RWKV_PAD_COLS), F32)],
        compiler_params=_params(("arbitrary",)), name="inproj_prep",
    )(*ins)


def _ones_block_diag(n, blk):
    idx = np.arange(n) // blk
    return jnp.asarray(idx[:, None] == idx[None, :], dtype=BF16)


def _tri_block_diag(n, blk):
    idx = np.arange(n)
    same = (idx[:, None] // blk) == (idx[None, :] // blk)
    return jnp.asarray(same & (idx[:, None] >= idx[None, :]), dtype=BF16)


def _scan_body(rt_ref, kt_ref, bt_ref, at_ref, v_ref, bonus_ref, g_ref, gl_ref, lng_ref, lnb_ref, s0_ref,
               y_ref, sout_ref, S_ref, *, passes):
    c = pl.program_id(1)

    @pl.when(c == 0)
    def _():
        S_ref[...] = s0_ref[0]

    C = CHUNK
    row = lax.broadcasted_iota(jnp.int32, (C, C), 0)
    col = lax.broadcasted_iota(jnp.int32, (C, C), 1)
    strict = row > col
    incl = row >= col
    eye = (row == col).astype(F32)
    mm = functools.partial(_mm, passes=passes)

    outs = []
    for h in range(N_HEADS):
        sl = slice(HEAD * h, HEAD * (h + 1))
        R, K, Bm, A, V = rt_ref[:, sl], kt_ref[:, sl], bt_ref[:, sl], at_ref[:, sl], v_ref[:, sl]
        g_c = gl_ref[0, :, sl]
        S0 = S_ref[h]
        big = mm(jnp.concatenate([A, R], axis=0), jnp.concatenate([Bm, K], axis=0), NT)
        a_ab = jnp.where(strict, big[:C, :C], 0.0)
        a_ak = jnp.where(strict, big[:C, C:], 0.0)
        l_rb = jnp.where(incl, big[C:, :C], 0.0)
        l_rk = jnp.where(incl, big[C:, C:], 0.0)
        X = a_ab
        T = eye + X
        for _ in range(int(math.log2(C)) - 1):
            X = mm(X, X)
            T = T + mm(T, X)
        pq = mm(T, jnp.concatenate([A, mm(a_ak, V)], axis=1))
        wz = mm(l_rb, pq)
        W = R + wz[:, :HEAD]
        Z = wz[:, HEAD:] + mm(l_rk, V)
        pq_b = mm(pq, Bm, TN)
        M = (eye + pq_b[:HEAD]) * g_c
        G = (pq_b[HEAD:] + mm(V, K, TN)) * g_c
        Y = mm(W, S0, NT) + Z
        S_ref[h] = mm(S0, M) + G

        mu = jnp.mean(Y, axis=-1, keepdims=True)
        d = Y - mu
        var = jnp.mean(d * d, axis=-1, keepdims=True)
        yn = d * lax.rsqrt(var + LNX_EPS) * lng_ref[:, sl] + lnb_ref[:, sl]
        outs.append((yn + bonus_ref[:, sl]) * g_ref[:, sl])
    y_ref[...] = jnp.concatenate(outs, axis=1)

    @pl.when(c == pl.num_programs(1) - 1)
    def _():
        sout_ref[0] = S_ref[...]


def _rwkv_scan(rt, kt, bt, at, v, bonus, g, gl, lnx_g, lnx_b, s0, *, n_seq, seq_len, passes):
    ncb = seq_len // CHUNK
    row = pl.BlockSpec((CHUNK, D_A), lambda b, c: (b * ncb + c, 0))
    st = pl.BlockSpec((1, N_HEADS, HEAD, HEAD), lambda b, c: (b, 0, 0, 0))
    vec = pl.BlockSpec((1, D_A), lambda b, c: (0, 0))
    return pl.pallas_call(
        functools.partial(_scan_body, passes=passes),
        grid=(n_seq, ncb),
        in_specs=[row] * 7 + [pl.BlockSpec((1, 1, D_A), lambda b, c: (b * ncb + c, 0, 0)), vec, vec, st],
        out_specs=[row, st],
        out_shape=[jax.ShapeDtypeStruct((n_seq * seq_len, D_A), F32),
                   jax.ShapeDtypeStruct((n_seq, N_HEADS, HEAD, HEAD), F32)],
        scratch_shapes=[pltpu.VMEM((N_HEADS, HEAD, HEAD), F32)],
        compiler_params=_params(("arbitrary", "arbitrary")), name="rwkv_scan",
    )(rt, kt, bt, at, v, bonus, g, gl, lnx_g, lnx_b, s0)


def _bucket_ranges():
    d = np.arange(WINDOW + 1)
    scaled = np.log(np.maximum(d, MAX_EXACT).astype(np.float32) / MAX_EXACT) / math.log(REL_MAX_DIST / MAX_EXACT)
    large = np.minimum(MAX_EXACT + (scaled * (N_BUCKETS - MAX_EXACT)).astype(np.int32), N_BUCKETS - 1)
    bucket = np.where(d < MAX_EXACT, d, large)
    frac = scaled.astype(np.float64) * (N_BUCKETS - MAX_EXACT)
    near = np.abs(frac - np.round(frac)) < 1e-3
    assert all(int(x) in (MAX_EXACT, REL_MAX_DIST) for x in d[(d >= MAX_EXACT) & near])
    ranges = []
    for b in range(N_BUCKETS):
        hit = d[bucket == b]
        ranges.append((int(hit.min()), int(hit.max())) if hit.size else None)
    return tuple(ranges)


def _swa_body(rb_ref, sink_ref, cur_ref, prev_ref, y_ref, bias_ref, *, blk_start, ranges):
    W = WINDOW
    i = pl.program_id(1) + blk_start

    @pl.when((pl.program_id(0) == 0) & (pl.program_id(1) == 0))
    def _():
        qi = lax.broadcasted_iota(jnp.int32, (W, 2 * W), 0)
        kj = lax.broadcasted_iota(jnp.int32, (W, 2 * W), 1)
        d = qi + W - kj
        valid = (d >= 0) & (d <= W)
        for hd in range(N_HEADS):
            t = jnp.zeros((W, 2 * W), F32)
            for bkt, rng in enumerate(ranges):
                if rng is not None:
                    t = jnp.where((d >= rng[0]) & (d <= rng[1]), rb_ref[bkt * N_HEADS + hd], t)
            grp = hd % Q_PER_KV
            bias_ref[hd // Q_PER_KV, grp * W:(grp + 1) * W, :] = jnp.where(valid, t, NEG_INF)

    colk = lax.broadcasted_iota(jnp.int32, (Q_PER_KV * W, 2 * W), 1)
    kill = colk < jnp.where(i == 0, W, 0)
    outs = [None] * N_HEADS
    for kvh in range(KV_HEADS):
        ko, vo = D_B + kvh * HEAD, D_B + KV_COLS + kvh * HEAD
        kb = jnp.concatenate([prev_ref[:, kvh * HEAD:(kvh + 1) * HEAD], cur_ref[:, ko:ko + HEAD]], axis=0)
        vb = jnp.concatenate([prev_ref[:, KV_COLS + kvh * HEAD:KV_COLS + (kvh + 1) * HEAD],
                              cur_ref[:, vo:vo + HEAD]], axis=0)
        q4 = jnp.concatenate([cur_ref[:, (kvh * Q_PER_KV + gq) * HEAD:(kvh * Q_PER_KV + gq + 1) * HEAD]
                              for gq in range(Q_PER_KV)], axis=0)
        s = _bdot(q4, kb, NT) * (HEAD ** -0.5) + bias_ref[kvh]
        s = jnp.where(kill, NEG_INF, s)
        sink = jnp.concatenate([jnp.full((W, 1), sink_ref[kvh * Q_PER_KV + gq], F32) for gq in range(Q_PER_KV)],
                               axis=0)
        m = jnp.maximum(jnp.max(s, axis=-1, keepdims=True), sink)
        p = jnp.exp(s - m)
        den = jnp.sum(p, axis=-1, keepdims=True) + jnp.exp(sink - m)
        o = _bdot(p, vb) / den
        for gq in range(Q_PER_KV):
            outs[kvh * Q_PER_KV + gq] = o[gq * W:(gq + 1) * W]
    y_ref[...] = jnp.concatenate(outs, axis=1)


def _swa(swa2d, rel_bias, sinks, *, n_seq, n_blk, blk_start):
    W = WINDOW
    nout = n_blk - blk_start
    smem = pl.BlockSpec(memory_space=pltpu.SMEM)
    return pl.pallas_call(
        functools.partial(_swa_body, blk_start=blk_start, ranges=_bucket_ranges()),
        grid=(n_seq, nout),
        in_specs=[smem, smem,
                  pl.BlockSpec((W, SWA_COLS), lambda b, j: (b * n_blk + j + blk_start, 0)),
                  pl.BlockSpec((W, 2 * KV_COLS), lambda b, j: (b * n_blk + jnp.maximum(j + blk_start - 1, 0), 2))],
        out_specs=pl.BlockSpec((W, D_B), lambda b, j: (b * nout + j, 0)),
        out_shape=jax.ShapeDtypeStruct((n_seq * nout * W, D_B), F32),
        scratch_shapes=[pltpu.VMEM((KV_HEADS, Q_PER_KV * W, 2 * W), F32)],
        compiler_params=_params(("arbitrary", "arbitrary")), name="swa",
    )(rel_bias.reshape(-1), sinks, swa2d, swa2d)


def _outproj_ln_body(ya_ref, yb_ref, x_ref, wa_ref, wb_ref, g_ref, b_ref, o_ref):
    h = _dg(ya_ref[...].astype(BF16), wa_ref[...]) + _dg(yb_ref[...].astype(BF16), wb_ref[...])
    o_ref[...] = _layer_norm(ALPHA * x_ref[...] + h, g_ref[...], b_ref[...])


def _outproj_ln(ya, yb, x2d, w_out, ln_g, ln_b, *, tm):
    n = x2d.shape[0]
    half = pl.BlockSpec((tm, D_A), lambda i: (i, 0))
    full = pl.BlockSpec((tm, D_MODEL), lambda i: (i, 0))
    wa, wb = w_out[:D_A].astype(BF16), w_out[D_A:].astype(BF16)
    return pl.pallas_call(
        _outproj_ln_body, grid=(n // tm,),
        in_specs=[half, half, full, _full(wa.shape), _full(wb.shape), _full((1, D_MODEL)), _full((1, D_MODEL))],
        out_specs=full, out_shape=jax.ShapeDtypeStruct((n, D_MODEL), F32),
        compiler_params=_params(("arbitrary",)), name="outproj_ln",
    )(ya, yb, x2d, wa, wb, ln_g.reshape(1, -1), ln_b.reshape(1, -1))


def _router_body(x_ref, wrt_ref, br_ref, eye_ref, gates_ref):
    logits = _dg(wrt_ref[...], x_ref[...].astype(BF16), NT) + br_ref[...]
    z = jnp.exp(logits - jnp.max(logits, axis=0, keepdims=True))
    probs = z / jnp.sum(z, axis=0, keepdims=True)
    p = [probs[e:e + 1, :] for e in range(N_EXPERTS)]

    def top2_mask(vals):
        out = []
        for a_ in range(len(vals)):
            rank = None
            for j in range(len(vals)):
                if j == a_:
                    continue
                ahead = (vals[j] >= vals[a_]) if j < a_ else (vals[j] > vals[a_])
                ahead = ahead.astype(jnp.int32)
                rank = ahead if rank is None else rank + ahead
            out.append(rank < 2)
        return out

    sel, score = [], []
    for gi in range(N_GROUPS):
        grp = p[gi * EXPERTS_PER_GROUP:(gi + 1) * EXPERTS_PER_GROUP]
        msk = top2_mask(grp)
        sel += msk
        sc = None
        for a_ in range(EXPERTS_PER_GROUP):
            t = jnp.where(msk[a_], grp[a_], 0.0)
            sc = t if sc is None else sc + t
        score.append(sc)
    best = []
    for gi in range(N_GROUPS):
        ok = None
        for j in range(N_GROUPS):
            if j == gi:
                continue
            c_ = (score[gi] > score[j]) if j < gi else (score[gi] >= score[j])
            ok = c_ if ok is None else (ok & c_)
        best.append(ok)
    active = [sel[e] & best[e // EXPERTS_PER_GROUP] for e in range(N_EXPERTS)]
    tot = None
    for e in range(N_EXPERTS):
        t = jnp.where(active[e], p[e], 0.0)
        tot = t if tot is None else tot + t
    gates_t = jnp.concatenate([jnp.where(active[e], p[e] / tot, 0.0) for e in range(N_EXPERTS)], axis=0)
    out = None
    for part in _split(gates_t, 3):
        t = _dg(eye_ref[...], part, NT)
        out = t if out is None else out + t
    gates_ref[...] = out


def _router(x2d, w_router, b_router, *, tm):
    n = x2d.shape[0]
    eye = jnp.asarray(np.eye(tm), dtype=BF16)
    return pl.pallas_call(
        _router_body, grid=(n // tm,),
        in_specs=[pl.BlockSpec((tm, D_MODEL), lambda i: (i, 0)), _full((N_EXPERTS, D_MODEL)),
                  _full((N_EXPERTS, 1)), _full((tm, tm))],
        out_specs=pl.BlockSpec((tm, N_EXPERTS), lambda i: (i, 0)),
        out_shape=jax.ShapeDtypeStruct((n, N_EXPERTS), F32),
        compiler_params=_params(("arbitrary",)), name="router",
    )(x2d, w_router.T.astype(BF16), b_router.reshape(-1, 1), eye)


def _moe_ln_body(x_ref, gates_ref, wg_ref, wu_ref, wd_ref, g_ref, b_ref, o_ref, acc_ref, xb_ref):
    e = pl.program_id(1)

    @pl.when(e == 0)
    def _():
        xb_ref[...] = x_ref[...].astype(BF16)
        acc_ref[...] = jnp.zeros_like(acc_ref)

    xb = xb_ref[...]
    hg = _dg(xb, wg_ref[0])
    hu = _dg(xb, wu_ref[0])
    h = hg * _sigmoid(hg) * hu
    ye = _dg(h.astype(BF16), wd_ref[0])
    lane = lax.broadcasted_iota(jnp.int32, gates_ref.shape, 1)
    gate = jnp.sum(jnp.where(lane == e, gates_ref[...], 0.0), axis=1, keepdims=True)
    acc_ref[...] += gate * ye

    @pl.when(e == N_EXPERTS - 1)
    def _():
        o_ref[...] = _layer_norm(ALPHA * x_ref[...] + acc_ref[...], g_ref[...], b_ref[...])


def _moe_ln(x2d, gates, wg, wu, wd, ln_g, ln_b, *, tm):
    n = x2d.shape[0]
    tok = pl.BlockSpec((tm, D_MODEL), lambda i, e: (i, 0))
    return pl.pallas_call(
        _moe_ln_body, grid=(n // tm, N_EXPERTS),
        in_specs=[tok, pl.BlockSpec((tm, N_EXPERTS), lambda i, e: (i, 0)),
                  pl.BlockSpec((1, D_MODEL, D_FF), lambda i, e: (e, 0, 0)),
                  pl.BlockSpec((1, D_MODEL, D_FF), lambda i, e: (e, 0, 0)),
                  pl.BlockSpec((1, D_FF, D_MODEL), lambda i, e: (e, 0, 0)),
                  pl.BlockSpec((1, D_MODEL), lambda i, e: (0, 0)), pl.BlockSpec((1, D_MODEL), lambda i, e: (0, 0))],
        out_specs=tok, out_shape=jax.ShapeDtypeStruct((n, D_MODEL), F32),
        scratch_shapes=[pltpu.VMEM((tm, D_MODEL), F32), pltpu.VMEM((tm, D_MODEL), BF16)],
        compiler_params=_params(("arbitrary", "arbitrary")), name="moe_ln",
    )(x2d, gates, wg, wu, wd, ln_g.reshape(1, -1), ln_b.reshape(1, -1))


HALO = 16


def _pool_ln_body(x_ref, halo_ref, wp_ref, ps_ref, g_ref, b_ref, o_ref, *, tm, seq_len, start_pos):
    i = pl.program_id(0)
    x = x_ref[...]
    idx = lax.broadcasted_iota(jnp.int32, (tm + HALO, 1), 0)
    pos_e = (i * tm - HALO + idx) & (seq_len - 1)
    cnt_pos = (pos_e[HALO:] + (start_pos + 1)).astype(F32)
    mixed = []
    for gi, w in enumerate(POOL_WINDOWS):
        cols = slice(gi * POOL_GROUP, (gi + 1) * POOL_GROUP)
        xg = x[:, cols]
        s = jnp.concatenate([halo_ref[:, cols], xg], axis=0)
        step = 1
        while step < w:
            s = s + jnp.where(pos_e >= step, pltpu.roll(s, step, axis=0), 0.0)
            step *= 2
        pooled = s[HALO:] / jnp.minimum(float(w), cnt_pos) - xg
        mixed.append(_bdot(pooled, wp_ref[gi]))
    h = jnp.concatenate(mixed, axis=1) * ps_ref[...]
    o_ref[...] = _layer_norm(ALPHA * x + h, g_ref[...], b_ref[...])


def _pool_ln(x2d, w_pool, pool_scale, ln_g, ln_b, *, tm, seq_len, start_pos):
    n = x2d.shape[0]
    assert seq_len & (seq_len - 1) == 0 and seq_len >= HALO and tm % HALO == 0
    tok = pl.BlockSpec((tm, D_MODEL), lambda i: (i, 0))
    return pl.pallas_call(
        functools.partial(_pool_ln_body, tm=tm, seq_len=seq_len, start_pos=start_pos), grid=(n // tm,),
        in_specs=[tok, pl.BlockSpec((HALO, D_MODEL), lambda i: (jnp.maximum(i * (tm // HALO) - 1, 0), 0)),
                  _full(w_pool.shape), _full((1, D_MODEL)), _full((1, D_MODEL)), _full((1, D_MODEL))],
        out_specs=tok, out_shape=jax.ShapeDtypeStruct((n, D_MODEL), F32),
        compiler_params=_params(("arbitrary",)), name="pool_ln",
    )(x2d, x2d, w_pool.astype(BF16), pool_scale.reshape(1, -1), ln_g.reshape(1, -1), ln_b.reshape(1, -1))


def _pad_lora_cols(w, width):
    return jnp.pad(w, ((0, 0), (0, LORA_PAD - width)))


def _even_layer_weights(prm, i):
    w_in = prm['w_in'][i]
    o = 3 * D_A
    cols = lambda w: jnp.concatenate(
        [w[:, :o], _pad_lora_cols(w[:, o:o + W_LORA], W_LORA),
         _pad_lora_cols(w[:, o + W_LORA:o + W_LORA + A_LORA], A_LORA),
         _pad_lora_cols(w[:, o + W_LORA + A_LORA:RWKV_COLS], G_LORA)], axis=1)
    pad_rows = lambda w: jnp.pad(w, ((0, LORA_PAD - w.shape[0]), (0, 0))).astype(BF16)
    vec = lambda a: a.reshape(1, -1)
    return dict(
        wr=cols(w_in[:, :RWKV_COLS]).astype(BF16), ws=w_in[:, RWKV_COLS:].astype(BF16),
        mu=cols(prm['tshift_mu'][i].reshape(1, -1)),
        w0=vec(prm['decay_w0'][i]), ww2=pad_rows(prm['decay_w2'][i]),
        a0=vec(prm['iclr_a0'][i]), wa2=pad_rows(prm['iclr_a2'][i]), wg2=pad_rows(prm['gate_w2'][i]),
        k_k=vec(prm['k_k'][i]), k_a=vec(prm['k_a'][i]), r_k=vec(prm['r_k'][i]),
    )


def _moe_block(x2d, prm, layer, *, tm):
    gates = _router(x2d, prm['w_router'], prm['b_router'], tm=min(tm, 512))
    return _moe_ln(x2d, gates, prm['wg'][layer], prm['wu'][layer], prm['wd'][layer],
                   prm['ln_ffn_g'][layer], prm['ln_ffn_b'][layer], tm=tm)


RWKV_PASSES = 1


def _prompt_trunk(x, prm):
    bsz, t, _ = x.shape
    n = bsz * t
    x0 = x.reshape(n, D_MODEL)
    wts = _even_layer_weights(prm, 0)
    rt, kt, bt, at, v, bonus, g, gl, swa = _inproj_prep(x0, None, wts, seq_len=t, chunk=CHUNK, tm=256)
    s0 = jnp.zeros((bsz, N_HEADS, HEAD, HEAD), F32)
    ya, s_new = _rwkv_scan(rt, kt, bt, at, v, bonus, g, gl, prm['lnx_g'][0].reshape(1, -1),
                           prm['lnx_b'][0].reshape(1, -1), s0, n_seq=bsz, seq_len=t, passes=RWKV_PASSES)
    yb = _swa(swa, prm['rel_bias'], prm['attn_sinks'][0], n_seq=bsz, n_blk=t // WINDOW, blk_start=0)
    x1 = _outproj_ln(ya, yb, x0, prm['w_out'][0], prm['ln_mix_g'][0], prm['ln_mix_b'][0], tm=min(512, n))
    x2 = _moe_block(x1, prm, 0, tm=min(1024, n))
    x3 = _pool_ln(x2, prm['w_pool'][0], prm['pool_scale'][0], prm['ln_mix_g'][1], prm['ln_mix_b'][1],
                  tm=min(512, n), seq_len=t, start_pos=0)
    x4 = _moe_block(x3, prm, 1, tm=min(1024, n))
    swa3 = swa.reshape(bsz, t, SWA_COLS)
    k_new = swa3[:, t - WINDOW:, D_B:D_B + KV_COLS].reshape(bsz, WINDOW, KV_HEADS, HEAD)
    v_new = swa3[:, t - WINDOW:, D_B + KV_COLS:].reshape(bsz, WINDOW, KV_HEADS, HEAD)
    pool_new = x2.reshape(bsz, t, D_MODEL)[:, t - POOL_KEEP:]
    return (x4.reshape(bsz, t, D_MODEL), s_new[None], x[:, -1][None], k_new[None], v_new[None], pool_new[None])


def _sample_trunk(x, st_rwkv, st_shift, c_k, c_v, st_pool, prm):
    bsz = x.shape[0]
    x0 = x.reshape(bsz, D_MODEL)
    wts = _even_layer_weights(prm, 0)
    outs = _inproj_prep(x0, st_shift[0], wts, seq_len=1, chunk=1, tm=bsz)
    gl, swa = outs[7], outs[8]
    padded = [jnp.zeros((bsz, CHUNK, D_A), F32).at[:, 0].set(a).reshape(bsz * CHUNK, D_A) for a in outs[:7]]
    ya, s_new = _rwkv_scan(*padded, gl.reshape(bsz, 1, D_A), prm['lnx_g'][0].reshape(1, -1),
                           prm['lnx_b'][0].reshape(1, -1), st_rwkv[0], n_seq=bsz, seq_len=CHUNK,
                           passes=RWKV_PASSES)
    ya = ya.reshape(bsz, CHUNK, D_A)[:, 0]
    kv_prev = jnp.concatenate([jnp.zeros((bsz, WINDOW, D_B), F32), c_k[0].reshape(bsz, WINDOW, KV_COLS),
                               c_v[0].reshape(bsz, WINDOW, KV_COLS)], axis=-1)
    cur = jnp.zeros((bsz, WINDOW, SWA_COLS), F32).at[:, 0].set(swa)
    seqs = jnp.concatenate([kv_prev, cur], axis=1).reshape(bsz * 2 * WINDOW, SWA_COLS)
    yb = _swa(seqs, prm['rel_bias'], prm['attn_sinks'][0], n_seq=bsz, n_blk=2, blk_start=1)
    yb = yb.reshape(bsz, WINDOW, D_B)[:, 0]
    x1 = _outproj_ln(ya, yb, x0, prm['w_out'][0], prm['ln_mix_g'][0], prm['ln_mix_b'][0], tm=bsz)
    x2 = _moe_block(x1, prm, 0, tm=bsz)
    xcat = jnp.concatenate([st_pool[0], x2[:, None]], axis=1)
    x3 = _pool_ln(xcat.reshape(bsz * HALO, D_MODEL), prm['w_pool'][0], prm['pool_scale'][0],
                  prm['ln_mix_g'][1], prm['ln_mix_b'][1], tm=min(512, bsz * HALO), seq_len=HALO,
                  start_pos=PAST_LEN - POOL_KEEP)
    x3 = x3.reshape(bsz, HALO, D_MODEL)[:, -1]
    x4 = _moe_block(x3, prm, 1, tm=bsz)
    k_new = jnp.concatenate([c_k[0][:, 1:], swa[:, D_B:D_B + KV_COLS].reshape(bsz, 1, KV_HEADS, HEAD)], axis=1)
    v_new = jnp.concatenate([c_v[0][:, 1:], swa[:, D_B + KV_COLS:].reshape(bsz, 1, KV_HEADS, HEAD)], axis=1)
    return (x4.reshape(bsz, 1, D_MODEL), s_new[None], x[:, -1][None], k_new[None], v_new[None],
            xcat[:, 1:][None])


def kernel(x_prompt, x_sample, state_rwkv, state_shift, cache_swa_k, cache_swa_v, state_pool, w_in, tshift_mu,
           decay_w0, decay_w2, iclr_a0, iclr_a2, gate_w2, k_k, k_a, r_k, lnx_g, lnx_b, attn_sinks, rel_bias, w_out,
           w_pool, pool_scale, ln_mix_g, ln_mix_b, ln_ffn_g, ln_ffn_b, w_router, b_router, w_ex_gate, w_ex_up,
           w_ex_down):
    prm = dict(w_in=w_in, tshift_mu=tshift_mu, decay_w0=decay_w0, decay_w2=decay_w2, iclr_a0=iclr_a0,
               iclr_a2=iclr_a2, gate_w2=gate_w2, k_k=k_k, k_a=k_a, r_k=r_k.reshape(r_k.shape[0], -1),
               lnx_g=lnx_g, lnx_b=lnx_b, attn_sinks=attn_sinks, rel_bias=rel_bias, w_out=w_out, w_pool=w_pool,
               pool_scale=pool_scale, ln_mix_g=ln_mix_g, ln_mix_b=ln_mix_b, ln_ffn_g=ln_ffn_g, ln_ffn_b=ln_ffn_b,
               w_router=w_router, b_router=b_router,
               wg=w_ex_gate.astype(BF16), wu=w_ex_up.astype(BF16), wd=w_ex_down.astype(BF16))
    y_p, rwkv_p, shift_p, k_p, v_p, pool_p = _prompt_trunk(x_prompt, prm)
    y_s, rwkv_s, shift_s, k_s, v_s, pool_s = _sample_trunk(x_sample, state_rwkv, state_shift, cache_swa_k,
                                                            cache_swa_v, state_pool, prm)
    return (y_p, y_s, rwkv_p, rwkv_s, shift_p, shift_s, k_p, k_s, v_p, v_s, pool_p, pool_s)
```

```python
import functools
import math

import numpy as np
import jax
import jax.numpy as jnp
from jax import lax
from jax.experimental import pallas as pl
from jax.experimental.pallas import tpu as pltpu

F32 = jnp.float32
BF16 = jnp.bfloat16

D_MODEL = 1024
DEPTH = 2
PAST_LEN = 16384
D_A = 512
HEAD = 64
N_HEADS = 8
W_LORA, A_LORA, G_LORA = 32, 64, 96
RWKV_COLS = 3 * D_A + W_LORA + A_LORA + G_LORA
D_B = 512
KV_HEADS = 2
Q_PER_KV = 4
KV_COLS = KV_HEADS * HEAD
SWA_COLS = D_B + 2 * KV_COLS
WINDOW = 128
N_BUCKETS = 32
MAX_EXACT = 16
REL_MAX_DIST = 128
POOL_WINDOWS = (2, 4, 8, 16)
POOL_GROUP = 256
POOL_KEEP = 15
N_EXPERTS = 16
N_GROUPS = 4
EXPERTS_PER_GROUP = 4
D_FF = 256
ALPHA = (2.0 * DEPTH) ** 0.25
LN_EPS = 1e-5
LNX_EPS = 64e-5
NEG_INF = -1e30

LANES = 128
SUBLANES = 8
VMEM_LIMIT_BYTES = 56 * 1024 * 1024

LORA_PAD = LANES
RWKV_PAD_COLS = 3 * D_A + 3 * LORA_PAD
CHUNK = 64

NN = ((1,), (0,))
NT = ((1,), (1,))
TN = ((0,), (0,))


def _dg(a, b, dims=NN):
    return lax.dot_general(a, b, (dims, ((), ())), preferred_element_type=F32)


def _bdot(a, b, dims=NN):
    return _dg(a.astype(BF16), b.astype(BF16), dims)


def _split(x, n):
    parts, rem = [], x
    for i in range(n):
        p = rem.astype(BF16)
        parts.append(p)
        if i + 1 < n:
            rem = rem - p.astype(F32)
    return parts


def _mm(a, b, dims=NN, passes=3):
    if passes == 1:
        return _bdot(a, b, dims)
    ah, al = _split(a, 2)
    bh, bl = _split(b, 2)
    return _dg(ah, bh, dims) + (_dg(ah, bl, dims) + _dg(al, bh, dims))


def _dot_exact_rhs(x, ones_bf16, n=3):
    out = None
    for p in _split(x, n):
        t = _dg(p, ones_bf16)
        out = t if out is None else out + t
    return out


def _dot_exact_lhs(ones_bf16, x, n=3):
    out = None
    for p in _split(x, n):
        t = _dg(ones_bf16, p)
        out = t if out is None else out + t
    return out


def _sigmoid(x):
    return 1.0 / (1.0 + jnp.exp(-x))


def _layer_norm(z, g, b):
    mu = jnp.mean(z, axis=-1, keepdims=True)
    d = z - mu
    var = jnp.mean(d * d, axis=-1, keepdims=True)
    return d * lax.rsqrt(var + LN_EPS) * g + b


def _params(sem):
    return pltpu.CompilerParams(dimension_semantics=sem, vmem_limit_bytes=VMEM_LIMIT_BYTES)


def _full(shape):
    nd = len(shape)
    return pl.BlockSpec(shape, lambda *_: (0,) * nd)


def _inproj_prep_body(*refs, tm, tiles_per_seq, chunk, has_prev):
    if has_prev:
        x_ref, xp_ref = refs[0], refs[1]
        refs = refs[2:]
    else:
        x_ref, xp_ref = refs[0], None
        refs = refs[1:]
    (wr_ref, ws_ref, mu_ref, w0_ref, ww2_ref, a0_ref, wa2_ref, wg2_ref, kk_ref, ka_ref, rk_ref,
     ones_ref, tri_ref,
     rt_ref, kt_ref, bt_ref, at_ref, v_ref, bonus_ref, g_ref, gl_ref, swa_ref, carry_ref) = refs

    i = pl.program_id(0)
    xb = x_ref[...].astype(BF16)
    pr = _dg(xb, wr_ref[...])
    swa_ref[...] = _dg(xb, ws_ref[...])

    if has_prev:
        shifted = _dg(xp_ref[...].astype(BF16), wr_ref[...])
    else:
        @pl.when(i == 0)
        def _():
            carry_ref[...] = jnp.zeros_like(carry_ref)

        first = (i % tiles_per_seq) == 0
        prev_last = jnp.where(first, 0.0, carry_ref[0:1, :])
        rolled = pltpu.roll(pr, 1, axis=0)
        row = lax.broadcasted_iota(jnp.int32, (tm, 1), 0)
        shifted = jnp.where(row == 0, prev_last, rolled)
        carry_ref[0:1, :] = pr[tm - 1:tm, :]
    m = pr + mu_ref[...] * (shifted - pr)

    r = m[:, 0:D_A]
    k = m[:, D_A:2 * D_A]
    v = m[:, 2 * D_A:3 * D_A]
    wd = m[:, 3 * D_A:3 * D_A + LORA_PAD]
    ad = m[:, 3 * D_A + LORA_PAD:3 * D_A + 2 * LORA_PAD]
    gd = m[:, 3 * D_A + 2 * LORA_PAD:3 * D_A + 3 * LORA_PAD]

    nz = -(w0_ref[...] + _bdot(jnp.tanh(wd), ww2_ref[...]))
    softplus = jnp.maximum(nz, 0.0) + jnp.log1p(jnp.exp(-jnp.abs(nz)))
    w_log = -softplus - 0.5
    logw = -jnp.exp(w_log)
    a = _sigmoid(a0_ref[...] + _bdot(ad, wa2_ref[...]))
    g = _bdot(_sigmoid(gd), wg2_ref[...])

    ones_bd = ones_ref[...]
    kkr = k * kk_ref[...]
    nrm = jnp.sqrt(_dot_exact_rhs(kkr * kkr, ones_bd))
    kk = kkr / jnp.maximum(nrm, 1e-12)
    k2 = k * (1.0 + (a - 1.0) * ka_ref[...])
    bonus_ref[...] = _dot_exact_rhs(r * k2 * rk_ref[...], ones_bd) * v

    if chunk > 1:
        cum = _dot_exact_lhs(tri_ref[...], logw)
    else:
        cum = logw
    gam = jnp.exp(cum)
    inv = jnp.exp(-cum)
    rt_ref[...] = r * gam
    kt_ref[...] = k2 * inv
    bt_ref[...] = kk * a * inv
    at_ref[...] = -kk * jnp.exp(cum - logw)
    v_ref[...] = v
    g_ref[...] = g
    if chunk > 1:
        for c in range(tm // chunk):
            gl_ref[c] = gam[(c + 1) * chunk - 1:(c + 1) * chunk, :]
    else:
        gl_ref[...] = gam


def _inproj_prep(x2d, x_prev, wts, *, seq_len, chunk, tm):
    n = x2d.shape[0]
    assert n % tm == 0 and (x_prev is not None or seq_len % tm == 0)
    assert tm % chunk == 0
    has_prev = x_prev is not None
    row = lambda w: pl.BlockSpec((tm, w), lambda i: (i, 0))
    ins = [x2d] + ([x_prev] if has_prev else [])
    in_specs = [row(D_MODEL)] + ([row(D_MODEL)] if has_prev else [])
    consts = [wts['wr'], wts['ws'], wts['mu'], wts['w0'], wts['ww2'], wts['a0'], wts['wa2'], wts['wg2'],
              wts['k_k'], wts['k_a'], wts['r_k'], _ones_block_diag(D_A, HEAD), _tri_block_diag(tm, chunk)]
    ins += consts
    in_specs += [_full(c.shape) for c in consts]
    if chunk > 1:
        gl_shape = jax.ShapeDtypeStruct((n // chunk, 1, D_A), F32)
        gl_spec = pl.BlockSpec((tm // chunk, 1, D_A), lambda i: (i, 0, 0))
    else:
        gl_shape = jax.ShapeDtypeStruct((n, D_A), F32)
        gl_spec = row(D_A)
    out_shape = [jax.ShapeDtypeStruct((n, D_A), F32)] * 7 + [gl_shape, jax.ShapeDtypeStruct((n, SWA_COLS), F32)]
    out_specs = [row(D_A)] * 7 + [gl_spec, row(SWA_COLS)]
    body = functools.partial(_inproj_prep_body, tm=tm, tiles_per_seq=max(seq_len // tm, 1), chunk=chunk,
                             has_prev=has_prev)
    return pl.pallas_call(
        body, grid=(n // tm,), in_specs=in_specs, out_specs=out_specs, out_shape=out_shape,
        scratch_shapes=[pltpu.VMEM((SUBLANES, RWKV_PAD_COLS), F32)],
        compiler_params=_params(("arbitrary",)), name="inproj_prep",
    )(*ins)


def _ones_block_diag(n, blk):
    idx = np.arange(n) // blk
    return jnp.asarray(idx[:, None] == idx[None, :], dtype=BF16)


def _tri_block_diag(n, blk):
    idx = np.arange(n)
    same = (idx[:, None] // blk) == (idx[None, :] // blk)
    return jnp.asarray(same & (idx[:, None] >= idx[None, :]), dtype=BF16)


PAIR = 2 * HEAD
N_PAIRS = N_HEADS // 2


def _scan_body(rt_ref, kt_ref, bt_ref, at_ref, v_ref, bonus_ref, g_ref, gl_ref, lng_ref, lnb_ref, s0_ref,
               y_ref, sout_ref, S_ref, *, C, nc):
    step = pl.program_id(1)

    @pl.when(step == 0)
    def _():
        S_ref[...] = s0_ref[0]

    head_a = lax.broadcasted_iota(jnp.int32, (1, PAIR), 1) < HEAD
    rowc = lax.broadcasted_iota(jnp.int32, (C, 2 * C), 0)
    colc = lax.broadcasted_iota(jnp.int32, (C, 2 * C), 1)
    first_c = colc < C
    col_in = jnp.where(first_c, colc, colc - C)
    strict = rowc > col_in
    incl = rowc >= col_in
    eye_c = (rowc == col_in).astype(F32)
    r128 = lax.broadcasted_iota(jnp.int32, (PAIR, PAIR), 0)
    c128 = lax.broadcasted_iota(jnp.int32, (PAIR, PAIR), 1)
    same_head = (r128 < HEAD) == (c128 < HEAD)
    eye128 = (r128 == c128).astype(F32)

    def by_head(x):
        xb = x.astype(BF16)
        z = jnp.zeros_like(xb)
        return jnp.concatenate([jnp.where(head_a, xb, z), jnp.where(head_a, z, xb)], axis=0)

    def blocks(x):
        xb = x.astype(BF16)
        z = jnp.zeros_like(xb)
        return jnp.concatenate([jnp.where(first_c, xb, z), jnp.where(first_c, z, xb)], axis=0)

    def head_mean(z):
        za = jnp.sum(jnp.where(head_a, z, 0.0), axis=-1, keepdims=True)
        zb = jnp.sum(jnp.where(head_a, 0.0, z), axis=-1, keepdims=True)
        return jnp.where(head_a, za, zb) * (1.0 / HEAD)

    n_fold = int(math.log2(C)) - 1
    zero_blk = jnp.zeros((2 * C, PAIR), BF16)
    items = [(ci, p) for ci in range(nc) for p in range(N_PAIRS)]
    each = lambda f, *lists: [f(*xs) for xs in zip(*lists)]

    def load(ref):
        return [ref[ci * C:(ci + 1) * C, p * PAIR:(p + 1) * PAIR] for ci, p in items]

    R, K, Bm, A, V = load(rt_ref), load(kt_ref), load(bt_ref), load(at_ref), load(v_ref)
    big = each(lambda a, r, b, k: _dg(jnp.concatenate([a, r], axis=0).astype(BF16),
                                      jnp.concatenate([by_head(b), by_head(k)], axis=0), NT), A, R, Bm, K)
    a_ab = [jnp.where(strict, x[:C, :2 * C], 0.0) for x in big]
    a_ak = [jnp.where(strict, x[:C, 2 * C:], 0.0) for x in big]
    l_rb = [jnp.where(incl, x[C:, :2 * C], 0.0) for x in big]
    l_rk = [jnp.where(incl, x[C:, 2 * C:], 0.0) for x in big]
    X = [_dg(x.astype(BF16), blocks(x)) for x in a_ab]
    T = [eye_c + x for x in a_ab]
    for k in range(1, n_fold + 1):
        if k < n_fold:
            res = each(lambda x, t: _dg(jnp.concatenate([x, t], axis=0).astype(BF16), blocks(x)), X, T)
            X = [r_[:C] for r_ in res]
            T = each(lambda t, r_: t + r_[C:], T, res)
        else:
            T = each(lambda t, x: t + _dg(t.astype(BF16), blocks(x)), T, X)
    akv = each(lambda m_, v_: _dg(m_.astype(BF16), by_head(v_)), a_ak, V)
    pq = each(lambda t, a, q: _dg(t.astype(BF16), jnp.concatenate([by_head(a), by_head(q)], axis=1)), T, A, akv)
    P = [x[:, :PAIR] for x in pq]
    Q = [x[:, PAIR:] for x in pq]
    wz = each(lambda lb, lk, p_, q_, v_: _dg(
        jnp.concatenate([lb, lk], axis=1).astype(BF16),
        jnp.concatenate([jnp.concatenate([by_head(p_), by_head(q_)], axis=1),
                         jnp.concatenate([zero_blk, by_head(v_)], axis=1)], axis=0)), l_rb, l_rk, P, Q, V)
    W = each(lambda r, x: r + x[:, :PAIR], R, wz)
    Z = [x[:, PAIR:] for x in wz]
    ptb = each(lambda p_, b: _bdot(p_, b, TN), P, Bm)
    gfull = each(lambda q_, v_, b, k: _bdot(jnp.concatenate([q_, v_], axis=0), jnp.concatenate([b, k], axis=0), TN),
                 Q, V, Bm, K)

    state = [S_ref[p] for p in range(N_PAIRS)]
    for idx, (ci, p) in enumerate(items):
        rows = slice(ci * C, (ci + 1) * C)
        ln = slice(p * PAIR, (p + 1) * PAIR)
        g_c = gl_ref[ci, :, ln]
        M = (eye128 + jnp.where(same_head, ptb[idx], 0.0)) * g_c
        G = jnp.where(head_a, gfull[idx][:HEAD], gfull[idx][HEAD:]) * g_c
        S0 = state[p]
        Y = _dg(W[idx].astype(BF16), by_head(S0), NT) + Z[idx]
        state[p] = _bdot(S0, M) + G

        mu = head_mean(Y)
        d = Y - mu
        var = head_mean(d * d)
        yn = d * lax.rsqrt(var + LNX_EPS) * lng_ref[:, ln] + lnb_ref[:, ln]
        y_ref[rows, ln] = (yn + bonus_ref[rows, ln]) * g_ref[rows, ln]
    for p in range(N_PAIRS):
        S_ref[p] = state[p]

    @pl.when(step == pl.num_programs(1) - 1)
    def _():
        for p in range(N_PAIRS):
            sout_ref[0, p] = state[p]


def _to_pairs(s):
    n = s.shape[0]
    return s.reshape(n, N_PAIRS, 2, HEAD, HEAD).transpose(0, 1, 3, 2, 4).reshape(n, N_PAIRS, HEAD, PAIR)


def _from_pairs(s):
    n = s.shape[0]
    return s.reshape(n, N_PAIRS, HEAD, 2, HEAD).transpose(0, 1, 3, 2, 4).reshape(n, N_HEADS, HEAD, HEAD)


def _rwkv_scan(rt, kt, bt, at, v, bonus, g, gl, lnx_g, lnx_b, s0, *, n_seq, seq_len, chunk, nc):
    rows = chunk * nc
    assert seq_len % rows == 0
    nsteps = seq_len // rows
    row = pl.BlockSpec((rows, D_A), lambda b, c: (b * nsteps + c, 0))
    st = pl.BlockSpec((1, N_PAIRS, HEAD, PAIR), lambda b, c: (b, 0, 0, 0))
    vec = pl.BlockSpec((1, D_A), lambda b, c: (0, 0))
    y, s_new = pl.pallas_call(
        functools.partial(_scan_body, C=chunk, nc=nc),
        grid=(n_seq, nsteps),
        in_specs=[row] * 7 + [pl.BlockSpec((nc, 1, D_A), lambda b, c: (b * nsteps + c, 0, 0)), vec, vec, st],
        out_specs=[row, st],
        out_shape=[jax.ShapeDtypeStruct((n_seq * seq_len, D_A), F32),
                   jax.ShapeDtypeStruct((n_seq, N_PAIRS, HEAD, PAIR), F32)],
        scratch_shapes=[pltpu.VMEM((N_PAIRS, HEAD, PAIR), F32)],
        compiler_params=_params(("arbitrary", "arbitrary")), name="rwkv_scan",
    )(rt, kt, bt, at, v, bonus, g, gl, lnx_g, lnx_b, _to_pairs(s0))
    return y, _from_pairs(s_new)


def _bucket_ranges():
    d = np.arange(WINDOW + 1)
    scaled = np.log(np.maximum(d, MAX_EXACT).astype(np.float32) / MAX_EXACT) / math.log(REL_MAX_DIST / MAX_EXACT)
    large = np.minimum(MAX_EXACT + (scaled * (N_BUCKETS - MAX_EXACT)).astype(np.int32), N_BUCKETS - 1)
    bucket = np.where(d < MAX_EXACT, d, large)
    frac = scaled.astype(np.float64) * (N_BUCKETS - MAX_EXACT)
    near = np.abs(frac - np.round(frac)) < 1e-3
    assert all(int(x) in (MAX_EXACT, REL_MAX_DIST) for x in d[(d >= MAX_EXACT) & near])
    ranges = []
    for b in range(N_BUCKETS):
        hit = d[bucket == b]
        ranges.append((int(hit.min()), int(hit.max())) if hit.size else None)
    return tuple(ranges)


def _swa_body(rb_ref, sink_ref, cur_ref, prev_ref, y_ref, bias_ref, *, qr, fresh, ranges):
    W = WINDOW
    nk = W + qr

    @pl.when((pl.program_id(0) == 0) & (pl.program_id(1) == 0))
    def _():
        qi = lax.broadcasted_iota(jnp.int32, (qr, nk), 0)
        kj = lax.broadcasted_iota(jnp.int32, (qr, nk), 1)
        d = qi + W - kj
        valid = (d >= 0) & (d <= W)
        for hd in range(N_HEADS):
            t = jnp.zeros((qr, nk), F32)
            for bkt, rng in enumerate(ranges):
                if rng is not None:
                    t = jnp.where((d >= rng[0]) & (d <= rng[1]), rb_ref[bkt * N_HEADS + hd], t)
            grp = hd % Q_PER_KV
            bias_ref[hd // Q_PER_KV, grp * qr:(grp + 1) * qr, :] = jnp.where(valid, t, NEG_INF)

    if fresh:
        colk = lax.broadcasted_iota(jnp.int32, (Q_PER_KV * qr, nk), 1)
        kill = colk < jnp.where(pl.program_id(1) == 0, W, 0)
    outs = [None] * N_HEADS
    for kvh in range(KV_HEADS):
        ko, vo = D_B + kvh * HEAD, D_B + KV_COLS + kvh * HEAD
        kb = jnp.concatenate([prev_ref[:, kvh * HEAD:(kvh + 1) * HEAD], cur_ref[:, ko:ko + HEAD]], axis=0)
        vb = jnp.concatenate([prev_ref[:, KV_COLS + kvh * HEAD:KV_COLS + (kvh + 1) * HEAD],
                              cur_ref[:, vo:vo + HEAD]], axis=0)
        q4 = jnp.concatenate([cur_ref[:, (kvh * Q_PER_KV + gq) * HEAD:(kvh * Q_PER_KV + gq + 1) * HEAD]
                              for gq in range(Q_PER_KV)], axis=0)
        s = _bdot(q4, kb, NT) * (HEAD ** -0.5) + bias_ref[kvh]
        if fresh:
            s = jnp.where(kill, NEG_INF, s)
        sink = jnp.concatenate([jnp.full((qr, 1), sink_ref[kvh * Q_PER_KV + gq], F32) for gq in range(Q_PER_KV)],
                               axis=0)
        m = jnp.maximum(jnp.max(s, axis=-1, keepdims=True), sink)
        p = jnp.exp(s - m)
        den = jnp.sum(p, axis=-1, keepdims=True) + jnp.exp(sink - m)
        o = _bdot(p, vb) / den
        for gq in range(Q_PER_KV):
            outs[kvh * Q_PER_KV + gq] = o[gq * qr:(gq + 1) * qr]
    y_ref[...] = jnp.concatenate(outs, axis=1)


def _swa(cur2d, prev2d, prev_map, rel_bias, sinks, *, n_seq, n_blk, qr, fresh):
    smem = pl.BlockSpec(memory_space=pltpu.SMEM)
    return pl.pallas_call(
        functools.partial(_swa_body, qr=qr, fresh=fresh, ranges=_bucket_ranges()),
        grid=(n_seq, n_blk),
        in_specs=[smem, smem,
                  pl.BlockSpec((qr, SWA_COLS), lambda b, j: (b * n_blk + j, 0)),
                  pl.BlockSpec((WINDOW, 2 * KV_COLS), prev_map)],
        out_specs=pl.BlockSpec((qr, D_B), lambda b, j: (b * n_blk + j, 0)),
        out_shape=jax.ShapeDtypeStruct((n_seq * n_blk * qr, D_B), F32),
        scratch_shapes=[pltpu.VMEM((KV_HEADS, Q_PER_KV * qr, WINDOW + qr), F32)],
        compiler_params=_params(("arbitrary", "arbitrary")), name="swa",
    )(rel_bias.reshape(-1), sinks, cur2d, prev2d)


def _outproj_ln_body(ya_ref, yb_ref, x_ref, wa_ref, wb_ref, g_ref, b_ref, o_ref):
    h = _dg(ya_ref[...].astype(BF16), wa_ref[...]) + _dg(yb_ref[...].astype(BF16), wb_ref[...])
    o_ref[...] = _layer_norm(ALPHA * x_ref[...] + h, g_ref[...], b_ref[...])


def _outproj_ln(ya, yb, x2d, w_out, ln_g, ln_b, *, tm):
    n = x2d.shape[0]
    half = pl.BlockSpec((tm, D_A), lambda i: (i, 0))
    full = pl.BlockSpec((tm, D_MODEL), lambda i: (i, 0))
    wa, wb = w_out[:D_A].astype(BF16), w_out[D_A:].astype(BF16)
    return pl.pallas_call(
        _outproj_ln_body, grid=(n // tm,),
        in_specs=[half, half, full, _full(wa.shape), _full(wb.shape), _full((1, D_MODEL)), _full((1, D_MODEL))],
        out_specs=full, out_shape=jax.ShapeDtypeStruct((n, D_MODEL), F32),
        compiler_params=_params(("arbitrary",)), name="outproj_ln",
    )(ya, yb, x2d, wa, wb, ln_g.reshape(1, -1), ln_b.reshape(1, -1))


def _router_body(x_ref, wrt_ref, br_ref, eye_ref, gates_ref):
    logits = _dg(wrt_ref[...], x_ref[...].astype(BF16), NT) + br_ref[...]
    z = jnp.exp(logits - jnp.max(logits, axis=0, keepdims=True))
    probs = z / jnp.sum(z, axis=0, keepdims=True)
    p = [probs[e:e + 1, :] for e in range(N_EXPERTS)]

    def top2_mask(vals):
        out = []
        for a_ in range(len(vals)):
            rank = None
            for j in range(len(vals)):
                if j == a_:
                    continue
                ahead = (vals[j] >= vals[a_]) if j < a_ else (vals[j] > vals[a_])
                ahead = ahead.astype(jnp.int32)
                rank = ahead if rank is None else rank + ahead
            out.append(rank < 2)
        return out

    sel, score = [], []
    for gi in range(N_GROUPS):
        grp = p[gi * EXPERTS_PER_GROUP:(gi + 1) * EXPERTS_PER_GROUP]
        msk = top2_mask(grp)
        sel += msk
        sc = None
        for a_ in range(EXPERTS_PER_GROUP):
            t = jnp.where(msk[a_], grp[a_], 0.0)
            sc = t if sc is None else sc + t
        score.append(sc)
    best = []
    for gi in range(N_GROUPS):
        ok = None
        for j in range(N_GROUPS):
            if j == gi:
                continue
            c_ = (score[gi] > score[j]) if j < gi else (score[gi] >= score[j])
            ok = c_ if ok is None else (ok & c_)
        best.append(ok)
    active = [sel[e] & best[e // EXPERTS_PER_GROUP] for e in range(N_EXPERTS)]
    tot = None
    for e in range(N_EXPERTS):
        t = jnp.where(active[e], p[e], 0.0)
        tot = t if tot is None else tot + t
    gates_t = jnp.concatenate([jnp.where(active[e], p[e] / tot, 0.0) for e in range(N_EXPERTS)], axis=0)
    out = None
    for part in _split(gates_t, 3):
        t = _dg(eye_ref[...], part, NT)
        out = t if out is None else out + t
    gates_ref[...] = out


def _router(x2d, w_router, b_router, *, tm):
    n = x2d.shape[0]
    eye = jnp.asarray(np.eye(tm), dtype=BF16)
    return pl.pallas_call(
        _router_body, grid=(n // tm,),
        in_specs=[pl.BlockSpec((tm, D_MODEL), lambda i: (i, 0)), _full((N_EXPERTS, D_MODEL)),
                  _full((N_EXPERTS, 1)), _full((tm, tm))],
        out_specs=pl.BlockSpec((tm, N_EXPERTS), lambda i: (i, 0)),
        out_shape=jax.ShapeDtypeStruct((n, N_EXPERTS), F32),
        compiler_params=_params(("arbitrary",)), name="router",
    )(x2d, w_router.T.astype(BF16), b_router.reshape(-1, 1), eye)


def _moe_ln_body(x_ref, gates_ref, wg_ref, wu_ref, wd_ref, g_ref, b_ref, o_ref, acc_ref, xb_ref):
    e = pl.program_id(1)

    @pl.when(e == 0)
    def _():
        xb_ref[...] = x_ref[...].astype(BF16)
        acc_ref[...] = jnp.zeros_like(acc_ref)

    xb = xb_ref[...]
    hg = _dg(xb, wg_ref[0])
    hu = _dg(xb, wu_ref[0])
    h = hg * _sigmoid(hg) * hu
    ye = _dg(h.astype(BF16), wd_ref[0])
    lane = lax.broadcasted_iota(jnp.int32, gates_ref.shape, 1)
    gate = jnp.sum(jnp.where(lane == e, gates_ref[...], 0.0), axis=1, keepdims=True)
    acc_ref[...] += gate * ye

    @pl.when(e == N_EXPERTS - 1)
    def _():
        o_ref[...] = _layer_norm(ALPHA * x_ref[...] + acc_ref[...], g_ref[...], b_ref[...])


def _moe_ln(x2d, gates, wg, wu, wd, ln_g, ln_b, *, tm):
    n = x2d.shape[0]
    tok = pl.BlockSpec((tm, D_MODEL), lambda i, e: (i, 0))
    return pl.pallas_call(
        _moe_ln_body, grid=(n // tm, N_EXPERTS),
        in_specs=[tok, pl.BlockSpec((tm, N_EXPERTS), lambda i, e: (i, 0)),
                  pl.BlockSpec((1, D_MODEL, D_FF), lambda i, e: (e, 0, 0)),
                  pl.BlockSpec((1, D_MODEL, D_FF), lambda i, e: (e, 0, 0)),
                  pl.BlockSpec((1, D_FF, D_MODEL), lambda i, e: (e, 0, 0)),
                  pl.BlockSpec((1, D_MODEL), lambda i, e: (0, 0)), pl.BlockSpec((1, D_MODEL), lambda i, e: (0, 0))],
        out_specs=tok, out_shape=jax.ShapeDtypeStruct((n, D_MODEL), F32),
        scratch_shapes=[pltpu.VMEM((tm, D_MODEL), F32), pltpu.VMEM((tm, D_MODEL), BF16)],
        compiler_params=_params(("arbitrary", "arbitrary")), name="moe_ln",
    )(x2d, gates, wg, wu, wd, ln_g.reshape(1, -1), ln_b.reshape(1, -1))


HALO = 16


def _pool_ln_body(x_ref, halo_ref, wp_ref, ps_ref, g_ref, b_ref, o_ref, *, tm, seq_len, start_pos):
    i = pl.program_id(0)
    x = x_ref[...]
    idx = lax.broadcasted_iota(jnp.int32, (tm + HALO, 1), 0)
    pos_e = (i * tm - HALO + idx) & (seq_len - 1)
    cnt_pos = (pos_e[HALO:] + (start_pos + 1)).astype(F32)
    mixed = []
    for gi, w in enumerate(POOL_WINDOWS):
        cols = slice(gi * POOL_GROUP, (gi + 1) * POOL_GROUP)
        xg = x[:, cols]
        s = jnp.concatenate([halo_ref[:, cols], xg], axis=0)
        step = 1
        while step < w:
            s = s + jnp.where(pos_e >= step, pltpu.roll(s, step, axis=0), 0.0)
            step *= 2
        pooled = s[HALO:] / jnp.minimum(float(w), cnt_pos) - xg
        mixed.append(_bdot(pooled, wp_ref[gi]))
    h = jnp.concatenate(mixed, axis=1) * ps_ref[...]
    o_ref[...] = _layer_norm(ALPHA * x + h, g_ref[...], b_ref[...])


def _pool_ln(x2d, w_pool, pool_scale, ln_g, ln_b, *, tm, seq_len, start_pos):
    n = x2d.shape[0]
    assert seq_len & (seq_len - 1) == 0 and seq_len >= HALO and tm % HALO == 0
    tok = pl.BlockSpec((tm, D_MODEL), lambda i: (i, 0))
    return pl.pallas_call(
        functools.partial(_pool_ln_body, tm=tm, seq_len=seq_len, start_pos=start_pos), grid=(n // tm,),
        in_specs=[tok, pl.BlockSpec((HALO, D_MODEL), lambda i: (jnp.maximum(i * (tm // HALO) - 1, 0), 0)),
                  _full(w_pool.shape), _full((1, D_MODEL)), _full((1, D_MODEL)), _full((1, D_MODEL))],
        out_specs=tok, out_shape=jax.ShapeDtypeStruct((n, D_MODEL), F32),
        compiler_params=_params(("arbitrary",)), name="pool_ln",
    )(x2d, x2d, w_pool.astype(BF16), pool_scale.reshape(1, -1), ln_g.reshape(1, -1), ln_b.reshape(1, -1))


def _pad_lora_cols(w, width):
    return jnp.pad(w, ((0, 0), (0, LORA_PAD - width)))


def _even_layer_weights(prm, i):
    w_in = prm['w_in'][i]
    o = 3 * D_A
    cols = lambda w: jnp.concatenate(
        [w[:, :o], _pad_lora_cols(w[:, o:o + W_LORA], W_LORA),
         _pad_lora_cols(w[:, o + W_LORA:o + W_LORA + A_LORA], A_LORA),
         _pad_lora_cols(w[:, o + W_LORA + A_LORA:RWKV_COLS], G_LORA)], axis=1)
    pad_rows = lambda w: jnp.pad(w, ((0, LORA_PAD - w.shape[0]), (0, 0))).astype(BF16)
    vec = lambda a: a.reshape(1, -1)
    return dict(
        wr=cols(w_in[:, :RWKV_COLS]).astype(BF16), ws=w_in[:, RWKV_COLS:].astype(BF16),
        mu=cols(prm['tshift_mu'][i].reshape(1, -1)),
        w0=vec(prm['decay_w0'][i]), ww2=pad_rows(prm['decay_w2'][i]),
        a0=vec(prm['iclr_a0'][i]), wa2=pad_rows(prm['iclr_a2'][i]), wg2=pad_rows(prm['gate_w2'][i]),
        k_k=vec(prm['k_k'][i]), k_a=vec(prm['k_a'][i]), r_k=vec(prm['r_k'][i]),
    )


def _moe_block(x2d, prm, layer, *, tm):
    gates = _router(x2d, prm['w_router'], prm['b_router'], tm=min(tm, 512))
    return _moe_ln(x2d, gates, prm['wg'][layer], prm['wu'][layer], prm['wd'][layer],
                   prm['ln_ffn_g'][layer], prm['ln_ffn_b'][layer], tm=tm)


PROMPT_CHUNKS_PER_STEP = 4
SAMPLE_CHUNK = 16
SAMPLE_QROWS = SUBLANES


def _prompt_trunk(x, prm):
    bsz, t, _ = x.shape
    n = bsz * t
    x0 = x.reshape(n, D_MODEL)
    wts = _even_layer_weights(prm, 0)
    rt, kt, bt, at, v, bonus, g, gl, swa = _inproj_prep(x0, None, wts, seq_len=t, chunk=CHUNK, tm=256)
    s0 = jnp.zeros((bsz, N_HEADS, HEAD, HEAD), F32)
    ya, s_new = _rwkv_scan(rt, kt, bt, at, v, bonus, g, gl, prm['lnx_g'][0].reshape(1, -1),
                           prm['lnx_b'][0].reshape(1, -1), s0, n_seq=bsz, seq_len=t, chunk=CHUNK,
                           nc=PROMPT_CHUNKS_PER_STEP)
    n_blk = t // WINDOW
    yb = _swa(swa, swa, lambda b, j: (b * n_blk + jnp.maximum(j - 1, 0), 2), prm['rel_bias'],
              prm['attn_sinks'][0], n_seq=bsz, n_blk=n_blk, qr=WINDOW, fresh=True)
    x1 = _outproj_ln(ya, yb, x0, prm['w_out'][0], prm['ln_mix_g'][0], prm['ln_mix_b'][0], tm=min(512, n))
    x2 = _moe_block(x1, prm, 0, tm=min(1024, n))
    x3 = _pool_ln(x2, prm['w_pool'][0], prm['pool_scale'][0], prm['ln_mix_g'][1], prm['ln_mix_b'][1],
                  tm=min(512, n), seq_len=t, start_pos=0)
    x4 = _moe_block(x3, prm, 1, tm=min(1024, n))
    swa3 = swa.reshape(bsz, t, SWA_COLS)
    k_new = swa3[:, t - WINDOW:, D_B:D_B + KV_COLS].reshape(bsz, WINDOW, KV_HEADS, HEAD)
    v_new = swa3[:, t - WINDOW:, D_B + KV_COLS:].reshape(bsz, WINDOW, KV_HEADS, HEAD)
    pool_new = x2.reshape(bsz, t, D_MODEL)[:, t - POOL_KEEP:]
    return (x4.reshape(bsz, t, D_MODEL), s_new[None], x[:, -1][None], k_new[None], v_new[None], pool_new[None])


def _sample_trunk(x, st_rwkv, st_shift, c_k, c_v, st_pool, prm):
    bsz = x.shape[0]
    x0 = x.reshape(bsz, D_MODEL)
    wts = _even_layer_weights(prm, 0)
    outs = _inproj_prep(x0, st_shift[0], wts, seq_len=1, chunk=1, tm=bsz)
    gl, swa = outs[7], outs[8]
    sc = SAMPLE_CHUNK
    padded = [jnp.zeros((bsz, sc, D_A), F32).at[:, 0].set(a).reshape(bsz * sc, D_A) for a in outs[:7]]
    ya, s_new = _rwkv_scan(*padded, gl.reshape(bsz, 1, D_A), prm['lnx_g'][0].reshape(1, -1),
                           prm['lnx_b'][0].reshape(1, -1), st_rwkv[0], n_seq=bsz, seq_len=sc, chunk=sc, nc=1)
    ya = ya.reshape(bsz, sc, D_A)[:, 0]
    qr = SAMPLE_QROWS
    kv_prev = jnp.concatenate([c_k[0].reshape(bsz * WINDOW, KV_COLS), c_v[0].reshape(bsz * WINDOW, KV_COLS)], axis=-1)
    cur = jnp.zeros((bsz, qr, SWA_COLS), F32).at[:, 0].set(swa).reshape(bsz * qr, SWA_COLS)
    yb = _swa(cur, kv_prev, lambda b, j: (b, 0), prm['rel_bias'], prm['attn_sinks'][0], n_seq=bsz, n_blk=1,
              qr=qr, fresh=False)
    yb = yb.reshape(bsz, qr, D_B)[:, 0]
    x1 = _outproj_ln(ya, yb, x0, prm['w_out'][0], prm['ln_mix_g'][0], prm['ln_mix_b'][0], tm=bsz)
    x2 = _moe_block(x1, prm, 0, tm=bsz)
    xcat = jnp.concatenate([st_pool[0], x2[:, None]], axis=1)
    x3 = _pool_ln(xcat.reshape(bsz * HALO, D_MODEL), prm['w_pool'][0], prm['pool_scale'][0],
                  prm['ln_mix_g'][1], prm['ln_mix_b'][1], tm=min(512, bsz * HALO), seq_len=HALO,
                  start_pos=PAST_LEN - POOL_KEEP)
    x3 = x3.reshape(bsz, HALO, D_MODEL)[:, -1]
    x4 = _moe_block(x3, prm, 1, tm=bsz)
    k_new = jnp.concatenate([c_k[0][:, 1:], swa[:, D_B:D_B + KV_COLS].reshape(bsz, 1, KV_HEADS, HEAD)], axis=1)
    v_new = jnp.concatenate([c_v[0][:, 1:], swa[:, D_B + KV_COLS:].reshape(bsz, 1, KV_HEADS, HEAD)], axis=1)
    return (x4.reshape(bsz, 1, D_MODEL), s_new[None], x[:, -1][None], k_new[None], v_new[None],
            xcat[:, 1:][None])


def kernel(x_prompt, x_sample, state_rwkv, state_shift, cache_swa_k, cache_swa_v, state_pool, w_in, tshift_mu,
           decay_w0, decay_w2, iclr_a0, iclr_a2, gate_w2, k_k, k_a, r_k, lnx_g, lnx_b, attn_sinks, rel_bias, w_out,
           w_pool, pool_scale, ln_mix_g, ln_mix_b, ln_ffn_g, ln_ffn_b, w_router, b_router, w_ex_gate, w_ex_up,
           w_ex_down):
    prm = dict(w_in=w_in, tshift_mu=tshift_mu, decay_w0=decay_w0, decay_w2=decay_w2, iclr_a0=iclr_a0,
               iclr_a2=iclr_a2, gate_w2=gate_w2, k_k=k_k, k_a=k_a, r_k=r_k.reshape(r_k.shape[0], -1),
               lnx_g=lnx_g, lnx_b=lnx_b, attn_sinks=attn_sinks, rel_bias=rel_bias, w_out=w_out, w_pool=w_pool,
               pool_scale=pool_scale, ln_mix_g=ln_mix_g, ln_mix_b=ln_mix_b, ln_ffn_g=ln_ffn_g, ln_ffn_b=ln_ffn_b,
               w_router=w_router, b_router=b_router,
               wg=w_ex_gate.astype(BF16), wu=w_ex_up.astype(BF16), wd=w_ex_down.astype(BF16))
    y_p, rwkv_p, shift_p, k_p, v_p, pool_p = _prompt_trunk(x_prompt, prm)
    y_s, rwkv_s, shift_s, k_s, v_s, pool_s = _sample_trunk(x_sample, state_rwkv, state_shift, cache_swa_k,
                                                            cache_swa_v, state_pool, prm)
    return (y_p, y_s, rwkv_p, rwkv_s, shift_p, shift_s, k_p, k_s, v_p, v_s, pool_p, pool_s)
```

```python
import functools
import math

import numpy as np
import jax
import jax.numpy as jnp
from jax import lax
from jax.experimental import pallas as pl
from jax.experimental.pallas import tpu as pltpu

F32 = jnp.float32
BF16 = jnp.bfloat16

D_MODEL = 1024
DEPTH = 2
PAST_LEN = 16384
D_A = 512
HEAD = 64
N_HEADS = 8
W_LORA, A_LORA, G_LORA = 32, 64, 96
RWKV_COLS = 3 * D_A + W_LORA + A_LORA + G_LORA
D_B = 512
KV_HEADS = 2
Q_PER_KV = 4
KV_COLS = KV_HEADS * HEAD
SWA_COLS = D_B + 2 * KV_COLS
WINDOW = 128
N_BUCKETS = 32
MAX_EXACT = 16
REL_MAX_DIST = 128
POOL_WINDOWS = (2, 4, 8, 16)
POOL_GROUP = 256
POOL_KEEP = 15
N_EXPERTS = 16
N_GROUPS = 4
EXPERTS_PER_GROUP = 4
D_FF = 256
ALPHA = (2.0 * DEPTH) ** 0.25
LN_EPS = 1e-5
LNX_EPS = 64e-5
NEG_INF = -1e30

LANES = 128
SUBLANES = 8
VMEM_LIMIT_BYTES = 56 * 1024 * 1024

LORA_PAD = LANES
RWKV_PAD_COLS = 3 * D_A + 3 * LORA_PAD
CHUNK = 64

NN = ((1,), (0,))
NT = ((1,), (1,))
TN = ((0,), (0,))


def _dg(a, b, dims=NN):
    return lax.dot_general(a, b, (dims, ((), ())), preferred_element_type=F32)


def _bdot(a, b, dims=NN):
    return _dg(a.astype(BF16), b.astype(BF16), dims)


def _split(x, n):
    parts, rem = [], x
    for i in range(n):
        p = rem.astype(BF16)
        parts.append(p)
        if i + 1 < n:
            rem = rem - p.astype(F32)
    return parts


def _mm(a, b, dims=NN, passes=3):
    if passes == 1:
        return _bdot(a, b, dims)
    ah, al = _split(a, 2)
    bh, bl = _split(b, 2)
    return _dg(ah, bh, dims) + (_dg(ah, bl, dims) + _dg(al, bh, dims))


def _dot_exact_rhs(x, ones_bf16, n=3):
    out = None
    for p in _split(x, n):
        t = _dg(p, ones_bf16)
        out = t if out is None else out + t
    return out


def _dot_exact_lhs(ones_bf16, x, n=3):
    out = None
    for p in _split(x, n):
        t = _dg(ones_bf16, p)
        out = t if out is None else out + t
    return out


def _sigmoid(x):
    return 1.0 / (1.0 + jnp.exp(-x))


def _layer_norm(z, g, b):
    mu = jnp.mean(z, axis=-1, keepdims=True)
    d = z - mu
    var = jnp.mean(d * d, axis=-1, keepdims=True)
    return d * lax.rsqrt(var + LN_EPS) * g + b


def _params(sem):
    return pltpu.CompilerParams(dimension_semantics=sem, vmem_limit_bytes=VMEM_LIMIT_BYTES)


def _full(shape):
    nd = len(shape)
    return pl.BlockSpec(shape, lambda *_: (0,) * nd)


def _inproj_prep_body(*refs, tm, tiles_per_seq, chunk, has_prev):
    if has_prev:
        x_ref, xp_ref = refs[0], refs[1]
        refs = refs[2:]
    else:
        x_ref, xp_ref = refs[0], None
        refs = refs[1:]
    (wr_ref, ws_ref, mu_ref, w0_ref, ww2_ref, a0_ref, wa2_ref, wg2_ref, kk_ref, ka_ref, rk_ref,
     ones_ref, tri_ref,
     rt_ref, kt_ref, bt_ref, at_ref, v_ref, bonus_ref, g_ref, gl_ref, swa_ref, carry_ref) = refs

    i = pl.program_id(0)
    xb = x_ref[...].astype(BF16)
    pr = _dg(xb, wr_ref[...])
    swa_ref[...] = _dg(xb, ws_ref[...])

    if has_prev:
        shifted = _dg(xp_ref[...].astype(BF16), wr_ref[...])
    else:
        @pl.when(i == 0)
        def _():
            carry_ref[...] = jnp.zeros_like(carry_ref)

        first = (i % tiles_per_seq) == 0
        prev_last = jnp.where(first, 0.0, carry_ref[0:1, :])
        rolled = pltpu.roll(pr, 1, axis=0)
        row = lax.broadcasted_iota(jnp.int32, (tm, 1), 0)
        shifted = jnp.where(row == 0, prev_last, rolled)
        carry_ref[0:1, :] = pr[tm - 1:tm, :]
    m = pr + mu_ref[...] * (shifted - pr)

    r = m[:, 0:D_A]
    k = m[:, D_A:2 * D_A]
    v = m[:, 2 * D_A:3 * D_A]
    wd = m[:, 3 * D_A:3 * D_A + LORA_PAD]
    ad = m[:, 3 * D_A + LORA_PAD:3 * D_A + 2 * LORA_PAD]
    gd = m[:, 3 * D_A + 2 * LORA_PAD:3 * D_A + 3 * LORA_PAD]

    nz = -(w0_ref[...] + _bdot(jnp.tanh(wd), ww2_ref[...]))
    softplus = jnp.maximum(nz, 0.0) + jnp.log1p(jnp.exp(-jnp.abs(nz)))
    w_log = -softplus - 0.5
    logw = -jnp.exp(w_log)
    a = _sigmoid(a0_ref[...] + _bdot(ad, wa2_ref[...]))
    g = _bdot(_sigmoid(gd), wg2_ref[...])

    ones_bd = ones_ref[...]
    kkr = k * kk_ref[...]
    nrm = jnp.sqrt(_dot_exact_rhs(kkr * kkr, ones_bd))
    kk = kkr / jnp.maximum(nrm, 1e-12)
    k2 = k * (1.0 + (a - 1.0) * ka_ref[...])
    bonus_ref[...] = _dot_exact_rhs(r * k2 * rk_ref[...], ones_bd) * v

    if chunk > 1:
        cum = _dot_exact_lhs(tri_ref[...], logw)
    else:
        cum = logw
    gam = jnp.exp(cum)
    inv = jnp.exp(-cum)
    rt_ref[...] = r * gam
    kt_ref[...] = k2 * inv
    bt_ref[...] = kk * a * inv
    at_ref[...] = -kk * jnp.exp(cum - logw)
    v_ref[...] = v
    g_ref[...] = g
    if chunk > 1:
        for c in range(tm // chunk):
            gl_ref[c] = gam[(c + 1) * chunk - 1:(c + 1) * chunk, :]
    else:
        gl_ref[...] = gam


def _inproj_prep(x2d, x_prev, wts, *, seq_len, chunk, tm):
    n = x2d.shape[0]
    assert n % tm == 0 and (x_prev is not None or seq_len % tm == 0)
    assert tm % chunk == 0
    has_prev = x_prev is not None
    row = lambda w: pl.BlockSpec((tm, w), lambda i: (i, 0))
    ins = [x2d] + ([x_prev] if has_prev else [])
    in_specs = [row(D_MODEL)] + ([row(D_MODEL)] if has_prev else [])
    consts = [wts['wr'], wts['ws'], wts['mu'], wts['w0'], wts['ww2'], wts['a0'], wts['wa2'], wts['wg2'],
              wts['k_k'], wts['k_a'], wts['r_k'], _ones_block_diag(D_A, HEAD), _tri_block_diag(tm, chunk)]
    ins += consts
    in_specs += [_full(c.shape) for c in consts]
    if chunk > 1:
        gl_shape = jax.ShapeDtypeStruct((n // chunk, 1, D_A), F32)
        gl_spec = pl.BlockSpec((tm // chunk, 1, D_A), lambda i: (i, 0, 0))
    else:
        gl_shape = jax.ShapeDtypeStruct((n, D_A), F32)
        gl_spec = row(D_A)
    out_shape = [jax.ShapeDtypeStruct((n, D_A), F32)] * 7 + [gl_shape, jax.ShapeDtypeStruct((n, SWA_COLS), F32)]
    out_specs = [row(D_A)] * 7 + [gl_spec, row(SWA_COLS)]
    body = functools.partial(_inproj_prep_body, tm=tm, tiles_per_seq=max(seq_len // tm, 1), chunk=chunk,
                             has_prev=has_prev)
    return pl.pallas_call(
        body, grid=(n // tm,), in_specs=in_specs, out_specs=out_specs, out_shape=out_shape,
        scratch_shapes=[pltpu.VMEM((SUBLANES, RWKV_PAD_COLS), F32)],
        compiler_params=_params(("arbitrary",)), name="inproj_prep",
    )(*ins)


def _ones_block_diag(n, blk):
    idx = np.arange(n) // blk
    return jnp.asarray(idx[:, None] == idx[None, :], dtype=BF16)


def _tri_block_diag(n, blk):
    idx = np.arange(n)
    same = (idx[:, None] // blk) == (idx[None, :] // blk)
    return jnp.asarray(same & (idx[:, None] >= idx[None, :]), dtype=BF16)


PAIR = 2 * HEAD
N_PAIRS = N_HEADS // 2


def _scan_body(rt_ref, kt_ref, bt_ref, at_ref, v_ref, bonus_ref, g_ref, gl_ref, lng_ref, lnb_ref, s0_ref,
               y_ref, sout_ref, S_ref, *, C, nc):
    step = pl.program_id(1)

    @pl.when(step == 0)
    def _():
        S_ref[...] = s0_ref[0]

    head_a = lax.broadcasted_iota(jnp.int32, (1, PAIR), 1) < HEAD
    rowc = lax.broadcasted_iota(jnp.int32, (C, 2 * C), 0)
    colc = lax.broadcasted_iota(jnp.int32, (C, 2 * C), 1)
    first_c = colc < C
    col_in = jnp.where(first_c, colc, colc - C)
    strict = rowc > col_in
    incl = rowc >= col_in
    eye_c = (rowc == col_in).astype(F32)
    r128 = lax.broadcasted_iota(jnp.int32, (PAIR, PAIR), 0)
    c128 = lax.broadcasted_iota(jnp.int32, (PAIR, PAIR), 1)
    same_head = (r128 < HEAD) == (c128 < HEAD)
    eye128 = (r128 == c128).astype(F32)

    def by_head(x):
        xb = x.astype(BF16)
        z = jnp.zeros_like(xb)
        return jnp.concatenate([jnp.where(head_a, xb, z), jnp.where(head_a, z, xb)], axis=0)

    def blocks(x):
        xb = x.astype(BF16)
        z = jnp.zeros_like(xb)
        return jnp.concatenate([jnp.where(first_c, xb, z), jnp.where(first_c, z, xb)], axis=0)

    def head_mean(z):
        za = jnp.sum(jnp.where(head_a, z, 0.0), axis=-1, keepdims=True)
        zb = jnp.sum(jnp.where(head_a, 0.0, z), axis=-1, keepdims=True)
        return jnp.where(head_a, za, zb) * (1.0 / HEAD)

    n_fold = int(math.log2(C)) - 1
    zero_blk = jnp.zeros((2 * C, PAIR), BF16)
    items = [(ci, p) for ci in range(nc) for p in range(N_PAIRS)]
    each = lambda f, *lists: [f(*xs) for xs in zip(*lists)]

    def load(ref):
        return [ref[ci * C:(ci + 1) * C, p * PAIR:(p + 1) * PAIR] for ci, p in items]

    R, K, Bm, A, V = load(rt_ref), load(kt_ref), load(bt_ref), load(at_ref), load(v_ref)
    big = each(lambda a, r, b, k: _dg(jnp.concatenate([a, r], axis=0).astype(BF16),
                                      jnp.concatenate([by_head(b), by_head(k)], axis=0), NT), A, R, Bm, K)
    a_ab = [jnp.where(strict, x[:C, :2 * C], 0.0) for x in big]
    a_ak = [jnp.where(strict, x[:C, 2 * C:], 0.0) for x in big]
    l_rb = [jnp.where(incl, x[C:, :2 * C], 0.0) for x in big]
    l_rk = [jnp.where(incl, x[C:, 2 * C:], 0.0) for x in big]
    X = [_dg(x.astype(BF16), blocks(x)) for x in a_ab]
    T = [eye_c + x for x in a_ab]
    for k in range(1, n_fold + 1):
        if k < n_fold:
            res = each(lambda x, t: _dg(jnp.concatenate([x, t], axis=0).astype(BF16), blocks(x)), X, T)
            X = [r_[:C] for r_ in res]
            T = each(lambda t, r_: t + r_[C:], T, res)
        else:
            T = each(lambda t, x: t + _dg(t.astype(BF16), blocks(x)), T, X)
    akv = each(lambda m_, v_: _dg(m_.astype(BF16), by_head(v_)), a_ak, V)
    pq = each(lambda t, a, q: _dg(t.astype(BF16), jnp.concatenate([by_head(a), by_head(q)], axis=1)), T, A, akv)
    P = [x[:, :PAIR] for x in pq]
    Q = [x[:, PAIR:] for x in pq]
    wz = each(lambda lb, lk, p_, q_, v_: _dg(
        jnp.concatenate([lb, lk], axis=1).astype(BF16),
        jnp.concatenate([jnp.concatenate([by_head(p_), by_head(q_)], axis=1),
                         jnp.concatenate([zero_blk, by_head(v_)], axis=1)], axis=0)), l_rb, l_rk, P, Q, V)
    W = each(lambda r, x: r + x[:, :PAIR], R, wz)
    Z = [x[:, PAIR:] for x in wz]
    ptb = each(lambda p_, b: _bdot(p_, b, TN), P, Bm)
    gfull = each(lambda q_, v_, b, k: _bdot(jnp.concatenate([q_, v_], axis=0), jnp.concatenate([b, k], axis=0), TN),
                 Q, V, Bm, K)

    state = [S_ref[p] for p in range(N_PAIRS)]
    for idx, (ci, p) in enumerate(items):
        rows = slice(ci * C, (ci + 1) * C)
        ln = slice(p * PAIR, (p + 1) * PAIR)
        g_c = gl_ref[ci, :, ln]
        M = (eye128 + jnp.where(same_head, ptb[idx], 0.0)) * g_c
        G = jnp.where(head_a, gfull[idx][:HEAD], gfull[idx][HEAD:]) * g_c
        S0 = state[p]
        Y = _dg(W[idx].astype(BF16), by_head(S0), NT) + Z[idx]
        state[p] = _bdot(S0, M) + G

        mu = head_mean(Y)
        d = Y - mu
        var = head_mean(d * d)
        yn = d * lax.rsqrt(var + LNX_EPS) * lng_ref[:, ln] + lnb_ref[:, ln]
        y_ref[rows, ln] = (yn + bonus_ref[rows, ln]) * g_ref[rows, ln]
    for p in range(N_PAIRS):
        S_ref[p] = state[p]

    @pl.when(step == pl.num_programs(1) - 1)
    def _():
        for p in range(N_PAIRS):
            sout_ref[0, p] = state[p]


def _to_pairs(s):
    n = s.shape[0]
    return s.reshape(n, N_PAIRS, 2, HEAD, HEAD).transpose(0, 1, 3, 2, 4).reshape(n, N_PAIRS, HEAD, PAIR)


def _from_pairs(s):
    n = s.shape[0]
    return s.reshape(n, N_PAIRS, HEAD, 2, HEAD).transpose(0, 1, 3, 2, 4).reshape(n, N_HEADS, HEAD, HEAD)


def _rwkv_scan(rt, kt, bt, at, v, bonus, g, gl, lnx_g, lnx_b, s0, *, n_seq, seq_len, chunk, nc):
    rows = chunk * nc
    assert seq_len % rows == 0
    nsteps = seq_len // rows
    row = pl.BlockSpec((rows, D_A), lambda b, c: (b * nsteps + c, 0))
    st = pl.BlockSpec((1, N_PAIRS, HEAD, PAIR), lambda b, c: (b, 0, 0, 0))
    vec = pl.BlockSpec((1, D_A), lambda b, c: (0, 0))
    y, s_new = pl.pallas_call(
        functools.partial(_scan_body, C=chunk, nc=nc),
        grid=(n_seq, nsteps),
        in_specs=[row] * 7 + [pl.BlockSpec((nc, 1, D_A), lambda b, c: (b * nsteps + c, 0, 0)), vec, vec, st],
        out_specs=[row, st],
        out_shape=[jax.ShapeDtypeStruct((n_seq * seq_len, D_A), F32),
                   jax.ShapeDtypeStruct((n_seq, N_PAIRS, HEAD, PAIR), F32)],
        scratch_shapes=[pltpu.VMEM((N_PAIRS, HEAD, PAIR), F32)],
        compiler_params=_params(("arbitrary", "arbitrary")), name="rwkv_scan",
    )(rt, kt, bt, at, v, bonus, g, gl, lnx_g, lnx_b, _to_pairs(s0))
    return y, _from_pairs(s_new)


def _bucket_ranges():
    d = np.arange(WINDOW + 1)
    scaled = np.log(np.maximum(d, MAX_EXACT).astype(np.float32) / MAX_EXACT) / math.log(REL_MAX_DIST / MAX_EXACT)
    large = np.minimum(MAX_EXACT + (scaled * (N_BUCKETS - MAX_EXACT)).astype(np.int32), N_BUCKETS - 1)
    bucket = np.where(d < MAX_EXACT, d, large)
    frac = scaled.astype(np.float64) * (N_BUCKETS - MAX_EXACT)
    near = np.abs(frac - np.round(frac)) < 1e-3
    assert all(int(x) in (MAX_EXACT, REL_MAX_DIST) for x in d[(d >= MAX_EXACT) & near])
    ranges = []
    for b in range(N_BUCKETS):
        hit = d[bucket == b]
        ranges.append((int(hit.min()), int(hit.max())) if hit.size else None)
    return tuple(ranges)


def _swa_body(rb_ref, sink_ref, cur_ref, prev_ref, y_ref, bias_ref, sinkcol_ref, *, qr, fresh, ranges):
    W = WINDOW
    nk = W + qr

    @pl.when((pl.program_id(0) == 0) & (pl.program_id(1) == 0))
    def _():
        qi = lax.broadcasted_iota(jnp.int32, (qr, nk), 0)
        kj = lax.broadcasted_iota(jnp.int32, (qr, nk), 1)
        d = qi + W - kj
        valid = (d >= 0) & (d <= W)
        for hd in range(N_HEADS):
            t = jnp.zeros((qr, nk), F32)
            for bkt, rng in enumerate(ranges):
                if rng is not None:
                    t = jnp.where((d >= rng[0]) & (d <= rng[1]), rb_ref[bkt * N_HEADS + hd], t)
            kvh, grp = hd // Q_PER_KV, hd % Q_PER_KV
            rows = slice(grp * qr, (grp + 1) * qr)
            bias_ref[kvh, rows, :] = jnp.where(valid, t, NEG_INF)
            bias_ref[KV_HEADS + kvh, rows, :] = jnp.where(valid & (kj >= W), t, NEG_INF)
            sinkcol_ref[kvh, rows, :] = jnp.full((qr, 1), sink_ref[hd], F32)

    table = jnp.where(pl.program_id(1) == 0, KV_HEADS, 0) if fresh else 0
    outs = [None] * N_HEADS
    for kvh in range(KV_HEADS):
        ko, vo = D_B + kvh * HEAD, D_B + KV_COLS + kvh * HEAD
        kb = jnp.concatenate([prev_ref[:, kvh * HEAD:(kvh + 1) * HEAD], cur_ref[:, ko:ko + HEAD]], axis=0)
        vb = jnp.concatenate([prev_ref[:, KV_COLS + kvh * HEAD:KV_COLS + (kvh + 1) * HEAD],
                              cur_ref[:, vo:vo + HEAD]], axis=0)
        q4 = jnp.concatenate([cur_ref[:, (kvh * Q_PER_KV + gq) * HEAD:(kvh * Q_PER_KV + gq + 1) * HEAD]
                              for gq in range(Q_PER_KV)], axis=0)
        s = _bdot(q4 * (HEAD ** -0.5), kb, NT) + bias_ref[table + kvh]
        sink = sinkcol_ref[kvh]
        m = jnp.maximum(jnp.max(s, axis=-1, keepdims=True), sink)
        p = jnp.exp(s - m)
        den = jnp.sum(p, axis=-1, keepdims=True) + jnp.exp(sink - m)
        o = _bdot(p, vb) / den
        for gq in range(Q_PER_KV):
            outs[kvh * Q_PER_KV + gq] = o[gq * qr:(gq + 1) * qr]
    y_ref[...] = jnp.concatenate(outs, axis=1)


def _swa(cur2d, prev2d, prev_map, rel_bias, sinks, *, n_seq, n_blk, qr, fresh):
    smem = pl.BlockSpec(memory_space=pltpu.SMEM)
    return pl.pallas_call(
        functools.partial(_swa_body, qr=qr, fresh=fresh, ranges=_bucket_ranges()),
        grid=(n_seq, n_blk),
        in_specs=[smem, smem,
                  pl.BlockSpec((qr, SWA_COLS), lambda b, j: (b * n_blk + j, 0)),
                  pl.BlockSpec((WINDOW, 2 * KV_COLS), prev_map)],
        out_specs=pl.BlockSpec((qr, D_B), lambda b, j: (b * n_blk + j, 0)),
        out_shape=jax.ShapeDtypeStruct((n_seq * n_blk * qr, D_B), F32),
        scratch_shapes=[pltpu.VMEM((2 * KV_HEADS, Q_PER_KV * qr, WINDOW + qr), F32),
                        pltpu.VMEM((KV_HEADS, Q_PER_KV * qr, 1), F32)],
        compiler_params=_params(("arbitrary", "arbitrary")), name="swa",
    )(rel_bias.reshape(-1), sinks, cur2d, prev2d)


def _outproj_ln_body(ya_ref, yb_ref, x_ref, wa_ref, wb_ref, g_ref, b_ref, o_ref):
    h = _dg(ya_ref[...].astype(BF16), wa_ref[...]) + _dg(yb_ref[...].astype(BF16), wb_ref[...])
    o_ref[...] = _layer_norm(ALPHA * x_ref[...] + h, g_ref[...], b_ref[...])


def _outproj_ln(ya, yb, x2d, w_out, ln_g, ln_b, *, tm):
    n = x2d.shape[0]
    half = pl.BlockSpec((tm, D_A), lambda i: (i, 0))
    full = pl.BlockSpec((tm, D_MODEL), lambda i: (i, 0))
    wa, wb = w_out[:D_A].astype(BF16), w_out[D_A:].astype(BF16)
    return pl.pallas_call(
        _outproj_ln_body, grid=(n // tm,),
        in_specs=[half, half, full, _full(wa.shape), _full(wb.shape), _full((1, D_MODEL)), _full((1, D_MODEL))],
        out_specs=full, out_shape=jax.ShapeDtypeStruct((n, D_MODEL), F32),
        compiler_params=_params(("arbitrary",)), name="outproj_ln",
    )(ya, yb, x2d, wa, wb, ln_g.reshape(1, -1), ln_b.reshape(1, -1))


def _router_body(x_ref, wrt_ref, br_ref, eye_ref, gates_ref):
    logits = _dg(wrt_ref[...], x_ref[...].astype(BF16), NT) + br_ref[...]
    z = jnp.exp(logits - jnp.max(logits, axis=0, keepdims=True))
    probs = z / jnp.sum(z, axis=0, keepdims=True)
    p = [probs[e:e + 1, :] for e in range(N_EXPERTS)]

    def top2_mask(vals):
        out = []
        for a_ in range(len(vals)):
            rank = None
            for j in range(len(vals)):
                if j == a_:
                    continue
                ahead = (vals[j] >= vals[a_]) if j < a_ else (vals[j] > vals[a_])
                ahead = ahead.astype(jnp.int32)
                rank = ahead if rank is None else rank + ahead
            out.append(rank < 2)
        return out

    sel, score = [], []
    for gi in range(N_GROUPS):
        grp = p[gi * EXPERTS_PER_GROUP:(gi + 1) * EXPERTS_PER_GROUP]
        msk = top2_mask(grp)
        sel += msk
        sc = None
        for a_ in range(EXPERTS_PER_GROUP):
            t = jnp.where(msk[a_], grp[a_], 0.0)
            sc = t if sc is None else sc + t
        score.append(sc)
    best = []
    for gi in range(N_GROUPS):
        ok = None
        for j in range(N_GROUPS):
            if j == gi:
                continue
            c_ = (score[gi] > score[j]) if j < gi else (score[gi] >= score[j])
            ok = c_ if ok is None else (ok & c_)
        best.append(ok)
    active = [sel[e] & best[e // EXPERTS_PER_GROUP] for e in range(N_EXPERTS)]
    tot = None
    for e in range(N_EXPERTS):
        t = jnp.where(active[e], p[e], 0.0)
        tot = t if tot is None else tot + t
    gates_t = jnp.concatenate([jnp.where(active[e], p[e] / tot, 0.0) for e in range(N_EXPERTS)], axis=0)
    out = None
    for part in _split(gates_t, 3):
        t = _dg(eye_ref[...], part, NT)
        out = t if out is None else out + t
    gates_ref[...] = out


def _router(x2d, w_router, b_router, *, tm):
    n = x2d.shape[0]
    eye = jnp.asarray(np.eye(tm), dtype=BF16)
    return pl.pallas_call(
        _router_body, grid=(n // tm,),
        in_specs=[pl.BlockSpec((tm, D_MODEL), lambda i: (i, 0)), _full((N_EXPERTS, D_MODEL)),
                  _full((N_EXPERTS, 1)), _full((tm, tm))],
        out_specs=pl.BlockSpec((tm, N_EXPERTS), lambda i: (i, 0)),
        out_shape=jax.ShapeDtypeStruct((n, N_EXPERTS), F32),
        compiler_params=_params(("arbitrary",)), name="router",
    )(x2d, w_router.T.astype(BF16), b_router.reshape(-1, 1), eye)


def _moe_ln_body(x_ref, gates_ref, wg_ref, wu_ref, wd_ref, g_ref, b_ref, o_ref, acc_ref, xb_ref):
    gi = pl.program_id(1)

    @pl.when(gi == 0)
    def _():
        xb_ref[...] = x_ref[...].astype(BF16)
        acc_ref[...] = jnp.zeros_like(acc_ref)

    xb = xb_ref[...]
    lane = lax.broadcasted_iota(jnp.int32, gates_ref.shape, 1)
    hs = []
    for j in range(EXPERTS_PER_GROUP):
        hg = _dg(xb, wg_ref[0, j])
        hu = _dg(xb, wu_ref[0, j])
        gate = jnp.sum(jnp.where(lane == gi * EXPERTS_PER_GROUP + j, gates_ref[...], 0.0), axis=1, keepdims=True)
        hs.append((hg * _sigmoid(hg) * hu * gate).astype(BF16))
    acc_ref[...] += _dg(jnp.concatenate(hs, axis=1), wd_ref[0])

    @pl.when(gi == N_GROUPS - 1)
    def _():
        o_ref[...] = _layer_norm(ALPHA * x_ref[...] + acc_ref[...], g_ref[...], b_ref[...])


def _moe_ln(x2d, gates, wg, wu, wd, ln_g, ln_b, *, tm):
    n = x2d.shape[0]
    tok = pl.BlockSpec((tm, D_MODEL), lambda i, e: (i, 0))
    wspec = pl.BlockSpec((1, EXPERTS_PER_GROUP, D_MODEL, D_FF), lambda i, e: (e, 0, 0, 0))
    return pl.pallas_call(
        _moe_ln_body, grid=(n // tm, N_GROUPS),
        in_specs=[tok, pl.BlockSpec((tm, N_EXPERTS), lambda i, e: (i, 0)), wspec, wspec,
                  pl.BlockSpec((1, EXPERTS_PER_GROUP * D_FF, D_MODEL), lambda i, e: (e, 0, 0)),
                  pl.BlockSpec((1, D_MODEL), lambda i, e: (0, 0)), pl.BlockSpec((1, D_MODEL), lambda i, e: (0, 0))],
        out_specs=tok, out_shape=jax.ShapeDtypeStruct((n, D_MODEL), F32),
        scratch_shapes=[pltpu.VMEM((tm, D_MODEL), F32), pltpu.VMEM((tm, D_MODEL), BF16)],
        compiler_params=_params(("arbitrary", "arbitrary")), name="moe_ln",
    )(x2d, gates, wg, wu, wd, ln_g.reshape(1, -1), ln_b.reshape(1, -1))


HALO = 16


def _pool_ln_body(x_ref, halo_ref, wp_ref, ps_ref, g_ref, b_ref, o_ref, *, tm, seq_len, start_pos):
    i = pl.program_id(0)
    x = x_ref[...]
    idx = lax.broadcasted_iota(jnp.int32, (tm + HALO, 1), 0)
    pos_e = (i * tm - HALO + idx) & (seq_len - 1)
    cnt_pos = (pos_e[HALO:] + (start_pos + 1)).astype(F32)
    mixed = []
    for gi, w in enumerate(POOL_WINDOWS):
        cols = slice(gi * POOL_GROUP, (gi + 1) * POOL_GROUP)
        xg = x[:, cols]
        s = jnp.concatenate([halo_ref[:, cols], xg], axis=0)
        step = 1
        while step < w:
            s = s + jnp.where(pos_e >= step, pltpu.roll(s, step, axis=0), 0.0)
            step *= 2
        pooled = s[HALO:] / jnp.minimum(float(w), cnt_pos) - xg
        mixed.append(_bdot(pooled, wp_ref[gi]))
    h = jnp.concatenate(mixed, axis=1) * ps_ref[...]
    o_ref[...] = _layer_norm(ALPHA * x + h, g_ref[...], b_ref[...])


def _pool_ln(x2d, w_pool, pool_scale, ln_g, ln_b, *, tm, seq_len, start_pos):
    n = x2d.shape[0]
    assert seq_len & (seq_len - 1) == 0 and seq_len >= HALO and tm % HALO == 0
    tok = pl.BlockSpec((tm, D_MODEL), lambda i: (i, 0))
    return pl.pallas_call(
        functools.partial(_pool_ln_body, tm=tm, seq_len=seq_len, start_pos=start_pos), grid=(n // tm,),
        in_specs=[tok, pl.BlockSpec((HALO, D_MODEL), lambda i: (jnp.maximum(i * (tm // HALO) - 1, 0), 0)),
                  _full(w_pool.shape), _full((1, D_MODEL)), _full((1, D_MODEL)), _full((1, D_MODEL))],
        out_specs=tok, out_shape=jax.ShapeDtypeStruct((n, D_MODEL), F32),
        compiler_params=_params(("arbitrary",)), name="pool_ln",
    )(x2d, x2d, w_pool.astype(BF16), pool_scale.reshape(1, -1), ln_g.reshape(1, -1), ln_b.reshape(1, -1))


def _pad_lora_cols(w, width):
    return jnp.pad(w, ((0, 0), (0, LORA_PAD - width)))


def _even_layer_weights(prm, i):
    w_in = prm['w_in'][i]
    o = 3 * D_A
    cols = lambda w: jnp.concatenate(
        [w[:, :o], _pad_lora_cols(w[:, o:o + W_LORA], W_LORA),
         _pad_lora_cols(w[:, o + W_LORA:o + W_LORA + A_LORA], A_LORA),
         _pad_lora_cols(w[:, o + W_LORA + A_LORA:RWKV_COLS], G_LORA)], axis=1)
    pad_rows = lambda w: jnp.pad(w, ((0, LORA_PAD - w.shape[0]), (0, 0))).astype(BF16)
    vec = lambda a: a.reshape(1, -1)
    return dict(
        wr=cols(w_in[:, :RWKV_COLS]).astype(BF16), ws=w_in[:, RWKV_COLS:].astype(BF16),
        mu=cols(prm['tshift_mu'][i].reshape(1, -1)),
        w0=vec(prm['decay_w0'][i]), ww2=pad_rows(prm['decay_w2'][i]),
        a0=vec(prm['iclr_a0'][i]), wa2=pad_rows(prm['iclr_a2'][i]), wg2=pad_rows(prm['gate_w2'][i]),
        k_k=vec(prm['k_k'][i]), k_a=vec(prm['k_a'][i]), r_k=vec(prm['r_k'][i]),
    )


def _group_experts(w):
    return w.astype(BF16).reshape(DEPTH, N_GROUPS, EXPERTS_PER_GROUP, D_MODEL, D_FF)


def _moe_block(x2d, prm, layer, *, tm):
    gates = _router(x2d, prm['w_router'], prm['b_router'], tm=min(tm, 512))
    return _moe_ln(x2d, gates, prm['wg'][layer], prm['wu'][layer], prm['wd'][layer],
                   prm['ln_ffn_g'][layer], prm['ln_ffn_b'][layer], tm=tm)


PROMPT_CHUNKS_PER_STEP = 4
SAMPLE_CHUNK = 16
SAMPLE_QROWS = SUBLANES


def _prompt_trunk(x, prm):
    bsz, t, _ = x.shape
    n = bsz * t
    x0 = x.reshape(n, D_MODEL)
    wts = _even_layer_weights(prm, 0)
    rt, kt, bt, at, v, bonus, g, gl, swa = _inproj_prep(x0, None, wts, seq_len=t, chunk=CHUNK, tm=256)
    s0 = jnp.zeros((bsz, N_HEADS, HEAD, HEAD), F32)
    ya, s_new = _rwkv_scan(rt, kt, bt, at, v, bonus, g, gl, prm['lnx_g'][0].reshape(1, -1),
                           prm['lnx_b'][0].reshape(1, -1), s0, n_seq=bsz, seq_len=t, chunk=CHUNK,
                           nc=PROMPT_CHUNKS_PER_STEP)
    n_blk = t // WINDOW
    yb = _swa(swa, swa, lambda b, j: (b * n_blk + jnp.maximum(j - 1, 0), 2), prm['rel_bias'],
              prm['attn_sinks'][0], n_seq=bsz, n_blk=n_blk, qr=WINDOW, fresh=True)
    x1 = _outproj_ln(ya, yb, x0, prm['w_out'][0], prm['ln_mix_g'][0], prm['ln_mix_b'][0], tm=min(512, n))
    x2 = _moe_block(x1, prm, 0, tm=min(1024, n))
    x3 = _pool_ln(x2, prm['w_pool'][0], prm['pool_scale'][0], prm['ln_mix_g'][1], prm['ln_mix_b'][1],
                  tm=min(512, n), seq_len=t, start_pos=0)
    x4 = _moe_block(x3, prm, 1, tm=min(1024, n))
    swa3 = swa.reshape(bsz, t, SWA_COLS)
    k_new = swa3[:, t - WINDOW:, D_B:D_B + KV_COLS].reshape(bsz, WINDOW, KV_HEADS, HEAD)
    v_new = swa3[:, t - WINDOW:, D_B + KV_COLS:].reshape(bsz, WINDOW, KV_HEADS, HEAD)
    pool_new = x2.reshape(bsz, t, D_MODEL)[:, t - POOL_KEEP:]
    return (x4.reshape(bsz, t, D_MODEL), s_new[None], x[:, -1][None], k_new[None], v_new[None], pool_new[None])


def _sample_trunk(x, st_rwkv, st_shift, c_k, c_v, st_pool, prm):
    bsz = x.shape[0]
    x0 = x.reshape(bsz, D_MODEL)
    wts = _even_layer_weights(prm, 0)
    outs = _inproj_prep(x0, st_shift[0], wts, seq_len=1, chunk=1, tm=bsz)
    gl, swa = outs[7], outs[8]
    sc = SAMPLE_CHUNK
    padded = [jnp.zeros((bsz, sc, D_A), F32).at[:, 0].set(a).reshape(bsz * sc, D_A) for a in outs[:7]]
    ya, s_new = _rwkv_scan(*padded, gl.reshape(bsz, 1, D_A), prm['lnx_g'][0].reshape(1, -1),
                           prm['lnx_b'][0].reshape(1, -1), st_rwkv[0], n_seq=bsz, seq_len=sc, chunk=sc, nc=1)
    ya = ya.reshape(bsz, sc, D_A)[:, 0]
    qr = SAMPLE_QROWS
    kv_prev = jnp.concatenate([c_k[0].reshape(bsz * WINDOW, KV_COLS), c_v[0].reshape(bsz * WINDOW, KV_COLS)], axis=-1)
    cur = jnp.zeros((bsz, qr, SWA_COLS), F32).at[:, 0].set(swa).reshape(bsz * qr, SWA_COLS)
    yb = _swa(cur, kv_prev, lambda b, j: (b, 0), prm['rel_bias'], prm['attn_sinks'][0], n_seq=bsz, n_blk=1,
              qr=qr, fresh=False)
    yb = yb.reshape(bsz, qr, D_B)[:, 0]
    x1 = _outproj_ln(ya, yb, x0, prm['w_out'][0], prm['ln_mix_g'][0], prm['ln_mix_b'][0], tm=bsz)
    x2 = _moe_block(x1, prm, 0, tm=bsz)
    xcat = jnp.concatenate([st_pool[0], x2[:, None]], axis=1)
    x3 = _pool_ln(xcat.reshape(bsz * HALO, D_MODEL), prm['w_pool'][0], prm['pool_scale'][0],
                  prm['ln_mix_g'][1], prm['ln_mix_b'][1], tm=min(512, bsz * HALO), seq_len=HALO,
                  start_pos=PAST_LEN - POOL_KEEP)
    x3 = x3.reshape(bsz, HALO, D_MODEL)[:, -1]
    x4 = _moe_block(x3, prm, 1, tm=bsz)
    k_new = jnp.concatenate([c_k[0][:, 1:], swa[:, D_B:D_B + KV_COLS].reshape(bsz, 1, KV_HEADS, HEAD)], axis=1)
    v_new = jnp.concatenate([c_v[0][:, 1:], swa[:, D_B + KV_COLS:].reshape(bsz, 1, KV_HEADS, HEAD)], axis=1)
    return (x4.reshape(bsz, 1, D_MODEL), s_new[None], x[:, -1][None], k_new[None], v_new[None],
            xcat[:, 1:][None])


def _prepare_params(raw):
    prm = dict(raw)
    prm['r_k'] = raw['r_k'].reshape(raw['r_k'].shape[0], -1)
    prm['wg'] = _group_experts(raw['w_ex_gate'])
    prm['wu'] = _group_experts(raw['w_ex_up'])
    prm['wd'] = raw['w_ex_down'].astype(BF16).reshape(DEPTH, N_GROUPS, EXPERTS_PER_GROUP * D_FF, D_MODEL)
    return prm


def kernel(x_prompt, x_sample, state_rwkv, state_shift, cache_swa_k, cache_swa_v, state_pool, w_in, tshift_mu,
           decay_w0, decay_w2, iclr_a0, iclr_a2, gate_w2, k_k, k_a, r_k, lnx_g, lnx_b, attn_sinks, rel_bias, w_out,
           w_pool, pool_scale, ln_mix_g, ln_mix_b, ln_ffn_g, ln_ffn_b, w_router, b_router, w_ex_gate, w_ex_up,
           w_ex_down):
    prm = _prepare_params(dict(
        w_in=w_in, tshift_mu=tshift_mu, decay_w0=decay_w0, decay_w2=decay_w2, iclr_a0=iclr_a0, iclr_a2=iclr_a2,
        gate_w2=gate_w2, k_k=k_k, k_a=k_a, r_k=r_k, lnx_g=lnx_g, lnx_b=lnx_b, attn_sinks=attn_sinks,
        rel_bias=rel_bias, w_out=w_out, w_pool=w_pool, pool_scale=pool_scale, ln_mix_g=ln_mix_g, ln_mix_b=ln_mix_b,
        ln_ffn_g=ln_ffn_g, ln_ffn_b=ln_ffn_b, w_router=w_router, b_router=b_router, w_ex_gate=w_ex_gate,
        w_ex_up=w_ex_up, w_ex_down=w_ex_down))
    y_p, rwkv_p, shift_p, k_p, v_p, pool_p = _prompt_trunk(x_prompt, prm)
    y_s, rwkv_s, shift_s, k_s, v_s, pool_s = _sample_trunk(x_sample, state_rwkv, state_shift, cache_swa_k,
                                                            cache_swa_v, state_pool, prm)
    return (y_p, y_s, rwkv_p, rwkv_s, shift_p, shift_s, k_p, k_s, v_p, v_s, pool_p, pool_s)
```

```python
import functools
import math

import numpy as np
import jax
import jax.numpy as jnp
from jax import lax
from jax.experimental import pallas as pl
from jax.experimental.pallas import tpu as pltpu

F32 = jnp.float32
BF16 = jnp.bfloat16

D_MODEL = 1024
DEPTH = 2
PAST_LEN = 16384
D_A = 512
HEAD = 64
N_HEADS = 8
W_LORA, A_LORA, G_LORA = 32, 64, 96
RWKV_COLS = 3 * D_A + W_LORA + A_LORA + G_LORA
D_B = 512
KV_HEADS = 2
Q_PER_KV = 4
KV_COLS = KV_HEADS * HEAD
SWA_COLS = D_B + 2 * KV_COLS
WINDOW = 128
N_BUCKETS = 32
MAX_EXACT = 16
REL_MAX_DIST = 128
POOL_WINDOWS = (2, 4, 8, 16)
POOL_GROUP = 256
POOL_KEEP = 15
N_EXPERTS = 16
N_GROUPS = 4
EXPERTS_PER_GROUP = 4
D_FF = 256
ALPHA = (2.0 * DEPTH) ** 0.25
LN_EPS = 1e-5
LNX_EPS = 64e-5
NEG_INF = -1e30

LANES = 128
SUBLANES = 8
VMEM_LIMIT_BYTES = 56 * 1024 * 1024

LORA_PAD = LANES
RWKV_PAD_COLS = 3 * D_A + 3 * LORA_PAD
CHUNK = 64

NN = ((1,), (0,))
NT = ((1,), (1,))
TN = ((0,), (0,))


def _dg(a, b, dims=NN):
    return lax.dot_general(a, b, (dims, ((), ())), preferred_element_type=F32)


def _bdot(a, b, dims=NN):
    return _dg(a.astype(BF16), b.astype(BF16), dims)


def _split(x, n):
    parts, rem = [], x
    for i in range(n):
        p = rem.astype(BF16)
        parts.append(p)
        if i + 1 < n:
            rem = rem - p.astype(F32)
    return parts


def _mm(a, b, dims=NN, passes=3):
    if passes == 1:
        return _bdot(a, b, dims)
    ah, al = _split(a, 2)
    bh, bl = _split(b, 2)
    return _dg(ah, bh, dims) + (_dg(ah, bl, dims) + _dg(al, bh, dims))


def _dot_exact_rhs(x, ones_bf16, n=3):
    out = None
    for p in _split(x, n):
        t = _dg(p, ones_bf16)
        out = t if out is None else out + t
    return out


def _dot_exact_lhs(ones_bf16, x, n=3):
    out = None
    for p in _split(x, n):
        t = _dg(ones_bf16, p)
        out = t if out is None else out + t
    return out


def _sigmoid(x):
    return 1.0 / (1.0 + jnp.exp(-x))


def _layer_norm(z, g, b):
    mu = jnp.mean(z, axis=-1, keepdims=True)
    d = z - mu
    var = jnp.mean(d * d, axis=-1, keepdims=True)
    return d * lax.rsqrt(var + LN_EPS) * g + b


def _params(sem):
    return pltpu.CompilerParams(dimension_semantics=sem, vmem_limit_bytes=VMEM_LIMIT_BYTES)


def _full(shape):
    nd = len(shape)
    return pl.BlockSpec(shape, lambda *_: (0,) * nd)


def _inproj_prep_body(*refs, tm, tiles_per_seq, chunk, has_prev):
    if has_prev:
        x_ref, xp_ref = refs[0], refs[1]
        refs = refs[2:]
    else:
        x_ref, xp_ref = refs[0], None
        refs = refs[1:]
    (wr_ref, ws_ref, mu_ref, w0_ref, ww2_ref, a0_ref, wa2_ref, wg2_ref, kk_ref, ka_ref, rk_ref,
     ones_ref, tri_ref,
     rt_ref, kt_ref, bt_ref, at_ref, v_ref, bonus_ref, g_ref, gl_ref, swa_ref, carry_ref) = refs

    i = pl.program_id(0)
    xb = x_ref[...].astype(BF16)
    pr = _dg(xb, wr_ref[...])
    swa_ref[...] = _dg(xb, ws_ref[...])

    if has_prev:
        shifted = _dg(xp_ref[...].astype(BF16), wr_ref[...])
    else:
        @pl.when(i == 0)
        def _():
            carry_ref[...] = jnp.zeros_like(carry_ref)

        first = (i % tiles_per_seq) == 0
        prev_last = jnp.where(first, 0.0, carry_ref[0:1, :])
        rolled = pltpu.roll(pr, 1, axis=0)
        row = lax.broadcasted_iota(jnp.int32, (tm, 1), 0)
        shifted = jnp.where(row == 0, prev_last, rolled)
        carry_ref[0:1, :] = pr[tm - 1:tm, :]
    m = pr + mu_ref[...] * (shifted - pr)

    r = m[:, 0:D_A]
    k = m[:, D_A:2 * D_A]
    v = m[:, 2 * D_A:3 * D_A]
    wd = m[:, 3 * D_A:3 * D_A + LORA_PAD]
    ad = m[:, 3 * D_A + LORA_PAD:3 * D_A + 2 * LORA_PAD]
    gd = m[:, 3 * D_A + 2 * LORA_PAD:3 * D_A + 3 * LORA_PAD]

    nz = -(w0_ref[...] + _bdot(jnp.tanh(wd), ww2_ref[...]))
    softplus = jnp.maximum(nz, 0.0) + jnp.log1p(jnp.exp(-jnp.abs(nz)))
    w_log = -softplus - 0.5
    logw = -jnp.exp(w_log)
    a = _sigmoid(a0_ref[...] + _bdot(ad, wa2_ref[...]))
    g = _bdot(_sigmoid(gd), wg2_ref[...])

    ones_bd = ones_ref[...]
    kkr = k * kk_ref[...]
    nrm = jnp.sqrt(_dot_exact_rhs(kkr * kkr, ones_bd))
    kk = kkr / jnp.maximum(nrm, 1e-12)
    k2 = k * (1.0 + (a - 1.0) * ka_ref[...])
    bonus_ref[...] = _dot_exact_rhs(r * k2 * rk_ref[...], ones_bd) * v

    if chunk > 1:
        cum = _dot_exact_lhs(tri_ref[...], logw)
    else:
        cum = logw
    gam = jnp.exp(cum)
    inv = jnp.exp(-cum)
    rt_ref[...] = r * gam
    kt_ref[...] = k2 * inv
    bt_ref[...] = kk * a * inv
    at_ref[...] = -kk * jnp.exp(cum - logw)
    v_ref[...] = v
    g_ref[...] = g
    if chunk > 1:
        for c in range(tm // chunk):
            gl_ref[c] = gam[(c + 1) * chunk - 1:(c + 1) * chunk, :]
    else:
        gl_ref[...] = gam


def _inproj_prep(x2d, x_prev, wts, *, seq_len, chunk, tm):
    n = x2d.shape[0]
    assert n % tm == 0 and (x_prev is not None or seq_len % tm == 0)
    assert tm % chunk == 0
    has_prev = x_prev is not None
    row = lambda w: pl.BlockSpec((tm, w), lambda i: (i, 0))
    ins = [x2d] + ([x_prev] if has_prev else [])
    in_specs = [row(D_MODEL)] + ([row(D_MODEL)] if has_prev else [])
    consts = [wts['wr'], wts['ws'], wts['mu'], wts['w0'], wts['ww2'], wts['a0'], wts['wa2'], wts['wg2'],
              wts['k_k'], wts['k_a'], wts['r_k'], _ones_block_diag(D_A, HEAD), _tri_block_diag(tm, chunk)]
    ins += consts
    in_specs += [_full(c.shape) for c in consts]
    if chunk > 1:
        gl_shape = jax.ShapeDtypeStruct((n // chunk, 1, D_A), F32)
        gl_spec = pl.BlockSpec((tm // chunk, 1, D_A), lambda i: (i, 0, 0))
    else:
        gl_shape = jax.ShapeDtypeStruct((n, D_A), F32)
        gl_spec = row(D_A)
    out_shape = [jax.ShapeDtypeStruct((n, D_A), F32)] * 7 + [gl_shape, jax.ShapeDtypeStruct((n, SWA_COLS), F32)]
    out_specs = [row(D_A)] * 7 + [gl_spec, row(SWA_COLS)]
    body = functools.partial(_inproj_prep_body, tm=tm, tiles_per_seq=max(seq_len // tm, 1), chunk=chunk,
                             has_prev=has_prev)
    return pl.pallas_call(
        body, grid=(n // tm,), in_specs=in_specs, out_specs=out_specs, out_shape=out_shape,
        scratch_shapes=[pltpu.VMEM((SUBLANES, RWKV_PAD_COLS), F32)],
        compiler_params=_params(("arbitrary",)), name="inproj_prep",
    )(*ins)


def _ones_block_diag(n, blk):
    idx = np.arange(n) // blk
    return jnp.asarray(idx[:, None] == idx[None, :], dtype=BF16)


def _tri_block_diag(n, blk):
    idx = np.arange(n)
    same = (idx[:, None] // blk) == (idx[None, :] // blk)
    return jnp.asarray(same & (idx[:, None] >= idx[None, :]), dtype=BF16)


PAIR = 2 * HEAD
N_PAIRS = N_HEADS // 2


def _scan_body(rt_ref, kt_ref, bt_ref, at_ref, v_ref, bonus_ref, g_ref, gl_ref, lng_ref, lnb_ref, s0_ref,
               y_ref, sout_ref, S_ref, *, C, nc):
    step = pl.program_id(1)

    @pl.when(step == 0)
    def _():
        S_ref[...] = s0_ref[0]

    head_a = lax.broadcasted_iota(jnp.int32, (1, PAIR), 1) < HEAD
    rowc = lax.broadcasted_iota(jnp.int32, (C, 2 * C), 0)
    colc = lax.broadcasted_iota(jnp.int32, (C, 2 * C), 1)
    first_c = colc < C
    col_in = jnp.where(first_c, colc, colc - C)
    strict = rowc > col_in
    incl = rowc >= col_in
    eye_c = (rowc == col_in).astype(F32)
    r128 = lax.broadcasted_iota(jnp.int32, (PAIR, PAIR), 0)
    c128 = lax.broadcasted_iota(jnp.int32, (PAIR, PAIR), 1)
    same_head = (r128 < HEAD) == (c128 < HEAD)
    eye128 = (r128 == c128).astype(F32)

    def by_head(x):
        xb = x.astype(BF16)
        z = jnp.zeros_like(xb)
        return jnp.concatenate([jnp.where(head_a, xb, z), jnp.where(head_a, z, xb)], axis=0)

    def blocks(x):
        xb = x.astype(BF16)
        z = jnp.zeros_like(xb)
        return jnp.concatenate([jnp.where(first_c, xb, z), jnp.where(first_c, z, xb)], axis=0)

    def head_mean(z):
        za = jnp.sum(jnp.where(head_a, z, 0.0), axis=-1, keepdims=True)
        zb = jnp.sum(jnp.where(head_a, 0.0, z), axis=-1, keepdims=True)
        return jnp.where(head_a, za, zb) * (1.0 / HEAD)

    n_fold = int(math.log2(C)) - 1
    zero_blk = jnp.zeros((2 * C, PAIR), BF16)
    items = [(ci, p) for ci in range(nc) for p in range(N_PAIRS)]
    each = lambda f, *lists: [f(*xs) for xs in zip(*lists)]

    def load(ref):
        return [ref[ci * C:(ci + 1) * C, p * PAIR:(p + 1) * PAIR] for ci, p in items]

    R, K, Bm, A, V = load(rt_ref), load(kt_ref), load(bt_ref), load(at_ref), load(v_ref)
    big = each(lambda a, r, b, k: _dg(jnp.concatenate([a, r], axis=0).astype(BF16),
                                      jnp.concatenate([by_head(b), by_head(k)], axis=0), NT), A, R, Bm, K)
    a_ab = [jnp.where(strict, x[:C, :2 * C], 0.0) for x in big]
    a_ak = [jnp.where(strict, x[:C, 2 * C:], 0.0) for x in big]
    l_rb = [jnp.where(incl, x[C:, :2 * C], 0.0) for x in big]
    l_rk = [jnp.where(incl, x[C:, 2 * C:], 0.0) for x in big]
    X = [_dg(x.astype(BF16), blocks(x)) for x in a_ab]
    T = [eye_c + x for x in a_ab]
    for k in range(1, n_fold + 1):
        if k < n_fold:
            res = each(lambda x, t: _dg(jnp.concatenate([x, t], axis=0).astype(BF16), blocks(x)), X, T)
            X = [r_[:C] for r_ in res]
            T = each(lambda t, r_: t + r_[C:], T, res)
        else:
            T = each(lambda t, x: t + _dg(t.astype(BF16), blocks(x)), T, X)
    akv = each(lambda m_, v_: _dg(m_.astype(BF16), by_head(v_)), a_ak, V)
    pq = each(lambda t, a, q: _dg(t.astype(BF16), jnp.concatenate([by_head(a), by_head(q)], axis=1)), T, A, akv)
    P = [x[:, :PAIR] for x in pq]
    Q = [x[:, PAIR:] for x in pq]
    wz = each(lambda lb, lk, p_, q_, v_: _dg(
        jnp.concatenate([lb, lk], axis=1).astype(BF16),
        jnp.concatenate([jnp.concatenate([by_head(p_), by_head(q_)], axis=1),
                         jnp.concatenate([zero_blk, by_head(v_)], axis=1)], axis=0)), l_rb, l_rk, P, Q, V)
    W = each(lambda r, x: r + x[:, :PAIR], R, wz)
    Z = [x[:, PAIR:] for x in wz]
    ptb = each(lambda p_, b: _bdot(p_, b, TN), P, Bm)
    gfull = each(lambda q_, v_, b, k: _bdot(jnp.concatenate([q_, v_], axis=0), jnp.concatenate([b, k], axis=0), TN),
                 Q, V, Bm, K)

    state = [S_ref[p] for p in range(N_PAIRS)]
    for idx, (ci, p) in enumerate(items):
        rows = slice(ci * C, (ci + 1) * C)
        ln = slice(p * PAIR, (p + 1) * PAIR)
        g_c = gl_ref[ci, :, ln]
        M = (eye128 + jnp.where(same_head, ptb[idx], 0.0)) * g_c
        G = jnp.where(head_a, gfull[idx][:HEAD], gfull[idx][HEAD:]) * g_c
        S0 = state[p]
        Y = _dg(W[idx].astype(BF16), by_head(S0), NT) + Z[idx]
        state[p] = _bdot(S0, M) + G

        mu = head_mean(Y)
        d = Y - mu
        var = head_mean(d * d)
        yn = d * lax.rsqrt(var + LNX_EPS) * lng_ref[:, ln] + lnb_ref[:, ln]
        y_ref[rows, ln] = (yn + bonus_ref[rows, ln]) * g_ref[rows, ln]
    for p in range(N_PAIRS):
        S_ref[p] = state[p]

    @pl.when(step == pl.num_programs(1) - 1)
    def _():
        for p in range(N_PAIRS):
            sout_ref[0, p] = state[p]


def _to_pairs(s):
    n = s.shape[0]
    return s.reshape(n, N_PAIRS, 2, HEAD, HEAD).transpose(0, 1, 3, 2, 4).reshape(n, N_PAIRS, HEAD, PAIR)


def _from_pairs(s):
    n = s.shape[0]
    return s.reshape(n, N_PAIRS, HEAD, 2, HEAD).transpose(0, 1, 3, 2, 4).reshape(n, N_HEADS, HEAD, HEAD)


def _rwkv_scan(rt, kt, bt, at, v, bonus, g, gl, lnx_g, lnx_b, s0, *, n_seq, seq_len, chunk, nc):
    rows = chunk * nc
    assert seq_len % rows == 0
    nsteps = seq_len // rows
    row = pl.BlockSpec((rows, D_A), lambda b, c: (b * nsteps + c, 0))
    st = pl.BlockSpec((1, N_PAIRS, HEAD, PAIR), lambda b, c: (b, 0, 0, 0))
    vec = pl.BlockSpec((1, D_A), lambda b, c: (0, 0))
    y, s_new = pl.pallas_call(
        functools.partial(_scan_body, C=chunk, nc=nc),
        grid=(n_seq, nsteps),
        in_specs=[row] * 7 + [pl.BlockSpec((nc, 1, D_A), lambda b, c: (b * nsteps + c, 0, 0)), vec, vec, st],
        out_specs=[row, st],
        out_shape=[jax.ShapeDtypeStruct((n_seq * seq_len, D_A), F32),
                   jax.ShapeDtypeStruct((n_seq, N_PAIRS, HEAD, PAIR), F32)],
        scratch_shapes=[pltpu.VMEM((N_PAIRS, HEAD, PAIR), F32)],
        compiler_params=_params(("arbitrary", "arbitrary")), name="rwkv_scan",
    )(rt, kt, bt, at, v, bonus, g, gl, lnx_g, lnx_b, _to_pairs(s0))
    return y, _from_pairs(s_new)


def _bucket_ranges():
    d = np.arange(WINDOW + 1)
    scaled = np.log(np.maximum(d, MAX_EXACT).astype(np.float32) / MAX_EXACT) / math.log(REL_MAX_DIST / MAX_EXACT)
    large = np.minimum(MAX_EXACT + (scaled * (N_BUCKETS - MAX_EXACT)).astype(np.int32), N_BUCKETS - 1)
    bucket = np.where(d < MAX_EXACT, d, large)
    frac = scaled.astype(np.float64) * (N_BUCKETS - MAX_EXACT)
    near = np.abs(frac - np.round(frac)) < 1e-3
    assert all(int(x) in (MAX_EXACT, REL_MAX_DIST) for x in d[(d >= MAX_EXACT) & near])
    ranges = []
    for b in range(N_BUCKETS):
        hit = d[bucket == b]
        ranges.append((int(hit.min()), int(hit.max())) if hit.size else None)
    return tuple(ranges)


def _swa_body(rb_ref, sink_ref, cur_ref, prev_ref, y_ref, bias_ref, sinkcol_ref, *, qr, fresh, ranges):
    W = WINDOW
    nk = W + qr

    @pl.when((pl.program_id(0) == 0) & (pl.program_id(1) == 0))
    def _():
        qi = lax.broadcasted_iota(jnp.int32, (qr, nk), 0)
        kj = lax.broadcasted_iota(jnp.int32, (qr, nk), 1)
        d = qi + W - kj
        valid = (d >= 0) & (d <= W)
        for hd in range(N_HEADS):
            t = jnp.zeros((qr, nk), F32)
            for bkt, rng in enumerate(ranges):
                if rng is not None:
                    t = jnp.where((d >= rng[0]) & (d <= rng[1]), rb_ref[bkt * N_HEADS + hd], t)
            kvh, grp = hd // Q_PER_KV, hd % Q_PER_KV
            rows = slice(grp * qr, (grp + 1) * qr)
            bias_ref[kvh, rows, :] = jnp.where(valid, t, NEG_INF)
            bias_ref[KV_HEADS + kvh, rows, :] = jnp.where(valid & (kj >= W), t, NEG_INF)
            sinkcol_ref[kvh, rows, :] = jnp.full((qr, 1), sink_ref[hd], F32)

    table = jnp.where(pl.program_id(1) == 0, KV_HEADS, 0) if fresh else 0
    outs = [None] * N_HEADS
    for kvh in range(KV_HEADS):
        ko, vo = D_B + kvh * HEAD, D_B + KV_COLS + kvh * HEAD
        kb = jnp.concatenate([prev_ref[:, kvh * HEAD:(kvh + 1) * HEAD], cur_ref[:, ko:ko + HEAD]], axis=0)
        vb = jnp.concatenate([prev_ref[:, KV_COLS + kvh * HEAD:KV_COLS + (kvh + 1) * HEAD],
                              cur_ref[:, vo:vo + HEAD]], axis=0)
        q4 = jnp.concatenate([cur_ref[:, (kvh * Q_PER_KV + gq) * HEAD:(kvh * Q_PER_KV + gq + 1) * HEAD]
                              for gq in range(Q_PER_KV)], axis=0)
        s = _bdot(q4 * (HEAD ** -0.5), kb, NT) + bias_ref[table + kvh]
        sink = sinkcol_ref[kvh]
        m = jnp.maximum(jnp.max(s, axis=-1, keepdims=True), sink)
        p = jnp.exp(s - m)
        den = jnp.sum(p, axis=-1, keepdims=True) + jnp.exp(sink - m)
        o = _bdot(p, vb) / den
        for gq in range(Q_PER_KV):
            outs[kvh * Q_PER_KV + gq] = o[gq * qr:(gq + 1) * qr]
    y_ref[...] = jnp.concatenate(outs, axis=1)


def _swa(cur2d, prev2d, prev_map, rel_bias, sinks, *, n_seq, n_blk, qr, fresh):
    smem = pl.BlockSpec(memory_space=pltpu.SMEM)
    return pl.pallas_call(
        functools.partial(_swa_body, qr=qr, fresh=fresh, ranges=_bucket_ranges()),
        grid=(n_seq, n_blk),
        in_specs=[smem, smem,
                  pl.BlockSpec((qr, SWA_COLS), lambda b, j: (b * n_blk + j, 0)),
                  pl.BlockSpec((WINDOW, 2 * KV_COLS), prev_map)],
        out_specs=pl.BlockSpec((qr, D_B), lambda b, j: (b * n_blk + j, 0)),
        out_shape=jax.ShapeDtypeStruct((n_seq * n_blk * qr, D_B), F32),
        scratch_shapes=[pltpu.VMEM((2 * KV_HEADS, Q_PER_KV * qr, WINDOW + qr), F32),
                        pltpu.VMEM((KV_HEADS, Q_PER_KV * qr, 1), F32)],
        compiler_params=_params(("arbitrary", "arbitrary")), name="swa",
    )(rel_bias.reshape(-1), sinks, cur2d, prev2d)


def _outproj_ln_body(ya_ref, yb_ref, x_ref, wa_ref, wb_ref, g_ref, b_ref, o_ref):
    h = _dg(ya_ref[...].astype(BF16), wa_ref[...]) + _dg(yb_ref[...].astype(BF16), wb_ref[...])
    o_ref[...] = _layer_norm(ALPHA * x_ref[...] + h, g_ref[...], b_ref[...])


def _outproj_ln(ya, yb, x2d, w_out, ln_g, ln_b, *, tm):
    n = x2d.shape[0]
    half = pl.BlockSpec((tm, D_A), lambda i: (i, 0))
    full = pl.BlockSpec((tm, D_MODEL), lambda i: (i, 0))
    wa, wb = w_out[:D_A].astype(BF16), w_out[D_A:].astype(BF16)
    return pl.pallas_call(
        _outproj_ln_body, grid=(n // tm,),
        in_specs=[half, half, full, _full(wa.shape), _full(wb.shape), _full((1, D_MODEL)), _full((1, D_MODEL))],
        out_specs=full, out_shape=jax.ShapeDtypeStruct((n, D_MODEL), F32),
        compiler_params=_params(("arbitrary",)), name="outproj_ln",
    )(ya, yb, x2d, wa, wb, ln_g.reshape(1, -1), ln_b.reshape(1, -1))


def _router_body(x_ref, wrt_ref, br_ref, eye_ref, tri_ref, meta_ref, slot_ref, cnt_ref):
    logits = _dg(wrt_ref[...], x_ref[...].astype(BF16), NT) + br_ref[...]
    z = jnp.exp(logits - jnp.max(logits, axis=0, keepdims=True))
    probs = z / jnp.sum(z, axis=0, keepdims=True)
    p = [probs[e:e + 1, :] for e in range(N_EXPERTS)]

    def top2_mask(vals):
        out = []
        for a_ in range(len(vals)):
            rank = None
            for j in range(len(vals)):
                if j == a_:
                    continue
                ahead = (vals[j] >= vals[a_]) if j < a_ else (vals[j] > vals[a_])
                ahead = ahead.astype(jnp.int32)
                rank = ahead if rank is None else rank + ahead
            out.append(rank < 2)
        return out

    sel, score = [], []
    for gi in range(N_GROUPS):
        grp = p[gi * EXPERTS_PER_GROUP:(gi + 1) * EXPERTS_PER_GROUP]
        msk = top2_mask(grp)
        sel += msk
        sc = None
        for a_ in range(EXPERTS_PER_GROUP):
            t = jnp.where(msk[a_], grp[a_], 0.0)
            sc = t if sc is None else sc + t
        score.append(sc)
    best = []
    for gi in range(N_GROUPS):
        ok = None
        for j in range(N_GROUPS):
            if j == gi:
                continue
            c_ = (score[gi] > score[j]) if j < gi else (score[gi] >= score[j])
            ok = c_ if ok is None else (ok & c_)
        best.append(ok)
    active = [sel[e] & best[e // EXPERTS_PER_GROUP] for e in range(N_EXPERTS)]
    tot = None
    for e in range(N_EXPERTS):
        t = jnp.where(active[e], p[e], 0.0)
        tot = t if tot is None else tot + t
    gates = [jnp.where(active[e], p[e] / tot, 0.0) for e in range(N_EXPERTS)]

    tm = x_ref.shape[0]
    onehot = jnp.concatenate([b_.astype(F32) for b_ in best] + [jnp.zeros((SUBLANES - N_GROUPS, tm), F32)], axis=0)
    oh_b = jnp.concatenate([onehot, jnp.zeros((LANES - SUBLANES, tm), F32)], axis=0).astype(BF16)
    incl = _dg(onehot.astype(BF16), tri_ref[...])
    slot = None
    offset = jnp.zeros((1, 1), F32)
    for gi in range(N_GROUPS):
        t = onehot[gi:gi + 1] * (offset + incl[gi:gi + 1] - 1.0)
        slot = t if slot is None else slot + t
        offset = offset + incl[gi:gi + 1, tm - 1:tm]
    slot_ref[0] = slot.astype(jnp.int32)
    cnt_ref[0] = _dg(jnp.ones((SUBLANES, tm), BF16), oh_b, NT)[0:1]
    rows = jnp.concatenate(gates + [slot, jnp.zeros((LANES - N_EXPERTS - 1, tm), F32)], axis=0)
    out = None
    for part in _split(rows, 3):
        t = _dg(eye_ref[...], part, NT)
        out = t if out is None else out + t
    meta_ref[...] = out


META_SLOT = N_EXPERTS


def _router(x2d, w_router, b_router, *, tm):
    n = x2d.shape[0]
    idx = np.arange(tm)
    eye = jnp.asarray(idx[:, None] == idx[None, :], dtype=BF16)
    tri = jnp.asarray(idx[:, None] <= idx[None, :], dtype=BF16)
    nt = n // tm
    return pl.pallas_call(
        _router_body, grid=(nt,),
        in_specs=[pl.BlockSpec((tm, D_MODEL), lambda i: (i, 0)), _full((N_EXPERTS, D_MODEL)),
                  _full((N_EXPERTS, 1)), _full((tm, tm)), _full((tm, tm))],
        out_specs=[pl.BlockSpec((tm, LANES), lambda i: (i, 0)), pl.BlockSpec((1, 1, tm), lambda i: (i, 0, 0)),
                   pl.BlockSpec((1, 1, LANES), lambda i: (i, 0, 0))],
        out_shape=[jax.ShapeDtypeStruct((n, LANES), F32), jax.ShapeDtypeStruct((nt, 1, tm), jnp.int32),
                   jax.ShapeDtypeStruct((nt, 1, LANES), F32)],
        compiler_params=_params(("arbitrary",)), name="router",
    )(x2d, w_router.T.astype(BF16), b_router.reshape(-1, 1), eye, tri)


def _moe_ln_body(x_ref, gates_ref, wg_ref, wu_ref, wd_ref, g_ref, b_ref, o_ref, acc_ref, xb_ref):
    gi = pl.program_id(1)

    @pl.when(gi == 0)
    def _():
        xb_ref[...] = x_ref[...].astype(BF16)
        acc_ref[...] = jnp.zeros_like(acc_ref)

    xb = xb_ref[...]
    lane = lax.broadcasted_iota(jnp.int32, gates_ref.shape, 1)
    hs = []
    for j in range(EXPERTS_PER_GROUP):
        hg = _dg(xb, wg_ref[0, j])
        hu = _dg(xb, wu_ref[0, j])
        gate = jnp.sum(jnp.where(lane == gi * EXPERTS_PER_GROUP + j, gates_ref[...], 0.0), axis=1, keepdims=True)
        hs.append((hg * _sigmoid(hg) * hu * gate).astype(BF16))
    acc_ref[...] += _dg(jnp.concatenate(hs, axis=1), wd_ref[0])

    @pl.when(gi == N_GROUPS - 1)
    def _():
        o_ref[...] = _layer_norm(ALPHA * x_ref[...] + acc_ref[...], g_ref[...], b_ref[...])


def _moe_ln(x2d, gates, wg, wu, wd, ln_g, ln_b, *, tm):
    n = x2d.shape[0]
    tok = pl.BlockSpec((tm, D_MODEL), lambda i, e: (i, 0))
    wspec = pl.BlockSpec((1, EXPERTS_PER_GROUP, D_MODEL, D_FF), lambda i, e: (e, 0, 0, 0))
    return pl.pallas_call(
        _moe_ln_body, grid=(n // tm, N_GROUPS),
        in_specs=[tok, pl.BlockSpec((tm, LANES), lambda i, e: (i, 0)), wspec, wspec,
                  pl.BlockSpec((1, EXPERTS_PER_GROUP * D_FF, D_MODEL), lambda i, e: (e, 0, 0)),
                  pl.BlockSpec((1, D_MODEL), lambda i, e: (0, 0)), pl.BlockSpec((1, D_MODEL), lambda i, e: (0, 0))],
        out_specs=tok, out_shape=jax.ShapeDtypeStruct((n, D_MODEL), F32),
        scratch_shapes=[pltpu.VMEM((tm, D_MODEL), F32), pltpu.VMEM((tm, D_MODEL), BF16)],
        compiler_params=_params(("arbitrary", "arbitrary")), name="moe_ln",
    )(x2d, gates, wg, wu, wd, ln_g.reshape(1, -1), ln_b.reshape(1, -1))


PACK = 16


def _moe_sorted_body(start_ref, nwin_ref, x_ref, meta_ref, slot_ref, wg_ref, wu_ref, wd_ref, g_ref, b_ref, o_ref,
                     xs_ref, gs_ref, ys_ref, *, tm, wn):
    i = pl.program_id(0)

    @pl.when(i == 0)
    def _():
        xs_ref[tm:, :] = jnp.zeros((wn, D_MODEL), BF16)
        gs_ref[tm:, :] = jnp.zeros((wn, LANES), F32)

    x = x_ref[...]
    meta = meta_ref[...]
    slot_iota = lax.broadcasted_iota(jnp.int32, (tm, tm), 0)
    perm = jnp.where(slot_iota == slot_ref[0], 1.0, 0.0).astype(BF16)
    xs_ref[:tm, :] = _dg(perm, x.astype(BF16)).astype(BF16)
    gs = None
    for part in _split(meta, 3):
        t = _dg(perm, part)
        gs = t if gs is None else gs + t
    gs_ref[:tm, :] = gs
    ys_ref[...] = jnp.zeros_like(ys_ref)

    for gi in range(N_GROUPS):
        first = start_ref[i * N_GROUPS + gi]

        def window(w, carry, gi=gi, first=first):
            st = pl.multiple_of(first + w * wn, PACK)
            xw = xs_ref[pl.ds(st, wn), :]
            gw = gs_ref[pl.ds(st, wn), :]
            hs = []
            for j in range(EXPERTS_PER_GROUP):
                e = gi * EXPERTS_PER_GROUP + j
                hg = _dg(xw, wg_ref[gi, j])
                hu = _dg(xw, wu_ref[gi, j])
                hs.append((hg * _sigmoid(hg) * hu * gw[:, e:e + 1]).astype(BF16))
            ys_ref[pl.ds(st, wn), :] += _dg(jnp.concatenate(hs, axis=1), wd_ref[gi])
            return carry

        lax.fori_loop(0, nwin_ref[i * N_GROUPS + gi], window, 0)

    tok_slot = meta[:, META_SLOT:META_SLOT + 1]
    lane_slot = lax.broadcasted_iota(jnp.int32, (tm, tm), 1).astype(F32)
    unperm = jnp.where(tok_slot == lane_slot, 1.0, 0.0).astype(BF16)
    y_hi, y_lo = _split(ys_ref[:tm, :], 2)
    y = _dg(unperm, y_hi) + _dg(unperm, y_lo)
    o_ref[...] = _layer_norm(ALPHA * x + y, g_ref[...], b_ref[...])


def _moe_sorted_ln(x2d, meta, slot_rows, counts, wg, wu, wd, ln_g, ln_b, *, tm, wn):
    n = x2d.shape[0]
    nt = n // tm
    assert wn % PACK == 0 and tm % PACK == 0
    cnt = counts[:, 0, :N_GROUPS].astype(jnp.int32)
    offs = jnp.cumsum(cnt, axis=1) - cnt
    first = (offs // PACK) * PACK
    nwin = jnp.where(cnt > 0, (offs - first + cnt + wn - 1) // wn, 0)
    tok = lambda w: pl.BlockSpec((tm, w), lambda i, *_: (i, 0))
    resident = lambda a: pl.BlockSpec(a.shape, lambda i, *_: (0,) * a.ndim, pipeline_mode=pl.Buffered(1))
    vec = pl.BlockSpec((1, D_MODEL), lambda i, *_: (0, 0))
    grid_spec = pltpu.PrefetchScalarGridSpec(
        num_scalar_prefetch=2, grid=(nt,),
        in_specs=[tok(D_MODEL), tok(LANES), pl.BlockSpec((1, 1, tm), lambda i, *_: (i, 0, 0)),
                  resident(wg), resident(wu), resident(wd), vec, vec],
        out_specs=tok(D_MODEL),
        scratch_shapes=[pltpu.VMEM((tm + wn, D_MODEL), BF16), pltpu.VMEM((tm + wn, LANES), F32),
                        pltpu.VMEM((tm + wn, D_MODEL), F32)])
    return pl.pallas_call(
        functools.partial(_moe_sorted_body, tm=tm, wn=wn), grid_spec=grid_spec,
        out_shape=jax.ShapeDtypeStruct((n, D_MODEL), F32),
        compiler_params=_params(("arbitrary",)), name="moe_sorted_ln",
    )(first.reshape(-1), nwin.reshape(-1), x2d, meta, slot_rows, wg, wu, wd, ln_g.reshape(1, -1), ln_b.reshape(1, -1))


HALO = 16


def _pool_ln_body(x_ref, halo_ref, wp_ref, ps_ref, g_ref, b_ref, o_ref, *, tm, seq_len, start_pos):
    i = pl.program_id(0)
    x = x_ref[...]
    idx = lax.broadcasted_iota(jnp.int32, (tm + HALO, 1), 0)
    pos_e = (i * tm - HALO + idx) & (seq_len - 1)
    cnt_pos = (pos_e[HALO:] + (start_pos + 1)).astype(F32)
    mixed = []
    for gi, w in enumerate(POOL_WINDOWS):
        cols = slice(gi * POOL_GROUP, (gi + 1) * POOL_GROUP)
        xg = x[:, cols]
        s = jnp.concatenate([halo_ref[:, cols], xg], axis=0)
        step = 1
        while step < w:
            s = s + jnp.where(pos_e >= step, pltpu.roll(s, step, axis=0), 0.0)
            step *= 2
        pooled = s[HALO:] / jnp.minimum(float(w), cnt_pos) - xg
        mixed.append(_bdot(pooled, wp_ref[gi]))
    h = jnp.concatenate(mixed, axis=1) * ps_ref[...]
    o_ref[...] = _layer_norm(ALPHA * x + h, g_ref[...], b_ref[...])


def _pool_ln(x2d, w_pool, pool_scale, ln_g, ln_b, *, tm, seq_len, start_pos):
    n = x2d.shape[0]
    assert seq_len & (seq_len - 1) == 0 and seq_len >= HALO and tm % HALO == 0
    tok = pl.BlockSpec((tm, D_MODEL), lambda i: (i, 0))
    return pl.pallas_call(
        functools.partial(_pool_ln_body, tm=tm, seq_len=seq_len, start_pos=start_pos), grid=(n // tm,),
        in_specs=[tok, pl.BlockSpec((HALO, D_MODEL), lambda i: (jnp.maximum(i * (tm // HALO) - 1, 0), 0)),
                  _full(w_pool.shape), _full((1, D_MODEL)), _full((1, D_MODEL)), _full((1, D_MODEL))],
        out_specs=tok, out_shape=jax.ShapeDtypeStruct((n, D_MODEL), F32),
        compiler_params=_params(("arbitrary",)), name="pool_ln",
    )(x2d, x2d, w_pool.astype(BF16), pool_scale.reshape(1, -1), ln_g.reshape(1, -1), ln_b.reshape(1, -1))


def _pad_lora_cols(w, width):
    return jnp.pad(w, ((0, 0), (0, LORA_PAD - width)))


def _even_layer_weights(prm, i):
    w_in = prm['w_in'][i]
    o = 3 * D_A
    cols = lambda w: jnp.concatenate(
        [w[:, :o], _pad_lora_cols(w[:, o:o + W_LORA], W_LORA),
         _pad_lora_cols(w[:, o + W_LORA:o + W_LORA + A_LORA], A_LORA),
         _pad_lora_cols(w[:, o + W_LORA + A_LORA:RWKV_COLS], G_LORA)], axis=1)
    pad_rows = lambda w: jnp.pad(w, ((0, LORA_PAD - w.shape[0]), (0, 0))).astype(BF16)
    vec = lambda a: a.reshape(1, -1)
    return dict(
        wr=cols(w_in[:, :RWKV_COLS]).astype(BF16), ws=w_in[:, RWKV_COLS:].astype(BF16),
        mu=cols(prm['tshift_mu'][i].reshape(1, -1)),
        w0=vec(prm['decay_w0'][i]), ww2=pad_rows(prm['decay_w2'][i]),
        a0=vec(prm['iclr_a0'][i]), wa2=pad_rows(prm['iclr_a2'][i]), wg2=pad_rows(prm['gate_w2'][i]),
        k_k=vec(prm['k_k'][i]), k_a=vec(prm['k_a'][i]), r_k=vec(prm['r_k'][i]),
    )


def _group_experts(w):
    return w.astype(BF16).reshape(DEPTH, N_GROUPS, EXPERTS_PER_GROUP, D_MODEL, D_FF)


MOE_SORT_TILE = 512
MOE_WINDOW = 160


def _moe_block(x2d, prm, layer, *, tm):
    n = x2d.shape[0]
    w = (prm['wg'][layer], prm['wu'][layer], prm['wd'][layer], prm['ln_ffn_g'][layer], prm['ln_ffn_b'][layer])
    if n % MOE_SORT_TILE == 0:
        meta, slot_rows, counts = _router(x2d, prm['w_router'], prm['b_router'], tm=MOE_SORT_TILE)
        return _moe_sorted_ln(x2d, meta, slot_rows, counts, *w, tm=MOE_SORT_TILE, wn=MOE_WINDOW)
    meta, _, _ = _router(x2d, prm['w_router'], prm['b_router'], tm=min(tm, 512))
    return _moe_ln(x2d, meta, *w, tm=tm)


PROMPT_CHUNKS_PER_STEP = 4
SAMPLE_CHUNK = 16
SAMPLE_QROWS = SUBLANES


def _prompt_trunk(x, prm):
    bsz, t, _ = x.shape
    n = bsz * t
    x0 = x.reshape(n, D_MODEL)
    wts = _even_layer_weights(prm, 0)
    rt, kt, bt, at, v, bonus, g, gl, swa = _inproj_prep(x0, None, wts, seq_len=t, chunk=CHUNK, tm=256)
    s0 = jnp.zeros((bsz, N_HEADS, HEAD, HEAD), F32)
    ya, s_new = _rwkv_scan(rt, kt, bt, at, v, bonus, g, gl, prm['lnx_g'][0].reshape(1, -1),
                           prm['lnx_b'][0].reshape(1, -1), s0, n_seq=bsz, seq_len=t, chunk=CHUNK,
                           nc=PROMPT_CHUNKS_PER_STEP)
    n_blk = t // WINDOW
    yb = _swa(swa, swa, lambda b, j: (b * n_blk + jnp.maximum(j - 1, 0), 2), prm['rel_bias'],
              prm['attn_sinks'][0], n_seq=bsz, n_blk=n_blk, qr=WINDOW, fresh=True)
    x1 = _outproj_ln(ya, yb, x0, prm['w_out'][0], prm['ln_mix_g'][0], prm['ln_mix_b'][0], tm=min(512, n))
    x2 = _moe_block(x1, prm, 0, tm=min(1024, n))
    x3 = _pool_ln(x2, prm['w_pool'][0], prm['pool_scale'][0], prm['ln_mix_g'][1], prm['ln_mix_b'][1],
                  tm=min(512, n), seq_len=t, start_pos=0)
    x4 = _moe_block(x3, prm, 1, tm=min(1024, n))
    swa3 = swa.reshape(bsz, t, SWA_COLS)
    k_new = swa3[:, t - WINDOW:, D_B:D_B + KV_COLS].reshape(bsz, WINDOW, KV_HEADS, HEAD)
    v_new = swa3[:, t - WINDOW:, D_B + KV_COLS:].reshape(bsz, WINDOW, KV_HEADS, HEAD)
    pool_new = x2.reshape(bsz, t, D_MODEL)[:, t - POOL_KEEP:]
    return (x4.reshape(bsz, t, D_MODEL), s_new[None], x[:, -1][None], k_new[None], v_new[None], pool_new[None])


def _sample_trunk(x, st_rwkv, st_shift, c_k, c_v, st_pool, prm):
    bsz = x.shape[0]
    x0 = x.reshape(bsz, D_MODEL)
    wts = _even_layer_weights(prm, 0)
    outs = _inproj_prep(x0, st_shift[0], wts, seq_len=1, chunk=1, tm=bsz)
    gl, swa = outs[7], outs[8]
    sc = SAMPLE_CHUNK
    padded = [jnp.zeros((bsz, sc, D_A), F32).at[:, 0].set(a).reshape(bsz * sc, D_A) for a in outs[:7]]
    ya, s_new = _rwkv_scan(*padded, gl.reshape(bsz, 1, D_A), prm['lnx_g'][0].reshape(1, -1),
                           prm['lnx_b'][0].reshape(1, -1), st_rwkv[0], n_seq=bsz, seq_len=sc, chunk=sc, nc=1)
    ya = ya.reshape(bsz, sc, D_A)[:, 0]
    qr = SAMPLE_QROWS
    kv_prev = jnp.concatenate([c_k[0].reshape(bsz * WINDOW, KV_COLS), c_v[0].reshape(bsz * WINDOW, KV_COLS)], axis=-1)
    cur = jnp.zeros((bsz, qr, SWA_COLS), F32).at[:, 0].set(swa).reshape(bsz * qr, SWA_COLS)
    yb = _swa(cur, kv_prev, lambda b, j: (b, 0), prm['rel_bias'], prm['attn_sinks'][0], n_seq=bsz, n_blk=1,
              qr=qr, fresh=False)
    yb = yb.reshape(bsz, qr, D_B)[:, 0]
    x1 = _outproj_ln(ya, yb, x0, prm['w_out'][0], prm['ln_mix_g'][0], prm['ln_mix_b'][0], tm=bsz)
    x2 = _moe_block(x1, prm, 0, tm=bsz)
    xcat = jnp.concatenate([st_pool[0], x2[:, None]], axis=1)
    x3 = _pool_ln(xcat.reshape(bsz * HALO, D_MODEL), prm['w_pool'][0], prm['pool_scale'][0],
                  prm['ln_mix_g'][1], prm['ln_mix_b'][1], tm=min(512, bsz * HALO), seq_len=HALO,
                  start_pos=PAST_LEN - POOL_KEEP)
    x3 = x3.reshape(bsz, HALO, D_MODEL)[:, -1]
    x4 = _moe_block(x3, prm, 1, tm=bsz)
    k_new = jnp.concatenate([c_k[0][:, 1:], swa[:, D_B:D_B + KV_COLS].reshape(bsz, 1, KV_HEADS, HEAD)], axis=1)
    v_new = jnp.concatenate([c_v[0][:, 1:], swa[:, D_B + KV_COLS:].reshape(bsz, 1, KV_HEADS, HEAD)], axis=1)
    return (x4.reshape(bsz, 1, D_MODEL), s_new[None], x[:, -1][None], k_new[None], v_new[None],
            xcat[:, 1:][None])


def _prepare_params(raw):
    prm = dict(raw)
    prm['r_k'] = raw['r_k'].reshape(raw['r_k'].shape[0], -1)
    prm['wg'] = _group_experts(raw['w_ex_gate'])
    prm['wu'] = _group_experts(raw['w_ex_up'])
    prm['wd'] = raw['w_ex_down'].astype(BF16).reshape(DEPTH, N_GROUPS, EXPERTS_PER_GROUP * D_FF, D_MODEL)
    return prm


def kernel(x_prompt, x_sample, state_rwkv, state_shift, cache_swa_k, cache_swa_v, state_pool, w_in, tshift_mu,
           decay_w0, decay_w2, iclr_a0, iclr_a2, gate_w2, k_k, k_a, r_k, lnx_g, lnx_b, attn_sinks, rel_bias, w_out,
           w_pool, pool_scale, ln_mix_g, ln_mix_b, ln_ffn_g, ln_ffn_b, w_router, b_router, w_ex_gate, w_ex_up,
           w_ex_down):
    prm = _prepare_params(dict(
        w_in=w_in, tshift_mu=tshift_mu, decay_w0=decay_w0, decay_w2=decay_w2, iclr_a0=iclr_a0, iclr_a2=iclr_a2,
        gate_w2=gate_w2, k_k=k_k, k_a=k_a, r_k=r_k, lnx_g=lnx_g, lnx_b=lnx_b, attn_sinks=attn_sinks,
        rel_bias=rel_bias, w_out=w_out, w_pool=w_pool, pool_scale=pool_scale, ln_mix_g=ln_mix_g, ln_mix_b=ln_mix_b,
        ln_ffn_g=ln_ffn_g, ln_ffn_b=ln_ffn_b, w_router=w_router, b_router=b_router, w_ex_gate=w_ex_gate,
        w_ex_up=w_ex_up, w_ex_down=w_ex_down))
    y_p, rwkv_p, shift_p, k_p, v_p, pool_p = _prompt_trunk(x_prompt, prm)
    y_s, rwkv_s, shift_s, k_s, v_s, pool_s = _sample_trunk(x_sample, state_rwkv, state_shift, cache_swa_k,
                                                            cache_swa_v, state_pool, prm)
    return (y_p, y_s, rwkv_p, rwkv_s, shift_p, shift_s, k_p, k_s, v_p, v_s, pool_p, pool_s)
```

```python
import functools
import math

import numpy as np
import jax
import jax.numpy as jnp
from jax import lax
from jax.experimental import pallas as pl
from jax.experimental.pallas import tpu as pltpu

F32 = jnp.float32
BF16 = jnp.bfloat16

D_MODEL = 1024
DEPTH = 2
PAST_LEN = 16384
D_A = 512
HEAD = 64
N_HEADS = 8
W_LORA, A_LORA, G_LORA = 32, 64, 96
RWKV_COLS = 3 * D_A + W_LORA + A_LORA + G_LORA
D_B = 512
KV_HEADS = 2
Q_PER_KV = 4
KV_COLS = KV_HEADS * HEAD
SWA_COLS = D_B + 2 * KV_COLS
WINDOW = 128
N_BUCKETS = 32
MAX_EXACT = 16
REL_MAX_DIST = 128
POOL_WINDOWS = (2, 4, 8, 16)
POOL_GROUP = 256
POOL_KEEP = 15
N_EXPERTS = 16
N_GROUPS = 4
EXPERTS_PER_GROUP = 4
D_FF = 256
ALPHA = (2.0 * DEPTH) ** 0.25
LN_EPS = 1e-5
LNX_EPS = 64e-5
NEG_INF = -1e30

LANES = 128
SUBLANES = 8
VMEM_LIMIT_BYTES = 56 * 1024 * 1024

LORA_PAD = LANES
RWKV_PAD_COLS = 3 * D_A + 3 * LORA_PAD
CHUNK = 64

NN = ((1,), (0,))
NT = ((1,), (1,))
TN = ((0,), (0,))


def _dg(a, b, dims=NN):
    return lax.dot_general(a, b, (dims, ((), ())), preferred_element_type=F32)


def _bdot(a, b, dims=NN):
    return _dg(a.astype(BF16), b.astype(BF16), dims)


def _split(x, n):
    parts, rem = [], x
    for i in range(n):
        p = rem.astype(BF16)
        parts.append(p)
        if i + 1 < n:
            rem = rem - p.astype(F32)
    return parts


def _mm(a, b, dims=NN, passes=3):
    if passes == 1:
        return _bdot(a, b, dims)
    ah, al = _split(a, 2)
    bh, bl = _split(b, 2)
    return _dg(ah, bh, dims) + (_dg(ah, bl, dims) + _dg(al, bh, dims))


def _dot_exact_rhs(x, ones_bf16, n=3):
    out = None
    for p in _split(x, n):
        t = _dg(p, ones_bf16)
        out = t if out is None else out + t
    return out


def _dot_exact_lhs(ones_bf16, x, n=3):
    out = None
    for p in _split(x, n):
        t = _dg(ones_bf16, p)
        out = t if out is None else out + t
    return out


def _sigmoid(x):
    return 1.0 / (1.0 + jnp.exp(-x))


def _layer_norm(z, g, b):
    mu = jnp.mean(z, axis=-1, keepdims=True)
    d = z - mu
    var = jnp.mean(d * d, axis=-1, keepdims=True)
    return d * lax.rsqrt(var + LN_EPS) * g + b


def _params(sem):
    return pltpu.CompilerParams(dimension_semantics=sem, vmem_limit_bytes=VMEM_LIMIT_BYTES)


def _full(shape):
    nd = len(shape)
    return pl.BlockSpec(shape, lambda *_: (0,) * nd)


def _inproj_prep_body(*refs, tm, tiles_per_seq, chunk, has_prev):
    if has_prev:
        x_ref, xp_ref = refs[0], refs[1]
        refs = refs[2:]
    else:
        x_ref, xp_ref = refs[0], None
        refs = refs[1:]
    (wr_ref, ws_ref, mu_ref, w0_ref, ww2_ref, a0_ref, wa2_ref, wg2_ref, kk_ref, ka_ref, rk_ref,
     ones_ref, tri_ref,
     rt_ref, kt_ref, bt_ref, at_ref, v_ref, bonus_ref, g_ref, gl_ref, swa_ref, carry_ref) = refs

    i = pl.program_id(0)
    xb = x_ref[...].astype(BF16)
    pr = _dg(xb, wr_ref[...])
    swa_ref[...] = _dg(xb, ws_ref[...])

    if has_prev:
        shifted = _dg(xp_ref[...].astype(BF16), wr_ref[...])
    else:
        @pl.when(i == 0)
        def _():
            carry_ref[...] = jnp.zeros_like(carry_ref)

        first = (i % tiles_per_seq) == 0
        prev_last = jnp.where(first, 0.0, carry_ref[0:1, :])
        rolled = pltpu.roll(pr, 1, axis=0)
        row = lax.broadcasted_iota(jnp.int32, (tm, 1), 0)
        shifted = jnp.where(row == 0, prev_last, rolled)
        carry_ref[0:1, :] = pr[tm - 1:tm, :]
    m = pr + mu_ref[...] * (shifted - pr)

    r = m[:, 0:D_A]
    k = m[:, D_A:2 * D_A]
    v = m[:, 2 * D_A:3 * D_A]
    wd = m[:, 3 * D_A:3 * D_A + LORA_PAD]
    ad = m[:, 3 * D_A + LORA_PAD:3 * D_A + 2 * LORA_PAD]
    gd = m[:, 3 * D_A + 2 * LORA_PAD:3 * D_A + 3 * LORA_PAD]

    nz = -(w0_ref[...] + _bdot(jnp.tanh(wd), ww2_ref[...]))
    softplus = jnp.maximum(nz, 0.0) + jnp.log1p(jnp.exp(-jnp.abs(nz)))
    w_log = -softplus - 0.5
    logw = -jnp.exp(w_log)
    a = _sigmoid(a0_ref[...] + _bdot(ad, wa2_ref[...]))
    g = _bdot(_sigmoid(gd), wg2_ref[...])

    ones_bd = ones_ref[...]
    half = D_A // 2

    def head_sums(z):
        return jnp.concatenate([_dot_exact_rhs(z[:, :half], ones_bd, 2), _dot_exact_rhs(z[:, half:], ones_bd, 2)],
                               axis=1)

    kkr = k * kk_ref[...]
    nrm = jnp.sqrt(head_sums(kkr * kkr))
    kk = kkr / jnp.maximum(nrm, 1e-12)
    k2 = k * (1.0 + (a - 1.0) * ka_ref[...])
    bonus_ref[...] = head_sums(r * k2 * rk_ref[...]) * v

    if chunk > 1:
        cum = _dot_exact_lhs(tri_ref[...], logw)
    else:
        cum = logw
    gam = jnp.exp(cum)
    inv = jnp.exp(-cum)
    rt_ref[...] = r * gam
    kt_ref[...] = k2 * inv
    bt_ref[...] = kk * a * inv
    at_ref[...] = -kk * jnp.exp(cum - logw)
    v_ref[...] = v
    g_ref[...] = g
    if chunk > 1:
        for c in range(tm // chunk):
            gl_ref[c] = gam[(c + 1) * chunk - 1:(c + 1) * chunk, :]
    else:
        gl_ref[...] = gam


def _inproj_prep(x2d, x_prev, wts, *, seq_len, chunk, tm):
    n = x2d.shape[0]
    assert n % tm == 0 and (x_prev is not None or seq_len % tm == 0)
    assert tm % chunk == 0
    has_prev = x_prev is not None
    row = lambda w: pl.BlockSpec((tm, w), lambda i: (i, 0))
    ins = [x2d] + ([x_prev] if has_prev else [])
    in_specs = [row(D_MODEL)] + ([row(D_MODEL)] if has_prev else [])
    consts = [wts['wr'], wts['ws'], wts['mu'], wts['w0'], wts['ww2'], wts['a0'], wts['wa2'], wts['wg2'],
              wts['k_k'], wts['k_a'], wts['r_k'], _ones_block_diag(D_A // 2, HEAD), _tri_block_diag(tm, chunk)]
    ins += consts
    in_specs += [_full(c.shape) for c in consts]
    if chunk > 1:
        gl_shape = jax.ShapeDtypeStruct((n // chunk, 1, D_A), F32)
        gl_spec = pl.BlockSpec((tm // chunk, 1, D_A), lambda i: (i, 0, 0))
    else:
        gl_shape = jax.ShapeDtypeStruct((n, D_A), F32)
        gl_spec = row(D_A)
    out_shape = [jax.ShapeDtypeStruct((n, D_A), F32)] * 7 + [gl_shape, jax.ShapeDtypeStruct((n, SWA_COLS), F32)]
    out_specs = [row(D_A)] * 7 + [gl_spec, row(SWA_COLS)]
    body = functools.partial(_inproj_prep_body, tm=tm, tiles_per_seq=max(seq_len // tm, 1), chunk=chunk,
                             has_prev=has_prev)
    return pl.pallas_call(
        body, grid=(n // tm,), in_specs=in_specs, out_specs=out_specs, out_shape=out_shape,
        scratch_shapes=[pltpu.VMEM((SUBLANES, RWKV_PAD_COLS), F32)],
        compiler_params=_params(("arbitrary",)), name="inproj_prep",
    )(*ins)


def _ones_block_diag(n, blk):
    idx = np.arange(n) // blk
    return jnp.asarray(idx[:, None] == idx[None, :], dtype=BF16)


def _tri_block_diag(n, blk):
    idx = np.arange(n)
    same = (idx[:, None] // blk) == (idx[None, :] // blk)
    return jnp.asarray(same & (idx[:, None] >= idx[None, :]), dtype=BF16)


PAIR = 2 * HEAD
N_PAIRS = N_HEADS // 2


def _scan_body(rt_ref, kt_ref, bt_ref, at_ref, v_ref, bonus_ref, g_ref, gl_ref, lng_ref, lnb_ref, s0_ref,
               y_ref, sout_ref, S_ref, *, C, nc, one_chunk_seqs):
    step = pl.program_id(1)

    def pair_state(ref, s, p):
        return jnp.concatenate([ref[s, 2 * p], ref[s, 2 * p + 1]], axis=1)

    if not one_chunk_seqs:
        @pl.when(step == 0)
        def _():
            for p in range(N_PAIRS):
                S_ref[p] = pair_state(s0_ref, 0, p)

    head_a = lax.broadcasted_iota(jnp.int32, (1, PAIR), 1) < HEAD
    rowc = lax.broadcasted_iota(jnp.int32, (C, 2 * C), 0)
    colc = lax.broadcasted_iota(jnp.int32, (C, 2 * C), 1)
    first_c = colc < C
    col_in = jnp.where(first_c, colc, colc - C)
    strict = rowc > col_in
    incl = rowc >= col_in
    eye_c = (rowc == col_in).astype(F32)
    r128 = lax.broadcasted_iota(jnp.int32, (PAIR, PAIR), 0)
    c128 = lax.broadcasted_iota(jnp.int32, (PAIR, PAIR), 1)
    same_head = (r128 < HEAD) == (c128 < HEAD)
    eye128 = (r128 == c128).astype(F32)

    def by_head(x):
        xb = x.astype(BF16)
        z = jnp.zeros_like(xb)
        return jnp.concatenate([jnp.where(head_a, xb, z), jnp.where(head_a, z, xb)], axis=0)

    def blocks(x):
        xb = x.astype(BF16)
        z = jnp.zeros_like(xb)
        return jnp.concatenate([jnp.where(first_c, xb, z), jnp.where(first_c, z, xb)], axis=0)

    def head_mean(z):
        za = jnp.sum(jnp.where(head_a, z, 0.0), axis=-1, keepdims=True)
        zb = jnp.sum(jnp.where(head_a, 0.0, z), axis=-1, keepdims=True)
        return jnp.where(head_a, za, zb) * (1.0 / HEAD)

    n_fold = int(math.log2(C)) - 1
    zero_blk = jnp.zeros((2 * C, PAIR), BF16)
    items = [(ci, p) for ci in range(nc) for p in range(N_PAIRS)]
    each = lambda f, *lists: [f(*xs) for xs in zip(*lists)]

    def load(ref):
        return [ref[ci * C:(ci + 1) * C, p * PAIR:(p + 1) * PAIR] for ci, p in items]

    R, K, Bm, A, V = load(rt_ref), load(kt_ref), load(bt_ref), load(at_ref), load(v_ref)
    big = each(lambda a, r, b, k: _dg(jnp.concatenate([a, r], axis=0).astype(BF16),
                                      jnp.concatenate([by_head(b), by_head(k)], axis=0), NT), A, R, Bm, K)
    a_ab = [jnp.where(strict, x[:C, :2 * C], 0.0) for x in big]
    a_ak = [jnp.where(strict, x[:C, 2 * C:], 0.0) for x in big]
    l_rb = [jnp.where(incl, x[C:, :2 * C], 0.0) for x in big]
    l_rk = [jnp.where(incl, x[C:, 2 * C:], 0.0) for x in big]
    X = [_dg(x.astype(BF16), blocks(x)) for x in a_ab]
    T = [eye_c + x for x in a_ab]
    for k in range(1, n_fold + 1):
        if k < n_fold:
            res = each(lambda x, t: _dg(jnp.concatenate([x, t], axis=0).astype(BF16), blocks(x)), X, T)
            X = [r_[:C] for r_ in res]
            T = each(lambda t, r_: t + r_[C:], T, res)
        else:
            T = each(lambda t, x: t + _dg(t.astype(BF16), blocks(x)), T, X)
    akv = each(lambda m_, v_: _dg(m_.astype(BF16), by_head(v_)), a_ak, V)
    pq = each(lambda t, a, q: _dg(t.astype(BF16), jnp.concatenate([by_head(a), by_head(q)], axis=1)), T, A, akv)
    P = [x[:, :PAIR] for x in pq]
    Q = [x[:, PAIR:] for x in pq]
    wz = each(lambda lb, lk, p_, q_, v_: _dg(
        jnp.concatenate([lb, lk], axis=1).astype(BF16),
        jnp.concatenate([jnp.concatenate([by_head(p_), by_head(q_)], axis=1),
                         jnp.concatenate([zero_blk, by_head(v_)], axis=1)], axis=0)), l_rb, l_rk, P, Q, V)
    W = each(lambda r, x: r + x[:, :PAIR], R, wz)
    Z = [x[:, PAIR:] for x in wz]
    ptb = each(lambda p_, b: _bdot(p_, b, TN), P, Bm)
    gfull = each(lambda q_, v_, b, k: _bdot(jnp.concatenate([q_, v_], axis=0), jnp.concatenate([b, k], axis=0), TN),
                 Q, V, Bm, K)

    def put_state(s, p, val):
        sout_ref[s, 2 * p] = val[:, :HEAD]
        sout_ref[s, 2 * p + 1] = val[:, HEAD:]

    state = [None if one_chunk_seqs else S_ref[p] for p in range(N_PAIRS)]
    for idx, (ci, p) in enumerate(items):
        rows = slice(ci * C, (ci + 1) * C)
        ln = slice(p * PAIR, (p + 1) * PAIR)
        g_c = gl_ref[ci, :, ln]
        M = (eye128 + jnp.where(same_head, ptb[idx], 0.0)) * g_c
        G = jnp.where(head_a, gfull[idx][:HEAD], gfull[idx][HEAD:]) * g_c
        S0 = pair_state(s0_ref, ci, p) if one_chunk_seqs else state[p]
        Y = _dg(W[idx].astype(BF16), by_head(S0), NT) + Z[idx]
        S1 = _bdot(S0, M) + G
        if one_chunk_seqs:
            put_state(ci, p, S1)
        else:
            state[p] = S1

        mu = head_mean(Y)
        d = Y - mu
        var = head_mean(d * d)
        yn = d * lax.rsqrt(var + LNX_EPS) * lng_ref[:, ln] + lnb_ref[:, ln]
        y_ref[rows, ln] = (yn + bonus_ref[rows, ln]) * g_ref[rows, ln]

    if not one_chunk_seqs:
        for p in range(N_PAIRS):
            S_ref[p] = state[p]

        @pl.when(step == pl.num_programs(1) - 1)
        def _():
            for p in range(N_PAIRS):
                put_state(0, p, state[p])


def _rwkv_scan(rt, kt, bt, at, v, bonus, g, gl, lnx_g, lnx_b, s0, *, n_seq, seq_len, chunk, nc):
    one_chunk_seqs = seq_len == chunk
    rows = chunk * nc
    if one_chunk_seqs:
        assert n_seq % nc == 0
        grid = (n_seq // nc, 1)
        blk = lambda b, c: b
        st = pl.BlockSpec((nc, N_HEADS, HEAD, HEAD), lambda b, c: (b, 0, 0, 0))
    else:
        assert seq_len % rows == 0
        nsteps = seq_len // rows
        grid = (n_seq, nsteps)
        blk = lambda b, c: b * nsteps + c
        st = pl.BlockSpec((1, N_HEADS, HEAD, HEAD), lambda b, c: (b, 0, 0, 0))
    row = pl.BlockSpec((rows, D_A), lambda b, c: (blk(b, c), 0))
    vec = pl.BlockSpec((1, D_A), lambda b, c: (0, 0))
    return pl.pallas_call(
        functools.partial(_scan_body, C=chunk, nc=nc, one_chunk_seqs=one_chunk_seqs),
        grid=grid,
        in_specs=[row] * 7 + [pl.BlockSpec((nc, 1, D_A), lambda b, c: (blk(b, c), 0, 0)), vec, vec, st],
        out_specs=[row, st],
        out_shape=[jax.ShapeDtypeStruct((n_seq * seq_len, D_A), F32),
                   jax.ShapeDtypeStruct((n_seq, N_HEADS, HEAD, HEAD), F32)],
        scratch_shapes=[pltpu.VMEM((N_PAIRS, HEAD, PAIR), F32)],
        compiler_params=_params(("arbitrary", "arbitrary")), name="rwkv_scan",
    )(rt, kt, bt, at, v, bonus, g, gl, lnx_g, lnx_b, s0)


def _bucket_ranges():
    d = np.arange(WINDOW + 1)
    scaled = np.log(np.maximum(d, MAX_EXACT).astype(np.float32) / MAX_EXACT) / math.log(REL_MAX_DIST / MAX_EXACT)
    large = np.minimum(MAX_EXACT + (scaled * (N_BUCKETS - MAX_EXACT)).astype(np.int32), N_BUCKETS - 1)
    bucket = np.where(d < MAX_EXACT, d, large)
    frac = scaled.astype(np.float64) * (N_BUCKETS - MAX_EXACT)
    near = np.abs(frac - np.round(frac)) < 1e-3
    assert all(int(x) in (MAX_EXACT, REL_MAX_DIST) for x in d[(d >= MAX_EXACT) & near])
    ranges = []
    for b in range(N_BUCKETS):
        hit = d[bucket == b]
        ranges.append((int(hit.min()), int(hit.max())) if hit.size else None)
    return tuple(ranges)


def _swa_body(rb_ref, sink_ref, cur_ref, prev_ref, y_ref, bias_ref, sinkcol_ref, *, qr, fresh, per_step, ranges):
    W = WINDOW
    nk = W + qr

    @pl.when((pl.program_id(0) == 0) & (pl.program_id(1) == 0))
    def _():
        qi = lax.broadcasted_iota(jnp.int32, (qr, nk), 0)
        kj = lax.broadcasted_iota(jnp.int32, (qr, nk), 1)
        d = qi + W - kj
        valid = (d >= 0) & (d <= W)
        for hd in range(N_HEADS):
            t = jnp.zeros((qr, nk), F32)
            for bkt, rng in enumerate(ranges):
                if rng is not None:
                    t = jnp.where((d >= rng[0]) & (d <= rng[1]), rb_ref[bkt * N_HEADS + hd], t)
            kvh, grp = hd // Q_PER_KV, hd % Q_PER_KV
            rows = slice(grp * qr, (grp + 1) * qr)
            bias_ref[kvh, rows, :] = jnp.where(valid, t, NEG_INF)
            bias_ref[KV_HEADS + kvh, rows, :] = jnp.where(valid & (kj >= W), t, NEG_INF)
            sinkcol_ref[kvh, rows, :] = jnp.full((qr, 1), sink_ref[hd], F32)

    table = jnp.where(pl.program_id(1) == 0, KV_HEADS, 0) if fresh else 0
    items = [(sub, kvh) for sub in range(per_step) for kvh in range(KV_HEADS)]
    each = lambda f, *lists: [f(*xs) for xs in zip(*lists)]

    def kv_rows(sub, col):
        return jnp.concatenate([prev_ref[sub * W:(sub + 1) * W, col - D_B:col - D_B + HEAD],
                                cur_ref[sub * qr:(sub + 1) * qr, col:col + HEAD]], axis=0).astype(BF16)

    kb = [kv_rows(sub, D_B + kvh * HEAD) for sub, kvh in items]
    vb = [kv_rows(sub, D_B + KV_COLS + kvh * HEAD) for sub, kvh in items]
    q4 = [(jnp.concatenate([cur_ref[sub * qr:(sub + 1) * qr, (kvh * Q_PER_KV + gq) * HEAD:(kvh * Q_PER_KV + gq + 1) * HEAD]
                            for gq in range(Q_PER_KV)], axis=0) * (HEAD ** -0.5)).astype(BF16)
          for sub, kvh in items]
    s = [_dg(q, k, NT) + bias_ref[table + kvh] for q, k, (_, kvh) in zip(q4, kb, items)]
    sink = [sinkcol_ref[kvh] for _, kvh in items]
    m = each(lambda s_, sk: jnp.maximum(jnp.max(s_, axis=-1, keepdims=True), sk), s, sink)
    p = each(lambda s_, m_: jnp.exp(s_ - m_), s, m)
    den = each(lambda p_, sk, m_: jnp.sum(p_, axis=-1, keepdims=True) + jnp.exp(sk - m_), p, sink, m)
    o = each(lambda p_, v_, d_: _dg(p_.astype(BF16), v_) / d_, p, vb, den)
    for sub in range(per_step):
        outs = [o[sub * KV_HEADS + kvh][gq * qr:(gq + 1) * qr] for kvh in range(KV_HEADS) for gq in range(Q_PER_KV)]
        y_ref[sub * qr:(sub + 1) * qr, :] = jnp.concatenate(outs, axis=1)


def _swa(cur2d, prev2d, prev_map, rel_bias, sinks, *, n_seq, n_blk, qr, fresh, per_step=1):
    assert per_step == 1 or (n_blk == 1 and not fresh and n_seq % per_step == 0)
    smem = pl.BlockSpec(memory_space=pltpu.SMEM)
    n_seq //= per_step
    qrows = per_step * qr
    return pl.pallas_call(
        functools.partial(_swa_body, qr=qr, fresh=fresh, per_step=per_step, ranges=_bucket_ranges()),
        grid=(n_seq, n_blk),
        in_specs=[smem, smem,
                  pl.BlockSpec((qrows, SWA_COLS), lambda b, j: (b * n_blk + j, 0)),
                  pl.BlockSpec((per_step * WINDOW, 2 * KV_COLS), prev_map)],
        out_specs=pl.BlockSpec((qrows, D_B), lambda b, j: (b * n_blk + j, 0)),
        out_shape=jax.ShapeDtypeStruct((n_seq * n_blk * qrows, D_B), F32),
        scratch_shapes=[pltpu.VMEM((2 * KV_HEADS, Q_PER_KV * qr, WINDOW + qr), F32),
                        pltpu.VMEM((KV_HEADS, Q_PER_KV * qr, 1), F32)],
        compiler_params=_params(("arbitrary", "arbitrary")), name="swa",
    )(rel_bias.reshape(-1), sinks, cur2d, prev2d)


def _outproj_ln_body(ya_ref, yb_ref, x_ref, wa_ref, wb_ref, g_ref, b_ref, o_ref):
    h = _dg(ya_ref[...].astype(BF16), wa_ref[...]) + _dg(yb_ref[...].astype(BF16), wb_ref[...])
    o_ref[...] = _layer_norm(ALPHA * x_ref[...] + h, g_ref[...], b_ref[...])


def _outproj_ln(ya, yb, x2d, w_out, ln_g, ln_b, *, tm):
    n = x2d.shape[0]
    half = pl.BlockSpec((tm, D_A), lambda i: (i, 0))
    full = pl.BlockSpec((tm, D_MODEL), lambda i: (i, 0))
    wa, wb = w_out[:D_A].astype(BF16), w_out[D_A:].astype(BF16)
    return pl.pallas_call(
        _outproj_ln_body, grid=(n // tm,),
        in_specs=[half, half, full, _full(wa.shape), _full(wb.shape), _full((1, D_MODEL)), _full((1, D_MODEL))],
        out_specs=full, out_shape=jax.ShapeDtypeStruct((n, D_MODEL), F32),
        compiler_params=_params(("arbitrary",)), name="outproj_ln",
    )(ya, yb, x2d, wa, wb, ln_g.reshape(1, -1), ln_b.reshape(1, -1))


def _router_body(x_ref, wrt_ref, br_ref, eye_ref, tri_ref, meta_ref, slot_ref, cnt_ref):
    logits = _dg(wrt_ref[...], x_ref[...].astype(BF16), NT) + br_ref[...]
    z = jnp.exp(logits - jnp.max(logits, axis=0, keepdims=True))
    probs = z / jnp.sum(z, axis=0, keepdims=True)
    p = [probs[e:e + 1, :] for e in range(N_EXPERTS)]

    def top2_mask(vals):
        out = []
        for a_ in range(len(vals)):
            rank = None
            for j in range(len(vals)):
                if j == a_:
                    continue
                ahead = (vals[j] >= vals[a_]) if j < a_ else (vals[j] > vals[a_])
                ahead = ahead.astype(jnp.int32)
                rank = ahead if rank is None else rank + ahead
            out.append(rank < 2)
        return out

    sel, score = [], []
    for gi in range(N_GROUPS):
        grp = p[gi * EXPERTS_PER_GROUP:(gi + 1) * EXPERTS_PER_GROUP]
        msk = top2_mask(grp)
        sel += msk
        sc = None
        for a_ in range(EXPERTS_PER_GROUP):
            t = jnp.where(msk[a_], grp[a_], 0.0)
            sc = t if sc is None else sc + t
        score.append(sc)
    best = []
    for gi in range(N_GROUPS):
        ok = None
        for j in range(N_GROUPS):
            if j == gi:
                continue
            c_ = (score[gi] > score[j]) if j < gi else (score[gi] >= score[j])
            ok = c_ if ok is None else (ok & c_)
        best.append(ok)
    active = [sel[e] & best[e // EXPERTS_PER_GROUP] for e in range(N_EXPERTS)]
    tot = None
    for e in range(N_EXPERTS):
        t = jnp.where(active[e], p[e], 0.0)
        tot = t if tot is None else tot + t
    gates = [jnp.where(active[e], p[e] / tot, 0.0) for e in range(N_EXPERTS)]

    tm = x_ref.shape[0]
    onehot = jnp.concatenate([b_.astype(F32) for b_ in best] + [jnp.zeros((SUBLANES - N_GROUPS, tm), F32)], axis=0)
    oh_b = jnp.concatenate([onehot, jnp.zeros((LANES - SUBLANES, tm), F32)], axis=0).astype(BF16)
    incl = _dg(onehot.astype(BF16), tri_ref[...])
    slot = None
    offset = jnp.zeros((1, 1), F32)
    for gi in range(N_GROUPS):
        t = onehot[gi:gi + 1] * (offset + incl[gi:gi + 1] - 1.0)
        slot = t if slot is None else slot + t
        offset = offset + incl[gi:gi + 1, tm - 1:tm]
    slot_ref[0] = slot.astype(jnp.int32)
    cnt_ref[0] = _dg(jnp.ones((SUBLANES, tm), BF16), oh_b, NT)[0:1]
    rows = jnp.concatenate(gates + [slot, jnp.zeros((LANES - N_EXPERTS - 1, tm), F32)], axis=0)
    out = None
    for part in _split(rows, 3):
        t = _dg(eye_ref[...], part, NT)
        out = t if out is None else out + t
    meta_ref[...] = out


META_SLOT = N_EXPERTS


def _router(x2d, w_router, b_router, *, tm):
    n = x2d.shape[0]
    idx = np.arange(tm)
    eye = jnp.asarray(idx[:, None] == idx[None, :], dtype=BF16)
    tri = jnp.asarray(idx[:, None] <= idx[None, :], dtype=BF16)
    nt = n // tm
    return pl.pallas_call(
        _router_body, grid=(nt,),
        in_specs=[pl.BlockSpec((tm, D_MODEL), lambda i: (i, 0)), _full((N_EXPERTS, D_MODEL)),
                  _full((N_EXPERTS, 1)), _full((tm, tm)), _full((tm, tm))],
        out_specs=[pl.BlockSpec((tm, LANES), lambda i: (i, 0)), pl.BlockSpec((1, 1, tm), lambda i: (i, 0, 0)),
                   pl.BlockSpec((1, 1, LANES), lambda i: (i, 0, 0))],
        out_shape=[jax.ShapeDtypeStruct((n, LANES), F32), jax.ShapeDtypeStruct((nt, 1, tm), jnp.int32),
                   jax.ShapeDtypeStruct((nt, 1, LANES), F32)],
        compiler_params=_params(("arbitrary",)), name="router",
    )(x2d, w_router.T.astype(BF16), b_router.reshape(-1, 1), eye, tri)


def _moe_ln_body(x_ref, gates_ref, wg_ref, wu_ref, wd_ref, g_ref, b_ref, o_ref, acc_ref, xb_ref):
    gi = pl.program_id(1)

    @pl.when(gi == 0)
    def _():
        xb_ref[...] = x_ref[...].astype(BF16)
        acc_ref[...] = jnp.zeros_like(acc_ref)

    xb = xb_ref[...]
    lane = lax.broadcasted_iota(jnp.int32, gates_ref.shape, 1)
    hs = []
    for j in range(EXPERTS_PER_GROUP):
        hg = _dg(xb, wg_ref[0, j])
        hu = _dg(xb, wu_ref[0, j])
        gate = jnp.sum(jnp.where(lane == gi * EXPERTS_PER_GROUP + j, gates_ref[...], 0.0), axis=1, keepdims=True)
        hs.append((hg * _sigmoid(hg) * hu * gate).astype(BF16))
    acc_ref[...] += _dg(jnp.concatenate(hs, axis=1), wd_ref[0])

    @pl.when(gi == N_GROUPS - 1)
    def _():
        o_ref[...] = _layer_norm(ALPHA * x_ref[...] + acc_ref[...], g_ref[...], b_ref[...])


def _moe_ln(x2d, gates, wg, wu, wd, ln_g, ln_b, *, tm):
    n = x2d.shape[0]
    tok = pl.BlockSpec((tm, D_MODEL), lambda i, e: (i, 0))
    wspec = pl.BlockSpec((1, EXPERTS_PER_GROUP, D_MODEL, D_FF), lambda i, e: (e, 0, 0, 0))
    return pl.pallas_call(
        _moe_ln_body, grid=(n // tm, N_GROUPS),
        in_specs=[tok, pl.BlockSpec((tm, LANES), lambda i, e: (i, 0)), wspec, wspec,
                  pl.BlockSpec((1, EXPERTS_PER_GROUP * D_FF, D_MODEL), lambda i, e: (e, 0, 0)),
                  pl.BlockSpec((1, D_MODEL), lambda i, e: (0, 0)), pl.BlockSpec((1, D_MODEL), lambda i, e: (0, 0))],
        out_specs=tok, out_shape=jax.ShapeDtypeStruct((n, D_MODEL), F32),
        scratch_shapes=[pltpu.VMEM((tm, D_MODEL), F32), pltpu.VMEM((tm, D_MODEL), BF16)],
        compiler_params=_params(("arbitrary", "arbitrary")), name="moe_ln",
    )(x2d, gates, wg, wu, wd, ln_g.reshape(1, -1), ln_b.reshape(1, -1))


PACK = 16


def _moe_sorted_body(start_ref, nwin_ref, x_ref, meta_ref, slot_ref, wg_ref, wu_ref, wd_ref, g_ref, b_ref, o_ref,
                     xs_ref, gs_ref, ys_ref, *, tm, wn):
    i = pl.program_id(0)

    @pl.when(i == 0)
    def _():
        xs_ref[tm:, :] = jnp.zeros((wn, D_MODEL), BF16)
        gs_ref[tm:, :] = jnp.zeros((wn, LANES), F32)

    x = x_ref[...]
    meta = meta_ref[...]
    slot_iota = lax.broadcasted_iota(jnp.int32, (tm, tm), 0)
    perm = jnp.where(slot_iota == slot_ref[0], 1.0, 0.0).astype(BF16)
    xs_ref[:tm, :] = _dg(perm, x.astype(BF16)).astype(BF16)
    gs = None
    for part in _split(meta, 3):
        t = _dg(perm, part)
        gs = t if gs is None else gs + t
    gs_ref[:tm, :] = gs
    ys_ref[...] = jnp.zeros_like(ys_ref)

    for gi in range(N_GROUPS):
        first = start_ref[i * N_GROUPS + gi]

        def window(w, carry, gi=gi, first=first):
            st = pl.multiple_of(first + w * wn, PACK)
            xw = xs_ref[pl.ds(st, wn), :]
            gw = gs_ref[pl.ds(st, wn), :]
            hs = []
            for j in range(EXPERTS_PER_GROUP):
                e = gi * EXPERTS_PER_GROUP + j
                hg = _dg(xw, wg_ref[gi, j])
                hu = _dg(xw, wu_ref[gi, j])
                hs.append((hg * _sigmoid(hg) * hu * gw[:, e:e + 1]).astype(BF16))
            ys_ref[pl.ds(st, wn), :] += _dg(jnp.concatenate(hs, axis=1), wd_ref[gi])
            return carry

        lax.fori_loop(0, nwin_ref[i * N_GROUPS + gi], window, 0)

    tok_slot = meta[:, META_SLOT:META_SLOT + 1]
    lane_slot = lax.broadcasted_iota(jnp.int32, (tm, tm), 1).astype(F32)
    unperm = jnp.where(tok_slot == lane_slot, 1.0, 0.0).astype(BF16)
    y_hi, y_lo = _split(ys_ref[:tm, :], 2)
    y = _dg(unperm, y_hi) + _dg(unperm, y_lo)
    o_ref[...] = _layer_norm(ALPHA * x + y, g_ref[...], b_ref[...])


def _moe_sorted_ln(x2d, meta, slot_rows, counts, wg, wu, wd, ln_g, ln_b, *, tm, wn):
    n = x2d.shape[0]
    nt = n // tm
    assert wn % PACK == 0 and tm % PACK == 0
    cnt = counts[:, 0, :N_GROUPS].astype(jnp.int32)
    offs = jnp.cumsum(cnt, axis=1) - cnt
    first = (offs // PACK) * PACK
    nwin = jnp.where(cnt > 0, (offs - first + cnt + wn - 1) // wn, 0)
    tok = lambda w: pl.BlockSpec((tm, w), lambda i, *_: (i, 0))
    resident = lambda a: pl.BlockSpec(a.shape, lambda i, *_: (0,) * a.ndim, pipeline_mode=pl.Buffered(1))
    vec = pl.BlockSpec((1, D_MODEL), lambda i, *_: (0, 0))
    grid_spec = pltpu.PrefetchScalarGridSpec(
        num_scalar_prefetch=2, grid=(nt,),
        in_specs=[tok(D_MODEL), tok(LANES), pl.BlockSpec((1, 1, tm), lambda i, *_: (i, 0, 0)),
                  resident(wg), resident(wu), resident(wd), vec, vec],
        out_specs=tok(D_MODEL),
        scratch_shapes=[pltpu.VMEM((tm + wn, D_MODEL), BF16), pltpu.VMEM((tm + wn, LANES), F32),
                        pltpu.VMEM((tm + wn, D_MODEL), F32)])
    return pl.pallas_call(
        functools.partial(_moe_sorted_body, tm=tm, wn=wn), grid_spec=grid_spec,
        out_shape=jax.ShapeDtypeStruct((n, D_MODEL), F32),
        compiler_params=_params(("arbitrary",)), name="moe_sorted_ln",
    )(first.reshape(-1), nwin.reshape(-1), x2d, meta, slot_rows, wg, wu, wd, ln_g.reshape(1, -1), ln_b.reshape(1, -1))


HALO = 16


def _pool_ln_body(x_ref, halo_ref, wp_ref, ps_ref, g_ref, b_ref, o_ref, *, tm, seq_len, start_pos):
    i = pl.program_id(0)
    x = x_ref[...]
    idx = lax.broadcasted_iota(jnp.int32, (tm + HALO, 1), 0)
    pos_e = (i * tm - HALO + idx) & (seq_len - 1)
    cnt_pos = (pos_e[HALO:] + (start_pos + 1)).astype(F32)
    mixed = []
    for gi, w in enumerate(POOL_WINDOWS):
        cols = slice(gi * POOL_GROUP, (gi + 1) * POOL_GROUP)
        xg = x[:, cols]
        s = jnp.concatenate([halo_ref[:, cols], xg], axis=0)
        step = 1
        while step < w:
            s = s + jnp.where(pos_e >= step, pltpu.roll(s, step, axis=0), 0.0)
            step *= 2
        pooled = s[HALO:] / jnp.minimum(float(w), cnt_pos) - xg
        mixed.append(_bdot(pooled, wp_ref[gi]))
    h = jnp.concatenate(mixed, axis=1) * ps_ref[...]
    o_ref[...] = _layer_norm(ALPHA * x + h, g_ref[...], b_ref[...])


def _pool_ln(x2d, w_pool, pool_scale, ln_g, ln_b, *, tm, seq_len, start_pos):
    n = x2d.shape[0]
    assert seq_len & (seq_len - 1) == 0 and seq_len >= HALO and tm % HALO == 0
    tok = pl.BlockSpec((tm, D_MODEL), lambda i: (i, 0))
    return pl.pallas_call(
        functools.partial(_pool_ln_body, tm=tm, seq_len=seq_len, start_pos=start_pos), grid=(n // tm,),
        in_specs=[tok, pl.BlockSpec((HALO, D_MODEL), lambda i: (jnp.maximum(i * (tm // HALO) - 1, 0), 0)),
                  _full(w_pool.shape), _full((1, D_MODEL)), _full((1, D_MODEL)), _full((1, D_MODEL))],
        out_specs=tok, out_shape=jax.ShapeDtypeStruct((n, D_MODEL), F32),
        compiler_params=_params(("arbitrary",)), name="pool_ln",
    )(x2d, x2d, w_pool.astype(BF16), pool_scale.reshape(1, -1), ln_g.reshape(1, -1), ln_b.reshape(1, -1))


def _pad_lora_cols(w, width):
    return jnp.pad(w, ((0, 0), (0, LORA_PAD - width)))


def _even_layer_weights(prm, i):
    w_in = prm['w_in'][i]
    o = 3 * D_A
    cols = lambda w: jnp.concatenate(
        [w[:, :o], _pad_lora_cols(w[:, o:o + W_LORA], W_LORA),
         _pad_lora_cols(w[:, o + W_LORA:o + W_LORA + A_LORA], A_LORA),
         _pad_lora_cols(w[:, o + W_LORA + A_LORA:RWKV_COLS], G_LORA)], axis=1)
    pad_rows = lambda w: jnp.pad(w, ((0, LORA_PAD - w.shape[0]), (0, 0))).astype(BF16)
    vec = lambda a: a.reshape(1, -1)
    return dict(
        wr=cols(w_in[:, :RWKV_COLS]).astype(BF16), ws=w_in[:, RWKV_COLS:].astype(BF16),
        mu=cols(prm['tshift_mu'][i].reshape(1, -1)),
        w0=vec(prm['decay_w0'][i]), ww2=pad_rows(prm['decay_w2'][i]),
        a0=vec(prm['iclr_a0'][i]), wa2=pad_rows(prm['iclr_a2'][i]), wg2=pad_rows(prm['gate_w2'][i]),
        k_k=vec(prm['k_k'][i]), k_a=vec(prm['k_a'][i]), r_k=vec(prm['r_k'][i]),
    )


def _group_experts(w):
    return w.astype(BF16).reshape(DEPTH, N_GROUPS, EXPERTS_PER_GROUP, D_MODEL, D_FF)


MOE_SORT_TILE = 512
MOE_WINDOW = 160


def _moe_block(x2d, prm, layer, *, tm):
    n = x2d.shape[0]
    w = (prm['wg'][layer], prm['wu'][layer], prm['wd'][layer], prm['ln_ffn_g'][layer], prm['ln_ffn_b'][layer])
    if n % MOE_SORT_TILE == 0:
        meta, slot_rows, counts = _router(x2d, prm['w_router'], prm['b_router'], tm=MOE_SORT_TILE)
        return _moe_sorted_ln(x2d, meta, slot_rows, counts, *w, tm=MOE_SORT_TILE, wn=MOE_WINDOW)
    meta, _, _ = _router(x2d, prm['w_router'], prm['b_router'], tm=min(tm, 512))
    return _moe_ln(x2d, meta, *w, tm=tm)


PROMPT_CHUNKS_PER_STEP = 4
SAMPLE_CHUNK = 16
SAMPLE_QROWS = SUBLANES
SAMPLE_SEQS_PER_STEP = 8


def _prompt_trunk(x, prm):
    bsz, t, _ = x.shape
    n = bsz * t
    x0 = x.reshape(n, D_MODEL)
    wts = _even_layer_weights(prm, 0)
    rt, kt, bt, at, v, bonus, g, gl, swa = _inproj_prep(x0, None, wts, seq_len=t, chunk=CHUNK, tm=256)
    s0 = jnp.zeros((bsz, N_HEADS, HEAD, HEAD), F32)
    ya, s_new = _rwkv_scan(rt, kt, bt, at, v, bonus, g, gl, prm['lnx_g'][0].reshape(1, -1),
                           prm['lnx_b'][0].reshape(1, -1), s0, n_seq=bsz, seq_len=t, chunk=CHUNK,
                           nc=PROMPT_CHUNKS_PER_STEP)
    n_blk = t // WINDOW
    yb = _swa(swa, swa, lambda b, j: (b * n_blk + jnp.maximum(j - 1, 0), 2), prm['rel_bias'],
              prm['attn_sinks'][0], n_seq=bsz, n_blk=n_blk, qr=WINDOW, fresh=True)
    x1 = _outproj_ln(ya, yb, x0, prm['w_out'][0], prm['ln_mix_g'][0], prm['ln_mix_b'][0], tm=min(512, n))
    x2 = _moe_block(x1, prm, 0, tm=min(1024, n))
    x3 = _pool_ln(x2, prm['w_pool'][0], prm['pool_scale'][0], prm['ln_mix_g'][1], prm['ln_mix_b'][1],
                  tm=min(512, n), seq_len=t, start_pos=0)
    x4 = _moe_block(x3, prm, 1, tm=min(1024, n))
    swa3 = swa.reshape(bsz, t, SWA_COLS)
    k_new = swa3[:, t - WINDOW:, D_B:D_B + KV_COLS].reshape(bsz, WINDOW, KV_HEADS, HEAD)
    v_new = swa3[:, t - WINDOW:, D_B + KV_COLS:].reshape(bsz, WINDOW, KV_HEADS, HEAD)
    pool_new = x2.reshape(bsz, t, D_MODEL)[:, t - POOL_KEEP:]
    return (x4.reshape(bsz, t, D_MODEL), s_new[None], x[:, -1][None], k_new[None], v_new[None], pool_new[None])


def _sample_trunk(x, st_rwkv, st_shift, c_k, c_v, st_pool, prm):
    bsz = x.shape[0]
    x0 = x.reshape(bsz, D_MODEL)
    wts = _even_layer_weights(prm, 0)
    outs = _inproj_prep(x0, st_shift[0], wts, seq_len=1, chunk=1, tm=bsz)
    gl, swa = outs[7], outs[8]
    sc = SAMPLE_CHUNK
    padded = [jnp.zeros((bsz, sc, D_A), F32).at[:, 0].set(a).reshape(bsz * sc, D_A) for a in outs[:7]]
    ya, s_new = _rwkv_scan(*padded, gl.reshape(bsz, 1, D_A), prm['lnx_g'][0].reshape(1, -1),
                           prm['lnx_b'][0].reshape(1, -1), st_rwkv[0], n_seq=bsz, seq_len=sc, chunk=sc,
                           nc=SAMPLE_SEQS_PER_STEP)
    ya = ya.reshape(bsz, sc, D_A)[:, 0]
    qr = SAMPLE_QROWS
    kv_prev = jnp.concatenate([c_k[0].reshape(bsz * WINDOW, KV_COLS), c_v[0].reshape(bsz * WINDOW, KV_COLS)], axis=-1)
    cur = jnp.zeros((bsz, qr, SWA_COLS), F32).at[:, 0].set(swa).reshape(bsz * qr, SWA_COLS)
    yb = _swa(cur, kv_prev, lambda b, j: (b, 0), prm['rel_bias'], prm['attn_sinks'][0], n_seq=bsz, n_blk=1,
              qr=qr, fresh=False, per_step=SAMPLE_SEQS_PER_STEP)
    yb = yb.reshape(bsz, qr, D_B)[:, 0]
    x1 = _outproj_ln(ya, yb, x0, prm['w_out'][0], prm['ln_mix_g'][0], prm['ln_mix_b'][0], tm=bsz)
    x2 = _moe_block(x1, prm, 0, tm=bsz)
    xcat = jnp.concatenate([st_pool[0], x2[:, None]], axis=1)
    x3 = _pool_ln(xcat.reshape(bsz * HALO, D_MODEL), prm['w_pool'][0], prm['pool_scale'][0],
                  prm['ln_mix_g'][1], prm['ln_mix_b'][1], tm=min(512, bsz * HALO), seq_len=HALO,
                  start_pos=PAST_LEN - POOL_KEEP)
    x3 = x3.reshape(bsz, HALO, D_MODEL)[:, -1]
    x4 = _moe_block(x3, prm, 1, tm=bsz)
    k_new = jnp.concatenate([c_k[0][:, 1:], swa[:, D_B:D_B + KV_COLS].reshape(bsz, 1, KV_HEADS, HEAD)], axis=1)
    v_new = jnp.concatenate([c_v[0][:, 1:], swa[:, D_B + KV_COLS:].reshape(bsz, 1, KV_HEADS, HEAD)], axis=1)
    return (x4.reshape(bsz, 1, D_MODEL), s_new[None], x[:, -1][None], k_new[None], v_new[None],
            xcat[:, 1:][None])


def _prepare_params(raw):
    prm = dict(raw)
    prm['r_k'] = raw['r_k'].reshape(raw['r_k'].shape[0], -1)
    prm['wg'] = _group_experts(raw['w_ex_gate'])
    prm['wu'] = _group_experts(raw['w_ex_up'])
    prm['wd'] = raw['w_ex_down'].astype(BF16).reshape(DEPTH, N_GROUPS, EXPERTS_PER_GROUP * D_FF, D_MODEL)
    return prm


def kernel(x_prompt, x_sample, state_rwkv, state_shift, cache_swa_k, cache_swa_v, state_pool, w_in, tshift_mu,
           decay_w0, decay_w2, iclr_a0, iclr_a2, gate_w2, k_k, k_a, r_k, lnx_g, lnx_b, attn_sinks, rel_bias, w_out,
           w_pool, pool_scale, ln_mix_g, ln_mix_b, ln_ffn_g, ln_ffn_b, w_router, b_router, w_ex_gate, w_ex_up,
           w_ex_down):
    prm = _prepare_params(dict(
        w_in=w_in, tshift_mu=tshift_mu, decay_w0=decay_w0, decay_w2=decay_w2, iclr_a0=iclr_a0, iclr_a2=iclr_a2,
        gate_w2=gate_w2, k_k=k_k, k_a=k_a, r_k=r_k, lnx_g=lnx_g, lnx_b=lnx_b, attn_sinks=attn_sinks,
        rel_bias=rel_bias, w_out=w_out, w_pool=w_pool, pool_scale=pool_scale, ln_mix_g=ln_mix_g, ln_mix_b=ln_mix_b,
        ln_ffn_g=ln_ffn_g, ln_ffn_b=ln_ffn_b, w_router=w_router, b_router=b_router, w_ex_gate=w_ex_gate,
        w_ex_up=w_ex_up, w_ex_down=w_ex_down))
    y_p, rwkv_p, shift_p, k_p, v_p, pool_p = _prompt_trunk(x_prompt, prm)
    y_s, rwkv_s, shift_s, k_s, v_s, pool_s = _sample_trunk(x_sample, state_rwkv, state_shift, cache_swa_k,
                                                            cache_swa_v, state_pool, prm)
    return (y_p, y_s, rwkv_p, rwkv_s, shift_p, shift_s, k_p, k_s, v_p, v_s, pool_p, pool_s)
```

```python
import functools
import math

import numpy as np
import jax
import jax.numpy as jnp
from jax import lax
from jax.experimental import pallas as pl
from jax.experimental.pallas import tpu as pltpu

F32 = jnp.float32
BF16 = jnp.bfloat16

D_MODEL = 1024
DEPTH = 2
PAST_LEN = 16384
D_A = 512
HEAD = 64
N_HEADS = 8
W_LORA, A_LORA, G_LORA = 32, 64, 96
RWKV_COLS = 3 * D_A + W_LORA + A_LORA + G_LORA
D_B = 512
KV_HEADS = 2
Q_PER_KV = 4
KV_COLS = KV_HEADS * HEAD
SWA_COLS = D_B + 2 * KV_COLS
WINDOW = 128
N_BUCKETS = 32
MAX_EXACT = 16
REL_MAX_DIST = 128
POOL_WINDOWS = (2, 4, 8, 16)
POOL_GROUP = 256
POOL_KEEP = 15
N_EXPERTS = 16
N_GROUPS = 4
EXPERTS_PER_GROUP = 4
D_FF = 256
ALPHA = (2.0 * DEPTH) ** 0.25
LN_EPS = 1e-5
LNX_EPS = 64e-5
NEG_INF = -1e30

LANES = 128
SUBLANES = 8
PACK = 16
VMEM_LIMIT_BYTES = 56 * 1024 * 1024

LORA_PAD = LANES
RWKV_PAD_COLS = 3 * D_A + 3 * LORA_PAD
CHUNK = 64

NN = ((1,), (0,))
NT = ((1,), (1,))
TN = ((0,), (0,))


def _dg(a, b, dims=NN):
    return lax.dot_general(a, b, (dims, ((), ())), preferred_element_type=F32)


def _bdot(a, b, dims=NN):
    return _dg(a.astype(BF16), b.astype(BF16), dims)


def _split(x, n):
    parts, rem = [], x
    for i in range(n):
        p = rem.astype(BF16)
        parts.append(p)
        if i + 1 < n:
            rem = rem - p.astype(F32)
    return parts


def _mm(a, b, dims=NN, passes=3):
    if passes == 1:
        return _bdot(a, b, dims)
    ah, al = _split(a, 2)
    bh, bl = _split(b, 2)
    return _dg(ah, bh, dims) + (_dg(ah, bl, dims) + _dg(al, bh, dims))


def _dot_exact_rhs(x, ones_bf16, n=3):
    out = None
    for p in _split(x, n):
        t = _dg(p, ones_bf16)
        out = t if out is None else out + t
    return out


def _dot_exact_lhs(ones_bf16, x, n=3):
    out = None
    for p in _split(x, n):
        t = _dg(ones_bf16, p)
        out = t if out is None else out + t
    return out


def _sigmoid(x):
    return 1.0 / (1.0 + jnp.exp(-x))


def _layer_norm(z, g, b):
    mu = jnp.mean(z, axis=-1, keepdims=True)
    d = z - mu
    var = jnp.mean(d * d, axis=-1, keepdims=True)
    return d * lax.rsqrt(var + LN_EPS) * g + b


def _params(sem):
    return pltpu.CompilerParams(dimension_semantics=sem, vmem_limit_bytes=VMEM_LIMIT_BYTES)


def _full(shape):
    nd = len(shape)
    return pl.BlockSpec(shape, lambda *_: (0,) * nd)


def _inproj_prep_body(*refs, tm, tiles_per_seq, chunk, has_prev):
    if has_prev:
        x_ref, xp_ref = refs[0], refs[1]
        refs = refs[2:]
    else:
        x_ref, xp_ref = refs[0], None
        refs = refs[1:]
    (wr_ref, ws_ref, mu_ref, w0_ref, ww2_ref, a0_ref, wa2_ref, wg2_ref, kk_ref, ka_ref, rk_ref,
     ones_ref, tri_ref,
     rt_ref, kt_ref, bt_ref, at_ref, v_ref, bonus_ref, g_ref, gl_ref, swa_ref, carry_ref) = refs

    i = pl.program_id(0)
    xb = x_ref[...].astype(BF16)
    pr = _dg(xb, wr_ref[...])
    swa_ref[...] = _dg(xb, ws_ref[...])

    if has_prev:
        shifted = _dg(xp_ref[...].astype(BF16), wr_ref[...])
    else:
        @pl.when(i == 0)
        def _():
            carry_ref[...] = jnp.zeros_like(carry_ref)

        first = (i % tiles_per_seq) == 0
        prev_last = jnp.where(first, 0.0, carry_ref[0:1, :])
        rolled = pltpu.roll(pr, 1, axis=0)
        row = lax.broadcasted_iota(jnp.int32, (tm, 1), 0)
        shifted = jnp.where(row == 0, prev_last, rolled)
        carry_ref[0:1, :] = pr[tm - 1:tm, :]
    m = pr + mu_ref[...] * (shifted - pr)

    r = m[:, 0:D_A]
    k = m[:, D_A:2 * D_A]
    v = m[:, 2 * D_A:3 * D_A]
    wd = m[:, 3 * D_A:3 * D_A + LORA_PAD]
    ad = m[:, 3 * D_A + LORA_PAD:3 * D_A + 2 * LORA_PAD]
    gd = m[:, 3 * D_A + 2 * LORA_PAD:3 * D_A + 3 * LORA_PAD]

    nz = -(w0_ref[...] + _bdot(jnp.tanh(wd), ww2_ref[...]))
    softplus = jnp.maximum(nz, 0.0) + jnp.log1p(jnp.exp(-jnp.abs(nz)))
    w_log = -softplus - 0.5
    logw = -jnp.exp(w_log)
    a = _sigmoid(a0_ref[...] + _bdot(ad, wa2_ref[...]))
    g = _bdot(_sigmoid(gd), wg2_ref[...])

    ones_bd = ones_ref[...]
    half = D_A // 2

    def head_sums(z):
        return jnp.concatenate([_dot_exact_rhs(z[:, :half], ones_bd, 2), _dot_exact_rhs(z[:, half:], ones_bd, 2)],
                               axis=1)

    kkr = k * kk_ref[...]
    nrm = jnp.sqrt(head_sums(kkr * kkr))
    kk = kkr / jnp.maximum(nrm, 1e-12)
    k2 = k * (1.0 + (a - 1.0) * ka_ref[...])
    bonus_ref[...] = head_sums(r * k2 * rk_ref[...]) * v

    if chunk > 1:
        cum = _dot_exact_lhs(tri_ref[...], logw)
    else:
        cum = logw
    gam = jnp.exp(cum)
    inv = jnp.exp(-cum)
    rt_ref[...] = r * gam
    kt_ref[...] = k2 * inv
    bt_ref[...] = kk * a * inv
    at_ref[...] = -kk * jnp.exp(cum - logw)
    v_ref[...] = v
    g_ref[...] = g
    if chunk > 1:
        for c in range(tm // chunk):
            gl_ref[c] = gam[(c + 1) * chunk - 1:(c + 1) * chunk, :]
    else:
        gl_ref[...] = gam


def _inproj_prep(x2d, x_prev, wts, *, seq_len, chunk, tm):
    n = x2d.shape[0]
    assert n % tm == 0 and (x_prev is not None or seq_len % tm == 0)
    assert tm % chunk == 0
    has_prev = x_prev is not None
    row = lambda w: pl.BlockSpec((tm, w), lambda i: (i, 0))
    ins = [x2d] + ([x_prev] if has_prev else [])
    in_specs = [row(D_MODEL)] + ([row(D_MODEL)] if has_prev else [])
    consts = [wts['wr'], wts['ws'], wts['mu'], wts['w0'], wts['ww2'], wts['a0'], wts['wa2'], wts['wg2'],
              wts['k_k'], wts['k_a'], wts['r_k'], _ones_block_diag(D_A // 2, HEAD), _tri_block_diag(tm, chunk)]
    ins += consts
    in_specs += [_full(c.shape) for c in consts]
    if chunk > 1:
        gl_shape = jax.ShapeDtypeStruct((n // chunk, 1, D_A), F32)
        gl_spec = pl.BlockSpec((tm // chunk, 1, D_A), lambda i: (i, 0, 0))
    else:
        gl_shape = jax.ShapeDtypeStruct((n, D_A), F32)
        gl_spec = row(D_A)
    out_shape = [jax.ShapeDtypeStruct((n, D_A), F32)] * 7 + [gl_shape, jax.ShapeDtypeStruct((n, SWA_COLS), F32)]
    out_specs = [row(D_A)] * 7 + [gl_spec, row(SWA_COLS)]
    body = functools.partial(_inproj_prep_body, tm=tm, tiles_per_seq=max(seq_len // tm, 1), chunk=chunk,
                             has_prev=has_prev)
    return pl.pallas_call(
        body, grid=(n // tm,), in_specs=in_specs, out_specs=out_specs, out_shape=out_shape,
        scratch_shapes=[pltpu.VMEM((SUBLANES, RWKV_PAD_COLS), F32)],
        compiler_params=_params(("arbitrary",)), name="inproj_prep",
    )(*ins)


def _ones_block_diag(n, blk):
    idx = np.arange(n) // blk
    return jnp.asarray(idx[:, None] == idx[None, :], dtype=BF16)


def _tri_block_diag(n, blk):
    idx = np.arange(n)
    same = (idx[:, None] // blk) == (idx[None, :] // blk)
    return jnp.asarray(same & (idx[:, None] >= idx[None, :]), dtype=BF16)


PAIR = 2 * HEAD
N_PAIRS = N_HEADS // 2


def _scan_body(rt_ref, kt_ref, bt_ref, at_ref, v_ref, bonus_ref, g_ref, gl_ref, lng_ref, lnb_ref, s0_ref,
               y_ref, sout_ref, S_ref, *, C, nc, one_chunk_seqs):
    step = pl.program_id(1)

    def pair_state(ref, s, p):
        return jnp.concatenate([ref[s, 2 * p], ref[s, 2 * p + 1]], axis=1)

    if not one_chunk_seqs:
        @pl.when(step == 0)
        def _():
            for p in range(N_PAIRS):
                S_ref[p] = pair_state(s0_ref, 0, p)

    head_a = lax.broadcasted_iota(jnp.int32, (1, PAIR), 1) < HEAD
    rowc = lax.broadcasted_iota(jnp.int32, (C, 2 * C), 0)
    colc = lax.broadcasted_iota(jnp.int32, (C, 2 * C), 1)
    first_c = colc < C
    col_in = jnp.where(first_c, colc, colc - C)
    strict = rowc > col_in
    incl = rowc >= col_in
    eye_c = (rowc == col_in).astype(F32)
    r128 = lax.broadcasted_iota(jnp.int32, (PAIR, PAIR), 0)
    c128 = lax.broadcasted_iota(jnp.int32, (PAIR, PAIR), 1)
    same_head = (r128 < HEAD) == (c128 < HEAD)
    eye128 = (r128 == c128).astype(F32)

    def by_head(x):
        xb = x.astype(BF16)
        z = jnp.zeros_like(xb)
        return jnp.concatenate([jnp.where(head_a, xb, z), jnp.where(head_a, z, xb)], axis=0)

    def blocks(x):
        xb = x.astype(BF16)
        z = jnp.zeros_like(xb)
        return jnp.concatenate([jnp.where(first_c, xb, z), jnp.where(first_c, z, xb)], axis=0)

    def head_mean(z):
        za = jnp.sum(jnp.where(head_a, z, 0.0), axis=-1, keepdims=True)
        zb = jnp.sum(jnp.where(head_a, 0.0, z), axis=-1, keepdims=True)
        return jnp.where(head_a, za, zb) * (1.0 / HEAD)

    n_fold = int(math.log2(C)) - 1
    zero_blk = jnp.zeros((2 * C, PAIR), BF16)
    items = [(ci, p) for ci in range(nc) for p in range(N_PAIRS)]
    each = lambda f, *lists: [f(*xs) for xs in zip(*lists)]

    def load(ref):
        if one_chunk_seqs:
            pad = jnp.zeros((C - 1, PAIR), F32)
            return [jnp.concatenate([ref[ci:ci + 1, p * PAIR:(p + 1) * PAIR], pad], axis=0) for ci, p in items]
        return [ref[ci * C:(ci + 1) * C, p * PAIR:(p + 1) * PAIR] for ci, p in items]

    R, K, Bm, A, V = load(rt_ref), load(kt_ref), load(bt_ref), load(at_ref), load(v_ref)
    big = each(lambda a, r, b, k: _dg(jnp.concatenate([a, r], axis=0).astype(BF16),
                                      jnp.concatenate([by_head(b), by_head(k)], axis=0), NT), A, R, Bm, K)
    a_ab = [jnp.where(strict, x[:C, :2 * C], 0.0) for x in big]
    a_ak = [jnp.where(strict, x[:C, 2 * C:], 0.0) for x in big]
    l_rb = [jnp.where(incl, x[C:, :2 * C], 0.0) for x in big]
    l_rk = [jnp.where(incl, x[C:, 2 * C:], 0.0) for x in big]
    X = [_dg(x.astype(BF16), blocks(x)) for x in a_ab]
    T = [eye_c + x for x in a_ab]
    for k in range(1, n_fold + 1):
        if k < n_fold:
            res = each(lambda x, t: _dg(jnp.concatenate([x, t], axis=0).astype(BF16), blocks(x)), X, T)
            X = [r_[:C] for r_ in res]
            T = each(lambda t, r_: t + r_[C:], T, res)
        else:
            T = each(lambda t, x: t + _dg(t.astype(BF16), blocks(x)), T, X)
    akv = each(lambda m_, v_: _dg(m_.astype(BF16), by_head(v_)), a_ak, V)
    pq = each(lambda t, a, q: _dg(t.astype(BF16), jnp.concatenate([by_head(a), by_head(q)], axis=1)), T, A, akv)
    P = [x[:, :PAIR] for x in pq]
    Q = [x[:, PAIR:] for x in pq]
    wz = each(lambda lb, lk, p_, q_, v_: _dg(
        jnp.concatenate([lb, lk], axis=1).astype(BF16),
        jnp.concatenate([jnp.concatenate([by_head(p_), by_head(q_)], axis=1),
                         jnp.concatenate([zero_blk, by_head(v_)], axis=1)], axis=0)), l_rb, l_rk, P, Q, V)
    W = each(lambda r, x: r + x[:, :PAIR], R, wz)
    Z = [x[:, PAIR:] for x in wz]
    ptb = each(lambda p_, b: _bdot(p_, b, TN), P, Bm)
    gfull = each(lambda q_, v_, b, k: _bdot(jnp.concatenate([q_, v_], axis=0), jnp.concatenate([b, k], axis=0), TN),
                 Q, V, Bm, K)

    def put_state(s, p, val):
        sout_ref[s, 2 * p] = val[:, :HEAD]
        sout_ref[s, 2 * p + 1] = val[:, HEAD:]

    state = [None if one_chunk_seqs else S_ref[p] for p in range(N_PAIRS)]
    for idx, (ci, p) in enumerate(items):
        rows = slice(ci, ci + 1) if one_chunk_seqs else slice(ci * C, (ci + 1) * C)
        ln = slice(p * PAIR, (p + 1) * PAIR)
        g_c = gl_ref[ci, :, ln]
        M = (eye128 + jnp.where(same_head, ptb[idx], 0.0)) * g_c
        G = jnp.where(head_a, gfull[idx][:HEAD], gfull[idx][HEAD:]) * g_c
        S0 = pair_state(s0_ref, ci, p) if one_chunk_seqs else state[p]
        Y = _dg(W[idx].astype(BF16), by_head(S0), NT) + Z[idx]
        S1 = _bdot(S0, M) + G
        if one_chunk_seqs:
            put_state(ci, p, S1)
        else:
            state[p] = S1

        if one_chunk_seqs:
            Y = Y[0:1]
        mu = head_mean(Y)
        d = Y - mu
        var = head_mean(d * d)
        yn = d * lax.rsqrt(var + LNX_EPS) * lng_ref[:, ln] + lnb_ref[:, ln]
        y_ref[rows, ln] = ((yn + bonus_ref[rows, ln]) * g_ref[rows, ln]).astype(y_ref.dtype)

    if not one_chunk_seqs:
        for p in range(N_PAIRS):
            S_ref[p] = state[p]

        @pl.when(step == pl.num_programs(1) - 1)
        def _():
            for p in range(N_PAIRS):
                put_state(0, p, state[p])


def _rwkv_scan(rt, kt, bt, at, v, bonus, g, gl, lnx_g, lnx_b, s0, *, n_seq, seq_len, chunk, nc):
    one_chunk_seqs = seq_len == chunk
    if one_chunk_seqs:
        assert n_seq % nc == 0
        grid = (n_seq // nc, 1)
        blk = lambda b, c: b
        st = pl.BlockSpec((nc, N_HEADS, HEAD, HEAD), lambda b, c: (b, 0, 0, 0))
        rows, n_rows = nc, n_seq
    else:
        rows, n_rows = chunk * nc, n_seq * seq_len
        assert seq_len % rows == 0
        nsteps = seq_len // rows
        grid = (n_seq, nsteps)
        blk = lambda b, c: b * nsteps + c
        st = pl.BlockSpec((1, N_HEADS, HEAD, HEAD), lambda b, c: (b, 0, 0, 0))
    row = pl.BlockSpec((rows, D_A), lambda b, c: (blk(b, c), 0))
    vec = pl.BlockSpec((1, D_A), lambda b, c: (0, 0))
    return pl.pallas_call(
        functools.partial(_scan_body, C=chunk, nc=nc, one_chunk_seqs=one_chunk_seqs),
        grid=grid,
        in_specs=[row] * 7 + [pl.BlockSpec((nc, 1, D_A), lambda b, c: (blk(b, c), 0, 0)), vec, vec, st],
        out_specs=[row, st],
        out_shape=[jax.ShapeDtypeStruct((n_rows, D_A), BF16 if n_rows % PACK == 0 and rows % PACK == 0 else F32),
                   jax.ShapeDtypeStruct((n_seq, N_HEADS, HEAD, HEAD), F32)],
        scratch_shapes=[pltpu.VMEM((N_PAIRS, HEAD, PAIR), F32)],
        compiler_params=_params(("arbitrary", "arbitrary")), name="rwkv_scan",
    )(rt, kt, bt, at, v, bonus, g, gl, lnx_g, lnx_b, s0)


def _bucket_ranges():
    d = np.arange(WINDOW + 1)
    scaled = np.log(np.maximum(d, MAX_EXACT).astype(np.float32) / MAX_EXACT) / math.log(REL_MAX_DIST / MAX_EXACT)
    large = np.minimum(MAX_EXACT + (scaled * (N_BUCKETS - MAX_EXACT)).astype(np.int32), N_BUCKETS - 1)
    bucket = np.where(d < MAX_EXACT, d, large)
    frac = scaled.astype(np.float64) * (N_BUCKETS - MAX_EXACT)
    near = np.abs(frac - np.round(frac)) < 1e-3
    assert all(int(x) in (MAX_EXACT, REL_MAX_DIST) for x in d[(d >= MAX_EXACT) & near])
    ranges = []
    for b in range(N_BUCKETS):
        hit = d[bucket == b]
        ranges.append((int(hit.min()), int(hit.max())) if hit.size else None)
    return tuple(ranges)


def _swa_body(rb_ref, sink_ref, cur_ref, prev_ref, y_ref, bias_ref, sinkcol_ref, *, qr, fresh, per_step, ranges):
    W = WINDOW
    nk = W + qr

    @pl.when((pl.program_id(0) == 0) & (pl.program_id(1) == 0))
    def _():
        qi = lax.broadcasted_iota(jnp.int32, (qr, nk), 0)
        kj = lax.broadcasted_iota(jnp.int32, (qr, nk), 1)
        d = qi + W - kj
        valid = (d >= 0) & (d <= W)
        for hd in range(N_HEADS):
            t = jnp.zeros((qr, nk), F32)
            for bkt, rng in enumerate(ranges):
                if rng is not None:
                    t = jnp.where((d >= rng[0]) & (d <= rng[1]), rb_ref[bkt * N_HEADS + hd], t)
            kvh, grp = hd // Q_PER_KV, hd % Q_PER_KV
            rows = slice(grp * qr, (grp + 1) * qr)
            bias_ref[kvh, rows, :] = jnp.where(valid, t, NEG_INF)
            bias_ref[KV_HEADS + kvh, rows, :] = jnp.where(valid & (kj >= W), t, NEG_INF)
            sinkcol_ref[kvh, rows, :] = jnp.full((qr, 1), sink_ref[hd], F32)

    first_table = jnp.where(pl.program_id(1) == 0, KV_HEADS, 0) if fresh else 0
    items = [(sub, kvh) for sub in range(per_step) for kvh in range(KV_HEADS)]
    each = lambda f, *lists: [f(*xs) for xs in zip(*lists)]

    def own_rows(sub, col):
        if fresh:
            return cur_ref[sub * qr:(sub + 1) * qr, col:col + HEAD]
        return jnp.concatenate([cur_ref[sub:sub + 1, col:col + HEAD], jnp.zeros((qr - 1, HEAD), F32)], axis=0)

    def kv_rows(sub, col):
        if fresh and sub > 0:
            before = cur_ref[(sub - 1) * qr:sub * qr, col:col + HEAD]
        else:
            first = 0 if fresh else sub * W
            before = prev_ref[first:first + W, col - D_B:col - D_B + HEAD]
        return jnp.concatenate([before, own_rows(sub, col)], axis=0).astype(BF16)

    kb = [kv_rows(sub, D_B + kvh * HEAD) for sub, kvh in items]
    vb = [kv_rows(sub, D_B + KV_COLS + kvh * HEAD) for sub, kvh in items]
    q4 = [(jnp.concatenate([own_rows(sub, (kvh * Q_PER_KV + gq) * HEAD) for gq in range(Q_PER_KV)], axis=0)
           * (HEAD ** -0.5)).astype(BF16) for sub, kvh in items]
    s = [_dg(q, k, NT) + bias_ref[(first_table if sub == 0 else 0) + kvh] for q, k, (sub, kvh) in zip(q4, kb, items)]
    sink = [sinkcol_ref[kvh] for _, kvh in items]
    m = each(lambda s_, sk: jnp.maximum(jnp.max(s_, axis=-1, keepdims=True), sk), s, sink)
    p = each(lambda s_, m_: jnp.exp(s_ - m_), s, m)
    den = each(lambda p_, sk, m_: jnp.sum(p_, axis=-1, keepdims=True) + jnp.exp(sk - m_), p, sink, m)
    o = each(lambda p_, v_, d_: _dg(p_.astype(BF16), v_) / d_, p, vb, den)
    for sub in range(per_step):
        outs = [o[sub * KV_HEADS + kvh][gq * qr:(gq + 1) * qr] for kvh in range(KV_HEADS) for gq in range(Q_PER_KV)]
        y = jnp.concatenate(outs, axis=1).astype(y_ref.dtype)
        if fresh:
            y_ref[sub * qr:(sub + 1) * qr, :] = y
        else:
            y_ref[sub:sub + 1, :] = y[0:1]


def _swa(cur2d, prev2d, prev_map, rel_bias, sinks, *, n_seq, n_blk, qr, fresh, per_step=1):
    smem = pl.BlockSpec(memory_space=pltpu.SMEM)
    if fresh:
        assert qr == WINDOW and n_blk % per_step == 0
        n_blk //= per_step
        prev_rows = WINDOW
    else:
        assert n_blk == 1 and n_seq % per_step == 0
        n_seq //= per_step
        prev_rows = per_step * WINDOW
    qrows = per_step * qr if fresh else per_step
    out_dtype = BF16 if fresh else F32
    return pl.pallas_call(
        functools.partial(_swa_body, qr=qr, fresh=fresh, per_step=per_step, ranges=_bucket_ranges()),
        grid=(n_seq, n_blk),
        in_specs=[smem, smem,
                  pl.BlockSpec((qrows, SWA_COLS), lambda b, j: (b * n_blk + j, 0)),
                  pl.BlockSpec((prev_rows, 2 * KV_COLS), prev_map)],
        out_specs=pl.BlockSpec((qrows, D_B), lambda b, j: (b * n_blk + j, 0)),
        out_shape=jax.ShapeDtypeStruct((n_seq * n_blk * qrows, D_B), out_dtype),
        scratch_shapes=[pltpu.VMEM((2 * KV_HEADS, Q_PER_KV * qr, WINDOW + qr), F32),
                        pltpu.VMEM((KV_HEADS, Q_PER_KV * qr, 1), F32)],
        compiler_params=_params(("arbitrary", "arbitrary")), name="swa",
    )(rel_bias.reshape(-1), sinks, cur2d, prev2d)


def _outproj_ln_body(ya_ref, yb_ref, x_ref, wa_ref, wb_ref, g_ref, b_ref, *rest):
    o_ref = rest[0] if len(rest) == 1 else rest[4]
    h = _dg(ya_ref[...].astype(BF16), wa_ref[...]) + _dg(yb_ref[...].astype(BF16), wb_ref[...])
    out = _layer_norm(ALPHA * x_ref[...] + h, g_ref[...], b_ref[...])
    o_ref[...] = out
    if len(rest) > 1:
        _route(out, *rest[:4], *rest[5:])


def _outproj_ln(ya, yb, x2d, w_out, ln_g, ln_b, *, tm, router=None):
    n = x2d.shape[0]
    half = pl.BlockSpec((tm, D_A), lambda i: (i, 0))
    full = pl.BlockSpec((tm, D_MODEL), lambda i: (i, 0))
    wa, wb = w_out[:D_A].astype(BF16), w_out[D_A:].astype(BF16)
    ins = [ya, yb, x2d, wa, wb, ln_g.reshape(1, -1), ln_b.reshape(1, -1)]
    in_specs = [half, half, full, _full(wa.shape), _full(wb.shape), _full((1, D_MODEL)), _full((1, D_MODEL))]
    out_specs, out_shape = [full], [jax.ShapeDtypeStruct((n, D_MODEL), F32)]
    if router is not None:
        r_ins, r_specs = _route_operands(*router, tm)
        ins, in_specs = ins + r_ins, in_specs + r_specs
        r_out_specs, r_out_shape = _route_outputs(n, tm)
        out_specs, out_shape = out_specs + r_out_specs, out_shape + r_out_shape
    res = pl.pallas_call(
        _outproj_ln_body, grid=(n // tm,), in_specs=in_specs, out_specs=out_specs, out_shape=out_shape,
        compiler_params=_params(("arbitrary",)), name="outproj_ln",
    )(*ins)
    return res[0] if router is None else res


def _route(x, wrt_ref, br_ref, eye_ref, tri_ref, meta_ref, slot_ref, cnt_ref):
    logits = _dg(wrt_ref[...], x.astype(BF16), NT) + br_ref[...]
    z = jnp.exp(logits - jnp.max(logits, axis=0, keepdims=True))
    probs = z / jnp.sum(z, axis=0, keepdims=True)
    p = [probs[e:e + 1, :] for e in range(N_EXPERTS)]

    def top2_mask(vals):
        out = []
        for a_ in range(len(vals)):
            rank = None
            for j in range(len(vals)):
                if j == a_:
                    continue
                ahead = (vals[j] >= vals[a_]) if j < a_ else (vals[j] > vals[a_])
                ahead = ahead.astype(jnp.int32)
                rank = ahead if rank is None else rank + ahead
            out.append(rank < 2)
        return out

    sel, score = [], []
    for gi in range(N_GROUPS):
        grp = p[gi * EXPERTS_PER_GROUP:(gi + 1) * EXPERTS_PER_GROUP]
        msk = top2_mask(grp)
        sel += msk
        sc = None
        for a_ in range(EXPERTS_PER_GROUP):
            t = jnp.where(msk[a_], grp[a_], 0.0)
            sc = t if sc is None else sc + t
        score.append(sc)
    best = []
    for gi in range(N_GROUPS):
        ok = None
        for j in range(N_GROUPS):
            if j == gi:
                continue
            c_ = (score[gi] > score[j]) if j < gi else (score[gi] >= score[j])
            ok = c_ if ok is None else (ok & c_)
        best.append(ok)
    active = [sel[e] & best[e // EXPERTS_PER_GROUP] for e in range(N_EXPERTS)]
    tot = None
    for e in range(N_EXPERTS):
        t = jnp.where(active[e], p[e], 0.0)
        tot = t if tot is None else tot + t
    gates = [jnp.where(active[e], p[e] / tot, 0.0) for e in range(N_EXPERTS)]

    tm = x.shape[0]
    onehot = jnp.concatenate([b_.astype(F32) for b_ in best] + [jnp.zeros((SUBLANES - N_GROUPS, tm), F32)], axis=0)
    oh_b = jnp.concatenate([onehot, jnp.zeros((LANES - SUBLANES, tm), F32)], axis=0).astype(BF16)
    incl = _dg(onehot.astype(BF16), tri_ref[...])
    slot = None
    offset = jnp.zeros((1, 1), F32)
    for gi in range(N_GROUPS):
        t = onehot[gi:gi + 1] * (offset + incl[gi:gi + 1] - 1.0)
        slot = t if slot is None else slot + t
        offset = offset + incl[gi:gi + 1, tm - 1:tm]
    slot_ref[0] = slot.astype(jnp.int32)
    cnt_ref[0] = _dg(jnp.ones((SUBLANES, tm), BF16), oh_b, NT)[0:1]
    rows = jnp.concatenate(gates + [slot, jnp.zeros((LANES - N_EXPERTS - 1, tm), F32)], axis=0)
    out = None
    for part in _split(rows, 3):
        t = _dg(eye_ref[...], part, NT)
        out = t if out is None else out + t
    meta_ref[...] = out


META_SLOT = N_EXPERTS


def _route_operands(w_router, b_router, tm):
    idx = np.arange(tm)
    eye = jnp.asarray(idx[:, None] == idx[None, :], dtype=BF16)
    tri = jnp.asarray(idx[:, None] <= idx[None, :], dtype=BF16)
    ins = [w_router.T.astype(BF16), b_router.reshape(-1, 1), eye, tri]
    return ins, [_full(a.shape) for a in ins]


def _route_outputs(n, tm):
    nt = n // tm
    specs = [pl.BlockSpec((tm, LANES), lambda i: (i, 0)), pl.BlockSpec((1, 1, tm), lambda i: (i, 0, 0)),
             pl.BlockSpec((1, 1, LANES), lambda i: (i, 0, 0))]
    shapes = [jax.ShapeDtypeStruct((n, LANES), F32), jax.ShapeDtypeStruct((nt, 1, tm), jnp.int32),
              jax.ShapeDtypeStruct((nt, 1, LANES), F32)]
    return specs, shapes


def _router_body(x_ref, *refs):
    _route(x_ref[...], *refs)


def _router(x2d, w_router, b_router, *, tm):
    n = x2d.shape[0]
    ins, in_specs = _route_operands(w_router, b_router, tm)
    out_specs, out_shape = _route_outputs(n, tm)
    return pl.pallas_call(
        _router_body, grid=(n // tm,),
        in_specs=[pl.BlockSpec((tm, D_MODEL), lambda i: (i, 0))] + in_specs,
        out_specs=out_specs, out_shape=out_shape,
        compiler_params=_params(("arbitrary",)), name="router",
    )(x2d, *ins)


def _moe_ln_body(x_ref, gates_ref, wg_ref, wu_ref, wd_ref, g_ref, b_ref, o_ref, acc_ref, xb_ref):
    gi = pl.program_id(1)

    @pl.when(gi == 0)
    def _():
        xb_ref[...] = x_ref[...].astype(BF16)
        acc_ref[...] = jnp.zeros_like(acc_ref)

    xb = xb_ref[...]
    lane = lax.broadcasted_iota(jnp.int32, gates_ref.shape, 1)
    hs = []
    for j in range(EXPERTS_PER_GROUP):
        hg = _dg(xb, wg_ref[0, j])
        hu = _dg(xb, wu_ref[0, j])
        gate = jnp.sum(jnp.where(lane == gi * EXPERTS_PER_GROUP + j, gates_ref[...], 0.0), axis=1, keepdims=True)
        hs.append((hg * _sigmoid(hg) * hu * gate).astype(BF16))
    acc_ref[...] += _dg(jnp.concatenate(hs, axis=1), wd_ref[0])

    @pl.when(gi == N_GROUPS - 1)
    def _():
        o_ref[...] = _layer_norm(ALPHA * x_ref[...] + acc_ref[...], g_ref[...], b_ref[...])


def _moe_ln(x2d, gates, wg, wu, wd, ln_g, ln_b, *, tm):
    n = x2d.shape[0]
    tok = pl.BlockSpec((tm, D_MODEL), lambda i, e: (i, 0))
    wspec = pl.BlockSpec((1, EXPERTS_PER_GROUP, D_MODEL, D_FF), lambda i, e: (e, 0, 0, 0))
    return pl.pallas_call(
        _moe_ln_body, grid=(n // tm, N_GROUPS),
        in_specs=[tok, pl.BlockSpec((tm, LANES), lambda i, e: (i, 0)), wspec, wspec,
                  pl.BlockSpec((1, EXPERTS_PER_GROUP * D_FF, D_MODEL), lambda i, e: (e, 0, 0)),
                  pl.BlockSpec((1, D_MODEL), lambda i, e: (0, 0)), pl.BlockSpec((1, D_MODEL), lambda i, e: (0, 0))],
        out_specs=tok, out_shape=jax.ShapeDtypeStruct((n, D_MODEL), F32),
        scratch_shapes=[pltpu.VMEM((tm, D_MODEL), F32), pltpu.VMEM((tm, D_MODEL), BF16)],
        compiler_params=_params(("arbitrary", "arbitrary")), name="moe_ln",
    )(x2d, gates, wg, wu, wd, ln_g.reshape(1, -1), ln_b.reshape(1, -1))


def _moe_sorted_body(start_ref, nwin_ref, x_ref, meta_ref, slot_ref, wg_ref, wu_ref, wd_ref, g_ref, b_ref, o_ref,
                     xs_ref, gs_ref, ys_ref, *, tm, wn):
    i = pl.program_id(0)

    @pl.when(i == 0)
    def _():
        xs_ref[tm:, :] = jnp.zeros((wn, D_MODEL), BF16)
        gs_ref[tm:, :] = jnp.zeros((wn, LANES), F32)

    x = x_ref[...]
    meta = meta_ref[...]
    slot_iota = lax.broadcasted_iota(jnp.int32, (tm, tm), 0)
    perm = jnp.where(slot_iota == slot_ref[0], 1.0, 0.0).astype(BF16)
    xs_ref[:tm, :] = _dg(perm, x.astype(BF16)).astype(BF16)
    gs = None
    for part in _split(meta, 3):
        t = _dg(perm, part)
        gs = t if gs is None else gs + t
    gs_ref[:tm, :] = gs
    ys_ref[...] = jnp.zeros_like(ys_ref)

    for gi in range(N_GROUPS):
        first = start_ref[i * N_GROUPS + gi]

        def window(w, carry, gi=gi, first=first):
            st = pl.multiple_of(first + w * wn, PACK)
            xw = xs_ref[pl.ds(st, wn), :]
            gw = gs_ref[pl.ds(st, wn), :]
            hs = []
            for j in range(EXPERTS_PER_GROUP):
                e = gi * EXPERTS_PER_GROUP + j
                hg = _dg(xw, wg_ref[gi, j])
                hu = _dg(xw, wu_ref[gi, j])
                hs.append((hg * _sigmoid(hg) * hu * gw[:, e:e + 1]).astype(BF16))
            ys_ref[pl.ds(st, wn), :] += _dg(jnp.concatenate(hs, axis=1), wd_ref[gi])
            return carry

        lax.fori_loop(0, nwin_ref[i * N_GROUPS + gi], window, 0)

    tok_slot = meta[:, META_SLOT:META_SLOT + 1]
    lane_slot = lax.broadcasted_iota(jnp.int32, (tm, tm), 1).astype(F32)
    unperm = jnp.where(tok_slot == lane_slot, 1.0, 0.0).astype(BF16)
    y_hi, y_lo = _split(ys_ref[:tm, :], 2)
    y = _dg(unperm, y_hi) + _dg(unperm, y_lo)
    o_ref[...] = _layer_norm(ALPHA * x + y, g_ref[...], b_ref[...])


def _moe_sorted_ln(x2d, meta, slot_rows, counts, wg, wu, wd, ln_g, ln_b, *, tm, wn):
    n = x2d.shape[0]
    nt = n // tm
    assert wn % PACK == 0 and tm % PACK == 0
    cnt = counts[:, 0, :N_GROUPS].astype(jnp.int32)
    offs = jnp.cumsum(cnt, axis=1) - cnt
    first = (offs // PACK) * PACK
    nwin = jnp.where(cnt > 0, (offs - first + cnt + wn - 1) // wn, 0)
    tok = lambda w: pl.BlockSpec((tm, w), lambda i, *_: (i, 0))
    resident = lambda a: pl.BlockSpec(a.shape, lambda i, *_: (0,) * a.ndim, pipeline_mode=pl.Buffered(1))
    vec = pl.BlockSpec((1, D_MODEL), lambda i, *_: (0, 0))
    grid_spec = pltpu.PrefetchScalarGridSpec(
        num_scalar_prefetch=2, grid=(nt,),
        in_specs=[tok(D_MODEL), tok(LANES), pl.BlockSpec((1, 1, tm), lambda i, *_: (i, 0, 0)),
                  resident(wg), resident(wu), resident(wd), vec, vec],
        out_specs=tok(D_MODEL),
        scratch_shapes=[pltpu.VMEM((tm + wn, D_MODEL), BF16), pltpu.VMEM((tm + wn, LANES), F32),
                        pltpu.VMEM((tm + wn, D_MODEL), F32)])
    return pl.pallas_call(
        functools.partial(_moe_sorted_body, tm=tm, wn=wn), grid_spec=grid_spec,
        out_shape=jax.ShapeDtypeStruct((n, D_MODEL), F32),
        compiler_params=_params(("arbitrary",)), name="moe_sorted_ln",
    )(first.reshape(-1), nwin.reshape(-1), x2d, meta, slot_rows, wg, wu, wd, ln_g.reshape(1, -1), ln_b.reshape(1, -1))


HALO = 16


def _pool_ln_body(x_ref, halo_ref, wp_ref, ps_ref, g_ref, b_ref, *rest, tm, seq_len, start_pos):
    o_ref = rest[0] if len(rest) == 1 else rest[4]
    i = pl.program_id(0)
    x = x_ref[...]
    idx = lax.broadcasted_iota(jnp.int32, (tm + HALO, 1), 0)
    pos_e = (i * tm - HALO + idx) & (seq_len - 1)
    cnt_pos = (pos_e[HALO:] + (start_pos + 1)).astype(F32)
    mixed = []
    for gi, w in enumerate(POOL_WINDOWS):
        cols = slice(gi * POOL_GROUP, (gi + 1) * POOL_GROUP)
        xg = x[:, cols]
        s = jnp.concatenate([halo_ref[:, cols], xg], axis=0)
        step = 1
        while step < w:
            s = s + jnp.where(pos_e >= step, pltpu.roll(s, step, axis=0), 0.0)
            step *= 2
        pooled = s[HALO:] / jnp.minimum(float(w), cnt_pos) - xg
        mixed.append(_bdot(pooled, wp_ref[gi]))
    h = jnp.concatenate(mixed, axis=1) * ps_ref[...]
    out = _layer_norm(ALPHA * x + h, g_ref[...], b_ref[...])
    o_ref[...] = out
    if len(rest) > 1:
        _route(out, *rest[:4], *rest[5:])


def _pool_ln(x2d, w_pool, pool_scale, ln_g, ln_b, *, tm, seq_len, start_pos, router=None):
    n = x2d.shape[0]
    assert seq_len & (seq_len - 1) == 0 and seq_len >= HALO and tm % HALO == 0
    tok = pl.BlockSpec((tm, D_MODEL), lambda i: (i, 0))
    ins = [x2d, x2d, w_pool.astype(BF16), pool_scale.reshape(1, -1), ln_g.reshape(1, -1), ln_b.reshape(1, -1)]
    in_specs = [tok, pl.BlockSpec((HALO, D_MODEL), lambda i: (jnp.maximum(i * (tm // HALO) - 1, 0), 0)),
                _full(w_pool.shape), _full((1, D_MODEL)), _full((1, D_MODEL)), _full((1, D_MODEL))]
    out_specs, out_shape = [tok], [jax.ShapeDtypeStruct((n, D_MODEL), F32)]
    if router is not None:
        r_ins, r_specs = _route_operands(*router, tm)
        ins, in_specs = ins + r_ins, in_specs + r_specs
        r_out_specs, r_out_shape = _route_outputs(n, tm)
        out_specs, out_shape = out_specs + r_out_specs, out_shape + r_out_shape
    res = pl.pallas_call(
        functools.partial(_pool_ln_body, tm=tm, seq_len=seq_len, start_pos=start_pos), grid=(n // tm,),
        in_specs=in_specs, out_specs=out_specs, out_shape=out_shape,
        compiler_params=_params(("arbitrary",)), name="pool_ln",
    )(*ins)
    return res[0] if router is None else res


def _pad_lora_cols(w, width):
    return jnp.pad(w, ((0, 0), (0, LORA_PAD - width)))


def _even_layer_weights(prm, i):
    w_in = prm['w_in'][i]
    o = 3 * D_A
    cols = lambda w: jnp.concatenate(
        [w[:, :o], _pad_lora_cols(w[:, o:o + W_LORA], W_LORA),
         _pad_lora_cols(w[:, o + W_LORA:o + W_LORA + A_LORA], A_LORA),
         _pad_lora_cols(w[:, o + W_LORA + A_LORA:RWKV_COLS], G_LORA)], axis=1)
    pad_rows = lambda w: jnp.pad(w, ((0, LORA_PAD - w.shape[0]), (0, 0))).astype(BF16)
    vec = lambda a: a.reshape(1, -1)
    return dict(
        wr=cols(w_in[:, :RWKV_COLS]).astype(BF16), ws=w_in[:, RWKV_COLS:].astype(BF16),
        mu=cols(prm['tshift_mu'][i].reshape(1, -1)),
        w0=vec(prm['decay_w0'][i]), ww2=pad_rows(prm['decay_w2'][i]),
        a0=vec(prm['iclr_a0'][i]), wa2=pad_rows(prm['iclr_a2'][i]), wg2=pad_rows(prm['gate_w2'][i]),
        k_k=vec(prm['k_k'][i]), k_a=vec(prm['k_a'][i]), r_k=vec(prm['r_k'][i]),
    )


def _group_experts(w):
    return w.astype(BF16).reshape(DEPTH, N_GROUPS, EXPERTS_PER_GROUP, D_MODEL, D_FF)


MOE_SORT_TILE = 512
MOE_WINDOW = 160


def _moe_weights(prm, layer):
    return (prm['wg'][layer], prm['wu'][layer], prm['wd'][layer], prm['ln_ffn_g'][layer], prm['ln_ffn_b'][layer])


def _moe_block(x2d, prm, layer, *, tm):
    meta, _, _ = _router(x2d, prm['w_router'], prm['b_router'], tm=min(tm, 512))
    return _moe_ln(x2d, meta, *_moe_weights(prm, layer), tm=tm)


PROMPT_CHUNKS_PER_STEP = 8
PROMPT_ATTN_BLOCKS_PER_STEP = 4
SAMPLE_CHUNK = 16
SAMPLE_QROWS = SUBLANES
SAMPLE_SEQS_PER_STEP = 8


def _prompt_trunk(x, prm):
    bsz, t, _ = x.shape
    n = bsz * t
    x0 = x.reshape(n, D_MODEL)
    wts = _even_layer_weights(prm, 0)
    rt, kt, bt, at, v, bonus, g, gl, swa = _inproj_prep(x0, None, wts, seq_len=t, chunk=CHUNK, tm=256)
    s0 = jnp.zeros((bsz, N_HEADS, HEAD, HEAD), F32)
    ya, s_new = _rwkv_scan(rt, kt, bt, at, v, bonus, g, gl, prm['lnx_g'][0].reshape(1, -1),
                           prm['lnx_b'][0].reshape(1, -1), s0, n_seq=bsz, seq_len=t, chunk=CHUNK,
                           nc=min(PROMPT_CHUNKS_PER_STEP, t // CHUNK))
    n_blk = t // WINDOW
    qb = min(PROMPT_ATTN_BLOCKS_PER_STEP, n_blk)
    yb = _swa(swa, swa, lambda b, j: (b * n_blk + jnp.maximum(j * qb - 1, 0), 2), prm['rel_bias'],
              prm['attn_sinks'][0], n_seq=bsz, n_blk=n_blk, qr=WINDOW, fresh=True, per_step=qb)
    tm = MOE_SORT_TILE
    assert n % tm == 0
    router = (prm['w_router'], prm['b_router'])
    x1, *routing = _outproj_ln(ya, yb, x0, prm['w_out'][0], prm['ln_mix_g'][0], prm['ln_mix_b'][0], tm=tm,
                               router=router)
    x2 = _moe_sorted_ln(x1, *routing, *_moe_weights(prm, 0), tm=tm, wn=MOE_WINDOW)
    x3, *routing = _pool_ln(x2, prm['w_pool'][0], prm['pool_scale'][0], prm['ln_mix_g'][1], prm['ln_mix_b'][1],
                            tm=tm, seq_len=t, start_pos=0, router=router)
    x4 = _moe_sorted_ln(x3, *routing, *_moe_weights(prm, 1), tm=tm, wn=MOE_WINDOW)
    swa3 = swa.reshape(bsz, t, SWA_COLS)
    k_new = swa3[:, t - WINDOW:, D_B:D_B + KV_COLS].reshape(bsz, WINDOW, KV_HEADS, HEAD)
    v_new = swa3[:, t - WINDOW:, D_B + KV_COLS:].reshape(bsz, WINDOW, KV_HEADS, HEAD)
    pool_new = x2.reshape(bsz, t, D_MODEL)[:, t - POOL_KEEP:]
    return (x4.reshape(bsz, t, D_MODEL), s_new[None], x[:, -1][None], k_new[None], v_new[None], pool_new[None])


def _sample_trunk(x, st_rwkv, st_shift, c_k, c_v, st_pool, prm):
    bsz = x.shape[0]
    x0 = x.reshape(bsz, D_MODEL)
    wts = _even_layer_weights(prm, 0)
    outs = _inproj_prep(x0, st_shift[0], wts, seq_len=1, chunk=1, tm=bsz)
    gl, swa = outs[7], outs[8]
    ya, s_new = _rwkv_scan(*outs[:7], gl.reshape(bsz, 1, D_A), prm['lnx_g'][0].reshape(1, -1),
                           prm['lnx_b'][0].reshape(1, -1), st_rwkv[0], n_seq=bsz, seq_len=SAMPLE_CHUNK,
                           chunk=SAMPLE_CHUNK, nc=SAMPLE_SEQS_PER_STEP)
    kv_prev = jnp.concatenate([c_k[0].reshape(bsz * WINDOW, KV_COLS), c_v[0].reshape(bsz * WINDOW, KV_COLS)], axis=-1)
    yb = _swa(swa, kv_prev, lambda b, j: (b, 0), prm['rel_bias'], prm['attn_sinks'][0], n_seq=bsz, n_blk=1,
              qr=SAMPLE_QROWS, fresh=False, per_step=SAMPLE_SEQS_PER_STEP)
    x1 = _outproj_ln(ya, yb, x0, prm['w_out'][0], prm['ln_mix_g'][0], prm['ln_mix_b'][0], tm=bsz)
    x2 = _moe_block(x1, prm, 0, tm=bsz)
    xcat = jnp.concatenate([st_pool[0], x2[:, None]], axis=1)
    x3 = _pool_ln(xcat.reshape(bsz * HALO, D_MODEL), prm['w_pool'][0], prm['pool_scale'][0],
                  prm['ln_mix_g'][1], prm['ln_mix_b'][1], tm=min(512, bsz * HALO), seq_len=HALO,
                  start_pos=PAST_LEN - POOL_KEEP)
    x3 = x3.reshape(bsz, HALO, D_MODEL)[:, -1]
    x4 = _moe_block(x3, prm, 1, tm=bsz)
    k_new = jnp.concatenate([c_k[0][:, 1:], swa[:, D_B:D_B + KV_COLS].reshape(bsz, 1, KV_HEADS, HEAD)], axis=1)
    v_new = jnp.concatenate([c_v[0][:, 1:], swa[:, D_B + KV_COLS:].reshape(bsz, 1, KV_HEADS, HEAD)], axis=1)
    return (x4.reshape(bsz, 1, D_MODEL), s_new[None], x[:, -1][None], k_new[None], v_new[None],
            xcat[:, 1:][None])


def _prepare_params(raw):
    prm = dict(raw)
    prm['r_k'] = raw['r_k'].reshape(raw['r_k'].shape[0], -1)
    prm['wg'] = _group_experts(raw['w_ex_gate'])
    prm['wu'] = _group_experts(raw['w_ex_up'])
    prm['wd'] = raw['w_ex_down'].astype(BF16).reshape(DEPTH, N_GROUPS, EXPERTS_PER_GROUP * D_FF, D_MODEL)
    return prm


def kernel(x_prompt, x_sample, state_rwkv, state_shift, cache_swa_k, cache_swa_v, state_pool, w_in, tshift_mu,
           decay_w0, decay_w2, iclr_a0, iclr_a2, gate_w2, k_k, k_a, r_k, lnx_g, lnx_b, attn_sinks, rel_bias, w_out,
           w_pool, pool_scale, ln_mix_g, ln_mix_b, ln_ffn_g, ln_ffn_b, w_router, b_router, w_ex_gate, w_ex_up,
           w_ex_down):
    prm = _prepare_params(dict(
        w_in=w_in, tshift_mu=tshift_mu, decay_w0=decay_w0, decay_w2=decay_w2, iclr_a0=iclr_a0, iclr_a2=iclr_a2,
        gate_w2=gate_w2, k_k=k_k, k_a=k_a, r_k=r_k, lnx_g=lnx_g, lnx_b=lnx_b, attn_sinks=attn_sinks,
        rel_bias=rel_bias, w_out=w_out, w_pool=w_pool, pool_scale=pool_scale, ln_mix_g=ln_mix_g, ln_mix_b=ln_mix_b,
        ln_ffn_g=ln_ffn_g, ln_ffn_b=ln_ffn_b, w_router=w_router, b_router=b_router, w_ex_gate=w_ex_gate,
        w_ex_up=w_ex_up, w_ex_down=w_ex_down))
    y_p, rwkv_p, shift_p, k_p, v_p, pool_p = _prompt_trunk(x_prompt, prm)
    y_s, rwkv_s, shift_s, k_s, v_s, pool_s = _sample_trunk(x_sample, state_rwkv, state_shift, cache_swa_k,
                                                            cache_swa_v, state_pool, prm)
    return (y_p, y_s, rwkv_p, rwkv_s, shift_p, shift_s, k_p, k_s, v_p, v_s, pool_p, pool_s)
```

```python
import functools
import math

import numpy as np
import jax
import jax.numpy as jnp
from jax import lax
from jax.experimental import pallas as pl
from jax.experimental.pallas import tpu as pltpu

F32 = jnp.float32
BF16 = jnp.bfloat16

D_MODEL = 1024
DEPTH = 2
PAST_LEN = 16384
D_A = 512
HEAD = 64
N_HEADS = 8
W_LORA, A_LORA, G_LORA = 32, 64, 96
RWKV_COLS = 3 * D_A + W_LORA + A_LORA + G_LORA
D_B = 512
KV_HEADS = 2
Q_PER_KV = 4
KV_COLS = KV_HEADS * HEAD
SWA_COLS = D_B + 2 * KV_COLS
WINDOW = 128
N_BUCKETS = 32
MAX_EXACT = 16
REL_MAX_DIST = 128
POOL_WINDOWS = (2, 4, 8, 16)
POOL_GROUP = 256
POOL_KEEP = 15
N_EXPERTS = 16
N_GROUPS = 4
EXPERTS_PER_GROUP = 4
D_FF = 256
ALPHA = (2.0 * DEPTH) ** 0.25
LN_EPS = 1e-5
LNX_EPS = 64e-5
NEG_INF = -1e30

LANES = 128
SUBLANES = 8
PACK = 16
VMEM_LIMIT_BYTES = 56 * 1024 * 1024

LORA_PAD = LANES
RWKV_PAD_COLS = 3 * D_A + 3 * LORA_PAD
CHUNK = 64

NN = ((1,), (0,))
NT = ((1,), (1,))
TN = ((0,), (0,))


def _dg(a, b, dims=NN):
    return lax.dot_general(a, b, (dims, ((), ())), preferred_element_type=F32)


def _bdot(a, b, dims=NN):
    return _dg(a.astype(BF16), b.astype(BF16), dims)


def _split(x, n):
    parts, rem = [], x
    for i in range(n):
        p = rem.astype(BF16)
        parts.append(p)
        if i + 1 < n:
            rem = rem - p.astype(F32)
    return parts


def _mm(a, b, dims=NN, passes=3):
    if passes == 1:
        return _bdot(a, b, dims)
    ah, al = _split(a, 2)
    bh, bl = _split(b, 2)
    return _dg(ah, bh, dims) + (_dg(ah, bl, dims) + _dg(al, bh, dims))


def _dot_exact_rhs(x, ones_bf16, n=3):
    out = None
    for p in _split(x, n):
        t = _dg(p, ones_bf16)
        out = t if out is None else out + t
    return out


def _dot_exact_lhs(ones_bf16, x, n=3):
    out = None
    for p in _split(x, n):
        t = _dg(ones_bf16, p)
        out = t if out is None else out + t
    return out


def _sigmoid(x):
    return 1.0 / (1.0 + jnp.exp(-x))


def _layer_norm(z, g, b):
    mu = jnp.mean(z, axis=-1, keepdims=True)
    d = z - mu
    var = jnp.mean(d * d, axis=-1, keepdims=True)
    return d * lax.rsqrt(var + LN_EPS) * g + b


def _params(sem):
    return pltpu.CompilerParams(dimension_semantics=sem, vmem_limit_bytes=VMEM_LIMIT_BYTES)


def _full(shape):
    nd = len(shape)
    return pl.BlockSpec(shape, lambda *_: (0,) * nd)


def _inproj_prep_body(*refs, tm, tiles_per_seq, chunk, has_prev):
    if has_prev:
        x_ref, xp_ref = refs[0], refs[1]
        refs = refs[2:]
    else:
        x_ref, xp_ref = refs[0], None
        refs = refs[1:]
    (wr_ref, ws_ref, mu_ref, w0_ref, ww2_ref, a0_ref, wa2_ref, wg2_ref, kk_ref, ka_ref, rk_ref,
     ones_ref, tri_ref,
     rt_ref, kt_ref, bt_ref, at_ref, v_ref, bonus_ref, g_ref, gl_ref, swa_ref, carry_ref) = refs

    i = pl.program_id(0)
    xb = x_ref[...].astype(BF16)
    pr = _dg(xb, wr_ref[...])
    swa_ref[...] = _dg(xb, ws_ref[...])

    if has_prev:
        shifted = _dg(xp_ref[...].astype(BF16), wr_ref[...])
    else:
        @pl.when(i == 0)
        def _():
            carry_ref[...] = jnp.zeros_like(carry_ref)

        first = (i % tiles_per_seq) == 0
        prev_last = jnp.where(first, 0.0, carry_ref[0:1, :])
        rolled = pltpu.roll(pr, 1, axis=0)
        row = lax.broadcasted_iota(jnp.int32, (tm, 1), 0)
        shifted = jnp.where(row == 0, prev_last, rolled)
        carry_ref[0:1, :] = pr[tm - 1:tm, :]
    m = pr + mu_ref[...] * (shifted - pr)

    r = m[:, 0:D_A]
    k = m[:, D_A:2 * D_A]
    v = m[:, 2 * D_A:3 * D_A]
    wd = m[:, 3 * D_A:3 * D_A + LORA_PAD]
    ad = m[:, 3 * D_A + LORA_PAD:3 * D_A + 2 * LORA_PAD]
    gd = m[:, 3 * D_A + 2 * LORA_PAD:3 * D_A + 3 * LORA_PAD]

    nz = -(w0_ref[...] + _bdot(jnp.tanh(wd), ww2_ref[...]))
    softplus = jnp.maximum(nz, 0.0) + jnp.log1p(jnp.exp(-jnp.abs(nz)))
    w_log = -softplus - 0.5
    logw = -jnp.exp(w_log)
    a = _sigmoid(a0_ref[...] + _bdot(ad, wa2_ref[...]))
    g = _bdot(_sigmoid(gd), wg2_ref[...])

    ones_bd = ones_ref[...]
    half = D_A // 2

    def head_sums(z):
        return jnp.concatenate([_dot_exact_rhs(z[:, :half], ones_bd, 2), _dot_exact_rhs(z[:, half:], ones_bd, 2)],
                               axis=1)

    kkr = k * kk_ref[...]
    nrm = jnp.sqrt(head_sums(kkr * kkr))
    kk = kkr / jnp.maximum(nrm, 1e-12)
    k2 = k * (1.0 + (a - 1.0) * ka_ref[...])
    bonus_ref[...] = head_sums(r * k2 * rk_ref[...]) * v

    if chunk > 1:
        cum = _dot_exact_lhs(tri_ref[...], logw, 2)
    else:
        cum = logw
    gam = jnp.exp(cum)
    inv = jnp.exp(-cum)
    rt_ref[...] = r * gam
    kt_ref[...] = k2 * inv
    bt_ref[...] = kk * a * inv
    at_ref[...] = -kk * jnp.exp(cum - logw)
    v_ref[...] = v
    g_ref[...] = g
    if chunk > 1:
        for c in range(tm // chunk):
            gl_ref[c] = gam[(c + 1) * chunk - 1:(c + 1) * chunk, :]
    else:
        gl_ref[...] = gam


def _inproj_prep(x2d, x_prev, wts, *, seq_len, chunk, tm):
    n = x2d.shape[0]
    assert n % tm == 0 and (x_prev is not None or seq_len % tm == 0)
    assert tm % chunk == 0
    has_prev = x_prev is not None
    row = lambda w: pl.BlockSpec((tm, w), lambda i: (i, 0))
    ins = [x2d] + ([x_prev] if has_prev else [])
    in_specs = [row(D_MODEL)] + ([row(D_MODEL)] if has_prev else [])
    consts = [wts['wr'], wts['ws'], wts['mu'], wts['w0'], wts['ww2'], wts['a0'], wts['wa2'], wts['wg2'],
              wts['k_k'], wts['k_a'], wts['r_k'], _ones_block_diag(D_A // 2, HEAD), _tri_block_diag(tm, chunk)]
    ins += consts
    in_specs += [_full(c.shape) for c in consts]
    if chunk > 1:
        gl_shape = jax.ShapeDtypeStruct((n // chunk, 1, D_A), F32)
        gl_spec = pl.BlockSpec((tm // chunk, 1, D_A), lambda i: (i, 0, 0))
    else:
        gl_shape = jax.ShapeDtypeStruct((n, D_A), F32)
        gl_spec = row(D_A)
    out_shape = [jax.ShapeDtypeStruct((n, D_A), F32)] * 7 + [gl_shape, jax.ShapeDtypeStruct((n, SWA_COLS), F32)]
    out_specs = [row(D_A)] * 7 + [gl_spec, row(SWA_COLS)]
    body = functools.partial(_inproj_prep_body, tm=tm, tiles_per_seq=max(seq_len // tm, 1), chunk=chunk,
                             has_prev=has_prev)
    return pl.pallas_call(
        body, grid=(n // tm,), in_specs=in_specs, out_specs=out_specs, out_shape=out_shape,
        scratch_shapes=[pltpu.VMEM((SUBLANES, RWKV_PAD_COLS), F32)],
        compiler_params=_params(("arbitrary",)), name="inproj_prep",
    )(*ins)


def _ones_block_diag(n, blk):
    idx = np.arange(n) // blk
    return jnp.asarray(idx[:, None] == idx[None, :], dtype=BF16)


def _tri_block_diag(n, blk):
    idx = np.arange(n)
    same = (idx[:, None] // blk) == (idx[None, :] // blk)
    return jnp.asarray(same & (idx[:, None] >= idx[None, :]), dtype=BF16)


PAIR = 2 * HEAD
N_PAIRS = N_HEADS // 2


def _scan_body(rt_ref, kt_ref, bt_ref, at_ref, v_ref, bonus_ref, g_ref, gl_ref, lng_ref, lnb_ref, s0_ref,
               y_ref, sout_ref, S_ref, *, C, nc, one_chunk_seqs):
    step = pl.program_id(1)

    def pair_state(ref, s, p):
        return jnp.concatenate([ref[s, 2 * p], ref[s, 2 * p + 1]], axis=1)

    if not one_chunk_seqs:
        @pl.when(step == 0)
        def _():
            for p in range(N_PAIRS):
                S_ref[p] = pair_state(s0_ref, 0, p)

    head_a = lax.broadcasted_iota(jnp.int32, (1, PAIR), 1) < HEAD
    rowc = lax.broadcasted_iota(jnp.int32, (C, 2 * C), 0)
    colc = lax.broadcasted_iota(jnp.int32, (C, 2 * C), 1)
    first_c = colc < C
    col_in = jnp.where(first_c, colc, colc - C)
    strict = rowc > col_in
    incl = rowc >= col_in
    eye_c = (rowc == col_in).astype(F32)
    r128 = lax.broadcasted_iota(jnp.int32, (PAIR, PAIR), 0)
    c128 = lax.broadcasted_iota(jnp.int32, (PAIR, PAIR), 1)
    same_head = (r128 < HEAD) == (c128 < HEAD)
    eye128 = (r128 == c128).astype(F32)

    def by_head(x):
        xb = x.astype(BF16)
        z = jnp.zeros_like(xb)
        return jnp.concatenate([jnp.where(head_a, xb, z), jnp.where(head_a, z, xb)], axis=0)

    def blocks(x):
        xb = x.astype(BF16)
        z = jnp.zeros_like(xb)
        return jnp.concatenate([jnp.where(first_c, xb, z), jnp.where(first_c, z, xb)], axis=0)

    def head_mean(z):
        za = jnp.sum(jnp.where(head_a, z, 0.0), axis=-1, keepdims=True)
        zb = jnp.sum(jnp.where(head_a, 0.0, z), axis=-1, keepdims=True)
        return jnp.where(head_a, za, zb) * (1.0 / HEAD)

    n_fold = int(math.log2(C)) - 1
    zero_blk = jnp.zeros((2 * C, PAIR), BF16)
    items = [(ci, p) for ci in range(nc) for p in range(N_PAIRS)]
    each = lambda f, *lists: [f(*xs) for xs in zip(*lists)]

    def load(ref):
        if one_chunk_seqs:
            pad = jnp.zeros((C - 1, PAIR), F32)
            return [jnp.concatenate([ref[ci:ci + 1, p * PAIR:(p + 1) * PAIR], pad], axis=0) for ci, p in items]
        return [ref[ci * C:(ci + 1) * C, p * PAIR:(p + 1) * PAIR] for ci, p in items]

    R, K, Bm, A, V = load(rt_ref), load(kt_ref), load(bt_ref), load(at_ref), load(v_ref)
    big = each(lambda a, r, b, k: _dg(jnp.concatenate([a, r], axis=0).astype(BF16),
                                      jnp.concatenate([by_head(b), by_head(k)], axis=0), NT), A, R, Bm, K)
    a_ab = [jnp.where(strict, x[:C, :2 * C], 0.0) for x in big]
    a_ak = [jnp.where(strict, x[:C, 2 * C:], 0.0) for x in big]
    l_rb = [jnp.where(incl, x[C:, :2 * C], 0.0) for x in big]
    l_rk = [jnp.where(incl, x[C:, 2 * C:], 0.0) for x in big]
    X = [_dg(x.astype(BF16), blocks(x)) for x in a_ab]
    T = [eye_c + x for x in a_ab]
    for k in range(1, n_fold + 1):
        if k < n_fold:
            res = each(lambda x, t: _dg(jnp.concatenate([x, t], axis=0).astype(BF16), blocks(x)), X, T)
            X = [r_[:C] for r_ in res]
            T = each(lambda t, r_: t + r_[C:], T, res)
        else:
            T = each(lambda t, x: t + _dg(t.astype(BF16), blocks(x)), T, X)
    akv = each(lambda m_, v_: _dg(m_.astype(BF16), by_head(v_)), a_ak, V)
    pq = each(lambda t, a, q: _dg(t.astype(BF16), jnp.concatenate([by_head(a), by_head(q)], axis=1)), T, A, akv)
    P = [x[:, :PAIR] for x in pq]
    Q = [x[:, PAIR:] for x in pq]
    wz = each(lambda lb, lk, p_, q_, v_: _dg(
        jnp.concatenate([lb, lk], axis=1).astype(BF16),
        jnp.concatenate([jnp.concatenate([by_head(p_), by_head(q_)], axis=1),
                         jnp.concatenate([zero_blk, by_head(v_)], axis=1)], axis=0)), l_rb, l_rk, P, Q, V)
    W = each(lambda r, x: r + x[:, :PAIR], R, wz)
    Z = [x[:, PAIR:] for x in wz]
    ptb = each(lambda p_, b: _bdot(p_, b, TN), P, Bm)
    gfull = each(lambda q_, v_, b, k: _bdot(jnp.concatenate([q_, v_], axis=0), jnp.concatenate([b, k], axis=0), TN),
                 Q, V, Bm, K)

    def put_state(s, p, val):
        sout_ref[s, 2 * p] = val[:, :HEAD]
        sout_ref[s, 2 * p + 1] = val[:, HEAD:]

    state = [None if one_chunk_seqs else S_ref[p] for p in range(N_PAIRS)]
    for idx, (ci, p) in enumerate(items):
        rows = slice(ci, ci + 1) if one_chunk_seqs else slice(ci * C, (ci + 1) * C)
        ln = slice(p * PAIR, (p + 1) * PAIR)
        g_c = gl_ref[ci, :, ln]
        M = (eye128 + jnp.where(same_head, ptb[idx], 0.0)) * g_c
        G = jnp.where(head_a, gfull[idx][:HEAD], gfull[idx][HEAD:]) * g_c
        S0 = pair_state(s0_ref, ci, p) if one_chunk_seqs else state[p]
        Y = _dg(W[idx].astype(BF16), by_head(S0), NT) + Z[idx]
        S1 = _bdot(S0, M) + G
        if one_chunk_seqs:
            put_state(ci, p, S1)
        else:
            state[p] = S1

        if one_chunk_seqs:
            Y = Y[0:1]
        mu = head_mean(Y)
        d = Y - mu
        var = head_mean(d * d)
        yn = d * lax.rsqrt(var + LNX_EPS) * lng_ref[:, ln] + lnb_ref[:, ln]
        y_ref[rows, ln] = ((yn + bonus_ref[rows, ln]) * g_ref[rows, ln]).astype(y_ref.dtype)

    if not one_chunk_seqs:
        for p in range(N_PAIRS):
            S_ref[p] = state[p]

        @pl.when(step == pl.num_programs(1) - 1)
        def _():
            for p in range(N_PAIRS):
                put_state(0, p, state[p])


def _rwkv_scan(rt, kt, bt, at, v, bonus, g, gl, lnx_g, lnx_b, s0, *, n_seq, seq_len, chunk, nc):
    one_chunk_seqs = seq_len == chunk
    if one_chunk_seqs:
        assert n_seq % nc == 0
        grid = (n_seq // nc, 1)
        blk = lambda b, c: b
        st = pl.BlockSpec((nc, N_HEADS, HEAD, HEAD), lambda b, c: (b, 0, 0, 0))
        rows, n_rows = nc, n_seq
    else:
        rows, n_rows = chunk * nc, n_seq * seq_len
        assert seq_len % rows == 0
        nsteps = seq_len // rows
        grid = (n_seq, nsteps)
        blk = lambda b, c: b * nsteps + c
        st = pl.BlockSpec((1, N_HEADS, HEAD, HEAD), lambda b, c: (b, 0, 0, 0))
    row = pl.BlockSpec((rows, D_A), lambda b, c: (blk(b, c), 0))
    vec = pl.BlockSpec((1, D_A), lambda b, c: (0, 0))
    return pl.pallas_call(
        functools.partial(_scan_body, C=chunk, nc=nc, one_chunk_seqs=one_chunk_seqs),
        grid=grid,
        in_specs=[row] * 7 + [pl.BlockSpec((nc, 1, D_A), lambda b, c: (blk(b, c), 0, 0)), vec, vec, st],
        out_specs=[row, st],
        out_shape=[jax.ShapeDtypeStruct((n_rows, D_A), BF16 if n_rows % PACK == 0 and rows % PACK == 0 else F32),
                   jax.ShapeDtypeStruct((n_seq, N_HEADS, HEAD, HEAD), F32)],
        scratch_shapes=[pltpu.VMEM((N_PAIRS, HEAD, PAIR), F32)],
        compiler_params=_params(("arbitrary", "arbitrary")), name="rwkv_scan",
    )(rt, kt, bt, at, v, bonus, g, gl, lnx_g, lnx_b, s0)


def _bucket_ranges():
    d = np.arange(WINDOW + 1)
    scaled = np.log(np.maximum(d, MAX_EXACT).astype(np.float32) / MAX_EXACT) / math.log(REL_MAX_DIST / MAX_EXACT)
    large = np.minimum(MAX_EXACT + (scaled * (N_BUCKETS - MAX_EXACT)).astype(np.int32), N_BUCKETS - 1)
    bucket = np.where(d < MAX_EXACT, d, large)
    frac = scaled.astype(np.float64) * (N_BUCKETS - MAX_EXACT)
    near = np.abs(frac - np.round(frac)) < 1e-3
    assert all(int(x) in (MAX_EXACT, REL_MAX_DIST) for x in d[(d >= MAX_EXACT) & near])
    ranges = []
    for b in range(N_BUCKETS):
        hit = d[bucket == b]
        ranges.append((int(hit.min()), int(hit.max())) if hit.size else None)
    return tuple(ranges)


def _swa_body(rb_ref, sink_ref, cur_ref, prev_ref, y_ref, bias_ref, sinkcol_ref, *, qr, fresh, per_step, ranges):
    W = WINDOW
    nk = W + qr

    @pl.when((pl.program_id(0) == 0) & (pl.program_id(1) == 0))
    def _():
        qi = lax.broadcasted_iota(jnp.int32, (qr, nk), 0)
        kj = lax.broadcasted_iota(jnp.int32, (qr, nk), 1)
        d = qi + W - kj
        valid = (d >= 0) & (d <= W)
        for hd in range(N_HEADS):
            t = jnp.zeros((qr, nk), F32)
            for bkt, rng in enumerate(ranges):
                if rng is not None:
                    t = jnp.where((d >= rng[0]) & (d <= rng[1]), rb_ref[bkt * N_HEADS + hd], t)
            kvh, grp = hd // Q_PER_KV, hd % Q_PER_KV
            rows = slice(grp * qr, (grp + 1) * qr)
            bias_ref[kvh, rows, :] = jnp.where(valid, t, NEG_INF)
            bias_ref[KV_HEADS + kvh, rows, :] = jnp.where(valid & (kj >= W), t, NEG_INF)
            sinkcol_ref[kvh, rows, :] = jnp.full((qr, 1), sink_ref[hd], F32)

    first_table = jnp.where(pl.program_id(1) == 0, KV_HEADS, 0) if fresh else 0
    items = [(sub, kvh) for sub in range(per_step) for kvh in range(KV_HEADS)]
    each = lambda f, *lists: [f(*xs) for xs in zip(*lists)]

    def own_rows(sub, col):
        if fresh:
            return cur_ref[sub * qr:(sub + 1) * qr, col:col + HEAD]
        return jnp.concatenate([cur_ref[sub:sub + 1, col:col + HEAD], jnp.zeros((qr - 1, HEAD), F32)], axis=0)

    def kv_rows(sub, col):
        if fresh and sub > 0:
            before = cur_ref[(sub - 1) * qr:sub * qr, col:col + HEAD]
        else:
            first = 0 if fresh else sub * W
            before = prev_ref[first:first + W, col - D_B:col - D_B + HEAD]
        return jnp.concatenate([before, own_rows(sub, col)], axis=0).astype(BF16)

    kb = [kv_rows(sub, D_B + kvh * HEAD) for sub, kvh in items]
    vb = [kv_rows(sub, D_B + KV_COLS + kvh * HEAD) for sub, kvh in items]
    q4 = [(jnp.concatenate([own_rows(sub, (kvh * Q_PER_KV + gq) * HEAD) for gq in range(Q_PER_KV)], axis=0)
           * (HEAD ** -0.5)).astype(BF16) for sub, kvh in items]
    s = [_dg(q, k, NT) + bias_ref[(first_table if sub == 0 else 0) + kvh] for q, k, (sub, kvh) in zip(q4, kb, items)]
    sink = [sinkcol_ref[kvh] for _, kvh in items]
    m = each(lambda s_, sk: jnp.maximum(jnp.max(s_, axis=-1, keepdims=True), sk), s, sink)
    p = each(lambda s_, m_: jnp.exp(s_ - m_), s, m)
    den = each(lambda p_, sk, m_: jnp.sum(p_, axis=-1, keepdims=True) + jnp.exp(sk - m_), p, sink, m)
    o = each(lambda p_, v_, d_: _dg(p_.astype(BF16), v_) / d_, p, vb, den)
    for sub in range(per_step):
        outs = [o[sub * KV_HEADS + kvh][gq * qr:(gq + 1) * qr] for kvh in range(KV_HEADS) for gq in range(Q_PER_KV)]
        y = jnp.concatenate(outs, axis=1).astype(y_ref.dtype)
        if fresh:
            y_ref[sub * qr:(sub + 1) * qr, :] = y
        else:
            y_ref[sub:sub + 1, :] = y[0:1]


def _swa(cur2d, prev2d, prev_map, rel_bias, sinks, *, n_seq, n_blk, qr, fresh, per_step=1):
    smem = pl.BlockSpec(memory_space=pltpu.SMEM)
    if fresh:
        assert qr == WINDOW and n_blk % per_step == 0
        n_blk //= per_step
        prev_rows = WINDOW
    else:
        assert n_blk == 1 and n_seq % per_step == 0
        n_seq //= per_step
        prev_rows = per_step * WINDOW
    qrows = per_step * qr if fresh else per_step
    out_dtype = BF16 if fresh else F32
    return pl.pallas_call(
        functools.partial(_swa_body, qr=qr, fresh=fresh, per_step=per_step, ranges=_bucket_ranges()),
        grid=(n_seq, n_blk),
        in_specs=[smem, smem,
                  pl.BlockSpec((qrows, SWA_COLS), lambda b, j: (b * n_blk + j, 0)),
                  pl.BlockSpec((prev_rows, 2 * KV_COLS), prev_map)],
        out_specs=pl.BlockSpec((qrows, D_B), lambda b, j: (b * n_blk + j, 0)),
        out_shape=jax.ShapeDtypeStruct((n_seq * n_blk * qrows, D_B), out_dtype),
        scratch_shapes=[pltpu.VMEM((2 * KV_HEADS, Q_PER_KV * qr, WINDOW + qr), F32),
                        pltpu.VMEM((KV_HEADS, Q_PER_KV * qr, 1), F32)],
        compiler_params=_params(("arbitrary", "arbitrary")), name="swa",
    )(rel_bias.reshape(-1), sinks, cur2d, prev2d)


def _outproj_ln_body(ya_ref, yb_ref, x_ref, wa_ref, wb_ref, g_ref, b_ref, *rest):
    o_ref = rest[0] if len(rest) == 1 else rest[N_ROUTE_INPUTS]
    h = _dg(ya_ref[...].astype(BF16), wa_ref[...]) + _dg(yb_ref[...].astype(BF16), wb_ref[...])
    out = _layer_norm(ALPHA * x_ref[...] + h, g_ref[...], b_ref[...])
    o_ref[...] = out
    if len(rest) > 1:
        _route(out, *rest[:N_ROUTE_INPUTS], *rest[N_ROUTE_INPUTS + 1:])


def _outproj_ln(ya, yb, x2d, w_out, ln_g, ln_b, *, tm, router=None):
    n = x2d.shape[0]
    half = pl.BlockSpec((tm, D_A), lambda i: (i, 0))
    full = pl.BlockSpec((tm, D_MODEL), lambda i: (i, 0))
    wa, wb = w_out[:D_A].astype(BF16), w_out[D_A:].astype(BF16)
    ins = [ya, yb, x2d, wa, wb, ln_g.reshape(1, -1), ln_b.reshape(1, -1)]
    in_specs = [half, half, full, _full(wa.shape), _full(wb.shape), _full((1, D_MODEL)), _full((1, D_MODEL))]
    out_specs, out_shape = [full], [jax.ShapeDtypeStruct((n, D_MODEL), F32)]
    if router is not None:
        r_ins, r_specs = _route_operands(*router, tm)
        ins, in_specs = ins + r_ins, in_specs + r_specs
        r_out_specs, r_out_shape = _route_outputs(n, tm)
        out_specs, out_shape = out_specs + r_out_specs, out_shape + r_out_shape
    res = pl.pallas_call(
        _outproj_ln_body, grid=(n // tm,), in_specs=in_specs, out_specs=out_specs, out_shape=out_shape,
        compiler_params=_params(("arbitrary",)), name="outproj_ln",
    )(*ins)
    return res[0] if router is None else res


def _route(x, wr_ref, br_ref, tri_ref, meta_ref, slot_ref, cnt_ref):
    logits = jnp.transpose(_dg(x.astype(BF16), wr_ref[...]))[:N_EXPERTS] + br_ref[...]
    z = jnp.exp(logits - jnp.max(logits, axis=0, keepdims=True))
    probs = z / jnp.sum(z, axis=0, keepdims=True)
    pa = [probs[a_ * N_GROUPS:(a_ + 1) * N_GROUPS] for a_ in range(EXPERTS_PER_GROUP)]

    sel = []
    for a_ in range(EXPERTS_PER_GROUP):
        rank = None
        for j in range(EXPERTS_PER_GROUP):
            if j == a_:
                continue
            ahead = ((pa[j] >= pa[a_]) if j < a_ else (pa[j] > pa[a_])).astype(jnp.int32)
            rank = ahead if rank is None else rank + ahead
        sel.append(rank < 2)
    score = None
    for a_ in range(EXPERTS_PER_GROUP):
        t = jnp.where(sel[a_], pa[a_], 0.0)
        score = t if score is None else score + t
    srow = [score[gi:gi + 1] for gi in range(N_GROUPS)]
    best = []
    for gi in range(N_GROUPS):
        ok = None
        for j in range(N_GROUPS):
            if j == gi:
                continue
            c_ = (srow[gi] > srow[j]) if j < gi else (srow[gi] >= srow[j])
            ok = c_ if ok is None else (ok & c_)
        best.append(ok.astype(F32))
    tm = x.shape[0]
    best4 = jnp.concatenate(best, axis=0)
    chosen = best4 > 0.5
    kept = [jnp.where(sel[a_] & chosen, pa[a_], 0.0) for a_ in range(EXPERTS_PER_GROUP)]
    tot = None
    for a_ in range(EXPERTS_PER_GROUP):
        tot = kept[a_] if tot is None else tot + kept[a_]
    tot = jnp.sum(tot, axis=0, keepdims=True)
    gates = [jnp.where(sel[a_] & chosen, pa[a_] / tot, 0.0) for a_ in range(EXPERTS_PER_GROUP)]

    onehot = jnp.concatenate([best4, jnp.zeros((SUBLANES - N_GROUPS, tm), F32)], axis=0)
    oh_b = jnp.concatenate([onehot, jnp.zeros((LANES - SUBLANES, tm), F32)], axis=0).astype(BF16)
    incl = _dg(onehot.astype(BF16), tri_ref[...])
    slot = None
    offset = jnp.zeros((1, 1), F32)
    for gi in range(N_GROUPS):
        t = onehot[gi:gi + 1] * (offset + incl[gi:gi + 1] - 1.0)
        slot = t if slot is None else slot + t
        offset = offset + incl[gi:gi + 1, tm - 1:tm]
    slot_ref[0] = slot.astype(jnp.int32)
    cnt_ref[0] = _dg(jnp.ones((SUBLANES, tm), BF16), oh_b, NT)[0:1]
    rows = jnp.concatenate(gates + [slot, jnp.zeros((LANES - N_EXPERTS - 1, tm), F32)], axis=0)
    meta_ref[...] = jnp.transpose(rows)


META_SLOT = N_EXPERTS
N_ROUTE_INPUTS = 3


def _gate_lane(group, j):
    return j * N_GROUPS + group


def _route_operands(w_router, b_router, tm):
    idx = np.arange(tm)
    tri = jnp.asarray(idx[:, None] <= idx[None, :], dtype=BF16)
    by_member = lambda a: a.reshape(N_GROUPS, EXPERTS_PER_GROUP, -1).transpose(1, 0, 2).reshape(N_EXPERTS, -1)
    wr = jnp.pad(by_member(w_router.T).T, ((0, 0), (0, LANES - N_EXPERTS))).astype(BF16)
    ins = [wr, by_member(b_router), tri]
    return ins, [_full(a.shape) for a in ins]


def _route_outputs(n, tm):
    nt = n // tm
    specs = [pl.BlockSpec((tm, LANES), lambda i: (i, 0)), pl.BlockSpec((1, 1, tm), lambda i: (i, 0, 0)),
             pl.BlockSpec((1, 1, LANES), lambda i: (i, 0, 0))]
    shapes = [jax.ShapeDtypeStruct((n, LANES), F32), jax.ShapeDtypeStruct((nt, 1, tm), jnp.int32),
              jax.ShapeDtypeStruct((nt, 1, LANES), F32)]
    return specs, shapes


def _router_body(x_ref, *refs):
    _route(x_ref[...], *refs)


def _router(x2d, w_router, b_router, *, tm):
    n = x2d.shape[0]
    ins, in_specs = _route_operands(w_router, b_router, tm)
    out_specs, out_shape = _route_outputs(n, tm)
    return pl.pallas_call(
        _router_body, grid=(n // tm,),
        in_specs=[pl.BlockSpec((tm, D_MODEL), lambda i: (i, 0))] + in_specs,
        out_specs=out_specs, out_shape=out_shape,
        compiler_params=_params(("arbitrary",)), name="router",
    )(x2d, *ins)


def _moe_ln_body(x_ref, gates_ref, wg_ref, wu_ref, wd_ref, g_ref, b_ref, o_ref, acc_ref, xb_ref):
    gi = pl.program_id(1)

    @pl.when(gi == 0)
    def _():
        xb_ref[...] = x_ref[...].astype(BF16)
        acc_ref[...] = jnp.zeros_like(acc_ref)

    xb = xb_ref[...]
    lane = lax.broadcasted_iota(jnp.int32, gates_ref.shape, 1)
    hs = []
    for j in range(EXPERTS_PER_GROUP):
        hg = _dg(xb, wg_ref[0, j])
        hu = _dg(xb, wu_ref[0, j])
        gate = jnp.sum(jnp.where(lane == _gate_lane(gi, j), gates_ref[...], 0.0), axis=1, keepdims=True)
        hs.append((hg * _sigmoid(hg) * hu * gate).astype(BF16))
    acc_ref[...] += _dg(jnp.concatenate(hs, axis=1), wd_ref[0])

    @pl.when(gi == N_GROUPS - 1)
    def _():
        o_ref[...] = _layer_norm(ALPHA * x_ref[...] + acc_ref[...], g_ref[...], b_ref[...])


def _moe_ln(x2d, gates, wg, wu, wd, ln_g, ln_b, *, tm):
    n = x2d.shape[0]
    tok = pl.BlockSpec((tm, D_MODEL), lambda i, e: (i, 0))
    wspec = pl.BlockSpec((1, EXPERTS_PER_GROUP, D_MODEL, D_FF), lambda i, e: (e, 0, 0, 0))
    return pl.pallas_call(
        _moe_ln_body, grid=(n // tm, N_GROUPS),
        in_specs=[tok, pl.BlockSpec((tm, LANES), lambda i, e: (i, 0)), wspec, wspec,
                  pl.BlockSpec((1, EXPERTS_PER_GROUP * D_FF, D_MODEL), lambda i, e: (e, 0, 0)),
                  pl.BlockSpec((1, D_MODEL), lambda i, e: (0, 0)), pl.BlockSpec((1, D_MODEL), lambda i, e: (0, 0))],
        out_specs=tok, out_shape=jax.ShapeDtypeStruct((n, D_MODEL), F32),
        scratch_shapes=[pltpu.VMEM((tm, D_MODEL), F32), pltpu.VMEM((tm, D_MODEL), BF16)],
        compiler_params=_params(("arbitrary", "arbitrary")), name="moe_ln",
    )(x2d, gates, wg, wu, wd, ln_g.reshape(1, -1), ln_b.reshape(1, -1))


def _moe_sorted_body(start_ref, nwin_ref, x_ref, meta_ref, slot_ref, wg_ref, wu_ref, wd_ref, g_ref, b_ref, o_ref,
                     xs_ref, gs_ref, ys_ref, *, tm, wn):
    i = pl.program_id(0)

    @pl.when(i == 0)
    def _():
        xs_ref[tm:, :] = jnp.zeros((wn, D_MODEL), BF16)
        gs_ref[tm:, :] = jnp.zeros((wn, LANES), F32)

    x = x_ref[...]
    meta = meta_ref[...]
    slot_iota = lax.broadcasted_iota(jnp.int32, (tm, tm), 0)
    perm = jnp.where(slot_iota == slot_ref[0], 1.0, 0.0).astype(BF16)
    xs_ref[:tm, :] = _dg(perm, x.astype(BF16)).astype(BF16)
    gs = None
    for part in _split(meta, 2):
        t = _dg(perm, part)
        gs = t if gs is None else gs + t
    gs_ref[:tm, :] = gs
    ys_ref[...] = jnp.zeros_like(ys_ref)

    for gi in range(N_GROUPS):
        first = start_ref[i * N_GROUPS + gi]

        def window(w, carry, gi=gi, first=first):
            st = pl.multiple_of(first + w * wn, PACK)
            xw = xs_ref[pl.ds(st, wn), :]
            gw = gs_ref[pl.ds(st, wn), :]
            hs = []
            for j in range(EXPERTS_PER_GROUP):
                e = _gate_lane(gi, j)
                hg = _dg(xw, wg_ref[gi, j])
                hu = _dg(xw, wu_ref[gi, j])
                hs.append((hg * _sigmoid(hg) * hu * gw[:, e:e + 1]).astype(BF16))
            ys_ref[pl.ds(st, wn), :] += _dg(jnp.concatenate(hs, axis=1), wd_ref[gi])
            return carry

        lax.fori_loop(0, nwin_ref[i * N_GROUPS + gi], window, 0)

    tok_slot = meta[:, META_SLOT:META_SLOT + 1]
    lane_slot = lax.broadcasted_iota(jnp.int32, (tm, tm), 1).astype(F32)
    unperm = jnp.where(tok_slot == lane_slot, 1.0, 0.0).astype(BF16)
    y_hi, y_lo = _split(ys_ref[:tm, :], 2)
    y = _dg(unperm, y_hi) + _dg(unperm, y_lo)
    o_ref[...] = _layer_norm(ALPHA * x + y, g_ref[...], b_ref[...])


def _moe_sorted_ln(x2d, meta, slot_rows, counts, wg, wu, wd, ln_g, ln_b, *, tm, wn):
    n = x2d.shape[0]
    nt = n // tm
    assert wn % PACK == 0 and tm % PACK == 0
    cnt = counts[:, 0, :N_GROUPS].astype(jnp.int32)
    offs = jnp.cumsum(cnt, axis=1) - cnt
    first = (offs // PACK) * PACK
    nwin = jnp.where(cnt > 0, (offs - first + cnt + wn - 1) // wn, 0)
    tok = lambda w: pl.BlockSpec((tm, w), lambda i, *_: (i, 0))
    resident = lambda a: pl.BlockSpec(a.shape, lambda i, *_: (0,) * a.ndim, pipeline_mode=pl.Buffered(1))
    vec = pl.BlockSpec((1, D_MODEL), lambda i, *_: (0, 0))
    grid_spec = pltpu.PrefetchScalarGridSpec(
        num_scalar_prefetch=2, grid=(nt,),
        in_specs=[tok(D_MODEL), tok(LANES), pl.BlockSpec((1, 1, tm), lambda i, *_: (i, 0, 0)),
                  resident(wg), resident(wu), resident(wd), vec, vec],
        out_specs=tok(D_MODEL),
        scratch_shapes=[pltpu.VMEM((tm + wn, D_MODEL), BF16), pltpu.VMEM((tm + wn, LANES), F32),
                        pltpu.VMEM((tm + wn, D_MODEL), F32)])
    return pl.pallas_call(
        functools.partial(_moe_sorted_body, tm=tm, wn=wn), grid_spec=grid_spec,
        out_shape=jax.ShapeDtypeStruct((n, D_MODEL), F32),
        compiler_params=_params(("arbitrary",)), name="moe_sorted_ln",
    )(first.reshape(-1), nwin.reshape(-1), x2d, meta, slot_rows, wg, wu, wd, ln_g.reshape(1, -1), ln_b.reshape(1, -1))


HALO = 16


def _pool_ln_body(x_ref, halo_ref, wp_ref, ps_ref, g_ref, b_ref, *rest, tm, seq_len, start_pos):
    o_ref = rest[0] if len(rest) == 1 else rest[N_ROUTE_INPUTS]
    i = pl.program_id(0)
    x = x_ref[...]
    idx = lax.broadcasted_iota(jnp.int32, (tm + HALO, 1), 0)
    pos_e = (i * tm - HALO + idx) & (seq_len - 1)
    cnt_pos = (pos_e[HALO:] + (start_pos + 1)).astype(F32)
    mixed = []
    for gi, w in enumerate(POOL_WINDOWS):
        cols = slice(gi * POOL_GROUP, (gi + 1) * POOL_GROUP)
        xg = x[:, cols]
        s = jnp.concatenate([halo_ref[:, cols], xg], axis=0)
        step = 1
        while step < w:
            s = s + jnp.where(pos_e >= step, pltpu.roll(s, step, axis=0), 0.0)
            step *= 2
        pooled = s[HALO:] / jnp.minimum(float(w), cnt_pos) - xg
        mixed.append(_bdot(pooled, wp_ref[gi]))
    h = jnp.concatenate(mixed, axis=1) * ps_ref[...]
    out = _layer_norm(ALPHA * x + h, g_ref[...], b_ref[...])
    o_ref[...] = out
    if len(rest) > 1:
        _route(out, *rest[:N_ROUTE_INPUTS], *rest[N_ROUTE_INPUTS + 1:])


def _pool_ln(x2d, w_pool, pool_scale, ln_g, ln_b, *, tm, seq_len, start_pos, router=None):
    n = x2d.shape[0]
    assert seq_len & (seq_len - 1) == 0 and seq_len >= HALO and tm % HALO == 0
    tok = pl.BlockSpec((tm, D_MODEL), lambda i: (i, 0))
    ins = [x2d, x2d, w_pool.astype(BF16), pool_scale.reshape(1, -1), ln_g.reshape(1, -1), ln_b.reshape(1, -1)]
    in_specs = [tok, pl.BlockSpec((HALO, D_MODEL), lambda i: (jnp.maximum(i * (tm // HALO) - 1, 0), 0)),
                _full(w_pool.shape), _full((1, D_MODEL)), _full((1, D_MODEL)), _full((1, D_MODEL))]
    out_specs, out_shape = [tok], [jax.ShapeDtypeStruct((n, D_MODEL), F32)]
    if router is not None:
        r_ins, r_specs = _route_operands(*router, tm)
        ins, in_specs = ins + r_ins, in_specs + r_specs
        r_out_specs, r_out_shape = _route_outputs(n, tm)
        out_specs, out_shape = out_specs + r_out_specs, out_shape + r_out_shape
    res = pl.pallas_call(
        functools.partial(_pool_ln_body, tm=tm, seq_len=seq_len, start_pos=start_pos), grid=(n // tm,),
        in_specs=in_specs, out_specs=out_specs, out_shape=out_shape,
        compiler_params=_params(("arbitrary",)), name="pool_ln",
    )(*ins)
    return res[0] if router is None else res


def _pad_lora_cols(w, width):
    return jnp.pad(w, ((0, 0), (0, LORA_PAD - width)))


def _even_layer_weights(prm, i):
    w_in = prm['w_in'][i]
    o = 3 * D_A
    cols = lambda w: jnp.concatenate(
        [w[:, :o], _pad_lora_cols(w[:, o:o + W_LORA], W_LORA),
         _pad_lora_cols(w[:, o + W_LORA:o + W_LORA + A_LORA], A_LORA),
         _pad_lora_cols(w[:, o + W_LORA + A_LORA:RWKV_COLS], G_LORA)], axis=1)
    pad_rows = lambda w: jnp.pad(w, ((0, LORA_PAD - w.shape[0]), (0, 0))).astype(BF16)
    vec = lambda a: a.reshape(1, -1)
    return dict(
        wr=cols(w_in[:, :RWKV_COLS]).astype(BF16), ws=w_in[:, RWKV_COLS:].astype(BF16),
        mu=cols(prm['tshift_mu'][i].reshape(1, -1)),
        w0=vec(prm['decay_w0'][i]), ww2=pad_rows(prm['decay_w2'][i]),
        a0=vec(prm['iclr_a0'][i]), wa2=pad_rows(prm['iclr_a2'][i]), wg2=pad_rows(prm['gate_w2'][i]),
        k_k=vec(prm['k_k'][i]), k_a=vec(prm['k_a'][i]), r_k=vec(prm['r_k'][i]),
    )


def _group_experts(w):
    return w.astype(BF16).reshape(DEPTH, N_GROUPS, EXPERTS_PER_GROUP, D_MODEL, D_FF)


MOE_SORT_TILE = 512
MOE_WINDOW = 160


def _moe_weights(prm, layer):
    return (prm['wg'][layer], prm['wu'][layer], prm['wd'][layer], prm['ln_ffn_g'][layer], prm['ln_ffn_b'][layer])


def _moe_block(x2d, prm, layer, *, tm):
    meta, _, _ = _router(x2d, prm['w_router'], prm['b_router'], tm=min(tm, 512))
    return _moe_ln(x2d, meta, *_moe_weights(prm, layer), tm=tm)


PROMPT_CHUNKS_PER_STEP = 8
PROMPT_ATTN_BLOCKS_PER_STEP = 4
SAMPLE_CHUNK = 16
SAMPLE_QROWS = SUBLANES
SAMPLE_SEQS_PER_STEP = 8


def _prompt_trunk(x, prm):
    bsz, t, _ = x.shape
    n = bsz * t
    x0 = x.reshape(n, D_MODEL)
    wts = _even_layer_weights(prm, 0)
    rt, kt, bt, at, v, bonus, g, gl, swa = _inproj_prep(x0, None, wts, seq_len=t, chunk=CHUNK, tm=256)
    s0 = jnp.zeros((bsz, N_HEADS, HEAD, HEAD), F32)
    ya, s_new = _rwkv_scan(rt, kt, bt, at, v, bonus, g, gl, prm['lnx_g'][0].reshape(1, -1),
                           prm['lnx_b'][0].reshape(1, -1), s0, n_seq=bsz, seq_len=t, chunk=CHUNK,
                           nc=min(PROMPT_CHUNKS_PER_STEP, t // CHUNK))
    n_blk = t // WINDOW
    qb = min(PROMPT_ATTN_BLOCKS_PER_STEP, n_blk)
    yb = _swa(swa, swa, lambda b, j: (b * n_blk + jnp.maximum(j * qb - 1, 0), 2), prm['rel_bias'],
              prm['attn_sinks'][0], n_seq=bsz, n_blk=n_blk, qr=WINDOW, fresh=True, per_step=qb)
    tm = MOE_SORT_TILE
    assert n % tm == 0
    router = (prm['w_router'], prm['b_router'])
    x1, *routing = _outproj_ln(ya, yb, x0, prm['w_out'][0], prm['ln_mix_g'][0], prm['ln_mix_b'][0], tm=tm,
                               router=router)
    x2 = _moe_sorted_ln(x1, *routing, *_moe_weights(prm, 0), tm=tm, wn=MOE_WINDOW)
    x3, *routing = _pool_ln(x2, prm['w_pool'][0], prm['pool_scale'][0], prm['ln_mix_g'][1], prm['ln_mix_b'][1],
                            tm=tm, seq_len=t, start_pos=0, router=router)
    x4 = _moe_sorted_ln(x3, *routing, *_moe_weights(prm, 1), tm=tm, wn=MOE_WINDOW)
    swa3 = swa.reshape(bsz, t, SWA_COLS)
    k_new = swa3[:, t - WINDOW:, D_B:D_B + KV_COLS].reshape(bsz, WINDOW, KV_HEADS, HEAD)
    v_new = swa3[:, t - WINDOW:, D_B + KV_COLS:].reshape(bsz, WINDOW, KV_HEADS, HEAD)
    pool_new = x2.reshape(bsz, t, D_MODEL)[:, t - POOL_KEEP:]
    return (x4.reshape(bsz, t, D_MODEL), s_new[None], x[:, -1][None], k_new[None], v_new[None], pool_new[None])


def _sample_trunk(x, st_rwkv, st_shift, c_k, c_v, st_pool, prm):
    bsz = x.shape[0]
    x0 = x.reshape(bsz, D_MODEL)
    wts = _even_layer_weights(prm, 0)
    outs = _inproj_prep(x0, st_shift[0], wts, seq_len=1, chunk=1, tm=bsz)
    gl, swa = outs[7], outs[8]
    ya, s_new = _rwkv_scan(*outs[:7], gl.reshape(bsz, 1, D_A), prm['lnx_g'][0].reshape(1, -1),
                           prm['lnx_b'][0].reshape(1, -1), st_rwkv[0], n_seq=bsz, seq_len=SAMPLE_CHUNK,
                           chunk=SAMPLE_CHUNK, nc=SAMPLE_SEQS_PER_STEP)
    kv_prev = jnp.concatenate([c_k[0].reshape(bsz * WINDOW, KV_COLS), c_v[0].reshape(bsz * WINDOW, KV_COLS)], axis=-1)
    yb = _swa(swa, kv_prev, lambda b, j: (b, 0), prm['rel_bias'], prm['attn_sinks'][0], n_seq=bsz, n_blk=1,
              qr=SAMPLE_QROWS, fresh=False, per_step=SAMPLE_SEQS_PER_STEP)
    x1 = _outproj_ln(ya, yb, x0, prm['w_out'][0], prm['ln_mix_g'][0], prm['ln_mix_b'][0], tm=bsz)
    x2 = _moe_block(x1, prm, 0, tm=bsz)
    xcat = jnp.concatenate([st_pool[0], x2[:, None]], axis=1)
    x3 = _pool_ln(xcat.reshape(bsz * HALO, D_MODEL), prm['w_pool'][0], prm['pool_scale'][0],
                  prm['ln_mix_g'][1], prm['ln_mix_b'][1], tm=min(512, bsz * HALO), seq_len=HALO,
                  start_pos=PAST_LEN - POOL_KEEP)
    x3 = x3.reshape(bsz, HALO, D_MODEL)[:, -1]
    x4 = _moe_block(x3, prm, 1, tm=bsz)
    k_new = jnp.concatenate([c_k[0][:, 1:], swa[:, D_B:D_B + KV_COLS].reshape(bsz, 1, KV_HEADS, HEAD)], axis=1)
    v_new = jnp.concatenate([c_v[0][:, 1:], swa[:, D_B + KV_COLS:].reshape(bsz, 1, KV_HEADS, HEAD)], axis=1)
    return (x4.reshape(bsz, 1, D_MODEL), s_new[None], x[:, -1][None], k_new[None], v_new[None],
            xcat[:, 1:][None])


def _prepare_params(raw):
    prm = dict(raw)
    prm['r_k'] = raw['r_k'].reshape(raw['r_k'].shape[0], -1)
    prm['wg'] = _group_experts(raw['w_ex_gate'])
    prm['wu'] = _group_experts(raw['w_ex_up'])
    prm['wd'] = raw['w_ex_down'].astype(BF16).reshape(DEPTH, N_GROUPS, EXPERTS_PER_GROUP * D_FF, D_MODEL)
    return prm


def kernel(x_prompt, x_sample, state_rwkv, state_shift, cache_swa_k, cache_swa_v, state_pool, w_in, tshift_mu,
           decay_w0, decay_w2, iclr_a0, iclr_a2, gate_w2, k_k, k_a, r_k, lnx_g, lnx_b, attn_sinks, rel_bias, w_out,
           w_pool, pool_scale, ln_mix_g, ln_mix_b, ln_ffn_g, ln_ffn_b, w_router, b_router, w_ex_gate, w_ex_up,
           w_ex_down):
    prm = _prepare_params(dict(
        w_in=w_in, tshift_mu=tshift_mu, decay_w0=decay_w0, decay_w2=decay_w2, iclr_a0=iclr_a0, iclr_a2=iclr_a2,
        gate_w2=gate_w2, k_k=k_k, k_a=k_a, r_k=r_k, lnx_g=lnx_g, lnx_b=lnx_b, attn_sinks=attn_sinks,
        rel_bias=rel_bias, w_out=w_out, w_pool=w_pool, pool_scale=pool_scale, ln_mix_g=ln_mix_g, ln_mix_b=ln_mix_b,
        ln_ffn_g=ln_ffn_g, ln_ffn_b=ln_ffn_b, w_router=w_router, b_router=b_router, w_ex_gate=w_ex_gate,
        w_ex_up=w_ex_up, w_ex_down=w_ex_down))
    y_p, rwkv_p, shift_p, k_p, v_p, pool_p = _prompt_trunk(x_prompt, prm)
    y_s, rwkv_s, shift_s, k_s, v_s, pool_s = _sample_trunk(x_sample, state_rwkv, state_shift, cache_swa_k,
                                                            cache_swa_v, state_pool, prm)
    return (y_p, y_s, rwkv_p, rwkv_s, shift_p, shift_s, k_p, k_s, v_p, v_s, pool_p, pool_s)
```

```python
import functools
import math

import numpy as np
import jax
import jax.numpy as jnp
from jax import lax
from jax.experimental import pallas as pl
from jax.experimental.pallas import tpu as pltpu

F32 = jnp.float32
BF16 = jnp.bfloat16

D_MODEL = 1024
DEPTH = 2
PAST_LEN = 16384
D_A = 512
HEAD = 64
N_HEADS = 8
W_LORA, A_LORA, G_LORA = 32, 64, 96
RWKV_COLS = 3 * D_A + W_LORA + A_LORA + G_LORA
D_B = 512
KV_HEADS = 2
Q_PER_KV = 4
KV_COLS = KV_HEADS * HEAD
SWA_COLS = D_B + 2 * KV_COLS
WINDOW = 128
N_BUCKETS = 32
MAX_EXACT = 16
REL_MAX_DIST = 128
POOL_WINDOWS = (2, 4, 8, 16)
POOL_GROUP = 256
POOL_KEEP = 15
N_EXPERTS = 16
N_GROUPS = 4
EXPERTS_PER_GROUP = 4
D_FF = 256
ALPHA = (2.0 * DEPTH) ** 0.25
LN_EPS = 1e-5
LNX_EPS = 64e-5
NEG_INF = -1e30

LANES = 128
SUBLANES = 8
PACK = 16
VMEM_LIMIT_BYTES = 56 * 1024 * 1024

LORA_PAD = LANES
RWKV_PAD_COLS = 3 * D_A + 3 * LORA_PAD
CHUNK = 64

NN = ((1,), (0,))
NT = ((1,), (1,))
TN = ((0,), (0,))


def _dg(a, b, dims=NN):
    return lax.dot_general(a, b, (dims, ((), ())), preferred_element_type=F32)


def _bdot(a, b, dims=NN):
    return _dg(a.astype(BF16), b.astype(BF16), dims)


def _split(x, n):
    parts, rem = [], x
    for i in range(n):
        p = rem.astype(BF16)
        parts.append(p)
        if i + 1 < n:
            rem = rem - p.astype(F32)
    return parts


def _mm(a, b, dims=NN, passes=3):
    if passes == 1:
        return _bdot(a, b, dims)
    ah, al = _split(a, 2)
    bh, bl = _split(b, 2)
    return _dg(ah, bh, dims) + (_dg(ah, bl, dims) + _dg(al, bh, dims))


def _dot_exact_rhs(x, ones_bf16, n=3):
    out = None
    for p in _split(x, n):
        t = _dg(p, ones_bf16)
        out = t if out is None else out + t
    return out


def _dot_exact_lhs(ones_bf16, x, n=3):
    out = None
    for p in _split(x, n):
        t = _dg(ones_bf16, p)
        out = t if out is None else out + t
    return out


def _sigmoid(x):
    return 1.0 / (1.0 + jnp.exp(-x))


def _layer_norm(z, g, b):
    mu = jnp.mean(z, axis=-1, keepdims=True)
    d = z - mu
    var = jnp.mean(d * d, axis=-1, keepdims=True)
    return d * lax.rsqrt(var + LN_EPS) * g + b


def _params(sem):
    return pltpu.CompilerParams(dimension_semantics=sem, vmem_limit_bytes=VMEM_LIMIT_BYTES)


def _full(shape):
    nd = len(shape)
    return pl.BlockSpec(shape, lambda *_: (0,) * nd)


def _inproj_prep_body(*refs, tm, tiles_per_seq, chunk, has_prev):
    if has_prev:
        x_ref, xp_ref = refs[0], refs[1]
        refs = refs[2:]
    else:
        x_ref, xp_ref = refs[0], None
        refs = refs[1:]
    (wr_ref, ws_ref, mu_ref, w0_ref, ww2_ref, a0_ref, wa2_ref, wg2_ref, kk_ref, ka_ref, rk_ref,
     ones_ref, tri_ref,
     rt_ref, kt_ref, bt_ref, at_ref, v_ref, bonus_ref, g_ref, gl_ref, swa_ref, carry_ref) = refs

    i = pl.program_id(0)
    xb = x_ref[...].astype(BF16)
    pr = _dg(xb, wr_ref[...])
    swa_ref[...] = _dg(xb, ws_ref[...])

    if has_prev:
        shifted = _dg(xp_ref[...].astype(BF16), wr_ref[...])
    else:
        @pl.when(i == 0)
        def _():
            carry_ref[...] = jnp.zeros_like(carry_ref)

        first = (i % tiles_per_seq) == 0
        prev_last = jnp.where(first, 0.0, carry_ref[0:1, :])
        rolled = pltpu.roll(pr, 1, axis=0)
        row = lax.broadcasted_iota(jnp.int32, (tm, 1), 0)
        shifted = jnp.where(row == 0, prev_last, rolled)
        carry_ref[0:1, :] = pr[tm - 1:tm, :]
    m = pr + mu_ref[...] * (shifted - pr)

    r = m[:, 0:D_A]
    k = m[:, D_A:2 * D_A]
    v = m[:, 2 * D_A:3 * D_A]
    wd = m[:, 3 * D_A:3 * D_A + LORA_PAD]
    ad = m[:, 3 * D_A + LORA_PAD:3 * D_A + 2 * LORA_PAD]
    gd = m[:, 3 * D_A + 2 * LORA_PAD:3 * D_A + 3 * LORA_PAD]

    nz = -(w0_ref[...] + _bdot(jnp.tanh(wd), ww2_ref[...]))
    softplus = jnp.maximum(nz, 0.0) + jnp.log1p(jnp.exp(-jnp.abs(nz)))
    w_log = -softplus - 0.5
    logw = -jnp.exp(w_log)
    a = _sigmoid(a0_ref[...] + _bdot(ad, wa2_ref[...]))
    g = _bdot(_sigmoid(gd), wg2_ref[...])

    ones_bd = ones_ref[...]
    half = D_A // 2

    def head_sums(z):
        return jnp.concatenate([_dot_exact_rhs(z[:, :half], ones_bd, 2), _dot_exact_rhs(z[:, half:], ones_bd, 2)],
                               axis=1)

    kkr = k * kk_ref[...]
    nrm = jnp.sqrt(head_sums(kkr * kkr))
    kk = kkr / jnp.maximum(nrm, 1e-12)
    k2 = k * (1.0 + (a - 1.0) * ka_ref[...])
    bonus_ref[...] = head_sums(r * k2 * rk_ref[...]) * v

    if chunk > 1:
        cum = _dot_exact_lhs(tri_ref[...], logw, 2)
    else:
        cum = logw
    gam = jnp.exp(cum)
    inv = jnp.exp(-cum)
    rt_ref[...] = r * gam
    kt_ref[...] = k2 * inv
    bt_ref[...] = kk * a * inv
    at_ref[...] = -kk * jnp.exp(cum - logw)
    v_ref[...] = v
    g_ref[...] = g
    if chunk > 1:
        for c in range(tm // chunk):
            gl_ref[c] = gam[(c + 1) * chunk - 1:(c + 1) * chunk, :]
    else:
        gl_ref[...] = gam


def _inproj_prep(x2d, x_prev, wts, *, seq_len, chunk, tm):
    n = x2d.shape[0]
    assert n % tm == 0 and (x_prev is not None or seq_len % tm == 0)
    assert tm % chunk == 0
    has_prev = x_prev is not None
    row = lambda w: pl.BlockSpec((tm, w), lambda i: (i, 0))
    ins = [x2d] + ([x_prev] if has_prev else [])
    in_specs = [row(D_MODEL)] + ([row(D_MODEL)] if has_prev else [])
    consts = [wts['wr'], wts['ws'], wts['mu'], wts['w0'], wts['ww2'], wts['a0'], wts['wa2'], wts['wg2'],
              wts['k_k'], wts['k_a'], wts['r_k'], _ones_block_diag(D_A // 2, HEAD), _tri_block_diag(tm, chunk)]
    ins += consts
    in_specs += [_full(c.shape) for c in consts]
    if chunk > 1:
        gl_shape = jax.ShapeDtypeStruct((n // chunk, 1, D_A), F32)
        gl_spec = pl.BlockSpec((tm // chunk, 1, D_A), lambda i: (i, 0, 0))
    else:
        gl_shape = jax.ShapeDtypeStruct((n, D_A), F32)
        gl_spec = row(D_A)
    out_shape = [jax.ShapeDtypeStruct((n, D_A), F32)] * 7 + [gl_shape, jax.ShapeDtypeStruct((n, SWA_COLS), F32)]
    out_specs = [row(D_A)] * 7 + [gl_spec, row(SWA_COLS)]
    body = functools.partial(_inproj_prep_body, tm=tm, tiles_per_seq=max(seq_len // tm, 1), chunk=chunk,
                             has_prev=has_prev)
    return pl.pallas_call(
        body, grid=(n // tm,), in_specs=in_specs, out_specs=out_specs, out_shape=out_shape,
        scratch_shapes=[pltpu.VMEM((SUBLANES, RWKV_PAD_COLS), F32)],
        compiler_params=_params(("arbitrary",)), name="inproj_prep",
    )(*ins)


def _ones_block_diag(n, blk):
    idx = np.arange(n) // blk
    return jnp.asarray(idx[:, None] == idx[None, :], dtype=BF16)


def _tri_block_diag(n, blk):
    idx = np.arange(n)
    same = (idx[:, None] // blk) == (idx[None, :] // blk)
    return jnp.asarray(same & (idx[:, None] >= idx[None, :]), dtype=BF16)


PAIR = 2 * HEAD
N_PAIRS = N_HEADS // 2


def _scan_body(rt_ref, kt_ref, bt_ref, at_ref, v_ref, bonus_ref, g_ref, gl_ref, lng_ref, lnb_ref, s0_ref,
               y_ref, sout_ref, S_ref, *, C, nc, one_chunk_seqs):
    step = pl.program_id(1)

    def pair_state(ref, s, p):
        return jnp.concatenate([ref[s, 2 * p], ref[s, 2 * p + 1]], axis=1)

    if not one_chunk_seqs:
        @pl.when(step == 0)
        def _():
            for p in range(N_PAIRS):
                S_ref[p] = pair_state(s0_ref, 0, p)

    head_a = lax.broadcasted_iota(jnp.int32, (1, PAIR), 1) < HEAD
    rowc = lax.broadcasted_iota(jnp.int32, (C, 2 * C), 0)
    colc = lax.broadcasted_iota(jnp.int32, (C, 2 * C), 1)
    first_c = colc < C
    col_in = jnp.where(first_c, colc, colc - C)
    strict = rowc > col_in
    incl = rowc >= col_in
    eye_c = (rowc == col_in).astype(F32)
    r128 = lax.broadcasted_iota(jnp.int32, (PAIR, PAIR), 0)
    c128 = lax.broadcasted_iota(jnp.int32, (PAIR, PAIR), 1)
    same_head = (r128 < HEAD) == (c128 < HEAD)
    eye128 = (r128 == c128).astype(F32)

    def by_head(x):
        xb = x.astype(BF16)
        z = jnp.zeros_like(xb)
        return jnp.concatenate([jnp.where(head_a, xb, z), jnp.where(head_a, z, xb)], axis=0)

    def head_mean(z):
        za = jnp.sum(jnp.where(head_a, z, 0.0), axis=-1, keepdims=True)
        zb = jnp.sum(jnp.where(head_a, 0.0, z), axis=-1, keepdims=True)
        return jnp.where(head_a, za, zb) * (1.0 / HEAD)

    n_fold = int(math.log2(C)) - 1
    zero_blk = jnp.zeros((2 * C, PAIR), BF16)
    items = [(ci, p) for ci in range(nc) for p in range(N_PAIRS)]
    each = lambda f, *lists: [f(*xs) for xs in zip(*lists)]

    def load(ref):
        if one_chunk_seqs:
            pad = jnp.zeros((C - 1, PAIR), F32)
            return [jnp.concatenate([ref[ci:ci + 1, p * PAIR:(p + 1) * PAIR], pad], axis=0) for ci, p in items]
        return [ref[ci * C:(ci + 1) * C, p * PAIR:(p + 1) * PAIR] for ci, p in items]

    R, K, Bm, A, V = load(rt_ref), load(kt_ref), load(bt_ref), load(at_ref), load(v_ref)
    big = each(lambda a, r, b, k: _dg(jnp.concatenate([a, r], axis=0).astype(BF16),
                                      jnp.concatenate([by_head(b), by_head(k)], axis=0), NT), A, R, Bm, K)
    a_ab = [jnp.where(strict, x[:C, :2 * C], 0.0) for x in big]
    a_ak = [jnp.where(strict, x[:C, 2 * C:], 0.0) for x in big]
    l_rb = [jnp.where(incl, x[C:, :2 * C], 0.0) for x in big]
    l_rk = [jnp.where(incl, x[C:, 2 * C:], 0.0) for x in big]
    def blocks(x):
        xb = x.astype(BF16)
        z = jnp.zeros_like(xb)
        return jnp.concatenate([jnp.where(first_c, xb, z), jnp.where(first_c, z, xb)], axis=0)

    X = [_dg(x.astype(BF16), blocks(x)) for x in a_ab]
    T = [eye_c + x for x in a_ab]
    for k in range(1, n_fold + 1):
        if k < n_fold:
            res = each(lambda x, t: _dg(jnp.concatenate([x, t], axis=0).astype(BF16), blocks(x)), X, T)
            X = [r_[:C] for r_ in res]
            T = each(lambda t, r_: t + r_[C:], T, res)
        else:
            T = each(lambda t, x: t + _dg(t.astype(BF16), blocks(x)), T, X)
    akv = each(lambda m_, v_: _dg(m_.astype(BF16), by_head(v_)), a_ak, V)
    pq = each(lambda t, a, q: _dg(t.astype(BF16), jnp.concatenate([by_head(a), by_head(q)], axis=1)), T, A, akv)
    P = [x[:, :PAIR] for x in pq]
    Q = [x[:, PAIR:] for x in pq]
    wz = each(lambda lb, lk, p_, q_, v_: _dg(
        jnp.concatenate([lb, lk], axis=1).astype(BF16),
        jnp.concatenate([jnp.concatenate([by_head(p_), by_head(q_)], axis=1),
                         jnp.concatenate([zero_blk, by_head(v_)], axis=1)], axis=0)), l_rb, l_rk, P, Q, V)
    W = each(lambda r, x: r + x[:, :PAIR], R, wz)
    Z = [x[:, PAIR:] for x in wz]
    ptb = each(lambda p_, b: _bdot(p_, b, TN), P, Bm)
    gfull = each(lambda q_, v_, b, k: _bdot(jnp.concatenate([q_, v_], axis=0), jnp.concatenate([b, k], axis=0), TN),
                 Q, V, Bm, K)

    def put_state(s, p, val):
        sout_ref[s, 2 * p] = val[:, :HEAD]
        sout_ref[s, 2 * p + 1] = val[:, HEAD:]

    state = [None if one_chunk_seqs else S_ref[p] for p in range(N_PAIRS)]
    for idx, (ci, p) in enumerate(items):
        rows = slice(ci, ci + 1) if one_chunk_seqs else slice(ci * C, (ci + 1) * C)
        ln = slice(p * PAIR, (p + 1) * PAIR)
        g_c = gl_ref[ci, :, ln]
        M = (eye128 + jnp.where(same_head, ptb[idx], 0.0)) * g_c
        G = jnp.where(head_a, gfull[idx][:HEAD], gfull[idx][HEAD:]) * g_c
        S0 = pair_state(s0_ref, ci, p) if one_chunk_seqs else state[p]
        Y = _dg(W[idx].astype(BF16), by_head(S0), NT) + Z[idx]
        S1 = _bdot(S0, M) + G
        if one_chunk_seqs:
            put_state(ci, p, S1)
        else:
            state[p] = S1

        if one_chunk_seqs:
            Y = Y[0:1]
        mu = head_mean(Y)
        d = Y - mu
        var = head_mean(d * d)
        yn = d * lax.rsqrt(var + LNX_EPS) * lng_ref[:, ln] + lnb_ref[:, ln]
        y_ref[rows, ln] = ((yn + bonus_ref[rows, ln]) * g_ref[rows, ln]).astype(y_ref.dtype)

    if not one_chunk_seqs:
        for p in range(N_PAIRS):
            S_ref[p] = state[p]

        @pl.when(step == pl.num_programs(1) - 1)
        def _():
            for p in range(N_PAIRS):
                put_state(0, p, state[p])


def _rwkv_scan(rt, kt, bt, at, v, bonus, g, gl, lnx_g, lnx_b, s0, *, n_seq, seq_len, chunk, nc):
    one_chunk_seqs = seq_len == chunk
    if one_chunk_seqs:
        assert n_seq % nc == 0
        grid = (n_seq // nc, 1)
        blk = lambda b, c: b
        st = pl.BlockSpec((nc, N_HEADS, HEAD, HEAD), lambda b, c: (b, 0, 0, 0))
        rows, n_rows = nc, n_seq
    else:
        rows, n_rows = chunk * nc, n_seq * seq_len
        assert seq_len % rows == 0
        nsteps = seq_len // rows
        grid = (n_seq, nsteps)
        blk = lambda b, c: b * nsteps + c
        st = pl.BlockSpec((1, N_HEADS, HEAD, HEAD), lambda b, c: (b, 0, 0, 0))
    row = pl.BlockSpec((rows, D_A), lambda b, c: (blk(b, c), 0))
    vec = pl.BlockSpec((1, D_A), lambda b, c: (0, 0))
    return pl.pallas_call(
        functools.partial(_scan_body, C=chunk, nc=nc, one_chunk_seqs=one_chunk_seqs),
        grid=grid,
        in_specs=[row] * 7 + [pl.BlockSpec((nc, 1, D_A), lambda b, c: (blk(b, c), 0, 0)), vec, vec, st],
        out_specs=[row, st],
        out_shape=[jax.ShapeDtypeStruct((n_rows, D_A), BF16 if n_rows % PACK == 0 and rows % PACK == 0 else F32),
                   jax.ShapeDtypeStruct((n_seq, N_HEADS, HEAD, HEAD), F32)],
        scratch_shapes=[pltpu.VMEM((N_PAIRS, HEAD, PAIR), F32)],
        compiler_params=_params(("arbitrary", "arbitrary")), name="rwkv_scan",
    )(rt, kt, bt, at, v, bonus, g, gl, lnx_g, lnx_b, s0)


def _bucket_ranges():
    d = np.arange(WINDOW + 1)
    scaled = np.log(np.maximum(d, MAX_EXACT).astype(np.float32) / MAX_EXACT) / math.log(REL_MAX_DIST / MAX_EXACT)
    large = np.minimum(MAX_EXACT + (scaled * (N_BUCKETS - MAX_EXACT)).astype(np.int32), N_BUCKETS - 1)
    bucket = np.where(d < MAX_EXACT, d, large)
    frac = scaled.astype(np.float64) * (N_BUCKETS - MAX_EXACT)
    near = np.abs(frac - np.round(frac)) < 1e-3
    assert all(int(x) in (MAX_EXACT, REL_MAX_DIST) for x in d[(d >= MAX_EXACT) & near])
    ranges = []
    for b in range(N_BUCKETS):
        hit = d[bucket == b]
        ranges.append((int(hit.min()), int(hit.max())) if hit.size else None)
    return tuple(ranges)


def _swa_body(rb_ref, sink_ref, cur_ref, *rest, qr, fresh, per_step, ranges):
    prevs, (y_ref, bias_ref, sinkcol_ref) = rest[:-3], rest[-3:]
    W = WINDOW
    nk = W + qr

    @pl.when((pl.program_id(0) == 0) & (pl.program_id(1) == 0))
    def _():
        qi = lax.broadcasted_iota(jnp.int32, (qr, nk), 0)
        kj = lax.broadcasted_iota(jnp.int32, (qr, nk), 1)
        d = qi + W - kj
        valid = (d >= 0) & (d <= W)
        for hd in range(N_HEADS):
            t = jnp.zeros((qr, nk), F32)
            for bkt, rng in enumerate(ranges):
                if rng is not None:
                    t = jnp.where((d >= rng[0]) & (d <= rng[1]), rb_ref[bkt * N_HEADS + hd], t)
            kvh, grp = hd // Q_PER_KV, hd % Q_PER_KV
            rows = slice(grp * qr, (grp + 1) * qr)
            bias_ref[kvh, rows, :] = jnp.where(valid, t, NEG_INF)
            bias_ref[KV_HEADS + kvh, rows, :] = jnp.where(valid & (kj >= W), t, NEG_INF)
            sinkcol_ref[kvh, rows, :] = jnp.full((qr, 1), sink_ref[hd], F32)

    first_table = jnp.where(pl.program_id(1) == 0, KV_HEADS, 0) if fresh else 0
    items = [(sub, kvh) for sub in range(per_step) for kvh in range(KV_HEADS)]
    each = lambda f, *lists: [f(*xs) for xs in zip(*lists)]

    def own_rows(sub, col):
        if fresh:
            return cur_ref[sub * qr:(sub + 1) * qr, col:col + HEAD]
        return jnp.concatenate([cur_ref[sub:sub + 1, col:col + HEAD], jnp.zeros((qr - 1, HEAD), F32)], axis=0)

    def kv_rows(sub, kvh, is_v):
        col = D_B + is_v * KV_COLS + kvh * HEAD
        if not fresh:
            before = prevs[is_v][sub, :, kvh, :]
        elif sub > 0:
            before = cur_ref[(sub - 1) * qr:sub * qr, col:col + HEAD]
        else:
            before = prevs[0][:, col - D_B:col - D_B + HEAD]
        return jnp.concatenate([before, own_rows(sub, col)], axis=0).astype(BF16)

    kb = [kv_rows(sub, kvh, 0) for sub, kvh in items]
    vb = [kv_rows(sub, kvh, 1) for sub, kvh in items]
    q4 = [(jnp.concatenate([own_rows(sub, (kvh * Q_PER_KV + gq) * HEAD) for gq in range(Q_PER_KV)], axis=0)
           * (HEAD ** -0.5)).astype(BF16) for sub, kvh in items]
    s = [_dg(q, k, NT) + bias_ref[(first_table if sub == 0 else 0) + kvh] for q, k, (sub, kvh) in zip(q4, kb, items)]
    sink = [sinkcol_ref[kvh] for _, kvh in items]
    m = each(lambda s_, sk: jnp.maximum(jnp.max(s_, axis=-1, keepdims=True), sk), s, sink)
    p = each(lambda s_, m_: jnp.exp(s_ - m_), s, m)
    den = each(lambda p_, sk, m_: jnp.sum(p_, axis=-1, keepdims=True) + jnp.exp(sk - m_), p, sink, m)
    o = each(lambda p_, v_, d_: _dg(p_.astype(BF16), v_) / d_, p, vb, den)
    for sub in range(per_step):
        outs = [o[sub * KV_HEADS + kvh][gq * qr:(gq + 1) * qr] for kvh in range(KV_HEADS) for gq in range(Q_PER_KV)]
        y = jnp.concatenate(outs, axis=1).astype(y_ref.dtype)
        if fresh:
            y_ref[sub * qr:(sub + 1) * qr, :] = y
        else:
            y_ref[sub:sub + 1, :] = y[0:1]


def _swa(cur2d, prevs, rel_bias, sinks, *, n_seq, n_blk, qr, fresh, per_step=1):
    smem = pl.BlockSpec(memory_space=pltpu.SMEM)
    if fresh:
        assert qr == WINDOW and n_blk % per_step == 0
        blocks = n_blk
        n_blk //= per_step
        prev_specs = [pl.BlockSpec((WINDOW, 2 * KV_COLS),
                                   lambda b, j: (b * blocks + jnp.maximum(j * per_step - 1, 0), 2))]
    else:
        assert n_blk == 1 and n_seq % per_step == 0
        n_seq //= per_step
        prev_specs = [pl.BlockSpec((per_step, WINDOW, KV_HEADS, HEAD), lambda b, j: (b, 0, 0, 0))] * 2
    qrows = per_step * qr if fresh else per_step
    out_dtype = BF16 if fresh else F32
    return pl.pallas_call(
        functools.partial(_swa_body, qr=qr, fresh=fresh, per_step=per_step, ranges=_bucket_ranges()),
        grid=(n_seq, n_blk),
        in_specs=[smem, smem, pl.BlockSpec((qrows, SWA_COLS), lambda b, j: (b * n_blk + j, 0))] + prev_specs,
        out_specs=pl.BlockSpec((qrows, D_B), lambda b, j: (b * n_blk + j, 0)),
        out_shape=jax.ShapeDtypeStruct((n_seq * n_blk * qrows, D_B), out_dtype),
        scratch_shapes=[pltpu.VMEM((2 * KV_HEADS, Q_PER_KV * qr, WINDOW + qr), F32),
                        pltpu.VMEM((KV_HEADS, Q_PER_KV * qr, 1), F32)],
        compiler_params=_params(("arbitrary", "arbitrary")), name="swa",
    )(rel_bias.reshape(-1), sinks, cur2d, *prevs)


def _outproj_ln_body(ya_ref, yb_ref, x_ref, wa_ref, wb_ref, g_ref, b_ref, *rest):
    o_ref = rest[0] if len(rest) == 1 else rest[N_ROUTE_INPUTS]
    h = _dg(ya_ref[...].astype(BF16), wa_ref[...]) + _dg(yb_ref[...].astype(BF16), wb_ref[...])
    out = _layer_norm(ALPHA * x_ref[...] + h, g_ref[...], b_ref[...])
    o_ref[...] = out
    if len(rest) > 1:
        _route(out, *rest[:N_ROUTE_INPUTS], *rest[N_ROUTE_INPUTS + 1:])


def _outproj_ln(ya, yb, x2d, w_out, ln_g, ln_b, *, tm, router=None):
    n = x2d.shape[0]
    half = pl.BlockSpec((tm, D_A), lambda i: (i, 0))
    full = pl.BlockSpec((tm, D_MODEL), lambda i: (i, 0))
    wa, wb = w_out[:D_A].astype(BF16), w_out[D_A:].astype(BF16)
    ins = [ya, yb, x2d, wa, wb, ln_g.reshape(1, -1), ln_b.reshape(1, -1)]
    in_specs = [half, half, full, _full(wa.shape), _full(wb.shape), _full((1, D_MODEL)), _full((1, D_MODEL))]
    out_specs, out_shape = [full], [jax.ShapeDtypeStruct((n, D_MODEL), F32)]
    if router is not None:
        r_ins, r_specs = _route_operands(*router, tm)
        ins, in_specs = ins + r_ins, in_specs + r_specs
        r_out_specs, r_out_shape = _route_outputs(n, tm)
        out_specs, out_shape = out_specs + r_out_specs, out_shape + r_out_shape
    res = pl.pallas_call(
        _outproj_ln_body, grid=(n // tm,), in_specs=in_specs, out_specs=out_specs, out_shape=out_shape,
        compiler_params=_params(("arbitrary",)), name="outproj_ln",
    )(*ins)
    return res[0] if router is None else res


def _route(x, wr_ref, br_ref, tri_ref, meta_ref, slot_ref, cnt_ref):
    logits = jnp.transpose(_dg(x.astype(BF16), wr_ref[...]))[:N_EXPERTS] + br_ref[...]
    z = jnp.exp(logits - jnp.max(logits, axis=0, keepdims=True))
    probs = z / jnp.sum(z, axis=0, keepdims=True)
    pa = [probs[a_ * N_GROUPS:(a_ + 1) * N_GROUPS] for a_ in range(EXPERTS_PER_GROUP)]

    sel = []
    for a_ in range(EXPERTS_PER_GROUP):
        rank = None
        for j in range(EXPERTS_PER_GROUP):
            if j == a_:
                continue
            ahead = ((pa[j] >= pa[a_]) if j < a_ else (pa[j] > pa[a_])).astype(jnp.int32)
            rank = ahead if rank is None else rank + ahead
        sel.append(rank < 2)
    score = None
    for a_ in range(EXPERTS_PER_GROUP):
        t = jnp.where(sel[a_], pa[a_], 0.0)
        score = t if score is None else score + t
    srow = [score[gi:gi + 1] for gi in range(N_GROUPS)]
    best = []
    for gi in range(N_GROUPS):
        ok = None
        for j in range(N_GROUPS):
            if j == gi:
                continue
            c_ = (srow[gi] > srow[j]) if j < gi else (srow[gi] >= srow[j])
            ok = c_ if ok is None else (ok & c_)
        best.append(ok.astype(F32))
    tm = x.shape[0]
    best4 = jnp.concatenate(best, axis=0)
    chosen = best4 > 0.5
    kept = [jnp.where(sel[a_] & chosen, pa[a_], 0.0) for a_ in range(EXPERTS_PER_GROUP)]
    tot = None
    for a_ in range(EXPERTS_PER_GROUP):
        tot = kept[a_] if tot is None else tot + kept[a_]
    tot = jnp.sum(tot, axis=0, keepdims=True)
    gates = [jnp.where(sel[a_] & chosen, pa[a_] / tot, 0.0) for a_ in range(EXPERTS_PER_GROUP)]

    onehot = jnp.concatenate([best4, jnp.zeros((SUBLANES - N_GROUPS, tm), F32)], axis=0)
    oh_b = jnp.concatenate([onehot, jnp.zeros((LANES - SUBLANES, tm), F32)], axis=0).astype(BF16)
    incl = _dg(onehot.astype(BF16), tri_ref[...])
    slot = None
    offset = jnp.zeros((1, 1), F32)
    for gi in range(N_GROUPS):
        t = onehot[gi:gi + 1] * (offset + incl[gi:gi + 1] - 1.0)
        slot = t if slot is None else slot + t
        offset = offset + incl[gi:gi + 1, tm - 1:tm]
    slot_ref[0] = slot.astype(jnp.int32)
    cnt_ref[0] = _dg(jnp.ones((SUBLANES, tm), BF16), oh_b, NT)[0:1]
    rows = jnp.concatenate(gates + [slot, jnp.zeros((LANES - N_EXPERTS - 1, tm), F32)], axis=0)
    meta_ref[...] = jnp.transpose(rows)


META_SLOT = N_EXPERTS
N_ROUTE_INPUTS = 3


def _gate_lane(group, j):
    return j * N_GROUPS + group


def _route_operands(w_router, b_router, tm):
    idx = np.arange(tm)
    tri = jnp.asarray(idx[:, None] <= idx[None, :], dtype=BF16)
    by_member = lambda a: a.reshape(N_GROUPS, EXPERTS_PER_GROUP, -1).transpose(1, 0, 2).reshape(N_EXPERTS, -1)
    wr = jnp.pad(by_member(w_router.T).T, ((0, 0), (0, LANES - N_EXPERTS))).astype(BF16)
    ins = [wr, by_member(b_router), tri]
    return ins, [_full(a.shape) for a in ins]


def _route_outputs(n, tm):
    nt = n // tm
    specs = [pl.BlockSpec((tm, LANES), lambda i: (i, 0)), pl.BlockSpec((1, 1, tm), lambda i: (i, 0, 0)),
             pl.BlockSpec((1, 1, LANES), lambda i: (i, 0, 0))]
    shapes = [jax.ShapeDtypeStruct((n, LANES), F32), jax.ShapeDtypeStruct((nt, 1, tm), jnp.int32),
              jax.ShapeDtypeStruct((nt, 1, LANES), F32)]
    return specs, shapes


def _router_body(x_ref, *refs):
    _route(x_ref[...], *refs)


def _router(x2d, w_router, b_router, *, tm):
    n = x2d.shape[0]
    ins, in_specs = _route_operands(w_router, b_router, tm)
    out_specs, out_shape = _route_outputs(n, tm)
    return pl.pallas_call(
        _router_body, grid=(n // tm,),
        in_specs=[pl.BlockSpec((tm, D_MODEL), lambda i: (i, 0))] + in_specs,
        out_specs=out_specs, out_shape=out_shape,
        compiler_params=_params(("arbitrary",)), name="router",
    )(x2d, *ins)


def _moe_ln_body(x_ref, gates_ref, wg_ref, wu_ref, wd_ref, g_ref, b_ref, o_ref, acc_ref, xb_ref):
    gi = pl.program_id(1)

    @pl.when(gi == 0)
    def _():
        xb_ref[...] = x_ref[...].astype(BF16)
        acc_ref[...] = jnp.zeros_like(acc_ref)

    xb = xb_ref[...]
    lane = lax.broadcasted_iota(jnp.int32, gates_ref.shape, 1)
    hs = []
    for j in range(EXPERTS_PER_GROUP):
        hg = _dg(xb, wg_ref[0, j])
        hu = _dg(xb, wu_ref[0, j])
        gate = jnp.sum(jnp.where(lane == _gate_lane(gi, j), gates_ref[...], 0.0), axis=1, keepdims=True)
        hs.append((hg * _sigmoid(hg) * hu * gate).astype(BF16))
    acc_ref[...] += _dg(jnp.concatenate(hs, axis=1), wd_ref[0])

    @pl.when(gi == N_GROUPS - 1)
    def _():
        o_ref[...] = _layer_norm(ALPHA * x_ref[...] + acc_ref[...], g_ref[...], b_ref[...])


def _moe_ln(x2d, gates, wg, wu, wd, ln_g, ln_b, *, tm):
    n = x2d.shape[0]
    tok = pl.BlockSpec((tm, D_MODEL), lambda i, e: (i, 0))
    wspec = pl.BlockSpec((1, EXPERTS_PER_GROUP, D_MODEL, D_FF), lambda i, e: (e, 0, 0, 0))
    return pl.pallas_call(
        _moe_ln_body, grid=(n // tm, N_GROUPS),
        in_specs=[tok, pl.BlockSpec((tm, LANES), lambda i, e: (i, 0)), wspec, wspec,
                  pl.BlockSpec((1, EXPERTS_PER_GROUP * D_FF, D_MODEL), lambda i, e: (e, 0, 0)),
                  pl.BlockSpec((1, D_MODEL), lambda i, e: (0, 0)), pl.BlockSpec((1, D_MODEL), lambda i, e: (0, 0))],
        out_specs=tok, out_shape=jax.ShapeDtypeStruct((n, D_MODEL), F32),
        scratch_shapes=[pltpu.VMEM((tm, D_MODEL), F32), pltpu.VMEM((tm, D_MODEL), BF16)],
        compiler_params=_params(("arbitrary", "arbitrary")), name="moe_ln",
    )(x2d, gates, wg, wu, wd, ln_g.reshape(1, -1), ln_b.reshape(1, -1))


def _moe_sorted_body(start_ref, nwin_ref, x_ref, meta_ref, slot_ref, wg_ref, wu_ref, wd_ref, g_ref, b_ref, o_ref,
                     xs_ref, gs_ref, ys_ref, *, tm, wn):
    i = pl.program_id(0)

    @pl.when(i == 0)
    def _():
        xs_ref[tm:, :] = jnp.zeros((wn, D_MODEL), BF16)
        gs_ref[tm:, :] = jnp.zeros((wn, LANES), F32)

    x = x_ref[...]
    meta = meta_ref[...]
    slot_iota = lax.broadcasted_iota(jnp.int32, (tm, tm), 0)
    perm = jnp.where(slot_iota == slot_ref[0], 1.0, 0.0).astype(BF16)
    xs_ref[:tm, :] = _dg(perm, x.astype(BF16)).astype(BF16)
    gs = None
    for part in _split(meta, 2):
        t = _dg(perm, part)
        gs = t if gs is None else gs + t
    gs_ref[:tm, :] = gs
    ys_ref[...] = jnp.zeros_like(ys_ref)

    for gi in range(N_GROUPS):
        first = start_ref[i * N_GROUPS + gi]

        def window(w, carry, gi=gi, first=first):
            st = pl.multiple_of(first + w * wn, PACK)
            xw = xs_ref[pl.ds(st, wn), :]
            gw = gs_ref[pl.ds(st, wn), :]
            hs = []
            for j in range(EXPERTS_PER_GROUP):
                e = _gate_lane(gi, j)
                hg = _dg(xw, wg_ref[gi, j])
                hu = _dg(xw, wu_ref[gi, j])
                hs.append((hg * _sigmoid(hg) * hu * gw[:, e:e + 1]).astype(BF16))
            ys_ref[pl.ds(st, wn), :] += _dg(jnp.concatenate(hs, axis=1), wd_ref[gi])
            return carry

        lax.fori_loop(0, nwin_ref[i * N_GROUPS + gi], window, 0)

    tok_slot = meta[:, META_SLOT:META_SLOT + 1]
    lane_slot = lax.broadcasted_iota(jnp.int32, (tm, tm), 1).astype(F32)
    unperm = jnp.where(tok_slot == lane_slot, 1.0, 0.0).astype(BF16)
    y = _bdot(unperm, ys_ref[:tm, :])
    o_ref[...] = _layer_norm(ALPHA * x + y, g_ref[...], b_ref[...])


def _moe_sorted_ln(x2d, meta, slot_rows, counts, wg, wu, wd, ln_g, ln_b, *, tm, wn):
    n = x2d.shape[0]
    nt = n // tm
    assert wn % PACK == 0 and tm % PACK == 0
    cnt = counts[:, 0, :N_GROUPS].astype(jnp.int32)
    offs = jnp.cumsum(cnt, axis=1) - cnt
    first = (offs // PACK) * PACK
    nwin = jnp.where(cnt > 0, (offs - first + cnt + wn - 1) // wn, 0)
    tok = lambda w: pl.BlockSpec((tm, w), lambda i, *_: (i, 0))
    resident = lambda a: pl.BlockSpec(a.shape, lambda i, *_: (0,) * a.ndim, pipeline_mode=pl.Buffered(1))
    vec = pl.BlockSpec((1, D_MODEL), lambda i, *_: (0, 0))
    grid_spec = pltpu.PrefetchScalarGridSpec(
        num_scalar_prefetch=2, grid=(nt,),
        in_specs=[tok(D_MODEL), tok(LANES), pl.BlockSpec((1, 1, tm), lambda i, *_: (i, 0, 0)),
                  resident(wg), resident(wu), resident(wd), vec, vec],
        out_specs=tok(D_MODEL),
        scratch_shapes=[pltpu.VMEM((tm + wn, D_MODEL), BF16), pltpu.VMEM((tm + wn, LANES), F32),
                        pltpu.VMEM((tm + wn, D_MODEL), F32)])
    return pl.pallas_call(
        functools.partial(_moe_sorted_body, tm=tm, wn=wn), grid_spec=grid_spec,
        out_shape=jax.ShapeDtypeStruct((n, D_MODEL), F32),
        compiler_params=_params(("arbitrary",)), name="moe_sorted_ln",
    )(first.reshape(-1), nwin.reshape(-1), x2d, meta, slot_rows, wg, wu, wd, ln_g.reshape(1, -1), ln_b.reshape(1, -1))


HALO = 16


def _pool_ln_body(x_ref, halo_ref, wp_ref, ps_ref, g_ref, b_ref, *rest, tm, seq_len, start_pos):
    o_ref = rest[0] if len(rest) == 1 else rest[N_ROUTE_INPUTS]
    i = pl.program_id(0)
    x = x_ref[...]
    idx = lax.broadcasted_iota(jnp.int32, (tm + HALO, 1), 0)
    pos_e = (i * tm - HALO + idx) & (seq_len - 1)
    cnt_pos = (pos_e[HALO:] + (start_pos + 1)).astype(F32)
    mixed = []
    for gi, w in enumerate(POOL_WINDOWS):
        cols = slice(gi * POOL_GROUP, (gi + 1) * POOL_GROUP)
        xg = x[:, cols]
        s = jnp.concatenate([halo_ref[:, cols], xg], axis=0)
        step = 1
        while step < w:
            s = s + jnp.where(pos_e >= step, pltpu.roll(s, step, axis=0), 0.0)
            step *= 2
        pooled = s[HALO:] / jnp.minimum(float(w), cnt_pos) - xg
        mixed.append(_bdot(pooled, wp_ref[gi]))
    h = jnp.concatenate(mixed, axis=1) * ps_ref[...]
    out = _layer_norm(ALPHA * x + h, g_ref[...], b_ref[...])
    o_ref[...] = out
    if len(rest) > 1:
        _route(out, *rest[:N_ROUTE_INPUTS], *rest[N_ROUTE_INPUTS + 1:])


def _pool_ln(x2d, w_pool, pool_scale, ln_g, ln_b, *, tm, seq_len, start_pos, router=None):
    n = x2d.shape[0]
    assert seq_len & (seq_len - 1) == 0 and seq_len >= HALO and tm % HALO == 0
    tok = pl.BlockSpec((tm, D_MODEL), lambda i: (i, 0))
    ins = [x2d, x2d, w_pool.astype(BF16), pool_scale.reshape(1, -1), ln_g.reshape(1, -1), ln_b.reshape(1, -1)]
    in_specs = [tok, pl.BlockSpec((HALO, D_MODEL), lambda i: (jnp.maximum(i * (tm // HALO) - 1, 0), 0)),
                _full(w_pool.shape), _full((1, D_MODEL)), _full((1, D_MODEL)), _full((1, D_MODEL))]
    out_specs, out_shape = [tok], [jax.ShapeDtypeStruct((n, D_MODEL), F32)]
    if router is not None:
        r_ins, r_specs = _route_operands(*router, tm)
        ins, in_specs = ins + r_ins, in_specs + r_specs
        r_out_specs, r_out_shape = _route_outputs(n, tm)
        out_specs, out_shape = out_specs + r_out_specs, out_shape + r_out_shape
    res = pl.pallas_call(
        functools.partial(_pool_ln_body, tm=tm, seq_len=seq_len, start_pos=start_pos), grid=(n // tm,),
        in_specs=in_specs, out_specs=out_specs, out_shape=out_shape,
        compiler_params=_params(("arbitrary",)), name="pool_ln",
    )(*ins)
    return res[0] if router is None else res


def _pad_lora_cols(w, width):
    return jnp.pad(w, ((0, 0), (0, LORA_PAD - width)))


def _even_layer_weights(prm, i):
    w_in = prm['w_in'][i]
    o = 3 * D_A
    cols = lambda w: jnp.concatenate(
        [w[:, :o], _pad_lora_cols(w[:, o:o + W_LORA], W_LORA),
         _pad_lora_cols(w[:, o + W_LORA:o + W_LORA + A_LORA], A_LORA),
         _pad_lora_cols(w[:, o + W_LORA + A_LORA:RWKV_COLS], G_LORA)], axis=1)
    pad_rows = lambda w: jnp.pad(w, ((0, LORA_PAD - w.shape[0]), (0, 0))).astype(BF16)
    vec = lambda a: a.reshape(1, -1)
    return dict(
        wr=cols(w_in[:, :RWKV_COLS]).astype(BF16), ws=w_in[:, RWKV_COLS:].astype(BF16),
        mu=cols(prm['tshift_mu'][i].reshape(1, -1)),
        w0=vec(prm['decay_w0'][i]), ww2=pad_rows(prm['decay_w2'][i]),
        a0=vec(prm['iclr_a0'][i]), wa2=pad_rows(prm['iclr_a2'][i]), wg2=pad_rows(prm['gate_w2'][i]),
        k_k=vec(prm['k_k'][i]), k_a=vec(prm['k_a'][i]), r_k=vec(prm['r_k'][i]),
    )


def _group_experts(w):
    return w.astype(BF16).reshape(DEPTH, N_GROUPS, EXPERTS_PER_GROUP, D_MODEL, D_FF)


MOE_SORT_TILE = 512
MOE_WINDOW = 160


def _moe_weights(prm, layer):
    return (prm['wg'][layer], prm['wu'][layer], prm['wd'][layer], prm['ln_ffn_g'][layer], prm['ln_ffn_b'][layer])


def _moe_block(x2d, prm, layer, *, tm):
    meta, _, _ = _router(x2d, prm['w_router'], prm['b_router'], tm=min(tm, 512))
    return _moe_ln(x2d, meta, *_moe_weights(prm, layer), tm=tm)


PROMPT_CHUNKS_PER_STEP = 8
PROMPT_ATTN_BLOCKS_PER_STEP = 4
SAMPLE_CHUNK = 16
SAMPLE_QROWS = SUBLANES
SAMPLE_SEQS_PER_STEP = 8


def _prompt_trunk(x, prm):
    bsz, t, _ = x.shape
    n = bsz * t
    x0 = x.reshape(n, D_MODEL)
    wts = _even_layer_weights(prm, 0)
    rt, kt, bt, at, v, bonus, g, gl, swa = _inproj_prep(x0, None, wts, seq_len=t, chunk=CHUNK, tm=512)
    s0 = jnp.zeros((bsz, N_HEADS, HEAD, HEAD), F32)
    ya, s_new = _rwkv_scan(rt, kt, bt, at, v, bonus, g, gl, prm['lnx_g'][0].reshape(1, -1),
                           prm['lnx_b'][0].reshape(1, -1), s0, n_seq=bsz, seq_len=t, chunk=CHUNK,
                           nc=min(PROMPT_CHUNKS_PER_STEP, t // CHUNK))
    n_blk = t // WINDOW
    qb = min(PROMPT_ATTN_BLOCKS_PER_STEP, n_blk)
    yb = _swa(swa, (swa,), prm['rel_bias'], prm['attn_sinks'][0], n_seq=bsz, n_blk=n_blk, qr=WINDOW, fresh=True,
              per_step=qb)
    tm = MOE_SORT_TILE
    assert n % tm == 0
    router = (prm['w_router'], prm['b_router'])
    x1, *routing = _outproj_ln(ya, yb, x0, prm['w_out'][0], prm['ln_mix_g'][0], prm['ln_mix_b'][0], tm=tm,
                               router=router)
    x2 = _moe_sorted_ln(x1, *routing, *_moe_weights(prm, 0), tm=tm, wn=MOE_WINDOW)
    x3, *routing = _pool_ln(x2, prm['w_pool'][0], prm['pool_scale'][0], prm['ln_mix_g'][1], prm['ln_mix_b'][1],
                            tm=tm, seq_len=t, start_pos=0, router=router)
    x4 = _moe_sorted_ln(x3, *routing, *_moe_weights(prm, 1), tm=tm, wn=MOE_WINDOW)
    swa3 = swa.reshape(bsz, t, SWA_COLS)
    k_new = swa3[:, t - WINDOW:, D_B:D_B + KV_COLS].reshape(bsz, WINDOW, KV_HEADS, HEAD)
    v_new = swa3[:, t - WINDOW:, D_B + KV_COLS:].reshape(bsz, WINDOW, KV_HEADS, HEAD)
    pool_new = x2.reshape(bsz, t, D_MODEL)[:, t - POOL_KEEP:]
    return (x4.reshape(bsz, t, D_MODEL), s_new[None], x[:, -1][None], k_new[None], v_new[None], pool_new[None])


def _sample_trunk(x, st_rwkv, st_shift, c_k, c_v, st_pool, prm):
    bsz = x.shape[0]
    x0 = x.reshape(bsz, D_MODEL)
    wts = _even_layer_weights(prm, 0)
    outs = _inproj_prep(x0, st_shift[0], wts, seq_len=1, chunk=1, tm=bsz)
    gl, swa = outs[7], outs[8]
    ya, s_new = _rwkv_scan(*outs[:7], gl.reshape(bsz, 1, D_A), prm['lnx_g'][0].reshape(1, -1),
                           prm['lnx_b'][0].reshape(1, -1), st_rwkv[0], n_seq=bsz, seq_len=SAMPLE_CHUNK,
                           chunk=SAMPLE_CHUNK, nc=SAMPLE_SEQS_PER_STEP)
    yb = _swa(swa, (c_k[0], c_v[0]), prm['rel_bias'], prm['attn_sinks'][0], n_seq=bsz, n_blk=1,
              qr=SAMPLE_QROWS, fresh=False, per_step=SAMPLE_SEQS_PER_STEP)
    x1 = _outproj_ln(ya, yb, x0, prm['w_out'][0], prm['ln_mix_g'][0], prm['ln_mix_b'][0], tm=bsz)
    x2 = _moe_block(x1, prm, 0, tm=bsz)
    xcat = jnp.concatenate([st_pool[0], x2[:, None]], axis=1)
    x3 = _pool_ln(xcat.reshape(bsz * HALO, D_MODEL), prm['w_pool'][0], prm['pool_scale'][0],
                  prm['ln_mix_g'][1], prm['ln_mix_b'][1], tm=min(512, bsz * HALO), seq_len=HALO,
                  start_pos=PAST_LEN - POOL_KEEP)
    x3 = x3.reshape(bsz, HALO, D_MODEL)[:, -1]
    x4 = _moe_block(x3, prm, 1, tm=bsz)
    k_new = jnp.concatenate([c_k[0][:, 1:], swa[:, D_B:D_B + KV_COLS].reshape(bsz, 1, KV_HEADS, HEAD)], axis=1)
    v_new = jnp.concatenate([c_v[0][:, 1:], swa[:, D_B + KV_COLS:].reshape(bsz, 1, KV_HEADS, HEAD)], axis=1)
    return (x4.reshape(bsz, 1, D_MODEL), s_new[None], x[:, -1][None], k_new[None], v_new[None],
            xcat[:, 1:][None])


def _prepare_params(raw):
    prm = dict(raw)
    prm['r_k'] = raw['r_k'].reshape(raw['r_k'].shape[0], -1)
    prm['wg'] = _group_experts(raw['w_ex_gate'])
    prm['wu'] = _group_experts(raw['w_ex_up'])
    prm['wd'] = raw['w_ex_down'].astype(BF16).reshape(DEPTH, N_GROUPS, EXPERTS_PER_GROUP * D_FF, D_MODEL)
    return prm


def kernel(x_prompt, x_sample, state_rwkv, state_shift, cache_swa_k, cache_swa_v, state_pool, w_in, tshift_mu,
           decay_w0, decay_w2, iclr_a0, iclr_a2, gate_w2, k_k, k_a, r_k, lnx_g, lnx_b, attn_sinks, rel_bias, w_out,
           w_pool, pool_scale, ln_mix_g, ln_mix_b, ln_ffn_g, ln_ffn_b, w_router, b_router, w_ex_gate, w_ex_up,
           w_ex_down):
    prm = _prepare_params(dict(
        w_in=w_in, tshift_mu=tshift_mu, decay_w0=decay_w0, decay_w2=decay_w2, iclr_a0=iclr_a0, iclr_a2=iclr_a2,
        gate_w2=gate_w2, k_k=k_k, k_a=k_a, r_k=r_k, lnx_g=lnx_g, lnx_b=lnx_b, attn_sinks=attn_sinks,
        rel_bias=rel_bias, w_out=w_out, w_pool=w_pool, pool_scale=pool_scale, ln_mix_g=ln_mix_g, ln_mix_b=ln_mix_b,
        ln_ffn_g=ln_ffn_g, ln_ffn_b=ln_ffn_b, w_router=w_router, b_router=b_router, w_ex_gate=w_ex_gate,
        w_ex_up=w_ex_up, w_ex_down=w_ex_down))
    y_p, rwkv_p, shift_p, k_p, v_p, pool_p = _prompt_trunk(x_prompt, prm)
    y_s, rwkv_s, shift_s, k_s, v_s, pool_s = _sample_trunk(x_sample, state_rwkv, state_shift, cache_swa_k,
                                                            cache_swa_v, state_pool, prm)
    return (y_p, y_s, rwkv_p, rwkv_s, shift_p, shift_s, k_p, k_s, v_p, v_s, pool_p, pool_s)
```

```python
import functools
import math

import numpy as np
import jax
import jax.numpy as jnp
from jax import lax
from jax.experimental import pallas as pl
from jax.experimental.pallas import tpu as pltpu

F32 = jnp.float32
BF16 = jnp.bfloat16

D_MODEL = 1024
DEPTH = 2
PAST_LEN = 16384
D_A = 512
HEAD = 64
N_HEADS = 8
W_LORA, A_LORA, G_LORA = 32, 64, 96
RWKV_COLS = 3 * D_A + W_LORA + A_LORA + G_LORA
D_B = 512
KV_HEADS = 2
Q_PER_KV = 4
KV_COLS = KV_HEADS * HEAD
SWA_COLS = D_B + 2 * KV_COLS
WINDOW = 128
N_BUCKETS = 32
MAX_EXACT = 16
REL_MAX_DIST = 128
POOL_WINDOWS = (2, 4, 8, 16)
POOL_GROUP = 256
POOL_KEEP = 15
N_EXPERTS = 16
N_GROUPS = 4
EXPERTS_PER_GROUP = 4
D_FF = 256
ALPHA = (2.0 * DEPTH) ** 0.25
LN_EPS = 1e-5
LNX_EPS = 64e-5
NEG_INF = -1e30

LANES = 128
SUBLANES = 8
PACK = 16
VMEM_LIMIT_BYTES = 56 * 1024 * 1024

LORA_PAD = LANES
RWKV_PAD_COLS = 3 * D_A + 3 * LORA_PAD
CHUNK = 64

NN = ((1,), (0,))
NT = ((1,), (1,))
TN = ((0,), (0,))


def _dg(a, b, dims=NN):
    return lax.dot_general(a, b, (dims, ((), ())), preferred_element_type=F32)


def _bdot(a, b, dims=NN):
    return _dg(a.astype(BF16), b.astype(BF16), dims)


def _split(x, n):
    parts, rem = [], x
    for i in range(n):
        p = rem.astype(BF16)
        parts.append(p)
        if i + 1 < n:
            rem = rem - p.astype(F32)
    return parts


def _mm(a, b, dims=NN, passes=3):
    if passes == 1:
        return _bdot(a, b, dims)
    ah, al = _split(a, 2)
    bh, bl = _split(b, 2)
    return _dg(ah, bh, dims) + (_dg(ah, bl, dims) + _dg(al, bh, dims))


def _dot_exact_rhs(x, ones_bf16, n=3):
    out = None
    for p in _split(x, n):
        t = _dg(p, ones_bf16)
        out = t if out is None else out + t
    return out


def _dot_exact_lhs(ones_bf16, x, n=3):
    out = None
    for p in _split(x, n):
        t = _dg(ones_bf16, p)
        out = t if out is None else out + t
    return out


def _sigmoid(x):
    return 1.0 / (1.0 + jnp.exp(-x))


def _layer_norm(z, g, b):
    mu = jnp.mean(z, axis=-1, keepdims=True)
    d = z - mu
    var = jnp.mean(d * d, axis=-1, keepdims=True)
    return d * lax.rsqrt(var + LN_EPS) * g + b


def _params(sem):
    return pltpu.CompilerParams(dimension_semantics=sem, vmem_limit_bytes=VMEM_LIMIT_BYTES)


def _full(shape):
    nd = len(shape)
    return pl.BlockSpec(shape, lambda *_: (0,) * nd)


def _inproj_prep_body(*refs, tm, tiles_per_seq, chunk, has_prev):
    if has_prev:
        x_ref, xp_ref = refs[0], refs[1]
        refs = refs[2:]
    else:
        x_ref, xp_ref = refs[0], None
        refs = refs[1:]
    (wr_ref, ws_ref, mu_ref, w0_ref, ww2_ref, a0_ref, wa2_ref, wg2_ref, kk_ref, ka_ref, rk_ref,
     ones_ref, tri_ref,
     rt_ref, kt_ref, bt_ref, at_ref, v_ref, bonus_ref, g_ref, gl_ref, swa_ref, carry_ref) = refs

    i = pl.program_id(0)
    xb = x_ref[...].astype(BF16)
    pr = _dg(xb, wr_ref[...])
    swa_ref[...] = _dg(xb, ws_ref[...])

    if has_prev:
        shifted = _dg(xp_ref[...].astype(BF16), wr_ref[...])
    else:
        @pl.when(i == 0)
        def _():
            carry_ref[...] = jnp.zeros_like(carry_ref)

        first = (i % tiles_per_seq) == 0
        prev_last = jnp.where(first, 0.0, carry_ref[0:1, :])
        rolled = pltpu.roll(pr, 1, axis=0)
        row = lax.broadcasted_iota(jnp.int32, (tm, 1), 0)
        shifted = jnp.where(row == 0, prev_last, rolled)
        carry_ref[0:1, :] = pr[tm - 1:tm, :]
    m = pr + mu_ref[...] * (shifted - pr)

    r = m[:, 0:D_A]
    k = m[:, D_A:2 * D_A]
    v = m[:, 2 * D_A:3 * D_A]
    wd = m[:, 3 * D_A:3 * D_A + LORA_PAD]
    ad = m[:, 3 * D_A + LORA_PAD:3 * D_A + 2 * LORA_PAD]
    gd = m[:, 3 * D_A + 2 * LORA_PAD:3 * D_A + 3 * LORA_PAD]

    nz = -(w0_ref[...] + _bdot(jnp.tanh(wd), ww2_ref[...]))
    softplus = jnp.maximum(nz, 0.0) + jnp.log1p(jnp.exp(-jnp.abs(nz)))
    w_log = -softplus - 0.5
    logw = -jnp.exp(w_log)
    a = _sigmoid(a0_ref[...] + _bdot(ad, wa2_ref[...]))
    g = _bdot(_sigmoid(gd), wg2_ref[...])

    ones_bd = ones_ref[...]
    half = D_A // 2

    def head_sums(z):
        return jnp.concatenate([_dot_exact_rhs(z[:, :half], ones_bd, 2), _dot_exact_rhs(z[:, half:], ones_bd, 2)],
                               axis=1)

    kkr = k * kk_ref[...]
    nrm = jnp.sqrt(head_sums(kkr * kkr))
    kk = kkr / jnp.maximum(nrm, 1e-12)
    k2 = k * (1.0 + (a - 1.0) * ka_ref[...])
    bonus_ref[...] = head_sums(r * k2 * rk_ref[...]) * v

    if chunk > 1:
        cum = _dot_exact_lhs(tri_ref[...], logw, 2)
    else:
        cum = logw
    gam = jnp.exp(cum)
    inv = jnp.exp(-cum)
    rt_ref[...] = r * gam
    kt_ref[...] = k2 * inv
    bt_ref[...] = kk * a * inv
    at_ref[...] = -kk * jnp.exp(cum - logw)
    v_ref[...] = v
    g_ref[...] = g
    if chunk > 1:
        for c in range(tm // chunk):
            gl_ref[c] = gam[(c + 1) * chunk - 1:(c + 1) * chunk, :]
    else:
        gl_ref[...] = gam


def _inproj_prep(x2d, x_prev, wts, *, seq_len, chunk, tm):
    n = x2d.shape[0]
    assert n % tm == 0 and (x_prev is not None or seq_len % tm == 0)
    assert tm % chunk == 0
    has_prev = x_prev is not None
    row = lambda w: pl.BlockSpec((tm, w), lambda i: (i, 0))
    ins = [x2d] + ([x_prev] if has_prev else [])
    in_specs = [row(D_MODEL)] + ([row(D_MODEL)] if has_prev else [])
    consts = [wts['wr'], wts['ws'], wts['mu'], wts['w0'], wts['ww2'], wts['a0'], wts['wa2'], wts['wg2'],
              wts['k_k'], wts['k_a'], wts['r_k'], _ones_block_diag(D_A // 2, HEAD), _tri_block_diag(tm, chunk)]
    ins += consts
    in_specs += [_full(c.shape) for c in consts]
    if chunk > 1:
        gl_shape = jax.ShapeDtypeStruct((n // chunk, 1, D_A), F32)
        gl_spec = pl.BlockSpec((tm // chunk, 1, D_A), lambda i: (i, 0, 0))
    else:
        gl_shape = jax.ShapeDtypeStruct((n, D_A), F32)
        gl_spec = row(D_A)
    out_shape = [jax.ShapeDtypeStruct((n, D_A), F32)] * 7 + [gl_shape, jax.ShapeDtypeStruct((n, SWA_COLS), F32)]
    out_specs = [row(D_A)] * 7 + [gl_spec, row(SWA_COLS)]
    body = functools.partial(_inproj_prep_body, tm=tm, tiles_per_seq=max(seq_len // tm, 1), chunk=chunk,
                             has_prev=has_prev)
    return pl.pallas_call(
        body, grid=(n // tm,), in_specs=in_specs, out_specs=out_specs, out_shape=out_shape,
        scratch_shapes=[pltpu.VMEM((SUBLANES, RWKV_PAD_COLS), F32)],
        compiler_params=_params(("arbitrary",)), name="inproj_prep",
    )(*ins)


def _ones_block_diag(n, blk):
    idx = np.arange(n) // blk
    return jnp.asarray(idx[:, None] == idx[None, :], dtype=BF16)


def _tri_block_diag(n, blk):
    idx = np.arange(n)
    same = (idx[:, None] // blk) == (idx[None, :] // blk)
    return jnp.asarray(same & (idx[:, None] >= idx[None, :]), dtype=BF16)


PAIR = 2 * HEAD
N_PAIRS = N_HEADS // 2


def _scan_body(rt_ref, kt_ref, bt_ref, at_ref, v_ref, bonus_ref, g_ref, gl_ref, lng_ref, lnb_ref, s0_ref,
               y_ref, sout_ref, S_ref, *, C, nc, one_chunk_seqs):
    step = pl.program_id(1)

    def pair_state(ref, s, p):
        return jnp.concatenate([ref[s, 2 * p], ref[s, 2 * p + 1]], axis=1)

    if not one_chunk_seqs:
        @pl.when(step == 0)
        def _():
            for p in range(N_PAIRS):
                S_ref[p] = pair_state(s0_ref, 0, p)

    head_a = lax.broadcasted_iota(jnp.int32, (1, PAIR), 1) < HEAD
    rowc = lax.broadcasted_iota(jnp.int32, (C, 2 * C), 0)
    colc = lax.broadcasted_iota(jnp.int32, (C, 2 * C), 1)
    first_c = colc < C
    col_in = jnp.where(first_c, colc, colc - C)
    strict = rowc > col_in
    incl = rowc >= col_in
    eye_c = (rowc == col_in).astype(F32)
    r128 = lax.broadcasted_iota(jnp.int32, (PAIR, PAIR), 0)
    c128 = lax.broadcasted_iota(jnp.int32, (PAIR, PAIR), 1)
    same_head = (r128 < HEAD) == (c128 < HEAD)
    eye128 = (r128 == c128).astype(F32)

    def by_head(x):
        xb = x.astype(BF16)
        z = jnp.zeros_like(xb)
        return jnp.concatenate([jnp.where(head_a, xb, z), jnp.where(head_a, z, xb)], axis=0)

    def head_mean(z):
        za = jnp.sum(jnp.where(head_a, z, 0.0), axis=-1, keepdims=True)
        zb = jnp.sum(jnp.where(head_a, 0.0, z), axis=-1, keepdims=True)
        return jnp.where(head_a, za, zb) * (1.0 / HEAD)

    n_fold = int(math.log2(C)) - 1
    zero_blk = jnp.zeros((2 * C, PAIR), BF16)
    items = [(ci, p) for ci in range(nc) for p in range(N_PAIRS)]
    each = lambda f, *lists: [f(*xs) for xs in zip(*lists)]

    def load(ref):
        if one_chunk_seqs:
            pad = jnp.zeros((C - 1, PAIR), F32)
            return [jnp.concatenate([ref[ci:ci + 1, p * PAIR:(p + 1) * PAIR], pad], axis=0) for ci, p in items]
        return [ref[ci * C:(ci + 1) * C, p * PAIR:(p + 1) * PAIR] for ci, p in items]

    R, K, Bm, A, V = load(rt_ref), load(kt_ref), load(bt_ref), load(at_ref), load(v_ref)
    big = each(lambda a, r, b, k: _dg(jnp.concatenate([a, r], axis=0).astype(BF16),
                                      jnp.concatenate([by_head(b), by_head(k)], axis=0), NT), A, R, Bm, K)
    a_ab = [jnp.where(strict, x[:C, :2 * C], 0.0) for x in big]
    a_ak = [jnp.where(strict, x[:C, 2 * C:], 0.0) for x in big]
    l_rb = [jnp.where(incl, x[C:, :2 * C], 0.0) for x in big]
    l_rk = [jnp.where(incl, x[C:, 2 * C:], 0.0) for x in big]
    def blocks(x):
        xb = x.astype(BF16)
        z = jnp.zeros_like(xb)
        return jnp.concatenate([jnp.where(first_c, xb, z), jnp.where(first_c, z, xb)], axis=0)

    X = [_dg(x.astype(BF16), blocks(x)) for x in a_ab]
    T = [eye_c + x for x in a_ab]
    for k in range(1, n_fold + 1):
        if k < n_fold:
            res = each(lambda x, t: _dg(jnp.concatenate([x, t], axis=0).astype(BF16), blocks(x)), X, T)
            X = [r_[:C] for r_ in res]
            T = each(lambda t, r_: t + r_[C:], T, res)
        else:
            T = each(lambda t, x: t + _dg(t.astype(BF16), blocks(x)), T, X)
    akv = each(lambda m_, v_: _dg(m_.astype(BF16), by_head(v_)), a_ak, V)
    pq = each(lambda t, a, q: _dg(t.astype(BF16), jnp.concatenate([by_head(a), by_head(q)], axis=1)), T, A, akv)
    P = [x[:, :PAIR] for x in pq]
    Q = [x[:, PAIR:] for x in pq]
    wz = each(lambda lb, lk, p_, q_, v_: _dg(
        jnp.concatenate([lb, lk], axis=1).astype(BF16),
        jnp.concatenate([jnp.concatenate([by_head(p_), by_head(q_)], axis=1),
                         jnp.concatenate([zero_blk, by_head(v_)], axis=1)], axis=0)), l_rb, l_rk, P, Q, V)
    W = each(lambda r, x: r + x[:, :PAIR], R, wz)
    Z = [x[:, PAIR:] for x in wz]
    ptb = each(lambda p_, b: _bdot(p_, b, TN), P, Bm)
    gfull = each(lambda q_, v_, b, k: _bdot(jnp.concatenate([q_, v_], axis=0), jnp.concatenate([b, k], axis=0), TN),
                 Q, V, Bm, K)

    def put_state(s, p, val):
        sout_ref[s, 2 * p] = val[:, :HEAD]
        sout_ref[s, 2 * p + 1] = val[:, HEAD:]

    state = [None if one_chunk_seqs else S_ref[p] for p in range(N_PAIRS)]
    for idx, (ci, p) in enumerate(items):
        rows = slice(ci, ci + 1) if one_chunk_seqs else slice(ci * C, (ci + 1) * C)
        ln = slice(p * PAIR, (p + 1) * PAIR)
        g_c = gl_ref[ci, :, ln]
        M = (eye128 + jnp.where(same_head, ptb[idx], 0.0)) * g_c
        G = jnp.where(head_a, gfull[idx][:HEAD], gfull[idx][HEAD:]) * g_c
        S0 = pair_state(s0_ref, ci, p) if one_chunk_seqs else state[p]
        Y = _dg(W[idx].astype(BF16), by_head(S0), NT) + Z[idx]
        S1 = _bdot(S0, M) + G
        if one_chunk_seqs:
            put_state(ci, p, S1)
        else:
            state[p] = S1

        if one_chunk_seqs:
            Y = Y[0:1]
        mu = head_mean(Y)
        d = Y - mu
        var = head_mean(d * d)
        yn = d * lax.rsqrt(var + LNX_EPS) * lng_ref[:, ln] + lnb_ref[:, ln]
        y_ref[rows, ln] = ((yn + bonus_ref[rows, ln]) * g_ref[rows, ln]).astype(y_ref.dtype)

    if not one_chunk_seqs:
        for p in range(N_PAIRS):
            S_ref[p] = state[p]

        @pl.when(step == pl.num_programs(1) - 1)
        def _():
            for p in range(N_PAIRS):
                put_state(0, p, state[p])


def _rwkv_scan(rt, kt, bt, at, v, bonus, g, gl, lnx_g, lnx_b, s0, *, n_seq, seq_len, chunk, nc):
    one_chunk_seqs = seq_len == chunk
    if one_chunk_seqs:
        assert n_seq % nc == 0
        grid = (n_seq // nc, 1)
        blk = lambda b, c: b
        st = pl.BlockSpec((nc, N_HEADS, HEAD, HEAD), lambda b, c: (b, 0, 0, 0))
        rows, n_rows = nc, n_seq
    else:
        rows, n_rows = chunk * nc, n_seq * seq_len
        assert seq_len % rows == 0
        nsteps = seq_len // rows
        grid = (n_seq, nsteps)
        blk = lambda b, c: b * nsteps + c
        st = pl.BlockSpec((1, N_HEADS, HEAD, HEAD), lambda b, c: (b, 0, 0, 0))
    row = pl.BlockSpec((rows, D_A), lambda b, c: (blk(b, c), 0))
    vec = pl.BlockSpec((1, D_A), lambda b, c: (0, 0))
    return pl.pallas_call(
        functools.partial(_scan_body, C=chunk, nc=nc, one_chunk_seqs=one_chunk_seqs),
        grid=grid,
        in_specs=[row] * 7 + [pl.BlockSpec((nc, 1, D_A), lambda b, c: (blk(b, c), 0, 0)), vec, vec, st],
        out_specs=[row, st],
        out_shape=[jax.ShapeDtypeStruct((n_rows, D_A), BF16 if n_rows % PACK == 0 and rows % PACK == 0 else F32),
                   jax.ShapeDtypeStruct((n_seq, N_HEADS, HEAD, HEAD), F32)],
        scratch_shapes=[pltpu.VMEM((N_PAIRS, HEAD, PAIR), F32)],
        compiler_params=_params(("arbitrary", "arbitrary")), name="rwkv_scan",
    )(rt, kt, bt, at, v, bonus, g, gl, lnx_g, lnx_b, s0)


def _bucket_ranges():
    d = np.arange(WINDOW + 1)
    scaled = np.log(np.maximum(d, MAX_EXACT).astype(np.float32) / MAX_EXACT) / math.log(REL_MAX_DIST / MAX_EXACT)
    large = np.minimum(MAX_EXACT + (scaled * (N_BUCKETS - MAX_EXACT)).astype(np.int32), N_BUCKETS - 1)
    bucket = np.where(d < MAX_EXACT, d, large)
    frac = scaled.astype(np.float64) * (N_BUCKETS - MAX_EXACT)
    near = np.abs(frac - np.round(frac)) < 1e-3
    assert all(int(x) in (MAX_EXACT, REL_MAX_DIST) for x in d[(d >= MAX_EXACT) & near])
    ranges = []
    for b in range(N_BUCKETS):
        hit = d[bucket == b]
        ranges.append((int(hit.min()), int(hit.max())) if hit.size else None)
    return tuple(ranges)


def _swa_body(rb_ref, sink_ref, cur_ref, *rest, qr, fresh, per_step, ranges):
    prevs, (y_ref, bias_ref, sinkcol_ref) = rest[:-3], rest[-3:]
    W = WINDOW
    nk = W + qr

    @pl.when((pl.program_id(0) == 0) & (pl.program_id(1) == 0))
    def _():
        qi = lax.broadcasted_iota(jnp.int32, (qr, nk), 0)
        kj = lax.broadcasted_iota(jnp.int32, (qr, nk), 1)
        d = qi + W - kj
        valid = (d >= 0) & (d <= W)
        for hd in range(N_HEADS):
            t = jnp.zeros((qr, nk), F32)
            for bkt, rng in enumerate(ranges):
                if rng is not None:
                    t = jnp.where((d >= rng[0]) & (d <= rng[1]), rb_ref[bkt * N_HEADS + hd], t)
            kvh, grp = hd // Q_PER_KV, hd % Q_PER_KV
            rows = slice(grp * qr, (grp + 1) * qr)
            bias_ref[kvh, rows, :] = jnp.where(valid, t, NEG_INF)
            bias_ref[KV_HEADS + kvh, rows, :] = jnp.where(valid & (kj >= W), t, NEG_INF)
            sinkcol_ref[kvh, rows, :] = jnp.full((qr, 1), sink_ref[hd], F32)

    first_table = jnp.where(pl.program_id(1) == 0, KV_HEADS, 0) if fresh else 0
    items = [(sub, kvh) for sub in range(per_step) for kvh in range(KV_HEADS)]
    each = lambda f, *lists: [f(*xs) for xs in zip(*lists)]

    def own_rows(sub, col):
        if fresh:
            return cur_ref[sub * qr:(sub + 1) * qr, col:col + HEAD]
        return jnp.concatenate([cur_ref[sub:sub + 1, col:col + HEAD], jnp.zeros((qr - 1, HEAD), F32)], axis=0)

    def kv_rows(sub, kvh, is_v):
        col = D_B + is_v * KV_COLS + kvh * HEAD
        if not fresh:
            before = prevs[is_v][sub, :, kvh, :]
        elif sub > 0:
            before = cur_ref[(sub - 1) * qr:sub * qr, col:col + HEAD]
        else:
            before = prevs[0][:, col - D_B:col - D_B + HEAD]
        return jnp.concatenate([before, own_rows(sub, col)], axis=0).astype(BF16)

    kb = [kv_rows(sub, kvh, 0) for sub, kvh in items]
    vb = [kv_rows(sub, kvh, 1) for sub, kvh in items]
    q4 = [(jnp.concatenate([own_rows(sub, (kvh * Q_PER_KV + gq) * HEAD) for gq in range(Q_PER_KV)], axis=0)
           * (HEAD ** -0.5)).astype(BF16) for sub, kvh in items]
    s = [_dg(q, k, NT) + bias_ref[(first_table if sub == 0 else 0) + kvh] for q, k, (sub, kvh) in zip(q4, kb, items)]
    sink = [sinkcol_ref[kvh] for _, kvh in items]
    m = each(lambda s_, sk: jnp.maximum(jnp.max(s_, axis=-1, keepdims=True), sk), s, sink)
    p = each(lambda s_, m_: jnp.exp(s_ - m_).astype(BF16), s, m)
    pad = jnp.concatenate([jnp.zeros((nk, LANES - HEAD), BF16), jnp.ones((nk, LANES), BF16)], axis=1)
    ov = each(lambda p_, v_: _dg(p_, jnp.concatenate([v_, pad], axis=1)), p, vb)
    o = each(lambda ov_, sk, m_: ov_[:, :LANES] / (ov_[:, LANES:] + jnp.exp(sk - m_)), ov, sink, m)
    for sub in range(per_step):
        outs = [o[sub * KV_HEADS + kvh][gq * qr:(gq + 1) * qr, :HEAD]
                for kvh in range(KV_HEADS) for gq in range(Q_PER_KV)]
        y = jnp.concatenate(outs, axis=1).astype(y_ref.dtype)
        if fresh:
            y_ref[sub * qr:(sub + 1) * qr, :] = y
        else:
            y_ref[sub:sub + 1, :] = y[0:1]


def _swa(cur2d, prevs, rel_bias, sinks, *, n_seq, n_blk, qr, fresh, per_step=1):
    smem = pl.BlockSpec(memory_space=pltpu.SMEM)
    if fresh:
        assert qr == WINDOW and n_blk % per_step == 0
        blocks = n_blk
        n_blk //= per_step
        prev_specs = [pl.BlockSpec((WINDOW, 2 * KV_COLS),
                                   lambda b, j: (b * blocks + jnp.maximum(j * per_step - 1, 0), 2))]
    else:
        assert n_blk == 1 and n_seq % per_step == 0
        n_seq //= per_step
        prev_specs = [pl.BlockSpec((per_step, WINDOW, KV_HEADS, HEAD), lambda b, j: (b, 0, 0, 0))] * 2
    qrows = per_step * qr if fresh else per_step
    out_dtype = BF16 if fresh else F32
    return pl.pallas_call(
        functools.partial(_swa_body, qr=qr, fresh=fresh, per_step=per_step, ranges=_bucket_ranges()),
        grid=(n_seq, n_blk),
        in_specs=[smem, smem, pl.BlockSpec((qrows, SWA_COLS), lambda b, j: (b * n_blk + j, 0))] + prev_specs,
        out_specs=pl.BlockSpec((qrows, D_B), lambda b, j: (b * n_blk + j, 0)),
        out_shape=jax.ShapeDtypeStruct((n_seq * n_blk * qrows, D_B), out_dtype),
        scratch_shapes=[pltpu.VMEM((2 * KV_HEADS, Q_PER_KV * qr, WINDOW + qr), F32),
                        pltpu.VMEM((KV_HEADS, Q_PER_KV * qr, 1), F32)],
        compiler_params=_params(("arbitrary", "arbitrary")), name="swa",
    )(rel_bias.reshape(-1), sinks, cur2d, *prevs)


def _outproj_ln_body(ya_ref, yb_ref, x_ref, wa_ref, wb_ref, g_ref, b_ref, *rest):
    o_ref = rest[0] if len(rest) == 1 else rest[N_ROUTE_INPUTS]
    h = _dg(ya_ref[...].astype(BF16), wa_ref[...]) + _dg(yb_ref[...].astype(BF16), wb_ref[...])
    out = _layer_norm(ALPHA * x_ref[...] + h, g_ref[...], b_ref[...])
    o_ref[...] = out
    if len(rest) > 1:
        _route(out, *rest[:N_ROUTE_INPUTS], *rest[N_ROUTE_INPUTS + 1:])


def _outproj_ln(ya, yb, x2d, w_out, ln_g, ln_b, *, tm, router=None):
    n = x2d.shape[0]
    half = pl.BlockSpec((tm, D_A), lambda i: (i, 0))
    full = pl.BlockSpec((tm, D_MODEL), lambda i: (i, 0))
    wa, wb = w_out[:D_A].astype(BF16), w_out[D_A:].astype(BF16)
    ins = [ya, yb, x2d, wa, wb, ln_g.reshape(1, -1), ln_b.reshape(1, -1)]
    in_specs = [half, half, full, _full(wa.shape), _full(wb.shape), _full((1, D_MODEL)), _full((1, D_MODEL))]
    out_specs, out_shape = [full], [jax.ShapeDtypeStruct((n, D_MODEL), F32)]
    if router is not None:
        r_ins, r_specs = _route_operands(*router, tm)
        ins, in_specs = ins + r_ins, in_specs + r_specs
        r_out_specs, r_out_shape = _route_outputs(n, tm)
        out_specs, out_shape = out_specs + r_out_specs, out_shape + r_out_shape
    res = pl.pallas_call(
        _outproj_ln_body, grid=(n // tm,), in_specs=in_specs, out_specs=out_specs, out_shape=out_shape,
        compiler_params=_params(("arbitrary",)), name="outproj_ln",
    )(*ins)
    return res[0] if router is None else res


def _route(x, wr_ref, br_ref, tri_ref, meta_ref, slot_ref, cnt_ref):
    logits = jnp.transpose(_dg(x.astype(BF16), wr_ref[...]))[:N_EXPERTS] + br_ref[...]
    z = jnp.exp(logits - jnp.max(logits, axis=0, keepdims=True))
    probs = z / jnp.sum(z, axis=0, keepdims=True)
    pa = [probs[a_ * N_GROUPS:(a_ + 1) * N_GROUPS] for a_ in range(EXPERTS_PER_GROUP)]

    sel = []
    for a_ in range(EXPERTS_PER_GROUP):
        rank = None
        for j in range(EXPERTS_PER_GROUP):
            if j == a_:
                continue
            ahead = ((pa[j] >= pa[a_]) if j < a_ else (pa[j] > pa[a_])).astype(jnp.int32)
            rank = ahead if rank is None else rank + ahead
        sel.append(rank < 2)
    score = None
    for a_ in range(EXPERTS_PER_GROUP):
        t = jnp.where(sel[a_], pa[a_], 0.0)
        score = t if score is None else score + t
    srow = [score[gi:gi + 1] for gi in range(N_GROUPS)]
    best = []
    for gi in range(N_GROUPS):
        ok = None
        for j in range(N_GROUPS):
            if j == gi:
                continue
            c_ = (srow[gi] > srow[j]) if j < gi else (srow[gi] >= srow[j])
            ok = c_ if ok is None else (ok & c_)
        best.append(ok.astype(F32))
    tm = x.shape[0]
    best4 = jnp.concatenate(best, axis=0)
    chosen = best4 > 0.5
    kept = [jnp.where(sel[a_] & chosen, pa[a_], 0.0) for a_ in range(EXPERTS_PER_GROUP)]
    tot = None
    for a_ in range(EXPERTS_PER_GROUP):
        tot = kept[a_] if tot is None else tot + kept[a_]
    tot = jnp.sum(tot, axis=0, keepdims=True)
    gates = [jnp.where(sel[a_] & chosen, pa[a_] / tot, 0.0) for a_ in range(EXPERTS_PER_GROUP)]

    onehot = jnp.concatenate([best4, jnp.zeros((SUBLANES - N_GROUPS, tm), F32)], axis=0)
    oh_b = jnp.concatenate([onehot, jnp.zeros((LANES - SUBLANES, tm), F32)], axis=0).astype(BF16)
    incl = _dg(onehot.astype(BF16), tri_ref[...])
    slot = None
    offset = jnp.zeros((1, 1), F32)
    for gi in range(N_GROUPS):
        t = onehot[gi:gi + 1] * (offset + incl[gi:gi + 1] - 1.0)
        slot = t if slot is None else slot + t
        offset = offset + incl[gi:gi + 1, tm - 1:tm]
    slot_ref[0] = slot.astype(jnp.int32)
    cnt_ref[0] = _dg(jnp.ones((SUBLANES, tm), BF16), oh_b, NT)[0:1]
    rows = jnp.concatenate(gates + [slot, jnp.zeros((LANES - N_EXPERTS - 1, tm), F32)], axis=0)
    meta_ref[...] = jnp.transpose(rows)


META_SLOT = N_EXPERTS
N_ROUTE_INPUTS = 3


def _gate_lane(group, j):
    return j * N_GROUPS + group


def _route_operands(w_router, b_router, tm):
    idx = np.arange(tm)
    tri = jnp.asarray(idx[:, None] <= idx[None, :], dtype=BF16)
    by_member = lambda a: a.reshape(N_GROUPS, EXPERTS_PER_GROUP, -1).transpose(1, 0, 2).reshape(N_EXPERTS, -1)
    wr = jnp.pad(by_member(w_router.T).T, ((0, 0), (0, LANES - N_EXPERTS))).astype(BF16)
    ins = [wr, by_member(b_router), tri]
    return ins, [_full(a.shape) for a in ins]


def _route_outputs(n, tm):
    nt = n // tm
    specs = [pl.BlockSpec((tm, LANES), lambda i: (i, 0)), pl.BlockSpec((1, 1, tm), lambda i: (i, 0, 0)),
             pl.BlockSpec((1, 1, LANES), lambda i: (i, 0, 0))]
    shapes = [jax.ShapeDtypeStruct((n, LANES), F32), jax.ShapeDtypeStruct((nt, 1, tm), jnp.int32),
              jax.ShapeDtypeStruct((nt, 1, LANES), F32)]
    return specs, shapes


def _router_body(x_ref, *refs):
    _route(x_ref[...], *refs)


def _router(x2d, w_router, b_router, *, tm):
    n = x2d.shape[0]
    ins, in_specs = _route_operands(w_router, b_router, tm)
    out_specs, out_shape = _route_outputs(n, tm)
    return pl.pallas_call(
        _router_body, grid=(n // tm,),
        in_specs=[pl.BlockSpec((tm, D_MODEL), lambda i: (i, 0))] + in_specs,
        out_specs=out_specs, out_shape=out_shape,
        compiler_params=_params(("arbitrary",)), name="router",
    )(x2d, *ins)


def _moe_ln_body(x_ref, gates_ref, wg_ref, wu_ref, wd_ref, g_ref, b_ref, o_ref, acc_ref, xb_ref):
    gi = pl.program_id(1)

    @pl.when(gi == 0)
    def _():
        xb_ref[...] = x_ref[...].astype(BF16)
        acc_ref[...] = jnp.zeros_like(acc_ref)

    xb = xb_ref[...]
    lane = lax.broadcasted_iota(jnp.int32, gates_ref.shape, 1)
    hs = []
    for j in range(EXPERTS_PER_GROUP):
        hg = _dg(xb, wg_ref[0, j])
        hu = _dg(xb, wu_ref[0, j])
        gate = jnp.sum(jnp.where(lane == _gate_lane(gi, j), gates_ref[...], 0.0), axis=1, keepdims=True)
        hs.append((hg * _sigmoid(hg) * hu * gate).astype(BF16))
    acc_ref[...] += _dg(jnp.concatenate(hs, axis=1), wd_ref[0])

    @pl.when(gi == N_GROUPS - 1)
    def _():
        o_ref[...] = _layer_norm(ALPHA * x_ref[...] + acc_ref[...], g_ref[...], b_ref[...])


def _moe_ln(x2d, gates, wg, wu, wd, ln_g, ln_b, *, tm):
    n = x2d.shape[0]
    tok = pl.BlockSpec((tm, D_MODEL), lambda i, e: (i, 0))
    wspec = pl.BlockSpec((1, EXPERTS_PER_GROUP, D_MODEL, D_FF), lambda i, e: (e, 0, 0, 0))
    return pl.pallas_call(
        _moe_ln_body, grid=(n // tm, N_GROUPS),
        in_specs=[tok, pl.BlockSpec((tm, LANES), lambda i, e: (i, 0)), wspec, wspec,
                  pl.BlockSpec((1, EXPERTS_PER_GROUP * D_FF, D_MODEL), lambda i, e: (e, 0, 0)),
                  pl.BlockSpec((1, D_MODEL), lambda i, e: (0, 0)), pl.BlockSpec((1, D_MODEL), lambda i, e: (0, 0))],
        out_specs=tok, out_shape=jax.ShapeDtypeStruct((n, D_MODEL), F32),
        scratch_shapes=[pltpu.VMEM((tm, D_MODEL), F32), pltpu.VMEM((tm, D_MODEL), BF16)],
        compiler_params=_params(("arbitrary", "arbitrary")), name="moe_ln",
    )(x2d, gates, wg, wu, wd, ln_g.reshape(1, -1), ln_b.reshape(1, -1))


def _moe_sorted_body(start_ref, nwin_ref, x_ref, meta_ref, slot_ref, wg_ref, wu_ref, wd_ref, g_ref, b_ref, o_ref,
                     xs_ref, gs_ref, ys_ref, *, tm, wn):
    i = pl.program_id(0)

    @pl.when(i == 0)
    def _():
        xs_ref[tm:, :] = jnp.zeros((wn, D_MODEL), BF16)
        gs_ref[tm:, :] = jnp.zeros((wn, LANES), F32)

    x = x_ref[...]
    meta = meta_ref[...]
    slot_iota = lax.broadcasted_iota(jnp.int32, (tm, tm), 0)
    perm = jnp.where(slot_iota == slot_ref[0], 1.0, 0.0).astype(BF16)
    xs_ref[:tm, :] = _dg(perm, x.astype(BF16)).astype(BF16)
    gs = None
    for part in _split(meta, 2):
        t = _dg(perm, part)
        gs = t if gs is None else gs + t
    gs_ref[:tm, :] = gs
    ys_ref[...] = jnp.zeros_like(ys_ref)

    for gi in range(N_GROUPS):
        first = start_ref[i * N_GROUPS + gi]

        def window(w, carry, gi=gi, first=first):
            st = pl.multiple_of(first + w * wn, PACK)
            xw = xs_ref[pl.ds(st, wn), :]
            gw = gs_ref[pl.ds(st, wn), :]
            hs = []
            for j in range(EXPERTS_PER_GROUP):
                e = _gate_lane(gi, j)
                hg = _dg(xw, wg_ref[gi, j])
                hu = _dg(xw, wu_ref[gi, j])
                hs.append((hg * _sigmoid(hg) * hu * gw[:, e:e + 1]).astype(BF16))
            ys_ref[pl.ds(st, wn), :] += _dg(jnp.concatenate(hs, axis=1), wd_ref[gi])
            return carry

        lax.fori_loop(0, nwin_ref[i * N_GROUPS + gi], window, 0)

    tok_slot = meta[:, META_SLOT:META_SLOT + 1]
    lane_slot = lax.broadcasted_iota(jnp.int32, (tm, tm), 1).astype(F32)
    unperm = jnp.where(tok_slot == lane_slot, 1.0, 0.0).astype(BF16)
    y = _bdot(unperm, ys_ref[:tm, :])
    o_ref[...] = _layer_norm(ALPHA * x + y, g_ref[...], b_ref[...])


def _moe_sorted_ln(x2d, meta, slot_rows, counts, wg, wu, wd, ln_g, ln_b, *, tm, wn):
    n = x2d.shape[0]
    nt = n // tm
    assert wn % PACK == 0 and tm % PACK == 0
    cnt = counts[:, 0, :N_GROUPS].astype(jnp.int32)
    offs = jnp.cumsum(cnt, axis=1) - cnt
    first = (offs // PACK) * PACK
    nwin = jnp.where(cnt > 0, (offs - first + cnt + wn - 1) // wn, 0)
    tok = lambda w: pl.BlockSpec((tm, w), lambda i, *_: (i, 0))
    resident = lambda a: pl.BlockSpec(a.shape, lambda i, *_: (0,) * a.ndim, pipeline_mode=pl.Buffered(1))
    vec = pl.BlockSpec((1, D_MODEL), lambda i, *_: (0, 0))
    grid_spec = pltpu.PrefetchScalarGridSpec(
        num_scalar_prefetch=2, grid=(nt,),
        in_specs=[tok(D_MODEL), tok(LANES), pl.BlockSpec((1, 1, tm), lambda i, *_: (i, 0, 0)),
                  resident(wg), resident(wu), resident(wd), vec, vec],
        out_specs=tok(D_MODEL),
        scratch_shapes=[pltpu.VMEM((tm + wn, D_MODEL), BF16), pltpu.VMEM((tm + wn, LANES), F32),
                        pltpu.VMEM((tm + wn, D_MODEL), F32)])
    return pl.pallas_call(
        functools.partial(_moe_sorted_body, tm=tm, wn=wn), grid_spec=grid_spec,
        out_shape=jax.ShapeDtypeStruct((n, D_MODEL), F32),
        compiler_params=_params(("arbitrary",)), name="moe_sorted_ln",
    )(first.reshape(-1), nwin.reshape(-1), x2d, meta, slot_rows, wg, wu, wd, ln_g.reshape(1, -1), ln_b.reshape(1, -1))


HALO = 16


def _pool_ln_body(x_ref, halo_ref, wp_ref, ps_ref, g_ref, b_ref, *rest, tm, seq_len, start_pos):
    o_ref = rest[0] if len(rest) == 1 else rest[N_ROUTE_INPUTS]
    i = pl.program_id(0)
    x = x_ref[...]
    idx = lax.broadcasted_iota(jnp.int32, (tm + HALO, 1), 0)
    pos_e = (i * tm - HALO + idx) & (seq_len - 1)
    cnt_pos = (pos_e[HALO:] + (start_pos + 1)).astype(F32)
    mixed = []
    for gi, w in enumerate(POOL_WINDOWS):
        cols = slice(gi * POOL_GROUP, (gi + 1) * POOL_GROUP)
        xg = x[:, cols]
        s = jnp.concatenate([halo_ref[:, cols], xg], axis=0)
        step = 1
        while step < w:
            s = s + jnp.where(pos_e >= step, pltpu.roll(s, step, axis=0), 0.0)
            step *= 2
        pooled = s[HALO:] / jnp.minimum(float(w), cnt_pos) - xg
        mixed.append(_bdot(pooled, wp_ref[gi]))
    h = jnp.concatenate(mixed, axis=1) * ps_ref[...]
    out = _layer_norm(ALPHA * x + h, g_ref[...], b_ref[...])
    o_ref[...] = out
    if len(rest) > 1:
        _route(out, *rest[:N_ROUTE_INPUTS], *rest[N_ROUTE_INPUTS + 1:])


def _pool_ln(x2d, w_pool, pool_scale, ln_g, ln_b, *, tm, seq_len, start_pos, router=None):
    n = x2d.shape[0]
    assert seq_len & (seq_len - 1) == 0 and seq_len >= HALO and tm % HALO == 0
    tok = pl.BlockSpec((tm, D_MODEL), lambda i: (i, 0))
    ins = [x2d, x2d, w_pool.astype(BF16), pool_scale.reshape(1, -1), ln_g.reshape(1, -1), ln_b.reshape(1, -1)]
    in_specs = [tok, pl.BlockSpec((HALO, D_MODEL), lambda i: (jnp.maximum(i * (tm // HALO) - 1, 0), 0)),
                _full(w_pool.shape), _full((1, D_MODEL)), _full((1, D_MODEL)), _full((1, D_MODEL))]
    out_specs, out_shape = [tok], [jax.ShapeDtypeStruct((n, D_MODEL), F32)]
    if router is not None:
        r_ins, r_specs = _route_operands(*router, tm)
        ins, in_specs = ins + r_ins, in_specs + r_specs
        r_out_specs, r_out_shape = _route_outputs(n, tm)
        out_specs, out_shape = out_specs + r_out_specs, out_shape + r_out_shape
    res = pl.pallas_call(
        functools.partial(_pool_ln_body, tm=tm, seq_len=seq_len, start_pos=start_pos), grid=(n // tm,),
        in_specs=in_specs, out_specs=out_specs, out_shape=out_shape,
        compiler_params=_params(("arbitrary",)), name="pool_ln",
    )(*ins)
    return res[0] if router is None else res


def _pad_lora_cols(w, width):
    return jnp.pad(w, ((0, 0), (0, LORA_PAD - width)))


def _even_layer_weights(prm, i):
    w_in = prm['w_in'][i]
    o = 3 * D_A
    cols = lambda w: jnp.concatenate(
        [w[:, :o], _pad_lora_cols(w[:, o:o + W_LORA], W_LORA),
         _pad_lora_cols(w[:, o + W_LORA:o + W_LORA + A_LORA], A_LORA),
         _pad_lora_cols(w[:, o + W_LORA + A_LORA:RWKV_COLS], G_LORA)], axis=1)
    pad_rows = lambda w: jnp.pad(w, ((0, LORA_PAD - w.shape[0]), (0, 0))).astype(BF16)
    vec = lambda a: a.reshape(1, -1)
    return dict(
        wr=cols(w_in[:, :RWKV_COLS]).astype(BF16), ws=w_in[:, RWKV_COLS:].astype(BF16),
        mu=cols(prm['tshift_mu'][i].reshape(1, -1)),
        w0=vec(prm['decay_w0'][i]), ww2=pad_rows(prm['decay_w2'][i]),
        a0=vec(prm['iclr_a0'][i]), wa2=pad_rows(prm['iclr_a2'][i]), wg2=pad_rows(prm['gate_w2'][i]),
        k_k=vec(prm['k_k'][i]), k_a=vec(prm['k_a'][i]), r_k=vec(prm['r_k'][i]),
    )


def _group_experts(w):
    return w.astype(BF16).reshape(DEPTH, N_GROUPS, EXPERTS_PER_GROUP, D_MODEL, D_FF)


MOE_SORT_TILE = 512
MOE_WINDOW = 160


def _moe_weights(prm, layer):
    return (prm['wg'][layer], prm['wu'][layer], prm['wd'][layer], prm['ln_ffn_g'][layer], prm['ln_ffn_b'][layer])


def _moe_block(x2d, prm, layer, *, tm):
    meta, _, _ = _router(x2d, prm['w_router'], prm['b_router'], tm=min(tm, 512))
    return _moe_ln(x2d, meta, *_moe_weights(prm, layer), tm=tm)


PROMPT_CHUNKS_PER_STEP = 8
PROMPT_ATTN_BLOCKS_PER_STEP = 8
SAMPLE_CHUNK = 16
SAMPLE_QROWS = SUBLANES
SAMPLE_SEQS_PER_STEP = 8


def _prompt_trunk(x, prm):
    bsz, t, _ = x.shape
    n = bsz * t
    x0 = x.reshape(n, D_MODEL)
    wts = _even_layer_weights(prm, 0)
    rt, kt, bt, at, v, bonus, g, gl, swa = _inproj_prep(x0, None, wts, seq_len=t, chunk=CHUNK, tm=512)
    s0 = jnp.zeros((bsz, N_HEADS, HEAD, HEAD), F32)
    ya, s_new = _rwkv_scan(rt, kt, bt, at, v, bonus, g, gl, prm['lnx_g'][0].reshape(1, -1),
                           prm['lnx_b'][0].reshape(1, -1), s0, n_seq=bsz, seq_len=t, chunk=CHUNK,
                           nc=min(PROMPT_CHUNKS_PER_STEP, t // CHUNK))
    n_blk = t // WINDOW
    qb = min(PROMPT_ATTN_BLOCKS_PER_STEP, n_blk)
    yb = _swa(swa, (swa,), prm['rel_bias'], prm['attn_sinks'][0], n_seq=bsz, n_blk=n_blk, qr=WINDOW, fresh=True,
              per_step=qb)
    tm = MOE_SORT_TILE
    assert n % tm == 0
    router = (prm['w_router'], prm['b_router'])
    x1, *routing = _outproj_ln(ya, yb, x0, prm['w_out'][0], prm['ln_mix_g'][0], prm['ln_mix_b'][0], tm=tm,
                               router=router)
    x2 = _moe_sorted_ln(x1, *routing, *_moe_weights(prm, 0), tm=tm, wn=MOE_WINDOW)
    x3, *routing = _pool_ln(x2, prm['w_pool'][0], prm['pool_scale'][0], prm['ln_mix_g'][1], prm['ln_mix_b'][1],
                            tm=tm, seq_len=t, start_pos=0, router=router)
    x4 = _moe_sorted_ln(x3, *routing, *_moe_weights(prm, 1), tm=tm, wn=MOE_WINDOW)
    swa3 = swa.reshape(bsz, t, SWA_COLS)
    k_new = swa3[:, t - WINDOW:, D_B:D_B + KV_COLS].reshape(bsz, WINDOW, KV_HEADS, HEAD)
    v_new = swa3[:, t - WINDOW:, D_B + KV_COLS:].reshape(bsz, WINDOW, KV_HEADS, HEAD)
    pool_new = x2.reshape(bsz, t, D_MODEL)[:, t - POOL_KEEP:]
    return (x4.reshape(bsz, t, D_MODEL), s_new[None], x[:, -1][None], k_new[None], v_new[None], pool_new[None])


def _sample_trunk(x, st_rwkv, st_shift, c_k, c_v, st_pool, prm):
    bsz = x.shape[0]
    x0 = x.reshape(bsz, D_MODEL)
    wts = _even_layer_weights(prm, 0)
    outs = _inproj_prep(x0, st_shift[0], wts, seq_len=1, chunk=1, tm=bsz)
    gl, swa = outs[7], outs[8]
    ya, s_new = _rwkv_scan(*outs[:7], gl.reshape(bsz, 1, D_A), prm['lnx_g'][0].reshape(1, -1),
                           prm['lnx_b'][0].reshape(1, -1), st_rwkv[0], n_seq=bsz, seq_len=SAMPLE_CHUNK,
                           chunk=SAMPLE_CHUNK, nc=SAMPLE_SEQS_PER_STEP)
    yb = _swa(swa, (c_k[0], c_v[0]), prm['rel_bias'], prm['attn_sinks'][0], n_seq=bsz, n_blk=1,
              qr=SAMPLE_QROWS, fresh=False, per_step=SAMPLE_SEQS_PER_STEP)
    x1 = _outproj_ln(ya, yb, x0, prm['w_out'][0], prm['ln_mix_g'][0], prm['ln_mix_b'][0], tm=bsz)
    x2 = _moe_block(x1, prm, 0, tm=bsz)
    xcat = jnp.concatenate([st_pool[0], x2[:, None]], axis=1)
    x3 = _pool_ln(xcat.reshape(bsz * HALO, D_MODEL), prm['w_pool'][0], prm['pool_scale'][0],
                  prm['ln_mix_g'][1], prm['ln_mix_b'][1], tm=min(512, bsz * HALO), seq_len=HALO,
                  start_pos=PAST_LEN - POOL_KEEP)
    x3 = x3.reshape(bsz, HALO, D_MODEL)[:, -1]
    x4 = _moe_block(x3, prm, 1, tm=bsz)
    k_new = jnp.concatenate([c_k[0][:, 1:], swa[:, D_B:D_B + KV_COLS].reshape(bsz, 1, KV_HEADS, HEAD)], axis=1)
    v_new = jnp.concatenate([c_v[0][:, 1:], swa[:, D_B + KV_COLS:].reshape(bsz, 1, KV_HEADS, HEAD)], axis=1)
    return (x4.reshape(bsz, 1, D_MODEL), s_new[None], x[:, -1][None], k_new[None], v_new[None],
            xcat[:, 1:][None])


def _prepare_params(raw):
    prm = dict(raw)
    prm['r_k'] = raw['r_k'].reshape(raw['r_k'].shape[0], -1)
    prm['wg'] = _group_experts(raw['w_ex_gate'])
    prm['wu'] = _group_experts(raw['w_ex_up'])
    prm['wd'] = raw['w_ex_down'].astype(BF16).reshape(DEPTH, N_GROUPS, EXPERTS_PER_GROUP * D_FF, D_MODEL)
    return prm


def kernel(x_prompt, x_sample, state_rwkv, state_shift, cache_swa_k, cache_swa_v, state_pool, w_in, tshift_mu,
           decay_w0, decay_w2, iclr_a0, iclr_a2, gate_w2, k_k, k_a, r_k, lnx_g, lnx_b, attn_sinks, rel_bias, w_out,
           w_pool, pool_scale, ln_mix_g, ln_mix_b, ln_ffn_g, ln_ffn_b, w_router, b_router, w_ex_gate, w_ex_up,
           w_ex_down):
    prm = _prepare_params(dict(
        w_in=w_in, tshift_mu=tshift_mu, decay_w0=decay_w0, decay_w2=decay_w2, iclr_a0=iclr_a0, iclr_a2=iclr_a2,
        gate_w2=gate_w2, k_k=k_k, k_a=k_a, r_k=r_k, lnx_g=lnx_g, lnx_b=lnx_b, attn_sinks=attn_sinks,
        rel_bias=rel_bias, w_out=w_out, w_pool=w_pool, pool_scale=pool_scale, ln_mix_g=ln_mix_g, ln_mix_b=ln_mix_b,
        ln_ffn_g=ln_ffn_g, ln_ffn_b=ln_ffn_b, w_router=w_router, b_router=b_router, w_ex_gate=w_ex_gate,
        w_ex_up=w_ex_up, w_ex_down=w_ex_down))
    y_p, rwkv_p, shift_p, k_p, v_p, pool_p = _prompt_trunk(x_prompt, prm)
    y_s, rwkv_s, shift_s, k_s, v_s, pool_s = _sample_trunk(x_sample, state_rwkv, state_shift, cache_swa_k,
                                                            cache_swa_v, state_pool, prm)
    return (y_p, y_s, rwkv_p, rwkv_s, shift_p, shift_s, k_p, k_s, v_p, v_s, pool_p, pool_s)
```

```python
import functools
import math

import numpy as np
import jax
import jax.numpy as jnp
from jax import lax
from jax.experimental import pallas as pl
from jax.experimental.pallas import tpu as pltpu

F32 = jnp.float32
BF16 = jnp.bfloat16

D_MODEL = 1024
DEPTH = 2
PAST_LEN = 16384
D_A = 512
HEAD = 64
N_HEADS = 8
W_LORA, A_LORA, G_LORA = 32, 64, 96
RWKV_COLS = 3 * D_A + W_LORA + A_LORA + G_LORA
D_B = 512
KV_HEADS = 2
Q_PER_KV = 4
KV_COLS = KV_HEADS * HEAD
SWA_COLS = D_B + 2 * KV_COLS
WINDOW = 128
N_BUCKETS = 32
MAX_EXACT = 16
REL_MAX_DIST = 128
POOL_WINDOWS = (2, 4, 8, 16)
POOL_GROUP = 256
POOL_KEEP = 15
N_EXPERTS = 16
N_GROUPS = 4
EXPERTS_PER_GROUP = 4
D_FF = 256
ALPHA = (2.0 * DEPTH) ** 0.25
LN_EPS = 1e-5
LNX_EPS = 64e-5
NEG_INF = -1e30

LANES = 128
SUBLANES = 8
PACK = 16
MXU_DEPTH = 256
VMEM_LIMIT_BYTES = 56 * 1024 * 1024

LORA_PAD = LANES
RWKV_PAD_COLS = 3 * D_A + 3 * LORA_PAD
CHUNK = 64

NN = ((1,), (0,))
NT = ((1,), (1,))
TN = ((0,), (0,))


def _dg(a, b, dims=NN):
    return lax.dot_general(a, b, (dims, ((), ())), preferred_element_type=F32)


def _bdot(a, b, dims=NN):
    return _dg(a.astype(BF16), b.astype(BF16), dims)


def _split(x, n):
    parts, rem = [], x
    for i in range(n):
        p = rem.astype(BF16)
        parts.append(p)
        if i + 1 < n:
            rem = rem - p.astype(F32)
    return parts


def _mm(a, b, dims=NN, passes=3):
    if passes == 1:
        return _bdot(a, b, dims)
    ah, al = _split(a, 2)
    bh, bl = _split(b, 2)
    return _dg(ah, bh, dims) + (_dg(ah, bl, dims) + _dg(al, bh, dims))


def _dot_exact_rhs(x, ones_bf16, n=3):
    out = None
    for p in _split(x, n):
        t = _dg(p, ones_bf16)
        out = t if out is None else out + t
    return out


def _dot_exact_lhs(ones_bf16, x, n=3):
    out = None
    for p in _split(x, n):
        t = _dg(ones_bf16, p)
        out = t if out is None else out + t
    return out


def _sigmoid(x):
    return 1.0 / (1.0 + jnp.exp(-x))


def _layer_norm(z, g, b):
    mu = jnp.mean(z, axis=-1, keepdims=True)
    d = z - mu
    var = jnp.mean(d * d, axis=-1, keepdims=True)
    return d * lax.rsqrt(var + LN_EPS) * g + b


def _params(sem):
    return pltpu.CompilerParams(dimension_semantics=sem, vmem_limit_bytes=VMEM_LIMIT_BYTES)


def _full(shape):
    nd = len(shape)
    return pl.BlockSpec(shape, lambda *_: (0,) * nd)


def _inproj_prep_body(*refs, tm, tiles_per_seq, chunk, has_prev):
    if has_prev:
        x_ref, xp_ref = refs[0], refs[1]
        refs = refs[2:]
    else:
        x_ref, xp_ref = refs[0], None
        refs = refs[1:]
    (wr_ref, ws_ref, mu_ref, w0_ref, ww2_ref, a0_ref, wa2_ref, wg2_ref, kk_ref, ka_ref, rk_ref,
     ones_ref, tri_ref,
     rt_ref, kt_ref, bt_ref, at_ref, v_ref, bonus_ref, g_ref, gl_ref, swa_ref, carry_ref) = refs

    i = pl.program_id(0)
    xb = x_ref[...].astype(BF16)
    pr = _dg(xb, wr_ref[...])
    swa_ref[...] = _dg(xb, ws_ref[...])

    if has_prev:
        shifted = _dg(xp_ref[...].astype(BF16), wr_ref[...])
    else:
        @pl.when(i == 0)
        def _():
            carry_ref[...] = jnp.zeros_like(carry_ref)

        first = (i % tiles_per_seq) == 0
        prev_last = jnp.where(first, 0.0, carry_ref[0:1, :])
        rolled = pltpu.roll(pr, 1, axis=0)
        row = lax.broadcasted_iota(jnp.int32, (tm, 1), 0)
        shifted = jnp.where(row == 0, prev_last, rolled)
        carry_ref[0:1, :] = pr[tm - 1:tm, :]
    m = pr + mu_ref[...] * (shifted - pr)

    r = m[:, 0:D_A]
    k = m[:, D_A:2 * D_A]
    v = m[:, 2 * D_A:3 * D_A]
    wd = m[:, 3 * D_A:3 * D_A + LORA_PAD]
    ad = m[:, 3 * D_A + LORA_PAD:3 * D_A + 2 * LORA_PAD]
    gd = m[:, 3 * D_A + 2 * LORA_PAD:3 * D_A + 3 * LORA_PAD]

    nz = -(w0_ref[...] + _bdot(jnp.tanh(wd), ww2_ref[...]))
    softplus = jnp.maximum(nz, 0.0) + jnp.log1p(jnp.exp(-jnp.abs(nz)))
    w_log = -softplus - 0.5
    logw = -jnp.exp(w_log)
    a = _sigmoid(a0_ref[...] + _bdot(ad, wa2_ref[...]))
    g = _bdot(_sigmoid(gd), wg2_ref[...])

    ones_bd = ones_ref[...]
    half = D_A // 2

    def head_sums(z):
        return jnp.concatenate([_dot_exact_rhs(z[:, :half], ones_bd, 2), _dot_exact_rhs(z[:, half:], ones_bd, 2)],
                               axis=1)

    kkr = k * kk_ref[...]
    nrm = jnp.sqrt(head_sums(kkr * kkr))
    kk = kkr / jnp.maximum(nrm, 1e-12)
    k2 = k * (1.0 + (a - 1.0) * ka_ref[...])
    bonus_ref[...] = head_sums(r * k2 * rk_ref[...]) * v

    if chunk > 1:
        tri_rows = tri_ref.shape[0]
        cum = jnp.concatenate([_dot_exact_lhs(tri_ref[...], logw[r0:r0 + tri_rows], 2)
                               for r0 in range(0, tm, tri_rows)], axis=0)
    else:
        cum = logw
    gam = jnp.exp(cum)
    inv = jnp.exp(-cum)
    rt_ref[...] = r * gam
    kt_ref[...] = k2 * inv
    bt_ref[...] = kk * a * inv
    at_ref[...] = -kk * jnp.exp(cum - logw)
    v_ref[...] = v
    g_ref[...] = g
    if chunk > 1:
        for c in range(tm // chunk):
            gl_ref[c] = gam[(c + 1) * chunk - 1:(c + 1) * chunk, :]
    else:
        gl_ref[...] = gam


def _inproj_prep(x2d, x_prev, wts, *, seq_len, chunk, tm):
    n = x2d.shape[0]
    assert n % tm == 0 and (x_prev is not None or seq_len % tm == 0)
    assert tm % chunk == 0
    has_prev = x_prev is not None
    row = lambda w: pl.BlockSpec((tm, w), lambda i: (i, 0))
    ins = [x2d] + ([x_prev] if has_prev else [])
    in_specs = [row(D_MODEL)] + ([row(D_MODEL)] if has_prev else [])
    consts = [wts['wr'], wts['ws'], wts['mu'], wts['w0'], wts['ww2'], wts['a0'], wts['wa2'], wts['wg2'],
              wts['k_k'], wts['k_a'], wts['r_k'], _ones_block_diag(D_A // 2, HEAD), _tri_block_diag(min(tm, MXU_DEPTH), chunk)]
    ins += consts
    in_specs += [_full(c.shape) for c in consts]
    if chunk > 1:
        gl_shape = jax.ShapeDtypeStruct((n // chunk, 1, D_A), F32)
        gl_spec = pl.BlockSpec((tm // chunk, 1, D_A), lambda i: (i, 0, 0))
    else:
        gl_shape = jax.ShapeDtypeStruct((n, D_A), F32)
        gl_spec = row(D_A)
    out_shape = [jax.ShapeDtypeStruct((n, D_A), F32)] * 7 + [gl_shape, jax.ShapeDtypeStruct((n, SWA_COLS), F32)]
    out_specs = [row(D_A)] * 7 + [gl_spec, row(SWA_COLS)]
    body = functools.partial(_inproj_prep_body, tm=tm, tiles_per_seq=max(seq_len // tm, 1), chunk=chunk,
                             has_prev=has_prev)
    return pl.pallas_call(
        body, grid=(n // tm,), in_specs=in_specs, out_specs=out_specs, out_shape=out_shape,
        scratch_shapes=[pltpu.VMEM((SUBLANES, RWKV_PAD_COLS), F32)],
        compiler_params=_params(("arbitrary",)), name="inproj_prep",
    )(*ins)


def _ones_block_diag(n, blk):
    idx = np.arange(n) // blk
    return jnp.asarray(idx[:, None] == idx[None, :], dtype=BF16)


def _tri_block_diag(n, blk):
    idx = np.arange(n)
    same = (idx[:, None] // blk) == (idx[None, :] // blk)
    return jnp.asarray(same & (idx[:, None] >= idx[None, :]), dtype=BF16)


PAIR = 2 * HEAD
N_PAIRS = N_HEADS // 2


def _scan_body(rt_ref, kt_ref, bt_ref, at_ref, v_ref, bonus_ref, g_ref, gl_ref, lng_ref, lnb_ref, s0_ref,
               y_ref, sout_ref, S_ref, *, C, nc, one_chunk_seqs):
    step = pl.program_id(1)

    def pair_state(ref, s, p):
        return jnp.concatenate([ref[s, 2 * p], ref[s, 2 * p + 1]], axis=1)

    if not one_chunk_seqs:
        @pl.when(step == 0)
        def _():
            for p in range(N_PAIRS):
                S_ref[p] = pair_state(s0_ref, 0, p)

    head_a = lax.broadcasted_iota(jnp.int32, (1, PAIR), 1) < HEAD
    rowc = lax.broadcasted_iota(jnp.int32, (C, 2 * C), 0)
    colc = lax.broadcasted_iota(jnp.int32, (C, 2 * C), 1)
    first_c = colc < C
    col_in = jnp.where(first_c, colc, colc - C)
    strict = rowc > col_in
    incl = rowc >= col_in
    eye_c = (rowc == col_in).astype(F32)
    r128 = lax.broadcasted_iota(jnp.int32, (PAIR, PAIR), 0)
    c128 = lax.broadcasted_iota(jnp.int32, (PAIR, PAIR), 1)
    same_head = (r128 < HEAD) == (c128 < HEAD)
    eye128 = (r128 == c128).astype(F32)

    def by_head(x):
        xb = x.astype(BF16)
        z = jnp.zeros_like(xb)
        return jnp.concatenate([jnp.where(head_a, xb, z), jnp.where(head_a, z, xb)], axis=0)

    def head_mean(z):
        za = jnp.sum(jnp.where(head_a, z, 0.0), axis=-1, keepdims=True)
        zb = jnp.sum(jnp.where(head_a, 0.0, z), axis=-1, keepdims=True)
        return jnp.where(head_a, za, zb) * (1.0 / HEAD)

    n_fold = int(math.log2(C)) - 1
    zero_blk = jnp.zeros((2 * C, PAIR), BF16)
    items = [(ci, p) for ci in range(nc) for p in range(N_PAIRS)]
    each = lambda f, *lists: [f(*xs) for xs in zip(*lists)]

    def load(ref):
        if one_chunk_seqs:
            pad = jnp.zeros((C - 1, PAIR), F32)
            return [jnp.concatenate([ref[ci:ci + 1, p * PAIR:(p + 1) * PAIR], pad], axis=0) for ci, p in items]
        return [ref[ci * C:(ci + 1) * C, p * PAIR:(p + 1) * PAIR] for ci, p in items]

    R, K, Bm, A, V = load(rt_ref), load(kt_ref), load(bt_ref), load(at_ref), load(v_ref)
    big = each(lambda a, r, b, k: _dg(jnp.concatenate([a, r], axis=0).astype(BF16),
                                      jnp.concatenate([by_head(b), by_head(k)], axis=0), NT), A, R, Bm, K)
    a_ab = [jnp.where(strict, x[:C, :2 * C], 0.0) for x in big]
    a_ak = [jnp.where(strict, x[:C, 2 * C:], 0.0) for x in big]
    l_rb = [jnp.where(incl, x[C:, :2 * C], 0.0) for x in big]
    l_rk = [jnp.where(incl, x[C:, 2 * C:], 0.0) for x in big]
    def blocks(x):
        xb = x.astype(BF16)
        z = jnp.zeros_like(xb)
        return jnp.concatenate([jnp.where(first_c, xb, z), jnp.where(first_c, z, xb)], axis=0)

    X = [_dg(x.astype(BF16), blocks(x)) for x in a_ab]
    T = [eye_c + x for x in a_ab]
    for k in range(1, n_fold + 1):
        if k < n_fold:
            res = each(lambda x, t: _dg(jnp.concatenate([x, t], axis=0).astype(BF16), blocks(x)), X, T)
            X = [r_[:C] for r_ in res]
            T = each(lambda t, r_: t + r_[C:], T, res)
        else:
            T = each(lambda t, x: t + _dg(t.astype(BF16), blocks(x)), T, X)
    akv = each(lambda m_, v_: _dg(m_.astype(BF16), by_head(v_)), a_ak, V)
    pq = each(lambda t, a, q: _dg(t.astype(BF16), jnp.concatenate([by_head(a), by_head(q)], axis=1)), T, A, akv)
    P = [x[:, :PAIR] for x in pq]
    Q = [x[:, PAIR:] for x in pq]
    wz = each(lambda lb, lk, p_, q_, v_: _dg(
        jnp.concatenate([lb, lk], axis=1).astype(BF16),
        jnp.concatenate([jnp.concatenate([by_head(p_), by_head(q_)], axis=1),
                         jnp.concatenate([zero_blk, by_head(v_)], axis=1)], axis=0)), l_rb, l_rk, P, Q, V)
    W = each(lambda r, x: r + x[:, :PAIR], R, wz)
    Z = [x[:, PAIR:] for x in wz]
    ptb = each(lambda p_, b: _bdot(p_, b, TN), P, Bm)
    gfull = each(lambda q_, v_, b, k: _bdot(jnp.concatenate([q_, v_], axis=0), jnp.concatenate([b, k], axis=0), TN),
                 Q, V, Bm, K)

    def put_state(s, p, val):
        sout_ref[s, 2 * p] = val[:, :HEAD]
        sout_ref[s, 2 * p + 1] = val[:, HEAD:]

    state = [None if one_chunk_seqs else S_ref[p] for p in range(N_PAIRS)]
    for idx, (ci, p) in enumerate(items):
        rows = slice(ci, ci + 1) if one_chunk_seqs else slice(ci * C, (ci + 1) * C)
        ln = slice(p * PAIR, (p + 1) * PAIR)
        g_c = gl_ref[ci, :, ln]
        M = (eye128 + jnp.where(same_head, ptb[idx], 0.0)) * g_c
        G = jnp.where(head_a, gfull[idx][:HEAD], gfull[idx][HEAD:]) * g_c
        S0 = pair_state(s0_ref, ci, p) if one_chunk_seqs else state[p]
        Y = _dg(W[idx].astype(BF16), by_head(S0), NT) + Z[idx]
        S1 = _bdot(S0, M) + G
        if one_chunk_seqs:
            put_state(ci, p, S1)
        else:
            state[p] = S1

        if one_chunk_seqs:
            Y = Y[0:1]
        mu = head_mean(Y)
        d = Y - mu
        var = head_mean(d * d)
        yn = d * lax.rsqrt(var + LNX_EPS) * lng_ref[:, ln] + lnb_ref[:, ln]
        y_ref[rows, ln] = ((yn + bonus_ref[rows, ln]) * g_ref[rows, ln]).astype(y_ref.dtype)

    if not one_chunk_seqs:
        for p in range(N_PAIRS):
            S_ref[p] = state[p]

        @pl.when(step == pl.num_programs(1) - 1)
        def _():
            for p in range(N_PAIRS):
                put_state(0, p, state[p])


def _rwkv_scan(rt, kt, bt, at, v, bonus, g, gl, lnx_g, lnx_b, s0, *, n_seq, seq_len, chunk, nc):
    one_chunk_seqs = seq_len == chunk
    if one_chunk_seqs:
        assert n_seq % nc == 0
        grid = (n_seq // nc, 1)
        blk = lambda b, c: b
        st = pl.BlockSpec((nc, N_HEADS, HEAD, HEAD), lambda b, c: (b, 0, 0, 0))
        rows, n_rows = nc, n_seq
    else:
        rows, n_rows = chunk * nc, n_seq * seq_len
        assert seq_len % rows == 0
        nsteps = seq_len // rows
        grid = (n_seq, nsteps)
        blk = lambda b, c: b * nsteps + c
        st = pl.BlockSpec((1, N_HEADS, HEAD, HEAD), lambda b, c: (b, 0, 0, 0))
    row = pl.BlockSpec((rows, D_A), lambda b, c: (blk(b, c), 0))
    vec = pl.BlockSpec((1, D_A), lambda b, c: (0, 0))
    return pl.pallas_call(
        functools.partial(_scan_body, C=chunk, nc=nc, one_chunk_seqs=one_chunk_seqs),
        grid=grid,
        in_specs=[row] * 7 + [pl.BlockSpec((nc, 1, D_A), lambda b, c: (blk(b, c), 0, 0)), vec, vec, st],
        out_specs=[row, st],
        out_shape=[jax.ShapeDtypeStruct((n_rows, D_A), BF16 if n_rows % PACK == 0 and rows % PACK == 0 else F32),
                   jax.ShapeDtypeStruct((n_seq, N_HEADS, HEAD, HEAD), F32)],
        scratch_shapes=[pltpu.VMEM((N_PAIRS, HEAD, PAIR), F32)],
        compiler_params=_params(("arbitrary", "arbitrary")), name="rwkv_scan",
    )(rt, kt, bt, at, v, bonus, g, gl, lnx_g, lnx_b, s0)


def _bucket_ranges():
    d = np.arange(WINDOW + 1)
    scaled = np.log(np.maximum(d, MAX_EXACT).astype(np.float32) / MAX_EXACT) / math.log(REL_MAX_DIST / MAX_EXACT)
    large = np.minimum(MAX_EXACT + (scaled * (N_BUCKETS - MAX_EXACT)).astype(np.int32), N_BUCKETS - 1)
    bucket = np.where(d < MAX_EXACT, d, large)
    frac = scaled.astype(np.float64) * (N_BUCKETS - MAX_EXACT)
    near = np.abs(frac - np.round(frac)) < 1e-3
    assert all(int(x) in (MAX_EXACT, REL_MAX_DIST) for x in d[(d >= MAX_EXACT) & near])
    ranges = []
    for b in range(N_BUCKETS):
        hit = d[bucket == b]
        ranges.append((int(hit.min()), int(hit.max())) if hit.size else None)
    return tuple(ranges)


def _swa_body(rb_ref, sink_ref, cur_ref, *rest, qr, fresh, per_step, ranges):
    prevs, (y_ref, bias_ref, sinkcol_ref) = rest[:-3], rest[-3:]
    W = WINDOW
    nk = W + qr

    @pl.when((pl.program_id(0) == 0) & (pl.program_id(1) == 0))
    def _():
        qi = lax.broadcasted_iota(jnp.int32, (qr, nk), 0)
        kj = lax.broadcasted_iota(jnp.int32, (qr, nk), 1)
        d = qi + W - kj
        valid = (d >= 0) & (d <= W)
        for hd in range(N_HEADS):
            t = jnp.zeros((qr, nk), F32)
            for bkt, rng in enumerate(ranges):
                if rng is not None:
                    t = jnp.where((d >= rng[0]) & (d <= rng[1]), rb_ref[bkt * N_HEADS + hd], t)
            kvh, grp = hd // Q_PER_KV, hd % Q_PER_KV
            rows = slice(grp * qr, (grp + 1) * qr)
            bias_ref[kvh, rows, :] = jnp.where(valid, t, NEG_INF)
            bias_ref[KV_HEADS + kvh, rows, :] = jnp.where(valid & (kj >= W), t, NEG_INF)
            sinkcol_ref[kvh, rows, :] = jnp.full((qr, 1), sink_ref[hd], F32)

    first_table = jnp.where(pl.program_id(1) == 0, KV_HEADS, 0) if fresh else 0
    items = [(sub, kvh) for sub in range(per_step) for kvh in range(KV_HEADS)]
    each = lambda f, *lists: [f(*xs) for xs in zip(*lists)]

    def own_rows(sub, col):
        if fresh:
            return cur_ref[sub * qr:(sub + 1) * qr, col:col + HEAD]
        return jnp.concatenate([cur_ref[sub:sub + 1, col:col + HEAD], jnp.zeros((qr - 1, HEAD), F32)], axis=0)

    def kv_rows(sub, kvh, is_v):
        col = D_B + is_v * KV_COLS + kvh * HEAD
        if not fresh:
            before = prevs[is_v][sub, :, kvh, :]
        elif sub > 0:
            before = cur_ref[(sub - 1) * qr:sub * qr, col:col + HEAD]
        else:
            before = prevs[0][:, col - D_B:col - D_B + HEAD]
        return jnp.concatenate([before, own_rows(sub, col)], axis=0).astype(BF16)

    kb = [kv_rows(sub, kvh, 0) for sub, kvh in items]
    vb = [kv_rows(sub, kvh, 1) for sub, kvh in items]
    q4 = [(jnp.concatenate([own_rows(sub, (kvh * Q_PER_KV + gq) * HEAD) for gq in range(Q_PER_KV)], axis=0)
           * (HEAD ** -0.5)).astype(BF16) for sub, kvh in items]
    s = [_dg(q, k, NT) + bias_ref[(first_table if sub == 0 else 0) + kvh] for q, k, (sub, kvh) in zip(q4, kb, items)]
    sink = [sinkcol_ref[kvh] for _, kvh in items]
    m = each(lambda s_, sk: jnp.maximum(jnp.max(s_, axis=-1, keepdims=True), sk), s, sink)
    p = each(lambda s_, m_: jnp.exp(s_ - m_).astype(BF16), s, m)
    pad = jnp.concatenate([jnp.zeros((nk, LANES - HEAD), BF16), jnp.ones((nk, LANES), BF16)], axis=1)
    ov = each(lambda p_, v_: _dg(p_, jnp.concatenate([v_, pad], axis=1)), p, vb)
    o = each(lambda ov_, sk, m_: ov_[:, :LANES] / (ov_[:, LANES:] + jnp.exp(sk - m_)), ov, sink, m)
    for sub in range(per_step):
        outs = [o[sub * KV_HEADS + kvh][gq * qr:(gq + 1) * qr, :HEAD]
                for kvh in range(KV_HEADS) for gq in range(Q_PER_KV)]
        y = jnp.concatenate(outs, axis=1).astype(y_ref.dtype)
        if fresh:
            y_ref[sub * qr:(sub + 1) * qr, :] = y
        else:
            y_ref[sub:sub + 1, :] = y[0:1]


def _swa(cur2d, prevs, rel_bias, sinks, *, n_seq, n_blk, qr, fresh, per_step=1):
    smem = pl.BlockSpec(memory_space=pltpu.SMEM)
    if fresh:
        assert qr == WINDOW and n_blk % per_step == 0
        blocks = n_blk
        n_blk //= per_step
        prev_specs = [pl.BlockSpec((WINDOW, 2 * KV_COLS),
                                   lambda b, j: (b * blocks + jnp.maximum(j * per_step - 1, 0), 2))]
    else:
        assert n_blk == 1 and n_seq % per_step == 0
        n_seq //= per_step
        prev_specs = [pl.BlockSpec((per_step, WINDOW, KV_HEADS, HEAD), lambda b, j: (b, 0, 0, 0))] * 2
    qrows = per_step * qr if fresh else per_step
    out_dtype = BF16 if fresh else F32
    return pl.pallas_call(
        functools.partial(_swa_body, qr=qr, fresh=fresh, per_step=per_step, ranges=_bucket_ranges()),
        grid=(n_seq, n_blk),
        in_specs=[smem, smem, pl.BlockSpec((qrows, SWA_COLS), lambda b, j: (b * n_blk + j, 0))] + prev_specs,
        out_specs=pl.BlockSpec((qrows, D_B), lambda b, j: (b * n_blk + j, 0)),
        out_shape=jax.ShapeDtypeStruct((n_seq * n_blk * qrows, D_B), out_dtype),
        scratch_shapes=[pltpu.VMEM((2 * KV_HEADS, Q_PER_KV * qr, WINDOW + qr), F32),
                        pltpu.VMEM((KV_HEADS, Q_PER_KV * qr, 1), F32)],
        compiler_params=_params(("arbitrary", "arbitrary")), name="swa",
    )(rel_bias.reshape(-1), sinks, cur2d, *prevs)


def _outproj_ln_body(ya_ref, yb_ref, x_ref, wa_ref, wb_ref, g_ref, b_ref, *rest):
    o_ref = rest[0] if len(rest) == 1 else rest[N_ROUTE_INPUTS]
    h = _dg(ya_ref[...].astype(BF16), wa_ref[...]) + _dg(yb_ref[...].astype(BF16), wb_ref[...])
    out = _layer_norm(ALPHA * x_ref[...] + h, g_ref[...], b_ref[...])
    o_ref[...] = out
    if len(rest) > 1:
        _route(out, *rest[:N_ROUTE_INPUTS], *rest[N_ROUTE_INPUTS + 1:])


def _outproj_ln(ya, yb, x2d, w_out, ln_g, ln_b, *, tm, router=None):
    n = x2d.shape[0]
    half = pl.BlockSpec((tm, D_A), lambda i: (i, 0))
    full = pl.BlockSpec((tm, D_MODEL), lambda i: (i, 0))
    wa, wb = w_out[:D_A].astype(BF16), w_out[D_A:].astype(BF16)
    ins = [ya, yb, x2d, wa, wb, ln_g.reshape(1, -1), ln_b.reshape(1, -1)]
    in_specs = [half, half, full, _full(wa.shape), _full(wb.shape), _full((1, D_MODEL)), _full((1, D_MODEL))]
    out_specs, out_shape = [full], [jax.ShapeDtypeStruct((n, D_MODEL), F32)]
    if router is not None:
        r_ins, r_specs = _route_operands(*router, tm)
        ins, in_specs = ins + r_ins, in_specs + r_specs
        r_out_specs, r_out_shape = _route_outputs(n, tm)
        out_specs, out_shape = out_specs + r_out_specs, out_shape + r_out_shape
    res = pl.pallas_call(
        _outproj_ln_body, grid=(n // tm,), in_specs=in_specs, out_specs=out_specs, out_shape=out_shape,
        compiler_params=_params(("arbitrary",)), name="outproj_ln",
    )(*ins)
    return res[0] if router is None else res


def _route(x, wr_ref, br_ref, tri_ref, meta_ref, slot_ref, cnt_ref):
    logits = jnp.transpose(_dg(x.astype(BF16), wr_ref[...]))[:N_EXPERTS] + br_ref[...]
    z = jnp.exp(logits - jnp.max(logits, axis=0, keepdims=True))
    probs = z / jnp.sum(z, axis=0, keepdims=True)
    pa = [probs[a_ * N_GROUPS:(a_ + 1) * N_GROUPS] for a_ in range(EXPERTS_PER_GROUP)]

    sel = []
    for a_ in range(EXPERTS_PER_GROUP):
        rank = None
        for j in range(EXPERTS_PER_GROUP):
            if j == a_:
                continue
            ahead = ((pa[j] >= pa[a_]) if j < a_ else (pa[j] > pa[a_])).astype(jnp.int32)
            rank = ahead if rank is None else rank + ahead
        sel.append(rank < 2)
    score = None
    for a_ in range(EXPERTS_PER_GROUP):
        t = jnp.where(sel[a_], pa[a_], 0.0)
        score = t if score is None else score + t
    srow = [score[gi:gi + 1] for gi in range(N_GROUPS)]
    best = []
    for gi in range(N_GROUPS):
        ok = None
        for j in range(N_GROUPS):
            if j == gi:
                continue
            c_ = (srow[gi] > srow[j]) if j < gi else (srow[gi] >= srow[j])
            ok = c_ if ok is None else (ok & c_)
        best.append(ok.astype(F32))
    tm = x.shape[0]
    best4 = jnp.concatenate(best, axis=0)
    chosen = best4 > 0.5
    kept = [jnp.where(sel[a_] & chosen, pa[a_], 0.0) for a_ in range(EXPERTS_PER_GROUP)]
    tot = None
    for a_ in range(EXPERTS_PER_GROUP):
        tot = kept[a_] if tot is None else tot + kept[a_]
    tot = jnp.sum(tot, axis=0, keepdims=True)
    gates = [jnp.where(sel[a_] & chosen, pa[a_] / tot, 0.0) for a_ in range(EXPERTS_PER_GROUP)]

    onehot = jnp.concatenate([best4, jnp.zeros((SUBLANES - N_GROUPS, tm), F32)], axis=0)
    oh_b = jnp.concatenate([onehot, jnp.zeros((LANES - SUBLANES, tm), F32)], axis=0).astype(BF16)
    incl = _dg(onehot.astype(BF16), tri_ref[...])
    slot = None
    offset = jnp.zeros((1, 1), F32)
    for gi in range(N_GROUPS):
        t = onehot[gi:gi + 1] * (offset + incl[gi:gi + 1] - 1.0)
        slot = t if slot is None else slot + t
        offset = offset + incl[gi:gi + 1, tm - 1:tm]
    slot_ref[0] = slot.astype(jnp.int32)
    cnt_ref[0] = _dg(jnp.ones((SUBLANES, tm), BF16), oh_b, NT)[0:1]
    rows = jnp.concatenate(gates + [slot, jnp.zeros((LANES - N_EXPERTS - 1, tm), F32)], axis=0)
    meta_ref[...] = jnp.transpose(rows)


META_SLOT = N_EXPERTS
N_ROUTE_INPUTS = 3


def _gate_lane(group, j):
    return j * N_GROUPS + group


def _route_operands(w_router, b_router, tm):
    idx = np.arange(tm)
    tri = jnp.asarray(idx[:, None] <= idx[None, :], dtype=BF16)
    by_member = lambda a: a.reshape(N_GROUPS, EXPERTS_PER_GROUP, -1).transpose(1, 0, 2).reshape(N_EXPERTS, -1)
    wr = jnp.pad(by_member(w_router.T).T, ((0, 0), (0, LANES - N_EXPERTS))).astype(BF16)
    ins = [wr, by_member(b_router), tri]
    return ins, [_full(a.shape) for a in ins]


def _route_outputs(n, tm):
    nt = n // tm
    specs = [pl.BlockSpec((tm, LANES), lambda i: (i, 0)), pl.BlockSpec((1, 1, tm), lambda i: (i, 0, 0)),
             pl.BlockSpec((1, 1, LANES), lambda i: (i, 0, 0))]
    shapes = [jax.ShapeDtypeStruct((n, LANES), F32), jax.ShapeDtypeStruct((nt, 1, tm), jnp.int32),
              jax.ShapeDtypeStruct((nt, 1, LANES), F32)]
    return specs, shapes


def _router_body(x_ref, *refs):
    _route(x_ref[...], *refs)


def _router(x2d, w_router, b_router, *, tm):
    n = x2d.shape[0]
    ins, in_specs = _route_operands(w_router, b_router, tm)
    out_specs, out_shape = _route_outputs(n, tm)
    return pl.pallas_call(
        _router_body, grid=(n // tm,),
        in_specs=[pl.BlockSpec((tm, D_MODEL), lambda i: (i, 0))] + in_specs,
        out_specs=out_specs, out_shape=out_shape,
        compiler_params=_params(("arbitrary",)), name="router",
    )(x2d, *ins)


def _moe_ln_body(x_ref, gates_ref, wg_ref, wu_ref, wd_ref, g_ref, b_ref, o_ref, acc_ref, xb_ref):
    gi = pl.program_id(1)

    @pl.when(gi == 0)
    def _():
        xb_ref[...] = x_ref[...].astype(BF16)
        acc_ref[...] = jnp.zeros_like(acc_ref)

    xb = xb_ref[...]
    lane = lax.broadcasted_iota(jnp.int32, gates_ref.shape, 1)
    hs = []
    for j in range(EXPERTS_PER_GROUP):
        hg = _dg(xb, wg_ref[0, j])
        hu = _dg(xb, wu_ref[0, j])
        gate = jnp.sum(jnp.where(lane == _gate_lane(gi, j), gates_ref[...], 0.0), axis=1, keepdims=True)
        hs.append((hg * _sigmoid(hg) * hu * gate).astype(BF16))
    acc_ref[...] += _dg(jnp.concatenate(hs, axis=1), wd_ref[0])

    @pl.when(gi == N_GROUPS - 1)
    def _():
        o_ref[...] = _layer_norm(ALPHA * x_ref[...] + acc_ref[...], g_ref[...], b_ref[...])


def _moe_ln(x2d, gates, wg, wu, wd, ln_g, ln_b, *, layer, tm):
    n = x2d.shape[0]
    tok = pl.BlockSpec((tm, D_MODEL), lambda i, e: (i, 0))
    wspec = pl.BlockSpec((None, 1, EXPERTS_PER_GROUP, D_MODEL, D_FF), lambda i, e: (layer, e, 0, 0, 0))
    return pl.pallas_call(
        _moe_ln_body, grid=(n // tm, N_GROUPS),
        in_specs=[tok, pl.BlockSpec((tm, LANES), lambda i, e: (i, 0)), wspec, wspec,
                  pl.BlockSpec((None, 1, EXPERTS_PER_GROUP * D_FF, D_MODEL), lambda i, e: (layer, e, 0, 0)),
                  pl.BlockSpec((1, D_MODEL), lambda i, e: (0, 0)), pl.BlockSpec((1, D_MODEL), lambda i, e: (0, 0))],
        out_specs=tok, out_shape=jax.ShapeDtypeStruct((n, D_MODEL), F32),
        scratch_shapes=[pltpu.VMEM((tm, D_MODEL), F32), pltpu.VMEM((tm, D_MODEL), BF16)],
        compiler_params=_params(("arbitrary", "arbitrary")), name="moe_ln",
    )(x2d, gates, wg, wu, wd, ln_g.reshape(1, -1), ln_b.reshape(1, -1))


def _moe_sorted_body(start_ref, nwin_ref, x_ref, meta_ref, slot_ref, wg_ref, wu_ref, wd_ref, g_ref, b_ref, o_ref,
                     xs_ref, gs_ref, ys_ref, *, tm, wn):
    i = pl.program_id(0)

    @pl.when(i == 0)
    def _():
        xs_ref[tm:, :] = jnp.zeros((wn, D_MODEL), BF16)
        gs_ref[tm:, :] = jnp.zeros((wn, LANES), F32)

    x = x_ref[...]
    meta = meta_ref[...]
    slot_iota = lax.broadcasted_iota(jnp.int32, (tm, tm), 0)
    perm = jnp.where(slot_iota == slot_ref[0], 1.0, 0.0).astype(BF16)
    m_hi, m_lo = _split(meta, 2)
    moved = _dg(perm, jnp.concatenate([x.astype(BF16), m_hi, m_lo], axis=1))
    xs_ref[:tm, :] = moved[:, :D_MODEL].astype(BF16)
    gs_ref[:tm, :] = moved[:, D_MODEL:D_MODEL + LANES] + moved[:, D_MODEL + LANES:]
    ys_ref[...] = jnp.zeros_like(ys_ref)

    for gi in range(N_GROUPS):
        first = start_ref[i * N_GROUPS + gi]

        def window(w, carry, gi=gi, first=first):
            st = pl.multiple_of(first + w * wn, PACK)
            xw = xs_ref[pl.ds(st, wn), :]
            gw = gs_ref[pl.ds(st, wn), :]
            hs = []
            for j in range(EXPERTS_PER_GROUP):
                e = _gate_lane(gi, j)
                hg = _dg(xw, wg_ref[gi, j])
                hu = _dg(xw, wu_ref[gi, j])
                hs.append((hg * _sigmoid(hg) * hu * gw[:, e:e + 1]).astype(BF16))
            ys_ref[pl.ds(st, wn), :] += _dg(jnp.concatenate(hs, axis=1), wd_ref[gi])
            return carry

        lax.fori_loop(0, nwin_ref[i * N_GROUPS + gi], window, 0)

    tok_slot = meta[:, META_SLOT:META_SLOT + 1]
    lane_slot = lax.broadcasted_iota(jnp.int32, (tm, tm), 1).astype(F32)
    unperm = jnp.where(tok_slot == lane_slot, 1.0, 0.0).astype(BF16)
    y = _bdot(unperm, ys_ref[:tm, :])
    o_ref[...] = _layer_norm(ALPHA * x + y, g_ref[...], b_ref[...])


def _moe_sorted_ln(x2d, meta, slot_rows, counts, wg, wu, wd, ln_g, ln_b, *, layer, tm, wn):
    n = x2d.shape[0]
    nt = n // tm
    assert wn % PACK == 0 and tm % PACK == 0
    cnt = counts[:, 0, :N_GROUPS].astype(jnp.int32)
    offs = jnp.cumsum(cnt, axis=1) - cnt
    first = (offs // PACK) * PACK
    nwin = jnp.where(cnt > 0, (offs - first + cnt + wn - 1) // wn, 0)
    tok = lambda w: pl.BlockSpec((tm, w), lambda i, *_: (i, 0))
    resident = lambda a: pl.BlockSpec((None,) + a.shape[1:], lambda i, *_: (layer,) + (0,) * (a.ndim - 1),
                                      pipeline_mode=pl.Buffered(1))
    vec = pl.BlockSpec((1, D_MODEL), lambda i, *_: (0, 0))
    grid_spec = pltpu.PrefetchScalarGridSpec(
        num_scalar_prefetch=2, grid=(nt,),
        in_specs=[tok(D_MODEL), tok(LANES), pl.BlockSpec((1, 1, tm), lambda i, *_: (i, 0, 0)),
                  resident(wg), resident(wu), resident(wd), vec, vec],
        out_specs=tok(D_MODEL),
        scratch_shapes=[pltpu.VMEM((tm + wn, D_MODEL), BF16), pltpu.VMEM((tm + wn, LANES), F32),
                        pltpu.VMEM((tm + wn, D_MODEL), F32)])
    return pl.pallas_call(
        functools.partial(_moe_sorted_body, tm=tm, wn=wn), grid_spec=grid_spec,
        out_shape=jax.ShapeDtypeStruct((n, D_MODEL), F32),
        compiler_params=_params(("arbitrary",)), name="moe_sorted_ln",
    )(first.reshape(-1), nwin.reshape(-1), x2d, meta, slot_rows, wg, wu, wd, ln_g.reshape(1, -1), ln_b.reshape(1, -1))


HALO = 16


def _pool_ln_body(x_ref, halo_ref, wp_ref, ps_ref, g_ref, b_ref, *rest, tm, seq_len, start_pos):
    o_ref = rest[0] if len(rest) == 1 else rest[N_ROUTE_INPUTS]
    i = pl.program_id(0)
    x = x_ref[...]
    idx = lax.broadcasted_iota(jnp.int32, (tm + HALO, 1), 0)
    pos_e = (i * tm - HALO + idx) & (seq_len - 1)
    cnt_pos = (pos_e[HALO:] + (start_pos + 1)).astype(F32)
    mixed = []
    for gi, w in enumerate(POOL_WINDOWS):
        cols = slice(gi * POOL_GROUP, (gi + 1) * POOL_GROUP)
        xg = x[:, cols]
        s = jnp.concatenate([halo_ref[:, cols], xg], axis=0)
        step = 1
        while step < w:
            s = s + jnp.where(pos_e >= step, pltpu.roll(s, step, axis=0), 0.0)
            step *= 2
        pooled = s[HALO:] / jnp.minimum(float(w), cnt_pos) - xg
        mixed.append(_bdot(pooled, wp_ref[gi]))
    h = jnp.concatenate(mixed, axis=1) * ps_ref[...]
    out = _layer_norm(ALPHA * x + h, g_ref[...], b_ref[...])
    o_ref[...] = out
    if len(rest) > 1:
        _route(out, *rest[:N_ROUTE_INPUTS], *rest[N_ROUTE_INPUTS + 1:])


def _pool_ln(x2d, w_pool, pool_scale, ln_g, ln_b, *, tm, seq_len, start_pos, router=None):
    n = x2d.shape[0]
    assert seq_len & (seq_len - 1) == 0 and seq_len >= HALO and tm % HALO == 0
    tok = pl.BlockSpec((tm, D_MODEL), lambda i: (i, 0))
    ins = [x2d, x2d, w_pool.astype(BF16), pool_scale.reshape(1, -1), ln_g.reshape(1, -1), ln_b.reshape(1, -1)]
    in_specs = [tok, pl.BlockSpec((HALO, D_MODEL), lambda i: (jnp.maximum(i * (tm // HALO) - 1, 0), 0)),
                _full(w_pool.shape), _full((1, D_MODEL)), _full((1, D_MODEL)), _full((1, D_MODEL))]
    out_specs, out_shape = [tok], [jax.ShapeDtypeStruct((n, D_MODEL), F32)]
    if router is not None:
        r_ins, r_specs = _route_operands(*router, tm)
        ins, in_specs = ins + r_ins, in_specs + r_specs
        r_out_specs, r_out_shape = _route_outputs(n, tm)
        out_specs, out_shape = out_specs + r_out_specs, out_shape + r_out_shape
    res = pl.pallas_call(
        functools.partial(_pool_ln_body, tm=tm, seq_len=seq_len, start_pos=start_pos), grid=(n // tm,),
        in_specs=in_specs, out_specs=out_specs, out_shape=out_shape,
        compiler_params=_params(("arbitrary",)), name="pool_ln",
    )(*ins)
    return res[0] if router is None else res


def _pad_lora_cols(w, width):
    return jnp.pad(w, ((0, 0), (0, LORA_PAD - width)))


def _even_layer_weights(prm, i):
    w_in = prm['w_in'][i]
    o = 3 * D_A
    cols = lambda w: jnp.concatenate(
        [w[:, :o], _pad_lora_cols(w[:, o:o + W_LORA], W_LORA),
         _pad_lora_cols(w[:, o + W_LORA:o + W_LORA + A_LORA], A_LORA),
         _pad_lora_cols(w[:, o + W_LORA + A_LORA:RWKV_COLS], G_LORA)], axis=1)
    pad_rows = lambda w: jnp.pad(w, ((0, LORA_PAD - w.shape[0]), (0, 0))).astype(BF16)
    vec = lambda a: a.reshape(1, -1)
    return dict(
        wr=cols(w_in[:, :RWKV_COLS]).astype(BF16), ws=w_in[:, RWKV_COLS:].astype(BF16),
        mu=cols(prm['tshift_mu'][i].reshape(1, -1)),
        w0=vec(prm['decay_w0'][i]), ww2=pad_rows(prm['decay_w2'][i]),
        a0=vec(prm['iclr_a0'][i]), wa2=pad_rows(prm['iclr_a2'][i]), wg2=pad_rows(prm['gate_w2'][i]),
        k_k=vec(prm['k_k'][i]), k_a=vec(prm['k_a'][i]), r_k=vec(prm['r_k'][i]),
    )


def _group_experts(w):
    return w.astype(BF16).reshape(DEPTH, N_GROUPS, EXPERTS_PER_GROUP, D_MODEL, D_FF)


MOE_SORT_TILE = 512
MOE_WINDOW = 160


def _moe_weights(prm, layer):
    return (prm['wg'], prm['wu'], prm['wd'], prm['ln_ffn_g'][layer], prm['ln_ffn_b'][layer])


def _moe_block(x2d, prm, layer, *, tm):
    meta, _, _ = _router(x2d, prm['w_router'], prm['b_router'], tm=min(tm, 512))
    return _moe_ln(x2d, meta, *_moe_weights(prm, layer), layer=layer, tm=tm)


PROMPT_CHUNKS_PER_STEP = 8
PROMPT_ATTN_BLOCKS_PER_STEP = 8
SAMPLE_CHUNK = 16
SAMPLE_QROWS = SUBLANES
SAMPLE_SEQS_PER_STEP = 8


def _prompt_trunk(x, prm):
    bsz, t, _ = x.shape
    n = bsz * t
    x0 = x.reshape(n, D_MODEL)
    wts = _even_layer_weights(prm, 0)
    rt, kt, bt, at, v, bonus, g, gl, swa = _inproj_prep(x0, None, wts, seq_len=t, chunk=CHUNK, tm=512)
    s0 = jnp.zeros((bsz, N_HEADS, HEAD, HEAD), F32)
    ya, s_new = _rwkv_scan(rt, kt, bt, at, v, bonus, g, gl, prm['lnx_g'][0].reshape(1, -1),
                           prm['lnx_b'][0].reshape(1, -1), s0, n_seq=bsz, seq_len=t, chunk=CHUNK,
                           nc=min(PROMPT_CHUNKS_PER_STEP, t // CHUNK))
    n_blk = t // WINDOW
    qb = min(PROMPT_ATTN_BLOCKS_PER_STEP, n_blk)
    yb = _swa(swa, (swa,), prm['rel_bias'], prm['attn_sinks'][0], n_seq=bsz, n_blk=n_blk, qr=WINDOW, fresh=True,
              per_step=qb)
    tm = MOE_SORT_TILE
    assert n % tm == 0
    router = (prm['w_router'], prm['b_router'])
    x1, *routing = _outproj_ln(ya, yb, x0, prm['w_out'][0], prm['ln_mix_g'][0], prm['ln_mix_b'][0], tm=tm,
                               router=router)
    x2 = _moe_sorted_ln(x1, *routing, *_moe_weights(prm, 0), layer=0, tm=tm, wn=MOE_WINDOW)
    x3, *routing = _pool_ln(x2, prm['w_pool'][0], prm['pool_scale'][0], prm['ln_mix_g'][1], prm['ln_mix_b'][1],
                            tm=tm, seq_len=t, start_pos=0, router=router)
    x4 = _moe_sorted_ln(x3, *routing, *_moe_weights(prm, 1), layer=1, tm=tm, wn=MOE_WINDOW)
    swa3 = swa.reshape(bsz, t, SWA_COLS)
    k_new = swa3[:, t - WINDOW:, D_B:D_B + KV_COLS].reshape(bsz, WINDOW, KV_HEADS, HEAD)
    v_new = swa3[:, t - WINDOW:, D_B + KV_COLS:].reshape(bsz, WINDOW, KV_HEADS, HEAD)
    pool_new = x2.reshape(bsz, t, D_MODEL)[:, t - POOL_KEEP:]
    return (x4.reshape(bsz, t, D_MODEL), s_new[None], x[:, -1][None], k_new[None], v_new[None], pool_new[None])


def _sample_trunk(x, st_rwkv, st_shift, c_k, c_v, st_pool, prm):
    bsz = x.shape[0]
    x0 = x.reshape(bsz, D_MODEL)
    wts = _even_layer_weights(prm, 0)
    outs = _inproj_prep(x0, st_shift[0], wts, seq_len=1, chunk=1, tm=bsz)
    gl, swa = outs[7], outs[8]
    ya, s_new = _rwkv_scan(*outs[:7], gl.reshape(bsz, 1, D_A), prm['lnx_g'][0].reshape(1, -1),
                           prm['lnx_b'][0].reshape(1, -1), st_rwkv[0], n_seq=bsz, seq_len=SAMPLE_CHUNK,
                           chunk=SAMPLE_CHUNK, nc=SAMPLE_SEQS_PER_STEP)
    yb = _swa(swa, (c_k[0], c_v[0]), prm['rel_bias'], prm['attn_sinks'][0], n_seq=bsz, n_blk=1,
              qr=SAMPLE_QROWS, fresh=False, per_step=SAMPLE_SEQS_PER_STEP)
    x1 = _outproj_ln(ya, yb, x0, prm['w_out'][0], prm['ln_mix_g'][0], prm['ln_mix_b'][0], tm=bsz)
    x2 = _moe_block(x1, prm, 0, tm=bsz)
    xcat = jnp.concatenate([st_pool[0], x2[:, None]], axis=1)
    x3 = _pool_ln(xcat.reshape(bsz * HALO, D_MODEL), prm['w_pool'][0], prm['pool_scale'][0],
                  prm['ln_mix_g'][1], prm['ln_mix_b'][1], tm=min(512, bsz * HALO), seq_len=HALO,
                  start_pos=PAST_LEN - POOL_KEEP)
    x3 = x3.reshape(bsz, HALO, D_MODEL)[:, -1]
    x4 = _moe_block(x3, prm, 1, tm=bsz)
    k_new = jnp.concatenate([c_k[0][:, 1:], swa[:, D_B:D_B + KV_COLS].reshape(bsz, 1, KV_HEADS, HEAD)], axis=1)
    v_new = jnp.concatenate([c_v[0][:, 1:], swa[:, D_B + KV_COLS:].reshape(bsz, 1, KV_HEADS, HEAD)], axis=1)
    return (x4.reshape(bsz, 1, D_MODEL), s_new[None], x[:, -1][None], k_new[None], v_new[None],
            xcat[:, 1:][None])


def _prepare_params(raw):
    prm = dict(raw)
    prm['r_k'] = raw['r_k'].reshape(raw['r_k'].shape[0], -1)
    prm['wg'] = _group_experts(raw['w_ex_gate'])
    prm['wu'] = _group_experts(raw['w_ex_up'])
    prm['wd'] = raw['w_ex_down'].astype(BF16).reshape(DEPTH, N_GROUPS, EXPERTS_PER_GROUP * D_FF, D_MODEL)
    return prm


def kernel(x_prompt, x_sample, state_rwkv, state_shift, cache_swa_k, cache_swa_v, state_pool, w_in, tshift_mu,
           decay_w0, decay_w2, iclr_a0, iclr_a2, gate_w2, k_k, k_a, r_k, lnx_g, lnx_b, attn_sinks, rel_bias, w_out,
           w_pool, pool_scale, ln_mix_g, ln_mix_b, ln_ffn_g, ln_ffn_b, w_router, b_router, w_ex_gate, w_ex_up,
           w_ex_down):
    prm = _prepare_params(dict(
        w_in=w_in, tshift_mu=tshift_mu, decay_w0=decay_w0, decay_w2=decay_w2, iclr_a0=iclr_a0, iclr_a2=iclr_a2,
        gate_w2=gate_w2, k_k=k_k, k_a=k_a, r_k=r_k, lnx_g=lnx_g, lnx_b=lnx_b, attn_sinks=attn_sinks,
        rel_bias=rel_bias, w_out=w_out, w_pool=w_pool, pool_scale=pool_scale, ln_mix_g=ln_mix_g, ln_mix_b=ln_mix_b,
        ln_ffn_g=ln_ffn_g, ln_ffn_b=ln_ffn_b, w_router=w_router, b_router=b_router, w_ex_gate=w_ex_gate,
        w_ex_up=w_ex_up, w_ex_down=w_ex_down))
    y_p, rwkv_p, shift_p, k_p, v_p, pool_p = _prompt_trunk(x_prompt, prm)
    y_s, rwkv_s, shift_s, k_s, v_s, pool_s = _sample_trunk(x_sample, state_rwkv, state_shift, cache_swa_k,
                                                            cache_swa_v, state_pool, prm)
    return (y_p, y_s, rwkv_p, rwkv_s, shift_p, shift_s, k_p, k_s, v_p, v_s, pool_p, pool_s)
```

```python
import functools
import math

import numpy as np
import jax
import jax.numpy as jnp
from jax import lax
from jax.experimental import pallas as pl
from jax.experimental.pallas import tpu as pltpu

F32 = jnp.float32
BF16 = jnp.bfloat16

D_MODEL = 1024
DEPTH = 2
PAST_LEN = 16384
D_A = 512
HEAD = 64
N_HEADS = 8
W_LORA, A_LORA, G_LORA = 32, 64, 96
RWKV_COLS = 3 * D_A + W_LORA + A_LORA + G_LORA
D_B = 512
KV_HEADS = 2
Q_PER_KV = 4
KV_COLS = KV_HEADS * HEAD
SWA_COLS = D_B + 2 * KV_COLS
WINDOW = 128
N_BUCKETS = 32
MAX_EXACT = 16
REL_MAX_DIST = 128
POOL_WINDOWS = (2, 4, 8, 16)
POOL_GROUP = 256
POOL_KEEP = 15
N_EXPERTS = 16
N_GROUPS = 4
EXPERTS_PER_GROUP = 4
D_FF = 256
ALPHA = (2.0 * DEPTH) ** 0.25
LN_EPS = 1e-5
LNX_EPS = 64e-5
NEG_INF = -1e30

LANES = 128
SUBLANES = 8
PACK = 16
PREP_ROWS = 128
VMEM_LIMIT_BYTES = 56 * 1024 * 1024

LORA_PAD = LANES
RWKV_PAD_COLS = 3 * D_A + 3 * LORA_PAD
CHUNK = 64

NN = ((1,), (0,))
NT = ((1,), (1,))
TN = ((0,), (0,))


def _dg(a, b, dims=NN):
    return lax.dot_general(a, b, (dims, ((), ())), preferred_element_type=F32)


def _bdot(a, b, dims=NN):
    return _dg(a.astype(BF16), b.astype(BF16), dims)


def _split(x, n):
    parts, rem = [], x
    for i in range(n):
        p = rem.astype(BF16)
        parts.append(p)
        if i + 1 < n:
            rem = rem - p.astype(F32)
    return parts


def _mm(a, b, dims=NN, passes=3):
    if passes == 1:
        return _bdot(a, b, dims)
    ah, al = _split(a, 2)
    bh, bl = _split(b, 2)
    return _dg(ah, bh, dims) + (_dg(ah, bl, dims) + _dg(al, bh, dims))


def _dot_exact_rhs(x, ones_bf16, n=3):
    out = None
    for p in _split(x, n):
        t = _dg(p, ones_bf16)
        out = t if out is None else out + t
    return out


def _dot_exact_lhs(ones_bf16, x, n=3):
    out = None
    for p in _split(x, n):
        t = _dg(ones_bf16, p)
        out = t if out is None else out + t
    return out


def _sigmoid(x):
    return 1.0 / (1.0 + jnp.exp(-x))


def _layer_norm(z, g, b):
    mu = jnp.mean(z, axis=-1, keepdims=True)
    d = z - mu
    var = jnp.mean(d * d, axis=-1, keepdims=True)
    return d * lax.rsqrt(var + LN_EPS) * g + b


def _params(sem):
    return pltpu.CompilerParams(dimension_semantics=sem, vmem_limit_bytes=VMEM_LIMIT_BYTES)


def _full(shape):
    nd = len(shape)
    return pl.BlockSpec(shape, lambda *_: (0,) * nd)


def _inproj_prep_body(*refs, tm, tiles_per_seq, chunk, has_prev):
    if has_prev:
        x_ref, xp_ref = refs[0], refs[1]
        refs = refs[2:]
    else:
        x_ref, xp_ref = refs[0], None
        refs = refs[1:]
    (wr_ref, ws_ref, mu_ref, w0_ref, ww2_ref, a0_ref, wa2_ref, wg2_ref, kk_ref, ka_ref, rk_ref,
     ones_ref, tri_ref,
     rt_ref, kt_ref, bt_ref, at_ref, v_ref, bonus_ref, g_ref, gl_ref, swa_ref, carry_ref) = refs

    i = pl.program_id(0)
    sub = tri_ref.shape[0]
    assert tm % sub == 0 and sub % chunk == 0
    blocks = [slice(r0, r0 + sub) for r0 in range(0, tm, sub)]
    st = [dict() for _ in blocks]
    ones_bd = ones_ref[...]
    half = D_A // 2

    if not has_prev:
        @pl.when(i == 0)
        def _():
            carry_ref[...] = jnp.zeros_like(carry_ref)

        first = (i % tiles_per_seq) == 0
        carried = jnp.where(first, 0.0, carry_ref[0:1, :])
        row = lax.broadcasted_iota(jnp.int32, (sub, 1), 0)

    def project(u):
        b_ = blocks[u]
        xb = x_ref[b_, :].astype(BF16)
        pr = _dg(xb, wr_ref[...])
        swa_ref[b_, :] = _dg(xb, ws_ref[...])
        if has_prev:
            shifted = _dg(xp_ref[b_, :].astype(BF16), wr_ref[...])
        else:
            before = carried if u == 0 else st[u - 1]['last']
            shifted = jnp.where(row == 0, before, pltpu.roll(pr, 1, axis=0))
            st[u]['last'] = pr[sub - 1:sub, :]
        st[u]['m'] = pr + mu_ref[...] * (shifted - pr)

    def low_rank(u):
        m = st[u].pop('m')
        s = st[u]
        s['r'], k, s['v'] = m[:, 0:D_A], m[:, D_A:2 * D_A], m[:, 2 * D_A:3 * D_A]
        wd, ad, gd = (m[:, 3 * D_A + j * LORA_PAD:3 * D_A + (j + 1) * LORA_PAD] for j in range(3))
        nz = -(w0_ref[...] + _bdot(jnp.tanh(wd), ww2_ref[...]))
        softplus = jnp.maximum(nz, 0.0) + jnp.log1p(jnp.exp(-jnp.abs(nz)))
        s['logw'] = -jnp.exp(-softplus - 0.5)
        s['a'] = _sigmoid(a0_ref[...] + _bdot(ad, wa2_ref[...]))
        s['g'] = _bdot(_sigmoid(gd), wg2_ref[...])
        s['kkr'] = k * kk_ref[...]
        s['k2'] = k * (1.0 + (s['a'] - 1.0) * ka_ref[...])

    def head_sums(z):
        return jnp.concatenate([_dot_exact_rhs(z[:, :half], ones_bd, 2), _dot_exact_rhs(z[:, half:], ones_bd, 2)],
                               axis=1)

    def sums(u):
        s = st[u]
        s['sq'] = head_sums(s['kkr'] * s['kkr'])
        s['rk'] = head_sums(s['r'] * s['k2'] * rk_ref[...])
        s['cum'] = _dot_exact_lhs(tri_ref[...], s['logw'], 2) if chunk > 1 else s['logw']

    def store(u):
        s, b_ = st[u], blocks[u]
        kk = s['kkr'] / jnp.maximum(jnp.sqrt(s['sq']), 1e-12)
        gam = jnp.exp(s['cum'])
        inv = jnp.exp(-s['cum'])
        bonus_ref[b_, :] = s['rk'] * s['v']
        rt_ref[b_, :] = s['r'] * gam
        kt_ref[b_, :] = s['k2'] * inv
        bt_ref[b_, :] = kk * s['a'] * inv
        at_ref[b_, :] = -kk * jnp.exp(s['cum'] - s['logw'])
        v_ref[b_, :] = s['v']
        g_ref[b_, :] = s['g']
        if chunk > 1:
            for c in range(sub // chunk):
                gl_ref[u * (sub // chunk) + c] = gam[(c + 1) * chunk - 1:(c + 1) * chunk, :]
        else:
            gl_ref[b_, :] = gam

    stages = (project, low_rank, sums, store)
    for t in range(len(blocks) + len(stages) - 1):
        for depth, stage in enumerate(stages):
            if 0 <= t - depth < len(blocks):
                stage(t - depth)
    if not has_prev:
        carry_ref[0:1, :] = st[-1]['last']


def _inproj_prep(x2d, x_prev, wts, *, seq_len, chunk, tm):
    n = x2d.shape[0]
    assert n % tm == 0 and (x_prev is not None or seq_len % tm == 0)
    assert tm % chunk == 0
    has_prev = x_prev is not None
    row = lambda w: pl.BlockSpec((tm, w), lambda i: (i, 0))
    ins = [x2d] + ([x_prev] if has_prev else [])
    in_specs = [row(D_MODEL)] + ([row(D_MODEL)] if has_prev else [])
    consts = [wts['wr'], wts['ws'], wts['mu'], wts['w0'], wts['ww2'], wts['a0'], wts['wa2'], wts['wg2'],
              wts['k_k'], wts['k_a'], wts['r_k'], _ones_block_diag(D_A // 2, HEAD), _tri_block_diag(min(tm, PREP_ROWS), chunk)]
    ins += consts
    in_specs += [_full(c.shape) for c in consts]
    if chunk > 1:
        gl_shape = jax.ShapeDtypeStruct((n // chunk, 1, D_A), F32)
        gl_spec = pl.BlockSpec((tm // chunk, 1, D_A), lambda i: (i, 0, 0))
    else:
        gl_shape = jax.ShapeDtypeStruct((n, D_A), F32)
        gl_spec = row(D_A)
    out_shape = [jax.ShapeDtypeStruct((n, D_A), F32)] * 7 + [gl_shape, jax.ShapeDtypeStruct((n, SWA_COLS), F32)]
    out_specs = [row(D_A)] * 7 + [gl_spec, row(SWA_COLS)]
    body = functools.partial(_inproj_prep_body, tm=tm, tiles_per_seq=max(seq_len // tm, 1), chunk=chunk,
                             has_prev=has_prev)
    return pl.pallas_call(
        body, grid=(n // tm,), in_specs=in_specs, out_specs=out_specs, out_shape=out_shape,
        scratch_shapes=[pltpu.VMEM((SUBLANES, RWKV_PAD_COLS), F32)],
        compiler_params=_params(("arbitrary",)), name="inproj_prep",
    )(*ins)


def _ones_block_diag(n, blk):
    idx = np.arange(n) // blk
    return jnp.asarray(idx[:, None] == idx[None, :], dtype=BF16)


def _tri_block_diag(n, blk):
    idx = np.arange(n)
    same = (idx[:, None] // blk) == (idx[None, :] // blk)
    return jnp.asarray(same & (idx[:, None] >= idx[None, :]), dtype=BF16)


PAIR = 2 * HEAD
N_PAIRS = N_HEADS // 2


def _scan_body(rt_ref, kt_ref, bt_ref, at_ref, v_ref, bonus_ref, g_ref, gl_ref, lng_ref, lnb_ref, s0_ref,
               y_ref, sout_ref, S_ref, *, C, nc, one_chunk_seqs):
    step = pl.program_id(1)

    def pair_state(ref, s, p):
        return jnp.concatenate([ref[s, 2 * p], ref[s, 2 * p + 1]], axis=1)

    if not one_chunk_seqs:
        @pl.when(step == 0)
        def _():
            for p in range(N_PAIRS):
                S_ref[p] = pair_state(s0_ref, 0, p)

    head_a = lax.broadcasted_iota(jnp.int32, (1, PAIR), 1) < HEAD
    rowc = lax.broadcasted_iota(jnp.int32, (C, 2 * C), 0)
    colc = lax.broadcasted_iota(jnp.int32, (C, 2 * C), 1)
    first_c = colc < C
    col_in = jnp.where(first_c, colc, colc - C)
    strict = rowc > col_in
    incl = rowc >= col_in
    eye_c = (rowc == col_in).astype(F32)
    r128 = lax.broadcasted_iota(jnp.int32, (PAIR, PAIR), 0)
    c128 = lax.broadcasted_iota(jnp.int32, (PAIR, PAIR), 1)
    same_head = (r128 < HEAD) == (c128 < HEAD)
    eye128 = (r128 == c128).astype(F32)

    def by_head(x):
        xb = x.astype(BF16)
        z = jnp.zeros_like(xb)
        return jnp.concatenate([jnp.where(head_a, xb, z), jnp.where(head_a, z, xb)], axis=0)

    def head_mean(z):
        za = jnp.sum(jnp.where(head_a, z, 0.0), axis=-1, keepdims=True)
        zb = jnp.sum(jnp.where(head_a, 0.0, z), axis=-1, keepdims=True)
        return jnp.where(head_a, za, zb) * (1.0 / HEAD)

    n_fold = int(math.log2(C)) - 1
    zero_blk = jnp.zeros((2 * C, PAIR), BF16)
    items = [(ci, p) for ci in range(nc) for p in range(N_PAIRS)]
    each = lambda f, *lists: [f(*xs) for xs in zip(*lists)]

    def load(ref):
        if one_chunk_seqs:
            pad = jnp.zeros((C - 1, PAIR), F32)
            return [jnp.concatenate([ref[ci:ci + 1, p * PAIR:(p + 1) * PAIR], pad], axis=0) for ci, p in items]
        return [ref[ci * C:(ci + 1) * C, p * PAIR:(p + 1) * PAIR] for ci, p in items]

    R, K, Bm, A, V = load(rt_ref), load(kt_ref), load(bt_ref), load(at_ref), load(v_ref)
    big = each(lambda a, r, b, k: _dg(jnp.concatenate([a, r], axis=0).astype(BF16),
                                      jnp.concatenate([by_head(b), by_head(k)], axis=0), NT), A, R, Bm, K)
    a_ab = [jnp.where(strict, x[:C, :2 * C], 0.0) for x in big]
    a_ak = [jnp.where(strict, x[:C, 2 * C:], 0.0) for x in big]
    l_rb = [jnp.where(incl, x[C:, :2 * C], 0.0) for x in big]
    l_rk = [jnp.where(incl, x[C:, 2 * C:], 0.0) for x in big]
    def blocks(x):
        xb = x.astype(BF16)
        z = jnp.zeros_like(xb)
        return jnp.concatenate([jnp.where(first_c, xb, z), jnp.where(first_c, z, xb)], axis=0)

    X = [_dg(x.astype(BF16), blocks(x)) for x in a_ab]
    T = [eye_c + x for x in a_ab]
    for k in range(1, n_fold + 1):
        if k < n_fold:
            res = each(lambda x, t: _dg(jnp.concatenate([x, t], axis=0).astype(BF16), blocks(x)), X, T)
            X = [r_[:C] for r_ in res]
            T = each(lambda t, r_: t + r_[C:], T, res)
        else:
            T = each(lambda t, x: t + _dg(t.astype(BF16), blocks(x)), T, X)
    akv = each(lambda m_, v_: _dg(m_.astype(BF16), by_head(v_)), a_ak, V)
    pq = each(lambda t, a, q: _dg(t.astype(BF16), jnp.concatenate([by_head(a), by_head(q)], axis=1)), T, A, akv)
    P = [x[:, :PAIR] for x in pq]
    Q = [x[:, PAIR:] for x in pq]
    wz = each(lambda lb, lk, p_, q_, v_: _dg(
        jnp.concatenate([lb, lk], axis=1).astype(BF16),
        jnp.concatenate([jnp.concatenate([by_head(p_), by_head(q_)], axis=1),
                         jnp.concatenate([zero_blk, by_head(v_)], axis=1)], axis=0)), l_rb, l_rk, P, Q, V)
    W = each(lambda r, x: r + x[:, :PAIR], R, wz)
    Z = [x[:, PAIR:] for x in wz]
    ptb = each(lambda p_, b: _bdot(p_, b, TN), P, Bm)
    gfull = each(lambda q_, v_, b, k: _bdot(jnp.concatenate([q_, v_], axis=0), jnp.concatenate([b, k], axis=0), TN),
                 Q, V, Bm, K)

    def put_state(s, p, val):
        sout_ref[s, 2 * p] = val[:, :HEAD]
        sout_ref[s, 2 * p + 1] = val[:, HEAD:]

    state = [None if one_chunk_seqs else S_ref[p] for p in range(N_PAIRS)]
    for idx, (ci, p) in enumerate(items):
        rows = slice(ci, ci + 1) if one_chunk_seqs else slice(ci * C, (ci + 1) * C)
        ln = slice(p * PAIR, (p + 1) * PAIR)
        g_c = gl_ref[ci, :, ln]
        M = (eye128 + jnp.where(same_head, ptb[idx], 0.0)) * g_c
        G = jnp.where(head_a, gfull[idx][:HEAD], gfull[idx][HEAD:]) * g_c
        S0 = pair_state(s0_ref, ci, p) if one_chunk_seqs else state[p]
        Y = _dg(W[idx].astype(BF16), by_head(S0), NT) + Z[idx]
        S1 = _bdot(S0, M) + G
        if one_chunk_seqs:
            put_state(ci, p, S1)
        else:
            state[p] = S1

        if one_chunk_seqs:
            Y = Y[0:1]
        mu = head_mean(Y)
        d = Y - mu
        var = head_mean(d * d)
        yn = d * lax.rsqrt(var + LNX_EPS) * lng_ref[:, ln] + lnb_ref[:, ln]
        y_ref[rows, ln] = ((yn + bonus_ref[rows, ln]) * g_ref[rows, ln]).astype(y_ref.dtype)

    if not one_chunk_seqs:
        for p in range(N_PAIRS):
            S_ref[p] = state[p]

        @pl.when(step == pl.num_programs(1) - 1)
        def _():
            for p in range(N_PAIRS):
                put_state(0, p, state[p])


def _rwkv_scan(rt, kt, bt, at, v, bonus, g, gl, lnx_g, lnx_b, s0, *, n_seq, seq_len, chunk, nc):
    one_chunk_seqs = seq_len == chunk
    if one_chunk_seqs:
        assert n_seq % nc == 0
        grid = (n_seq // nc, 1)
        blk = lambda b, c: b
        st = pl.BlockSpec((nc, N_HEADS, HEAD, HEAD), lambda b, c: (b, 0, 0, 0))
        rows, n_rows = nc, n_seq
    else:
        rows, n_rows = chunk * nc, n_seq * seq_len
        assert seq_len % rows == 0
        nsteps = seq_len // rows
        grid = (n_seq, nsteps)
        blk = lambda b, c: b * nsteps + c
        st = pl.BlockSpec((1, N_HEADS, HEAD, HEAD), lambda b, c: (b, 0, 0, 0))
    row = pl.BlockSpec((rows, D_A), lambda b, c: (blk(b, c), 0))
    vec = pl.BlockSpec((1, D_A), lambda b, c: (0, 0))
    return pl.pallas_call(
        functools.partial(_scan_body, C=chunk, nc=nc, one_chunk_seqs=one_chunk_seqs),
        grid=grid,
        in_specs=[row] * 7 + [pl.BlockSpec((nc, 1, D_A), lambda b, c: (blk(b, c), 0, 0)), vec, vec, st],
        out_specs=[row, st],
        out_shape=[jax.ShapeDtypeStruct((n_rows, D_A), BF16 if n_rows % PACK == 0 and rows % PACK == 0 else F32),
                   jax.ShapeDtypeStruct((n_seq, N_HEADS, HEAD, HEAD), F32)],
        scratch_shapes=[pltpu.VMEM((N_PAIRS, HEAD, PAIR), F32)],
        compiler_params=_params(("arbitrary", "arbitrary")), name="rwkv_scan",
    )(rt, kt, bt, at, v, bonus, g, gl, lnx_g, lnx_b, s0)


def _bucket_ranges():
    d = np.arange(WINDOW + 1)
    scaled = np.log(np.maximum(d, MAX_EXACT).astype(np.float32) / MAX_EXACT) / math.log(REL_MAX_DIST / MAX_EXACT)
    large = np.minimum(MAX_EXACT + (scaled * (N_BUCKETS - MAX_EXACT)).astype(np.int32), N_BUCKETS - 1)
    bucket = np.where(d < MAX_EXACT, d, large)
    frac = scaled.astype(np.float64) * (N_BUCKETS - MAX_EXACT)
    near = np.abs(frac - np.round(frac)) < 1e-3
    assert all(int(x) in (MAX_EXACT, REL_MAX_DIST) for x in d[(d >= MAX_EXACT) & near])
    ranges = []
    for b in range(N_BUCKETS):
        hit = d[bucket == b]
        ranges.append((int(hit.min()), int(hit.max())) if hit.size else None)
    return tuple(ranges)


def _swa_body(rb_ref, sink_ref, cur_ref, *rest, qr, fresh, per_step, ranges):
    prevs, (y_ref, bias_ref, sinkcol_ref) = rest[:-3], rest[-3:]
    W = WINDOW
    nk = W + qr

    @pl.when((pl.program_id(0) == 0) & (pl.program_id(1) == 0))
    def _():
        qi = lax.broadcasted_iota(jnp.int32, (qr, nk), 0)
        kj = lax.broadcasted_iota(jnp.int32, (qr, nk), 1)
        d = qi + W - kj
        valid = (d >= 0) & (d <= W)
        for hd in range(N_HEADS):
            t = jnp.zeros((qr, nk), F32)
            for bkt, rng in enumerate(ranges):
                if rng is not None:
                    t = jnp.where((d >= rng[0]) & (d <= rng[1]), rb_ref[bkt * N_HEADS + hd], t)
            kvh, grp = hd // Q_PER_KV, hd % Q_PER_KV
            rows = slice(grp * qr, (grp + 1) * qr)
            bias_ref[kvh, rows, :] = jnp.where(valid, t, NEG_INF)
            bias_ref[KV_HEADS + kvh, rows, :] = jnp.where(valid & (kj >= W), t, NEG_INF)
            sinkcol_ref[kvh, rows, :] = jnp.full((qr, 1), sink_ref[hd], F32)

    first_table = jnp.where(pl.program_id(1) == 0, KV_HEADS, 0) if fresh else 0
    items = [(sub, kvh) for sub in range(per_step) for kvh in range(KV_HEADS)]
    each = lambda f, *lists: [f(*xs) for xs in zip(*lists)]

    def own_rows(sub, col):
        if fresh:
            return cur_ref[sub * qr:(sub + 1) * qr, col:col + HEAD]
        return jnp.concatenate([cur_ref[sub:sub + 1, col:col + HEAD], jnp.zeros((qr - 1, HEAD), F32)], axis=0)

    def kv_rows(sub, kvh, is_v):
        col = D_B + is_v * KV_COLS + kvh * HEAD
        if not fresh:
            before = prevs[is_v][sub, :, kvh, :]
        elif sub > 0:
            before = cur_ref[(sub - 1) * qr:sub * qr, col:col + HEAD]
        else:
            before = prevs[0][:, col - D_B:col - D_B + HEAD]
        return jnp.concatenate([before, own_rows(sub, col)], axis=0).astype(BF16)

    kb = [kv_rows(sub, kvh, 0) for sub, kvh in items]
    vb = [kv_rows(sub, kvh, 1) for sub, kvh in items]
    q4 = [(jnp.concatenate([own_rows(sub, (kvh * Q_PER_KV + gq) * HEAD) for gq in range(Q_PER_KV)], axis=0)
           * (HEAD ** -0.5)).astype(BF16) for sub, kvh in items]
    s = [_dg(q, k, NT) + bias_ref[(first_table if sub == 0 else 0) + kvh] for q, k, (sub, kvh) in zip(q4, kb, items)]
    sink = [sinkcol_ref[kvh] for _, kvh in items]
    m = each(lambda s_, sk: jnp.maximum(jnp.max(s_, axis=-1, keepdims=True), sk), s, sink)
    p = each(lambda s_, m_: jnp.exp(s_ - m_).astype(BF16), s, m)
    pad = jnp.concatenate([jnp.zeros((nk, LANES - HEAD), BF16), jnp.ones((nk, LANES), BF16)], axis=1)
    ov = each(lambda p_, v_: _dg(p_, jnp.concatenate([v_, pad], axis=1)), p, vb)
    o = each(lambda ov_, sk, m_: ov_[:, :LANES] / (ov_[:, LANES:] + jnp.exp(sk - m_)), ov, sink, m)
    for sub in range(per_step):
        outs = [o[sub * KV_HEADS + kvh][gq * qr:(gq + 1) * qr, :HEAD]
                for kvh in range(KV_HEADS) for gq in range(Q_PER_KV)]
        y = jnp.concatenate(outs, axis=1).astype(y_ref.dtype)
        if fresh:
            y_ref[sub * qr:(sub + 1) * qr, :] = y
        else:
            y_ref[sub:sub + 1, :] = y[0:1]


def _swa(cur2d, prevs, rel_bias, sinks, *, n_seq, n_blk, qr, fresh, per_step=1):
    smem = pl.BlockSpec(memory_space=pltpu.SMEM)
    if fresh:
        assert qr == WINDOW and n_blk % per_step == 0
        blocks = n_blk
        n_blk //= per_step
        prev_specs = [pl.BlockSpec((WINDOW, 2 * KV_COLS),
                                   lambda b, j: (b * blocks + jnp.maximum(j * per_step - 1, 0), 2))]
    else:
        assert n_blk == 1 and n_seq % per_step == 0
        n_seq //= per_step
        prev_specs = [pl.BlockSpec((per_step, WINDOW, KV_HEADS, HEAD), lambda b, j: (b, 0, 0, 0))] * 2
    qrows = per_step * qr if fresh else per_step
    out_dtype = BF16 if fresh else F32
    return pl.pallas_call(
        functools.partial(_swa_body, qr=qr, fresh=fresh, per_step=per_step, ranges=_bucket_ranges()),
        grid=(n_seq, n_blk),
        in_specs=[smem, smem, pl.BlockSpec((qrows, SWA_COLS), lambda b, j: (b * n_blk + j, 0))] + prev_specs,
        out_specs=pl.BlockSpec((qrows, D_B), lambda b, j: (b * n_blk + j, 0)),
        out_shape=jax.ShapeDtypeStruct((n_seq * n_blk * qrows, D_B), out_dtype),
        scratch_shapes=[pltpu.VMEM((2 * KV_HEADS, Q_PER_KV * qr, WINDOW + qr), F32),
                        pltpu.VMEM((KV_HEADS, Q_PER_KV * qr, 1), F32)],
        compiler_params=_params(("arbitrary", "arbitrary")), name="swa",
    )(rel_bias.reshape(-1), sinks, cur2d, *prevs)


def _outproj_ln_body(ya_ref, yb_ref, x_ref, wa_ref, wb_ref, g_ref, b_ref, *rest):
    o_ref = rest[0] if len(rest) == 1 else rest[N_ROUTE_INPUTS]
    h = _dg(ya_ref[...].astype(BF16), wa_ref[...]) + _dg(yb_ref[...].astype(BF16), wb_ref[...])
    out = _layer_norm(ALPHA * x_ref[...] + h, g_ref[...], b_ref[...])
    o_ref[...] = out
    if len(rest) > 1:
        _route(out, *rest[:N_ROUTE_INPUTS], *rest[N_ROUTE_INPUTS + 1:])


def _outproj_ln(ya, yb, x2d, w_out, ln_g, ln_b, *, tm, router=None):
    n = x2d.shape[0]
    half = pl.BlockSpec((tm, D_A), lambda i: (i, 0))
    full = pl.BlockSpec((tm, D_MODEL), lambda i: (i, 0))
    wa, wb = w_out[:D_A].astype(BF16), w_out[D_A:].astype(BF16)
    ins = [ya, yb, x2d, wa, wb, ln_g.reshape(1, -1), ln_b.reshape(1, -1)]
    in_specs = [half, half, full, _full(wa.shape), _full(wb.shape), _full((1, D_MODEL)), _full((1, D_MODEL))]
    out_specs, out_shape = [full], [jax.ShapeDtypeStruct((n, D_MODEL), F32)]
    if router is not None:
        r_ins, r_specs = _route_operands(*router, tm)
        ins, in_specs = ins + r_ins, in_specs + r_specs
        r_out_specs, r_out_shape = _route_outputs(n, tm)
        out_specs, out_shape = out_specs + r_out_specs, out_shape + r_out_shape
    res = pl.pallas_call(
        _outproj_ln_body, grid=(n // tm,), in_specs=in_specs, out_specs=out_specs, out_shape=out_shape,
        compiler_params=_params(("arbitrary",)), name="outproj_ln",
    )(*ins)
    return res[0] if router is None else res


def _route(x, wr_ref, br_ref, tri_ref, meta_ref, slot_ref, cnt_ref):
    logits = jnp.transpose(_dg(x.astype(BF16), wr_ref[...]))[:N_EXPERTS] + br_ref[...]
    z = jnp.exp(logits - jnp.max(logits, axis=0, keepdims=True))
    probs = z / jnp.sum(z, axis=0, keepdims=True)
    pa = [probs[a_ * N_GROUPS:(a_ + 1) * N_GROUPS] for a_ in range(EXPERTS_PER_GROUP)]

    sel = []
    for a_ in range(EXPERTS_PER_GROUP):
        rank = None
        for j in range(EXPERTS_PER_GROUP):
            if j == a_:
                continue
            ahead = ((pa[j] >= pa[a_]) if j < a_ else (pa[j] > pa[a_])).astype(jnp.int32)
            rank = ahead if rank is None else rank + ahead
        sel.append(rank < 2)
    score = None
    for a_ in range(EXPERTS_PER_GROUP):
        t = jnp.where(sel[a_], pa[a_], 0.0)
        score = t if score is None else score + t
    srow = [score[gi:gi + 1] for gi in range(N_GROUPS)]
    best = []
    for gi in range(N_GROUPS):
        ok = None
        for j in range(N_GROUPS):
            if j == gi:
                continue
            c_ = (srow[gi] > srow[j]) if j < gi else (srow[gi] >= srow[j])
            ok = c_ if ok is None else (ok & c_)
        best.append(ok.astype(F32))
    tm = x.shape[0]
    best4 = jnp.concatenate(best, axis=0)
    chosen = best4 > 0.5
    kept = [jnp.where(sel[a_] & chosen, pa[a_], 0.0) for a_ in range(EXPERTS_PER_GROUP)]
    tot = None
    for a_ in range(EXPERTS_PER_GROUP):
        tot = kept[a_] if tot is None else tot + kept[a_]
    tot = jnp.sum(tot, axis=0, keepdims=True)
    gates = [jnp.where(sel[a_] & chosen, pa[a_] / tot, 0.0) for a_ in range(EXPERTS_PER_GROUP)]

    onehot = jnp.concatenate([best4, jnp.zeros((SUBLANES - N_GROUPS, tm), F32)], axis=0)
    oh_b = jnp.concatenate([onehot, jnp.zeros((LANES - SUBLANES, tm), F32)], axis=0).astype(BF16)
    incl = _dg(onehot.astype(BF16), tri_ref[...])
    slot = None
    offset = jnp.zeros((1, 1), F32)
    for gi in range(N_GROUPS):
        t = onehot[gi:gi + 1] * (offset + incl[gi:gi + 1] - 1.0)
        slot = t if slot is None else slot + t
        offset = offset + incl[gi:gi + 1, tm - 1:tm]
    slot_ref[0] = slot.astype(jnp.int32)
    cnt_ref[0] = _dg(jnp.ones((SUBLANES, tm), BF16), oh_b, NT)[0:1]
    rows = jnp.concatenate(gates + [slot, jnp.zeros((LANES - N_EXPERTS - 1, tm), F32)], axis=0)
    meta_ref[...] = jnp.transpose(rows)


META_SLOT = N_EXPERTS
N_ROUTE_INPUTS = 3


def _gate_lane(group, j):
    return j * N_GROUPS + group


def _route_operands(w_router, b_router, tm):
    idx = np.arange(tm)
    tri = jnp.asarray(idx[:, None] <= idx[None, :], dtype=BF16)
    by_member = lambda a: a.reshape(N_GROUPS, EXPERTS_PER_GROUP, -1).transpose(1, 0, 2).reshape(N_EXPERTS, -1)
    wr = jnp.pad(by_member(w_router.T).T, ((0, 0), (0, LANES - N_EXPERTS))).astype(BF16)
    ins = [wr, by_member(b_router), tri]
    return ins, [_full(a.shape) for a in ins]


def _route_outputs(n, tm):
    nt = n // tm
    specs = [pl.BlockSpec((tm, LANES), lambda i: (i, 0)), pl.BlockSpec((1, 1, tm), lambda i: (i, 0, 0)),
             pl.BlockSpec((1, 1, LANES), lambda i: (i, 0, 0))]
    shapes = [jax.ShapeDtypeStruct((n, LANES), F32), jax.ShapeDtypeStruct((nt, 1, tm), jnp.int32),
              jax.ShapeDtypeStruct((nt, 1, LANES), F32)]
    return specs, shapes


def _router_body(x_ref, *refs):
    _route(x_ref[...], *refs)


def _router(x2d, w_router, b_router, *, tm):
    n = x2d.shape[0]
    ins, in_specs = _route_operands(w_router, b_router, tm)
    out_specs, out_shape = _route_outputs(n, tm)
    return pl.pallas_call(
        _router_body, grid=(n // tm,),
        in_specs=[pl.BlockSpec((tm, D_MODEL), lambda i: (i, 0))] + in_specs,
        out_specs=out_specs, out_shape=out_shape,
        compiler_params=_params(("arbitrary",)), name="router",
    )(x2d, *ins)


def _moe_ln_body(x_ref, gates_ref, wg_ref, wu_ref, wd_ref, g_ref, b_ref, o_ref, acc_ref, xb_ref):
    gi = pl.program_id(1)

    @pl.when(gi == 0)
    def _():
        xb_ref[...] = x_ref[...].astype(BF16)
        acc_ref[...] = jnp.zeros_like(acc_ref)

    xb = xb_ref[...]
    lane = lax.broadcasted_iota(jnp.int32, gates_ref.shape, 1)
    hs = []
    for j in range(EXPERTS_PER_GROUP):
        hg = _dg(xb, wg_ref[0, j])
        hu = _dg(xb, wu_ref[0, j])
        gate = jnp.sum(jnp.where(lane == _gate_lane(gi, j), gates_ref[...], 0.0), axis=1, keepdims=True)
        hs.append((hg * _sigmoid(hg) * hu * gate).astype(BF16))
    acc_ref[...] += _dg(jnp.concatenate(hs, axis=1), wd_ref[0])

    @pl.when(gi == N_GROUPS - 1)
    def _():
        o_ref[...] = _layer_norm(ALPHA * x_ref[...] + acc_ref[...], g_ref[...], b_ref[...])


def _moe_ln(x2d, gates, wg, wu, wd, ln_g, ln_b, *, layer, tm):
    n = x2d.shape[0]
    tok = pl.BlockSpec((tm, D_MODEL), lambda i, e: (i, 0))
    wspec = pl.BlockSpec((None, 1, EXPERTS_PER_GROUP, D_MODEL, D_FF), lambda i, e: (layer, e, 0, 0, 0))
    return pl.pallas_call(
        _moe_ln_body, grid=(n // tm, N_GROUPS),
        in_specs=[tok, pl.BlockSpec((tm, LANES), lambda i, e: (i, 0)), wspec, wspec,
                  pl.BlockSpec((None, 1, EXPERTS_PER_GROUP * D_FF, D_MODEL), lambda i, e: (layer, e, 0, 0)),
                  pl.BlockSpec((1, D_MODEL), lambda i, e: (0, 0)), pl.BlockSpec((1, D_MODEL), lambda i, e: (0, 0))],
        out_specs=tok, out_shape=jax.ShapeDtypeStruct((n, D_MODEL), F32),
        scratch_shapes=[pltpu.VMEM((tm, D_MODEL), F32), pltpu.VMEM((tm, D_MODEL), BF16)],
        compiler_params=_params(("arbitrary", "arbitrary")), name="moe_ln",
    )(x2d, gates, wg, wu, wd, ln_g.reshape(1, -1), ln_b.reshape(1, -1))


def _moe_sorted_body(start_ref, nwin_ref, x_ref, meta_ref, slot_ref, wg_ref, wu_ref, wd_ref, g_ref, b_ref, o_ref,
                     xs_ref, gs_ref, ys_ref, *, tm, wn):
    i = pl.program_id(0)

    @pl.when(i == 0)
    def _():
        xs_ref[tm:, :] = jnp.zeros((wn, D_MODEL), BF16)
        gs_ref[tm:, :] = jnp.zeros((wn, LANES), F32)

    x = x_ref[...]
    meta = meta_ref[...]
    slot_iota = lax.broadcasted_iota(jnp.int32, (tm, tm), 0)
    perm = jnp.where(slot_iota == slot_ref[0], 1.0, 0.0).astype(BF16)
    m_hi, m_lo = _split(meta, 2)
    moved = _dg(perm, jnp.concatenate([x.astype(BF16), m_hi, m_lo], axis=1))
    xs_ref[:tm, :] = moved[:, :D_MODEL].astype(BF16)
    gs_ref[:tm, :] = moved[:, D_MODEL:D_MODEL + LANES] + moved[:, D_MODEL + LANES:]
    ys_ref[...] = jnp.zeros_like(ys_ref)

    for gi in range(N_GROUPS):
        first = start_ref[i * N_GROUPS + gi]

        def window(w, carry, gi=gi, first=first):
            st = pl.multiple_of(first + w * wn, PACK)
            xw = xs_ref[pl.ds(st, wn), :]
            gw = gs_ref[pl.ds(st, wn), :]
            hs = []
            for j in range(EXPERTS_PER_GROUP):
                e = _gate_lane(gi, j)
                hg = _dg(xw, wg_ref[gi, j])
                hu = _dg(xw, wu_ref[gi, j])
                hs.append((hg * _sigmoid(hg) * hu * gw[:, e:e + 1]).astype(BF16))
            ys_ref[pl.ds(st, wn), :] += _dg(jnp.concatenate(hs, axis=1), wd_ref[gi])
            return carry

        lax.fori_loop(0, nwin_ref[i * N_GROUPS + gi], window, 0)

    tok_slot = meta[:, META_SLOT:META_SLOT + 1]
    lane_slot = lax.broadcasted_iota(jnp.int32, (tm, tm), 1).astype(F32)
    unperm = jnp.where(tok_slot == lane_slot, 1.0, 0.0).astype(BF16)
    y = _bdot(unperm, ys_ref[:tm, :])
    o_ref[...] = _layer_norm(ALPHA * x + y, g_ref[...], b_ref[...])


def _moe_sorted_ln(x2d, meta, slot_rows, counts, wg, wu, wd, ln_g, ln_b, *, layer, tm, wn):
    n = x2d.shape[0]
    nt = n // tm
    assert wn % PACK == 0 and tm % PACK == 0
    cnt = counts[:, 0, :N_GROUPS].astype(jnp.int32)
    offs = jnp.cumsum(cnt, axis=1) - cnt
    first = (offs // PACK) * PACK
    nwin = jnp.where(cnt > 0, (offs - first + cnt + wn - 1) // wn, 0)
    tok = lambda w: pl.BlockSpec((tm, w), lambda i, *_: (i, 0))
    resident = lambda a: pl.BlockSpec((None,) + a.shape[1:], lambda i, *_: (layer,) + (0,) * (a.ndim - 1),
                                      pipeline_mode=pl.Buffered(1))
    vec = pl.BlockSpec((1, D_MODEL), lambda i, *_: (0, 0))
    grid_spec = pltpu.PrefetchScalarGridSpec(
        num_scalar_prefetch=2, grid=(nt,),
        in_specs=[tok(D_MODEL), tok(LANES), pl.BlockSpec((1, 1, tm), lambda i, *_: (i, 0, 0)),
                  resident(wg), resident(wu), resident(wd), vec, vec],
        out_specs=tok(D_MODEL),
        scratch_shapes=[pltpu.VMEM((tm + wn, D_MODEL), BF16), pltpu.VMEM((tm + wn, LANES), F32),
                        pltpu.VMEM((tm + wn, D_MODEL), F32)])
    return pl.pallas_call(
        functools.partial(_moe_sorted_body, tm=tm, wn=wn), grid_spec=grid_spec,
        out_shape=jax.ShapeDtypeStruct((n, D_MODEL), F32),
        compiler_params=_params(("arbitrary",)), name="moe_sorted_ln",
    )(first.reshape(-1), nwin.reshape(-1), x2d, meta, slot_rows, wg, wu, wd, ln_g.reshape(1, -1), ln_b.reshape(1, -1))


HALO = 16


def _pool_ln_body(x_ref, halo_ref, wp_ref, ps_ref, g_ref, b_ref, *rest, tm, seq_len, start_pos):
    o_ref = rest[0] if len(rest) == 1 else rest[N_ROUTE_INPUTS]
    i = pl.program_id(0)
    x = x_ref[...]
    idx = lax.broadcasted_iota(jnp.int32, (tm + HALO, 1), 0)
    pos_e = (i * tm - HALO + idx) & (seq_len - 1)
    cnt_pos = (pos_e[HALO:] + (start_pos + 1)).astype(F32)
    mixed = []
    for gi, w in enumerate(POOL_WINDOWS):
        cols = slice(gi * POOL_GROUP, (gi + 1) * POOL_GROUP)
        xg = x[:, cols]
        s = jnp.concatenate([halo_ref[:, cols], xg], axis=0)
        step = 1
        while step < w:
            s = s + jnp.where(pos_e >= step, pltpu.roll(s, step, axis=0), 0.0)
            step *= 2
        pooled = s[HALO:] / jnp.minimum(float(w), cnt_pos) - xg
        mixed.append(_bdot(pooled, wp_ref[gi]))
    h = jnp.concatenate(mixed, axis=1) * ps_ref[...]
    out = _layer_norm(ALPHA * x + h, g_ref[...], b_ref[...])
    o_ref[...] = out
    if len(rest) > 1:
        _route(out, *rest[:N_ROUTE_INPUTS], *rest[N_ROUTE_INPUTS + 1:])


def _pool_ln(x2d, w_pool, pool_scale, ln_g, ln_b, *, tm, seq_len, start_pos, router=None):
    n = x2d.shape[0]
    assert seq_len & (seq_len - 1) == 0 and seq_len >= HALO and tm % HALO == 0
    tok = pl.BlockSpec((tm, D_MODEL), lambda i: (i, 0))
    ins = [x2d, x2d, w_pool.astype(BF16), pool_scale.reshape(1, -1), ln_g.reshape(1, -1), ln_b.reshape(1, -1)]
    in_specs = [tok, pl.BlockSpec((HALO, D_MODEL), lambda i: (jnp.maximum(i * (tm // HALO) - 1, 0), 0)),
                _full(w_pool.shape), _full((1, D_MODEL)), _full((1, D_MODEL)), _full((1, D_MODEL))]
    out_specs, out_shape = [tok], [jax.ShapeDtypeStruct((n, D_MODEL), F32)]
    if router is not None:
        r_ins, r_specs = _route_operands(*router, tm)
        ins, in_specs = ins + r_ins, in_specs + r_specs
        r_out_specs, r_out_shape = _route_outputs(n, tm)
        out_specs, out_shape = out_specs + r_out_specs, out_shape + r_out_shape
    res = pl.pallas_call(
        functools.partial(_pool_ln_body, tm=tm, seq_len=seq_len, start_pos=start_pos), grid=(n // tm,),
        in_specs=in_specs, out_specs=out_specs, out_shape=out_shape,
        compiler_params=_params(("arbitrary",)), name="pool_ln",
    )(*ins)
    return res[0] if router is None else res


def _pad_lora_cols(w, width):
    return jnp.pad(w, ((0, 0), (0, LORA_PAD - width)))


def _even_layer_weights(prm, i):
    w_in = prm['w_in'][i]
    o = 3 * D_A
    cols = lambda w: jnp.concatenate(
        [w[:, :o], _pad_lora_cols(w[:, o:o + W_LORA], W_LORA),
         _pad_lora_cols(w[:, o + W_LORA:o + W_LORA + A_LORA], A_LORA),
         _pad_lora_cols(w[:, o + W_LORA + A_LORA:RWKV_COLS], G_LORA)], axis=1)
    pad_rows = lambda w: jnp.pad(w, ((0, LORA_PAD - w.shape[0]), (0, 0))).astype(BF16)
    vec = lambda a: a.reshape(1, -1)
    return dict(
        wr=cols(w_in[:, :RWKV_COLS]).astype(BF16), ws=w_in[:, RWKV_COLS:].astype(BF16),
        mu=cols(prm['tshift_mu'][i].reshape(1, -1)),
        w0=vec(prm['decay_w0'][i]), ww2=pad_rows(prm['decay_w2'][i]),
        a0=vec(prm['iclr_a0'][i]), wa2=pad_rows(prm['iclr_a2'][i]), wg2=pad_rows(prm['gate_w2'][i]),
        k_k=vec(prm['k_k'][i]), k_a=vec(prm['k_a'][i]), r_k=vec(prm['r_k'][i]),
    )


def _group_experts(w):
    return w.astype(BF16).reshape(DEPTH, N_GROUPS, EXPERTS_PER_GROUP, D_MODEL, D_FF)


MOE_SORT_TILE = 512
MOE_WINDOW = 160


def _moe_weights(prm, layer):
    return (prm['wg'], prm['wu'], prm['wd'], prm['ln_ffn_g'][layer], prm['ln_ffn_b'][layer])


def _moe_block(x2d, prm, layer, *, tm):
    meta, _, _ = _router(x2d, prm['w_router'], prm['b_router'], tm=min(tm, 512))
    return _moe_ln(x2d, meta, *_moe_weights(prm, layer), layer=layer, tm=tm)


PROMPT_CHUNKS_PER_STEP = 8
PROMPT_ATTN_BLOCKS_PER_STEP = 8
SAMPLE_CHUNK = 16
SAMPLE_QROWS = SUBLANES
SAMPLE_SEQS_PER_STEP = 8


def _prompt_trunk(x, prm):
    bsz, t, _ = x.shape
    n = bsz * t
    x0 = x.reshape(n, D_MODEL)
    wts = _even_layer_weights(prm, 0)
    rt, kt, bt, at, v, bonus, g, gl, swa = _inproj_prep(x0, None, wts, seq_len=t, chunk=CHUNK, tm=512)
    s0 = jnp.zeros((bsz, N_HEADS, HEAD, HEAD), F32)
    ya, s_new = _rwkv_scan(rt, kt, bt, at, v, bonus, g, gl, prm['lnx_g'][0].reshape(1, -1),
                           prm['lnx_b'][0].reshape(1, -1), s0, n_seq=bsz, seq_len=t, chunk=CHUNK,
                           nc=min(PROMPT_CHUNKS_PER_STEP, t // CHUNK))
    n_blk = t // WINDOW
    qb = min(PROMPT_ATTN_BLOCKS_PER_STEP, n_blk)
    yb = _swa(swa, (swa,), prm['rel_bias'], prm['attn_sinks'][0], n_seq=bsz, n_blk=n_blk, qr=WINDOW, fresh=True,
              per_step=qb)
    tm = MOE_SORT_TILE
    assert n % tm == 0
    router = (prm['w_router'], prm['b_router'])
    x1, *routing = _outproj_ln(ya, yb, x0, prm['w_out'][0], prm['ln_mix_g'][0], prm['ln_mix_b'][0], tm=tm,
                               router=router)
    x2 = _moe_sorted_ln(x1, *routing, *_moe_weights(prm, 0), layer=0, tm=tm, wn=MOE_WINDOW)
    x3, *routing = _pool_ln(x2, prm['w_pool'][0], prm['pool_scale'][0], prm['ln_mix_g'][1], prm['ln_mix_b'][1],
                            tm=tm, seq_len=t, start_pos=0, router=router)
    x4 = _moe_sorted_ln(x3, *routing, *_moe_weights(prm, 1), layer=1, tm=tm, wn=MOE_WINDOW)
    swa3 = swa.reshape(bsz, t, SWA_COLS)
    k_new = swa3[:, t - WINDOW:, D_B:D_B + KV_COLS].reshape(bsz, WINDOW, KV_HEADS, HEAD)
    v_new = swa3[:, t - WINDOW:, D_B + KV_COLS:].reshape(bsz, WINDOW, KV_HEADS, HEAD)
    pool_new = x2.reshape(bsz, t, D_MODEL)[:, t - POOL_KEEP:]
    return (x4.reshape(bsz, t, D_MODEL), s_new[None], x[:, -1][None], k_new[None], v_new[None], pool_new[None])


def _sample_trunk(x, st_rwkv, st_shift, c_k, c_v, st_pool, prm):
    bsz = x.shape[0]
    x0 = x.reshape(bsz, D_MODEL)
    wts = _even_layer_weights(prm, 0)
    outs = _inproj_prep(x0, st_shift[0], wts, seq_len=1, chunk=1, tm=bsz)
    gl, swa = outs[7], outs[8]
    ya, s_new = _rwkv_scan(*outs[:7], gl.reshape(bsz, 1, D_A), prm['lnx_g'][0].reshape(1, -1),
                           prm['lnx_b'][0].reshape(1, -1), st_rwkv[0], n_seq=bsz, seq_len=SAMPLE_CHUNK,
                           chunk=SAMPLE_CHUNK, nc=SAMPLE_SEQS_PER_STEP)
    yb = _swa(swa, (c_k[0], c_v[0]), prm['rel_bias'], prm['attn_sinks'][0], n_seq=bsz, n_blk=1,
              qr=SAMPLE_QROWS, fresh=False, per_step=SAMPLE_SEQS_PER_STEP)
    x1 = _outproj_ln(ya, yb, x0, prm['w_out'][0], prm['ln_mix_g'][0], prm['ln_mix_b'][0], tm=bsz)
    x2 = _moe_block(x1, prm, 0, tm=bsz)
    xcat = jnp.concatenate([st_pool[0], x2[:, None]], axis=1)
    x3 = _pool_ln(xcat.reshape(bsz * HALO, D_MODEL), prm['w_pool'][0], prm['pool_scale'][0],
                  prm['ln_mix_g'][1], prm['ln_mix_b'][1], tm=min(512, bsz * HALO), seq_len=HALO,
                  start_pos=PAST_LEN - POOL_KEEP)
    x3 = x3.reshape(bsz, HALO, D_MODEL)[:, -1]
    x4 = _moe_block(x3, prm, 1, tm=bsz)
    k_new = jnp.concatenate([c_k[0][:, 1:], swa[:, D_B:D_B + KV_COLS].reshape(bsz, 1, KV_HEADS, HEAD)], axis=1)
    v_new = jnp.concatenate([c_v[0][:, 1:], swa[:, D_B + KV_COLS:].reshape(bsz, 1, KV_HEADS, HEAD)], axis=1)
    return (x4.reshape(bsz, 1, D_MODEL), s_new[None], x[:, -1][None], k_new[None], v_new[None],
            xcat[:, 1:][None])


def _prepare_params(raw):
    prm = dict(raw)
    prm['r_k'] = raw['r_k'].reshape(raw['r_k'].shape[0], -1)
    prm['wg'] = _group_experts(raw['w_ex_gate'])
    prm['wu'] = _group_experts(raw['w_ex_up'])
    prm['wd'] = raw['w_ex_down'].astype(BF16).reshape(DEPTH, N_GROUPS, EXPERTS_PER_GROUP * D_FF, D_MODEL)
    return prm


def kernel(x_prompt, x_sample, state_rwkv, state_shift, cache_swa_k, cache_swa_v, state_pool, w_in, tshift_mu,
           decay_w0, decay_w2, iclr_a0, iclr_a2, gate_w2, k_k, k_a, r_k, lnx_g, lnx_b, attn_sinks, rel_bias, w_out,
           w_pool, pool_scale, ln_mix_g, ln_mix_b, ln_ffn_g, ln_ffn_b, w_router, b_router, w_ex_gate, w_ex_up,
           w_ex_down):
    prm = _prepare_params(dict(
        w_in=w_in, tshift_mu=tshift_mu, decay_w0=decay_w0, decay_w2=decay_w2, iclr_a0=iclr_a0, iclr_a2=iclr_a2,
        gate_w2=gate_w2, k_k=k_k, k_a=k_a, r_k=r_k, lnx_g=lnx_g, lnx_b=lnx_b, attn_sinks=attn_sinks,
        rel_bias=rel_bias, w_out=w_out, w_pool=w_pool, pool_scale=pool_scale, ln_mix_g=ln_mix_g, ln_mix_b=ln_mix_b,
        ln_ffn_g=ln_ffn_g, ln_ffn_b=ln_ffn_b, w_router=w_router, b_router=b_router, w_ex_gate=w_ex_gate,
        w_ex_up=w_ex_up, w_ex_down=w_ex_down))
    y_p, rwkv_p, shift_p, k_p, v_p, pool_p = _prompt_trunk(x_prompt, prm)
    y_s, rwkv_s, shift_s, k_s, v_s, pool_s = _sample_trunk(x_sample, state_rwkv, state_shift, cache_swa_k,
                                                            cache_swa_v, state_pool, prm)
    return (y_p, y_s, rwkv_p, rwkv_s, shift_p, shift_s, k_p, k_s, v_p, v_s, pool_p, pool_s)
```

```python
import functools
import math

import numpy as np
import jax
import jax.numpy as jnp
from jax import lax
from jax.experimental import pallas as pl
from jax.experimental.pallas import tpu as pltpu

F32 = jnp.float32
BF16 = jnp.bfloat16

D_MODEL = 1024
DEPTH = 2
PAST_LEN = 16384
D_A = 512
HEAD = 64
N_HEADS = 8
W_LORA, A_LORA, G_LORA = 32, 64, 96
RWKV_COLS = 3 * D_A + W_LORA + A_LORA + G_LORA
D_B = 512
KV_HEADS = 2
Q_PER_KV = 4
KV_COLS = KV_HEADS * HEAD
SWA_COLS = D_B + 2 * KV_COLS
WINDOW = 128
N_BUCKETS = 32
MAX_EXACT = 16
REL_MAX_DIST = 128
POOL_WINDOWS = (2, 4, 8, 16)
POOL_GROUP = 256
POOL_KEEP = 15
N_EXPERTS = 16
N_GROUPS = 4
EXPERTS_PER_GROUP = 4
D_FF = 256
ALPHA = (2.0 * DEPTH) ** 0.25
LN_EPS = 1e-5
LNX_EPS = 64e-5
NEG_INF = -1e30

LANES = 128
SUBLANES = 8
PACK = 16
PREP_ROWS = 128
EPILOGUE_ROWS = 256
VMEM_LIMIT_BYTES = 56 * 1024 * 1024

LORA_PAD = LANES
RWKV_PAD_COLS = 3 * D_A + 3 * LORA_PAD
CHUNK = 64

NN = ((1,), (0,))
NT = ((1,), (1,))
TN = ((0,), (0,))


def _dg(a, b, dims=NN):
    return lax.dot_general(a, b, (dims, ((), ())), preferred_element_type=F32)


def _bdot(a, b, dims=NN):
    return _dg(a.astype(BF16), b.astype(BF16), dims)


def _split(x, n):
    parts, rem = [], x
    for i in range(n):
        p = rem.astype(BF16)
        parts.append(p)
        if i + 1 < n:
            rem = rem - p.astype(F32)
    return parts


def _mm(a, b, dims=NN, passes=3):
    if passes == 1:
        return _bdot(a, b, dims)
    ah, al = _split(a, 2)
    bh, bl = _split(b, 2)
    return _dg(ah, bh, dims) + (_dg(ah, bl, dims) + _dg(al, bh, dims))


def _dot_exact_rhs(x, ones_bf16, n=3):
    out = None
    for p in _split(x, n):
        t = _dg(p, ones_bf16)
        out = t if out is None else out + t
    return out


def _dot_exact_lhs(ones_bf16, x, n=3):
    out = None
    for p in _split(x, n):
        t = _dg(ones_bf16, p)
        out = t if out is None else out + t
    return out


def _sigmoid(x):
    return 1.0 / (1.0 + jnp.exp(-x))


def _layer_norm(z, g, b):
    mu = jnp.mean(z, axis=-1, keepdims=True)
    d = z - mu
    var = jnp.mean(d * d, axis=-1, keepdims=True)
    return d * lax.rsqrt(var + LN_EPS) * g + b


def _params(sem):
    return pltpu.CompilerParams(dimension_semantics=sem, vmem_limit_bytes=VMEM_LIMIT_BYTES)


def _full(shape):
    nd = len(shape)
    return pl.BlockSpec(shape, lambda *_: (0,) * nd)


def _inproj_prep_body(*refs, tm, tiles_per_seq, chunk, has_prev):
    if has_prev:
        x_ref, xp_ref = refs[0], refs[1]
        refs = refs[2:]
    else:
        x_ref, xp_ref = refs[0], None
        refs = refs[1:]
    (wr_ref, ws_ref, mu_ref, w0_ref, ww2_ref, a0_ref, wa2_ref, wg2_ref, kk_ref, ka_ref, rk_ref,
     ones_ref, tri_ref,
     rt_ref, kt_ref, bt_ref, at_ref, v_ref, bonus_ref, g_ref, gl_ref, swa_ref, carry_ref) = refs

    i = pl.program_id(0)
    sub = tri_ref.shape[0]
    assert tm % sub == 0 and sub % chunk == 0
    blocks = [slice(r0, r0 + sub) for r0 in range(0, tm, sub)]
    st = [dict() for _ in blocks]
    ones_bd = ones_ref[...]
    half = D_A // 2

    if not has_prev:
        @pl.when(i == 0)
        def _():
            carry_ref[...] = jnp.zeros_like(carry_ref)

        first = (i % tiles_per_seq) == 0
        carried = jnp.where(first, 0.0, carry_ref[0:1, :])
        row = lax.broadcasted_iota(jnp.int32, (sub, 1), 0)

    def project(u):
        b_ = blocks[u]
        xb = x_ref[b_, :].astype(BF16)
        pr = _dg(xb, wr_ref[...])
        swa_ref[b_, :] = _dg(xb, ws_ref[...])
        if has_prev:
            shifted = _dg(xp_ref[b_, :].astype(BF16), wr_ref[...])
        else:
            before = carried if u == 0 else st[u - 1]['last']
            shifted = jnp.where(row == 0, before, pltpu.roll(pr, 1, axis=0))
            st[u]['last'] = pr[sub - 1:sub, :]
        st[u]['m'] = pr + mu_ref[...] * (shifted - pr)

    def low_rank(u):
        m = st[u].pop('m')
        s = st[u]
        s['r'], k, s['v'] = m[:, 0:D_A], m[:, D_A:2 * D_A], m[:, 2 * D_A:3 * D_A]
        wd, ad, gd = (m[:, 3 * D_A + j * LORA_PAD:3 * D_A + (j + 1) * LORA_PAD] for j in range(3))
        nz = -(w0_ref[...] + _bdot(jnp.tanh(wd), ww2_ref[...]))
        softplus = jnp.maximum(nz, 0.0) + jnp.log1p(jnp.exp(-jnp.abs(nz)))
        s['logw'] = -jnp.exp(-softplus - 0.5)
        s['a'] = _sigmoid(a0_ref[...] + _bdot(ad, wa2_ref[...]))
        s['g'] = _bdot(_sigmoid(gd), wg2_ref[...])
        s['kkr'] = k * kk_ref[...]
        s['k2'] = k * (1.0 + (s['a'] - 1.0) * ka_ref[...])

    def head_sums(z):
        return jnp.concatenate([_dot_exact_rhs(z[:, :half], ones_bd, 2), _dot_exact_rhs(z[:, half:], ones_bd, 2)],
                               axis=1)

    def sums(u):
        s = st[u]
        s['sq'] = head_sums(s['kkr'] * s['kkr'])
        s['rk'] = head_sums(s['r'] * s['k2'] * rk_ref[...])
        s['cum'] = _dot_exact_lhs(tri_ref[...], s['logw'], 2) if chunk > 1 else s['logw']

    def store(u):
        s, b_ = st[u], blocks[u]
        kk = s['kkr'] / jnp.maximum(jnp.sqrt(s['sq']), 1e-12)
        gam = jnp.exp(s['cum'])
        inv = jnp.exp(-s['cum'])
        bonus_ref[b_, :] = s['rk'] * s['v']
        rt_ref[b_, :] = s['r'] * gam
        kt_ref[b_, :] = s['k2'] * inv
        bt_ref[b_, :] = kk * s['a'] * inv
        at_ref[b_, :] = -kk * jnp.exp(s['cum'] - s['logw'])
        v_ref[b_, :] = s['v']
        g_ref[b_, :] = s['g']
        if chunk > 1:
            for c in range(sub // chunk):
                gl_ref[u * (sub // chunk) + c] = gam[(c + 1) * chunk - 1:(c + 1) * chunk, :]
        else:
            gl_ref[b_, :] = gam

    stages = (project, low_rank, sums, store)
    for t in range(len(blocks) + len(stages) - 1):
        for depth, stage in enumerate(stages):
            if 0 <= t - depth < len(blocks):
                stage(t - depth)
    if not has_prev:
        carry_ref[0:1, :] = st[-1]['last']


def _inproj_prep(x2d, x_prev, wts, *, seq_len, chunk, tm):
    n = x2d.shape[0]
    assert n % tm == 0 and (x_prev is not None or seq_len % tm == 0)
    assert tm % chunk == 0
    has_prev = x_prev is not None
    row = lambda w: pl.BlockSpec((tm, w), lambda i: (i, 0))
    ins = [x2d] + ([x_prev] if has_prev else [])
    in_specs = [row(D_MODEL)] + ([row(D_MODEL)] if has_prev else [])
    consts = [wts['wr'], wts['ws'], wts['mu'], wts['w0'], wts['ww2'], wts['a0'], wts['wa2'], wts['wg2'],
              wts['k_k'], wts['k_a'], wts['r_k'], _ones_block_diag(D_A // 2, HEAD), _tri_block_diag(min(tm, PREP_ROWS), chunk)]
    ins += consts
    in_specs += [_full(c.shape) for c in consts]
    if chunk > 1:
        gl_shape = jax.ShapeDtypeStruct((n // chunk, 1, D_A), F32)
        gl_spec = pl.BlockSpec((tm // chunk, 1, D_A), lambda i: (i, 0, 0))
    else:
        gl_shape = jax.ShapeDtypeStruct((n, D_A), F32)
        gl_spec = row(D_A)
    out_shape = [jax.ShapeDtypeStruct((n, D_A), F32)] * 7 + [gl_shape, jax.ShapeDtypeStruct((n, SWA_COLS), F32)]
    out_specs = [row(D_A)] * 7 + [gl_spec, row(SWA_COLS)]
    body = functools.partial(_inproj_prep_body, tm=tm, tiles_per_seq=max(seq_len // tm, 1), chunk=chunk,
                             has_prev=has_prev)
    return pl.pallas_call(
        body, grid=(n // tm,), in_specs=in_specs, out_specs=out_specs, out_shape=out_shape,
        scratch_shapes=[pltpu.VMEM((SUBLANES, RWKV_PAD_COLS), F32)],
        compiler_params=_params(("arbitrary",)), name="inproj_prep",
    )(*ins)


def _ones_block_diag(n, blk):
    idx = np.arange(n) // blk
    return jnp.asarray(idx[:, None] == idx[None, :], dtype=BF16)


def _tri_block_diag(n, blk):
    idx = np.arange(n)
    same = (idx[:, None] // blk) == (idx[None, :] // blk)
    return jnp.asarray(same & (idx[:, None] >= idx[None, :]), dtype=BF16)


PAIR = 2 * HEAD
N_PAIRS = N_HEADS // 2


def _scan_body(rt_ref, kt_ref, bt_ref, at_ref, v_ref, bonus_ref, g_ref, gl_ref, lng_ref, lnb_ref, s0_ref,
               y_ref, sout_ref, S_ref, *, C, nc, one_chunk_seqs):
    step = pl.program_id(1)

    def pair_state(ref, s, p):
        return jnp.concatenate([ref[s, 2 * p], ref[s, 2 * p + 1]], axis=1)

    if not one_chunk_seqs:
        @pl.when(step == 0)
        def _():
            for p in range(N_PAIRS):
                S_ref[p] = pair_state(s0_ref, 0, p)

    head_a = lax.broadcasted_iota(jnp.int32, (1, PAIR), 1) < HEAD
    rowc = lax.broadcasted_iota(jnp.int32, (C, 2 * C), 0)
    colc = lax.broadcasted_iota(jnp.int32, (C, 2 * C), 1)
    first_c = colc < C
    col_in = jnp.where(first_c, colc, colc - C)
    strict = rowc > col_in
    incl = rowc >= col_in
    eye_c = (rowc == col_in).astype(F32)
    r128 = lax.broadcasted_iota(jnp.int32, (PAIR, PAIR), 0)
    c128 = lax.broadcasted_iota(jnp.int32, (PAIR, PAIR), 1)
    same_head = (r128 < HEAD) == (c128 < HEAD)
    eye128 = (r128 == c128).astype(F32)

    def by_head(x):
        xb = x.astype(BF16)
        z = jnp.zeros_like(xb)
        return jnp.concatenate([jnp.where(head_a, xb, z), jnp.where(head_a, z, xb)], axis=0)

    def head_mean(z):
        za = jnp.sum(jnp.where(head_a, z, 0.0), axis=-1, keepdims=True)
        zb = jnp.sum(jnp.where(head_a, 0.0, z), axis=-1, keepdims=True)
        return jnp.where(head_a, za, zb) * (1.0 / HEAD)

    n_fold = int(math.log2(C)) - 1
    zero_blk = jnp.zeros((2 * C, PAIR), BF16)
    items = [(ci, p) for ci in range(nc) for p in range(N_PAIRS)]
    each = lambda f, *lists: [f(*xs) for xs in zip(*lists)]

    def load(ref):
        if one_chunk_seqs:
            pad = jnp.zeros((C - 1, PAIR), F32)
            return [jnp.concatenate([ref[ci:ci + 1, p * PAIR:(p + 1) * PAIR], pad], axis=0) for ci, p in items]
        return [ref[ci * C:(ci + 1) * C, p * PAIR:(p + 1) * PAIR] for ci, p in items]

    R, K, Bm, A, V = load(rt_ref), load(kt_ref), load(bt_ref), load(at_ref), load(v_ref)
    big = each(lambda a, r, b, k: _dg(jnp.concatenate([a, r], axis=0).astype(BF16),
                                      jnp.concatenate([by_head(b), by_head(k)], axis=0), NT), A, R, Bm, K)
    a_ab = [jnp.where(strict, x[:C, :2 * C], 0.0) for x in big]
    a_ak = [jnp.where(strict, x[:C, 2 * C:], 0.0) for x in big]
    l_rb = [jnp.where(incl, x[C:, :2 * C], 0.0) for x in big]
    l_rk = [jnp.where(incl, x[C:, 2 * C:], 0.0) for x in big]
    def blocks(x):
        xb = x.astype(BF16)
        z = jnp.zeros_like(xb)
        return jnp.concatenate([jnp.where(first_c, xb, z), jnp.where(first_c, z, xb)], axis=0)

    X = [_dg(x.astype(BF16), blocks(x)) for x in a_ab]
    T = [eye_c + x for x in a_ab]
    for k in range(1, n_fold + 1):
        if k < n_fold:
            res = each(lambda x, t: _dg(jnp.concatenate([x, t], axis=0).astype(BF16), blocks(x)), X, T)
            X = [r_[:C] for r_ in res]
            T = each(lambda t, r_: t + r_[C:], T, res)
        else:
            T = each(lambda t, x: t + _dg(t.astype(BF16), blocks(x)), T, X)
    akv = each(lambda m_, v_: _dg(m_.astype(BF16), by_head(v_)), a_ak, V)
    pq = each(lambda t, a, q: _dg(t.astype(BF16), jnp.concatenate([by_head(a), by_head(q)], axis=1)), T, A, akv)
    P = [x[:, :PAIR] for x in pq]
    Q = [x[:, PAIR:] for x in pq]
    wz = each(lambda lb, lk, p_, q_, v_: _dg(
        jnp.concatenate([lb, lk], axis=1).astype(BF16),
        jnp.concatenate([jnp.concatenate([by_head(p_), by_head(q_)], axis=1),
                         jnp.concatenate([zero_blk, by_head(v_)], axis=1)], axis=0)), l_rb, l_rk, P, Q, V)
    W = each(lambda r, x: r + x[:, :PAIR], R, wz)
    Z = [x[:, PAIR:] for x in wz]
    ptb = each(lambda p_, b: _bdot(p_, b, TN), P, Bm)
    gfull = each(lambda q_, v_, b, k: _bdot(jnp.concatenate([q_, v_], axis=0), jnp.concatenate([b, k], axis=0), TN),
                 Q, V, Bm, K)

    def put_state(s, p, val):
        sout_ref[s, 2 * p] = val[:, :HEAD]
        sout_ref[s, 2 * p + 1] = val[:, HEAD:]

    state = [None if one_chunk_seqs else S_ref[p] for p in range(N_PAIRS)]
    for idx, (ci, p) in enumerate(items):
        rows = slice(ci, ci + 1) if one_chunk_seqs else slice(ci * C, (ci + 1) * C)
        ln = slice(p * PAIR, (p + 1) * PAIR)
        g_c = gl_ref[ci, :, ln]
        M = (eye128 + jnp.where(same_head, ptb[idx], 0.0)) * g_c
        G = jnp.where(head_a, gfull[idx][:HEAD], gfull[idx][HEAD:]) * g_c
        S0 = pair_state(s0_ref, ci, p) if one_chunk_seqs else state[p]
        Y = _dg(W[idx].astype(BF16), by_head(S0), NT) + Z[idx]
        S1 = _bdot(S0, M) + G
        if one_chunk_seqs:
            put_state(ci, p, S1)
        else:
            state[p] = S1

        if one_chunk_seqs:
            Y = Y[0:1]
        mu = head_mean(Y)
        d = Y - mu
        var = head_mean(d * d)
        yn = d * lax.rsqrt(var + LNX_EPS) * lng_ref[:, ln] + lnb_ref[:, ln]
        y_ref[rows, ln] = ((yn + bonus_ref[rows, ln]) * g_ref[rows, ln]).astype(y_ref.dtype)

    if not one_chunk_seqs:
        for p in range(N_PAIRS):
            S_ref[p] = state[p]

        @pl.when(step == pl.num_programs(1) - 1)
        def _():
            for p in range(N_PAIRS):
                put_state(0, p, state[p])


def _rwkv_scan(rt, kt, bt, at, v, bonus, g, gl, lnx_g, lnx_b, s0, *, n_seq, seq_len, chunk, nc):
    one_chunk_seqs = seq_len == chunk
    if one_chunk_seqs:
        assert n_seq % nc == 0
        grid = (n_seq // nc, 1)
        blk = lambda b, c: b
        st = pl.BlockSpec((nc, N_HEADS, HEAD, HEAD), lambda b, c: (b, 0, 0, 0))
        rows, n_rows = nc, n_seq
    else:
        rows, n_rows = chunk * nc, n_seq * seq_len
        assert seq_len % rows == 0
        nsteps = seq_len // rows
        grid = (n_seq, nsteps)
        blk = lambda b, c: b * nsteps + c
        st = pl.BlockSpec((1, N_HEADS, HEAD, HEAD), lambda b, c: (b, 0, 0, 0))
    row = pl.BlockSpec((rows, D_A), lambda b, c: (blk(b, c), 0))
    vec = pl.BlockSpec((1, D_A), lambda b, c: (0, 0))
    return pl.pallas_call(
        functools.partial(_scan_body, C=chunk, nc=nc, one_chunk_seqs=one_chunk_seqs),
        grid=grid,
        in_specs=[row] * 7 + [pl.BlockSpec((nc, 1, D_A), lambda b, c: (blk(b, c), 0, 0)), vec, vec, st],
        out_specs=[row, st],
        out_shape=[jax.ShapeDtypeStruct((n_rows, D_A), BF16 if n_rows % PACK == 0 and rows % PACK == 0 else F32),
                   jax.ShapeDtypeStruct((n_seq, N_HEADS, HEAD, HEAD), F32)],
        scratch_shapes=[pltpu.VMEM((N_PAIRS, HEAD, PAIR), F32)],
        compiler_params=_params(("arbitrary", "arbitrary")), name="rwkv_scan",
    )(rt, kt, bt, at, v, bonus, g, gl, lnx_g, lnx_b, s0)


def _bucket_ranges():
    d = np.arange(WINDOW + 1)
    scaled = np.log(np.maximum(d, MAX_EXACT).astype(np.float32) / MAX_EXACT) / math.log(REL_MAX_DIST / MAX_EXACT)
    large = np.minimum(MAX_EXACT + (scaled * (N_BUCKETS - MAX_EXACT)).astype(np.int32), N_BUCKETS - 1)
    bucket = np.where(d < MAX_EXACT, d, large)
    frac = scaled.astype(np.float64) * (N_BUCKETS - MAX_EXACT)
    near = np.abs(frac - np.round(frac)) < 1e-3
    assert all(int(x) in (MAX_EXACT, REL_MAX_DIST) for x in d[(d >= MAX_EXACT) & near])
    ranges = []
    for b in range(N_BUCKETS):
        hit = d[bucket == b]
        ranges.append((int(hit.min()), int(hit.max())) if hit.size else None)
    return tuple(ranges)


def _swa_body(rb_ref, sink_ref, cur_ref, *rest, qr, fresh, per_step, ranges):
    prevs, (y_ref, bias_ref, sinkcol_ref) = rest[:-3], rest[-3:]
    W = WINDOW
    nk = W + qr

    @pl.when((pl.program_id(0) == 0) & (pl.program_id(1) == 0))
    def _():
        qi = lax.broadcasted_iota(jnp.int32, (qr, nk), 0)
        kj = lax.broadcasted_iota(jnp.int32, (qr, nk), 1)
        d = qi + W - kj
        valid = (d >= 0) & (d <= W)
        for hd in range(N_HEADS):
            t = jnp.zeros((qr, nk), F32)
            for bkt, rng in enumerate(ranges):
                if rng is not None:
                    t = jnp.where((d >= rng[0]) & (d <= rng[1]), rb_ref[bkt * N_HEADS + hd], t)
            kvh, grp = hd // Q_PER_KV, hd % Q_PER_KV
            rows = slice(grp * qr, (grp + 1) * qr)
            bias_ref[kvh, rows, :] = jnp.where(valid, t, NEG_INF)
            bias_ref[KV_HEADS + kvh, rows, :] = jnp.where(valid & (kj >= W), t, NEG_INF)
            sinkcol_ref[kvh, rows, :] = jnp.full((qr, 1), sink_ref[hd], F32)

    first_table = jnp.where(pl.program_id(1) == 0, KV_HEADS, 0) if fresh else 0
    items = [(sub, kvh) for sub in range(per_step) for kvh in range(KV_HEADS)]
    each = lambda f, *lists: [f(*xs) for xs in zip(*lists)]

    def own_rows(sub, col):
        if fresh:
            return cur_ref[sub * qr:(sub + 1) * qr, col:col + HEAD]
        return jnp.concatenate([cur_ref[sub:sub + 1, col:col + HEAD], jnp.zeros((qr - 1, HEAD), F32)], axis=0)

    def kv_rows(sub, kvh, is_v):
        col = D_B + is_v * KV_COLS + kvh * HEAD
        if not fresh:
            before = prevs[is_v][sub, :, kvh, :]
        elif sub > 0:
            before = cur_ref[(sub - 1) * qr:sub * qr, col:col + HEAD]
        else:
            before = prevs[0][:, col - D_B:col - D_B + HEAD]
        return jnp.concatenate([before, own_rows(sub, col)], axis=0).astype(BF16)

    kb = [kv_rows(sub, kvh, 0) for sub, kvh in items]
    vb = [kv_rows(sub, kvh, 1) for sub, kvh in items]
    q4 = [(jnp.concatenate([own_rows(sub, (kvh * Q_PER_KV + gq) * HEAD) for gq in range(Q_PER_KV)], axis=0)
           * (HEAD ** -0.5)).astype(BF16) for sub, kvh in items]
    s = [_dg(q, k, NT) + bias_ref[(first_table if sub == 0 else 0) + kvh] for q, k, (sub, kvh) in zip(q4, kb, items)]
    sink = [sinkcol_ref[kvh] for _, kvh in items]
    m = each(lambda s_, sk: jnp.maximum(jnp.max(s_, axis=-1, keepdims=True), sk), s, sink)
    p = each(lambda s_, m_: jnp.exp(s_ - m_).astype(BF16), s, m)
    pad = jnp.concatenate([jnp.zeros((nk, LANES - HEAD), BF16), jnp.ones((nk, LANES), BF16)], axis=1)
    ov = each(lambda p_, v_: _dg(p_, jnp.concatenate([v_, pad], axis=1)), p, vb)
    o = each(lambda ov_, sk, m_: ov_[:, :LANES] / (ov_[:, LANES:] + jnp.exp(sk - m_)), ov, sink, m)
    for sub in range(per_step):
        outs = [o[sub * KV_HEADS + kvh][gq * qr:(gq + 1) * qr, :HEAD]
                for kvh in range(KV_HEADS) for gq in range(Q_PER_KV)]
        y = jnp.concatenate(outs, axis=1).astype(y_ref.dtype)
        if fresh:
            y_ref[sub * qr:(sub + 1) * qr, :] = y
        else:
            y_ref[sub:sub + 1, :] = y[0:1]


def _swa(cur2d, prevs, rel_bias, sinks, *, n_seq, n_blk, qr, fresh, per_step=1):
    smem = pl.BlockSpec(memory_space=pltpu.SMEM)
    if fresh:
        assert qr == WINDOW and n_blk % per_step == 0
        blocks = n_blk
        n_blk //= per_step
        prev_specs = [pl.BlockSpec((WINDOW, 2 * KV_COLS),
                                   lambda b, j: (b * blocks + jnp.maximum(j * per_step - 1, 0), 2))]
    else:
        assert n_blk == 1 and n_seq % per_step == 0
        n_seq //= per_step
        prev_specs = [pl.BlockSpec((per_step, WINDOW, KV_HEADS, HEAD), lambda b, j: (b, 0, 0, 0))] * 2
    qrows = per_step * qr if fresh else per_step
    out_dtype = BF16 if fresh else F32
    return pl.pallas_call(
        functools.partial(_swa_body, qr=qr, fresh=fresh, per_step=per_step, ranges=_bucket_ranges()),
        grid=(n_seq, n_blk),
        in_specs=[smem, smem, pl.BlockSpec((qrows, SWA_COLS), lambda b, j: (b * n_blk + j, 0))] + prev_specs,
        out_specs=pl.BlockSpec((qrows, D_B), lambda b, j: (b * n_blk + j, 0)),
        out_shape=jax.ShapeDtypeStruct((n_seq * n_blk * qrows, D_B), out_dtype),
        scratch_shapes=[pltpu.VMEM((2 * KV_HEADS, Q_PER_KV * qr, WINDOW + qr), F32),
                        pltpu.VMEM((KV_HEADS, Q_PER_KV * qr, 1), F32)],
        compiler_params=_params(("arbitrary", "arbitrary")), name="swa",
    )(rel_bias.reshape(-1), sinks, cur2d, *prevs)


def _outproj_ln_body(ya_ref, yb_ref, x_ref, wa_ref, wb_ref, g_ref, b_ref, *rest):
    o_ref = rest[0] if len(rest) == 1 else rest[N_ROUTE_INPUTS]
    tm = x_ref.shape[0]
    step = min(tm, EPILOGUE_ROWS)
    logits = []
    for r0 in range(0, tm, step):
        rows = slice(r0, r0 + step)
        h = _dg(ya_ref[rows, :].astype(BF16), wa_ref[...]) + _dg(yb_ref[rows, :].astype(BF16), wb_ref[...])
        out = _layer_norm(ALPHA * x_ref[rows, :] + h, g_ref[...], b_ref[...])
        o_ref[rows, :] = out
        if len(rest) > 1:
            logits.append(_dg(out.astype(BF16), rest[0][...]))
    if len(rest) > 1:
        _route(logits, *rest[:N_ROUTE_INPUTS], *rest[N_ROUTE_INPUTS + 1:])


def _outproj_ln(ya, yb, x2d, w_out, ln_g, ln_b, *, tm, router=None):
    n = x2d.shape[0]
    half = pl.BlockSpec((tm, D_A), lambda i: (i, 0))
    full = pl.BlockSpec((tm, D_MODEL), lambda i: (i, 0))
    wa, wb = w_out[:D_A].astype(BF16), w_out[D_A:].astype(BF16)
    ins = [ya, yb, x2d, wa, wb, ln_g.reshape(1, -1), ln_b.reshape(1, -1)]
    in_specs = [half, half, full, _full(wa.shape), _full(wb.shape), _full((1, D_MODEL)), _full((1, D_MODEL))]
    out_specs, out_shape = [full], [jax.ShapeDtypeStruct((n, D_MODEL), F32)]
    if router is not None:
        r_ins, r_specs = _route_operands(*router, tm)
        ins, in_specs = ins + r_ins, in_specs + r_specs
        r_out_specs, r_out_shape = _route_outputs(n, tm)
        out_specs, out_shape = out_specs + r_out_specs, out_shape + r_out_shape
    res = pl.pallas_call(
        _outproj_ln_body, grid=(n // tm,), in_specs=in_specs, out_specs=out_specs, out_shape=out_shape,
        compiler_params=_params(("arbitrary",)), name="outproj_ln",
    )(*ins)
    return res[0] if router is None else res


def _route(x, wr_ref, br_ref, tri_ref, meta_ref, slot_ref, cnt_ref):
    logits_tm = jnp.concatenate(x, axis=0) if isinstance(x, list) else _dg(x.astype(BF16), wr_ref[...])
    x = logits_tm
    logits = jnp.transpose(logits_tm)[:N_EXPERTS] + br_ref[...]
    z = jnp.exp(logits - jnp.max(logits, axis=0, keepdims=True))
    probs = z / jnp.sum(z, axis=0, keepdims=True)
    pa = [probs[a_ * N_GROUPS:(a_ + 1) * N_GROUPS] for a_ in range(EXPERTS_PER_GROUP)]

    sel = []
    for a_ in range(EXPERTS_PER_GROUP):
        rank = None
        for j in range(EXPERTS_PER_GROUP):
            if j == a_:
                continue
            ahead = ((pa[j] >= pa[a_]) if j < a_ else (pa[j] > pa[a_])).astype(jnp.int32)
            rank = ahead if rank is None else rank + ahead
        sel.append(rank < 2)
    score = None
    for a_ in range(EXPERTS_PER_GROUP):
        t = jnp.where(sel[a_], pa[a_], 0.0)
        score = t if score is None else score + t
    srow = [score[gi:gi + 1] for gi in range(N_GROUPS)]
    best = []
    for gi in range(N_GROUPS):
        ok = None
        for j in range(N_GROUPS):
            if j == gi:
                continue
            c_ = (srow[gi] > srow[j]) if j < gi else (srow[gi] >= srow[j])
            ok = c_ if ok is None else (ok & c_)
        best.append(ok.astype(F32))
    tm = x.shape[0]
    best4 = jnp.concatenate(best, axis=0)
    chosen = best4 > 0.5
    kept = [jnp.where(sel[a_] & chosen, pa[a_], 0.0) for a_ in range(EXPERTS_PER_GROUP)]
    tot = None
    for a_ in range(EXPERTS_PER_GROUP):
        tot = kept[a_] if tot is None else tot + kept[a_]
    tot = jnp.sum(tot, axis=0, keepdims=True)
    gates = [jnp.where(sel[a_] & chosen, pa[a_] / tot, 0.0) for a_ in range(EXPERTS_PER_GROUP)]

    onehot = jnp.concatenate([best4, jnp.zeros((SUBLANES - N_GROUPS, tm), F32)], axis=0)
    oh_b = jnp.concatenate([onehot, jnp.zeros((LANES - SUBLANES, tm), F32)], axis=0).astype(BF16)
    incl = _dg(onehot.astype(BF16), tri_ref[...])
    slot = None
    offset = jnp.zeros((1, 1), F32)
    for gi in range(N_GROUPS):
        t = onehot[gi:gi + 1] * (offset + incl[gi:gi + 1] - 1.0)
        slot = t if slot is None else slot + t
        offset = offset + incl[gi:gi + 1, tm - 1:tm]
    slot_ref[0] = slot.astype(jnp.int32)
    cnt_ref[0] = _dg(jnp.ones((SUBLANES, tm), BF16), oh_b, NT)[0:1]
    rows = jnp.concatenate(gates + [slot, jnp.zeros((LANES - N_EXPERTS - 1, tm), F32)], axis=0)
    meta_ref[...] = jnp.transpose(rows)


META_SLOT = N_EXPERTS
N_ROUTE_INPUTS = 3


def _gate_lane(group, j):
    return j * N_GROUPS + group


def _route_operands(w_router, b_router, tm):
    idx = np.arange(tm)
    tri = jnp.asarray(idx[:, None] <= idx[None, :], dtype=BF16)
    by_member = lambda a: a.reshape(N_GROUPS, EXPERTS_PER_GROUP, -1).transpose(1, 0, 2).reshape(N_EXPERTS, -1)
    wr = jnp.pad(by_member(w_router.T).T, ((0, 0), (0, LANES - N_EXPERTS))).astype(BF16)
    ins = [wr, by_member(b_router), tri]
    return ins, [_full(a.shape) for a in ins]


def _route_outputs(n, tm):
    nt = n // tm
    specs = [pl.BlockSpec((tm, LANES), lambda i: (i, 0)), pl.BlockSpec((1, 1, tm), lambda i: (i, 0, 0)),
             pl.BlockSpec((1, 1, LANES), lambda i: (i, 0, 0))]
    shapes = [jax.ShapeDtypeStruct((n, LANES), F32), jax.ShapeDtypeStruct((nt, 1, tm), jnp.int32),
              jax.ShapeDtypeStruct((nt, 1, LANES), F32)]
    return specs, shapes


def _router_body(x_ref, *refs):
    _route(x_ref[...], *refs)


def _router(x2d, w_router, b_router, *, tm):
    n = x2d.shape[0]
    ins, in_specs = _route_operands(w_router, b_router, tm)
    out_specs, out_shape = _route_outputs(n, tm)
    return pl.pallas_call(
        _router_body, grid=(n // tm,),
        in_specs=[pl.BlockSpec((tm, D_MODEL), lambda i: (i, 0))] + in_specs,
        out_specs=out_specs, out_shape=out_shape,
        compiler_params=_params(("arbitrary",)), name="router",
    )(x2d, *ins)


def _moe_ln_body(x_ref, gates_ref, wg_ref, wu_ref, wd_ref, g_ref, b_ref, o_ref, acc_ref, xb_ref):
    gi = pl.program_id(1)

    @pl.when(gi == 0)
    def _():
        xb_ref[...] = x_ref[...].astype(BF16)
        acc_ref[...] = jnp.zeros_like(acc_ref)

    xb = xb_ref[...]
    lane = lax.broadcasted_iota(jnp.int32, gates_ref.shape, 1)
    hs = []
    for j in range(EXPERTS_PER_GROUP):
        hg = _dg(xb, wg_ref[0, j])
        hu = _dg(xb, wu_ref[0, j])
        gate = jnp.sum(jnp.where(lane == _gate_lane(gi, j), gates_ref[...], 0.0), axis=1, keepdims=True)
        hs.append((hg * _sigmoid(hg) * hu * gate).astype(BF16))
    acc_ref[...] += _dg(jnp.concatenate(hs, axis=1), wd_ref[0])

    @pl.when(gi == N_GROUPS - 1)
    def _():
        o_ref[...] = _layer_norm(ALPHA * x_ref[...] + acc_ref[...], g_ref[...], b_ref[...])


def _moe_ln(x2d, gates, wg, wu, wd, ln_g, ln_b, *, layer, tm):
    n = x2d.shape[0]
    tok = pl.BlockSpec((tm, D_MODEL), lambda i, e: (i, 0))
    wspec = pl.BlockSpec((None, 1, EXPERTS_PER_GROUP, D_MODEL, D_FF), lambda i, e: (layer, e, 0, 0, 0))
    return pl.pallas_call(
        _moe_ln_body, grid=(n // tm, N_GROUPS),
        in_specs=[tok, pl.BlockSpec((tm, LANES), lambda i, e: (i, 0)), wspec, wspec,
                  pl.BlockSpec((None, 1, EXPERTS_PER_GROUP * D_FF, D_MODEL), lambda i, e: (layer, e, 0, 0)),
                  pl.BlockSpec((1, D_MODEL), lambda i, e: (0, 0)), pl.BlockSpec((1, D_MODEL), lambda i, e: (0, 0))],
        out_specs=tok, out_shape=jax.ShapeDtypeStruct((n, D_MODEL), F32),
        scratch_shapes=[pltpu.VMEM((tm, D_MODEL), F32), pltpu.VMEM((tm, D_MODEL), BF16)],
        compiler_params=_params(("arbitrary", "arbitrary")), name="moe_ln",
    )(x2d, gates, wg, wu, wd, ln_g.reshape(1, -1), ln_b.reshape(1, -1))


UNSORT_ROWS = EPILOGUE_ROWS


def _moe_sorted_body(start_ref, nwin_ref, x_ref, meta_ref, slot_ref, wg_ref, wu_ref, wd_ref, g_ref, b_ref, o_ref,
                     xs_ref, gs_ref, ys_ref, unperm_ref, *, tm, wn):
    i = pl.program_id(0)

    @pl.when(i == 0)
    def _():
        xs_ref[tm:, :] = jnp.zeros((wn, D_MODEL), BF16)
        gs_ref[tm:, :] = jnp.zeros((wn, LANES), F32)

    x = x_ref[...]
    meta = meta_ref[...]
    slot_iota = lax.broadcasted_iota(jnp.int32, (tm, tm), 0)
    perm = jnp.where(slot_iota == slot_ref[0], 1.0, 0.0).astype(BF16)
    m_hi, m_lo = _split(meta, 2)
    moved = _dg(perm, jnp.concatenate([x.astype(BF16), m_hi, m_lo], axis=1))
    xs_ref[:tm, :] = moved[:, :D_MODEL].astype(BF16)
    gs_ref[:tm, :] = moved[:, D_MODEL:D_MODEL + LANES] + moved[:, D_MODEL + LANES:]
    ys_ref[...] = jnp.zeros_like(ys_ref)
    tok_slot = meta[:, META_SLOT:META_SLOT + 1]
    lane_slot = lax.broadcasted_iota(jnp.int32, (tm, tm), 1).astype(F32)
    unperm_ref[...] = jnp.where(tok_slot == lane_slot, 1.0, 0.0).astype(BF16)

    for gi in range(N_GROUPS):
        first = start_ref[i * N_GROUPS + gi]

        def window(w, carry, gi=gi, first=first):
            st = pl.multiple_of(first + w * wn, PACK)
            xw = xs_ref[pl.ds(st, wn), :]
            gw = gs_ref[pl.ds(st, wn), :]
            hs = []
            for j in range(EXPERTS_PER_GROUP):
                e = _gate_lane(gi, j)
                hg = _dg(xw, wg_ref[gi, j])
                hu = _dg(xw, wu_ref[gi, j])
                hs.append((hg * _sigmoid(hg) * hu * gw[:, e:e + 1]).astype(BF16))
            ys_ref[pl.ds(st, wn), :] += _dg(jnp.concatenate(hs, axis=1), wd_ref[gi])
            return carry

        lax.fori_loop(0, nwin_ref[i * N_GROUPS + gi], window, 0)

    ys = ys_ref[:tm, :].astype(BF16)
    for r0 in range(0, tm, UNSORT_ROWS):
        rows = slice(r0, r0 + UNSORT_ROWS)
        y = _dg(unperm_ref[rows, :], ys)
        o_ref[rows, :] = _layer_norm(ALPHA * x_ref[rows, :] + y, g_ref[...], b_ref[...])


def _moe_sorted_ln(x2d, meta, slot_rows, counts, wg, wu, wd, ln_g, ln_b, *, layer, tm, wn):
    n = x2d.shape[0]
    nt = n // tm
    assert wn % PACK == 0 and tm % PACK == 0
    cnt = counts[:, 0, :N_GROUPS].astype(jnp.int32)
    offs = jnp.cumsum(cnt, axis=1) - cnt
    first = (offs // PACK) * PACK
    nwin = jnp.where(cnt > 0, (offs - first + cnt + wn - 1) // wn, 0)
    tok = lambda w: pl.BlockSpec((tm, w), lambda i, *_: (i, 0))
    resident = lambda a: pl.BlockSpec((None,) + a.shape[1:], lambda i, *_: (layer,) + (0,) * (a.ndim - 1),
                                      pipeline_mode=pl.Buffered(1))
    vec = pl.BlockSpec((1, D_MODEL), lambda i, *_: (0, 0))
    grid_spec = pltpu.PrefetchScalarGridSpec(
        num_scalar_prefetch=2, grid=(nt,),
        in_specs=[tok(D_MODEL), tok(LANES), pl.BlockSpec((1, 1, tm), lambda i, *_: (i, 0, 0)),
                  resident(wg), resident(wu), resident(wd), vec, vec],
        out_specs=tok(D_MODEL),
        scratch_shapes=[pltpu.VMEM((tm + wn, D_MODEL), BF16), pltpu.VMEM((tm + wn, LANES), F32),
                        pltpu.VMEM((tm + wn, D_MODEL), F32), pltpu.VMEM((tm, tm), BF16)])
    return pl.pallas_call(
        functools.partial(_moe_sorted_body, tm=tm, wn=wn), grid_spec=grid_spec,
        out_shape=jax.ShapeDtypeStruct((n, D_MODEL), F32),
        compiler_params=_params(("arbitrary",)), name="moe_sorted_ln",
    )(first.reshape(-1), nwin.reshape(-1), x2d, meta, slot_rows, wg, wu, wd, ln_g.reshape(1, -1), ln_b.reshape(1, -1))


HALO = 16


def _pool_ln_body(x_ref, halo_ref, wp_ref, ps_ref, g_ref, b_ref, *rest, tm, seq_len, start_pos):
    o_ref = rest[0] if len(rest) == 1 else rest[N_ROUTE_INPUTS]
    i = pl.program_id(0)
    nrow = min(tm, EPILOGUE_ROWS)
    logits = []
    for r0 in range(0, tm, nrow):
        rows = slice(r0, r0 + nrow)
        x = x_ref[rows, :]
        before = halo_ref if r0 == 0 else x_ref.at[r0 - HALO:r0, :]
        idx = lax.broadcasted_iota(jnp.int32, (nrow + HALO, 1), 0)
        pos_e = (i * tm + r0 - HALO + idx) & (seq_len - 1)
        cnt_pos = (pos_e[HALO:] + (start_pos + 1)).astype(F32)
        mixed = []
        for gi, w in enumerate(POOL_WINDOWS):
            cols = slice(gi * POOL_GROUP, (gi + 1) * POOL_GROUP)
            xg = x[:, cols]
            s = jnp.concatenate([before[:, cols], xg], axis=0)
            step = 1
            while step < w:
                s = s + jnp.where(pos_e >= step, pltpu.roll(s, step, axis=0), 0.0)
                step *= 2
            pooled = s[HALO:] / jnp.minimum(float(w), cnt_pos) - xg
            mixed.append(_bdot(pooled, wp_ref[gi]))
        h = jnp.concatenate(mixed, axis=1) * ps_ref[...]
        out = _layer_norm(ALPHA * x + h, g_ref[...], b_ref[...])
        o_ref[rows, :] = out
        if len(rest) > 1:
            logits.append(_dg(out.astype(BF16), rest[0][...]))
    if len(rest) > 1:
        _route(logits, *rest[:N_ROUTE_INPUTS], *rest[N_ROUTE_INPUTS + 1:])


def _pool_ln(x2d, w_pool, pool_scale, ln_g, ln_b, *, tm, seq_len, start_pos, router=None):
    n = x2d.shape[0]
    assert seq_len & (seq_len - 1) == 0 and seq_len >= HALO and tm % HALO == 0
    tok = pl.BlockSpec((tm, D_MODEL), lambda i: (i, 0))
    ins = [x2d, x2d, w_pool.astype(BF16), pool_scale.reshape(1, -1), ln_g.reshape(1, -1), ln_b.reshape(1, -1)]
    in_specs = [tok, pl.BlockSpec((HALO, D_MODEL), lambda i: (jnp.maximum(i * (tm // HALO) - 1, 0), 0)),
                _full(w_pool.shape), _full((1, D_MODEL)), _full((1, D_MODEL)), _full((1, D_MODEL))]
    out_specs, out_shape = [tok], [jax.ShapeDtypeStruct((n, D_MODEL), F32)]
    if router is not None:
        r_ins, r_specs = _route_operands(*router, tm)
        ins, in_specs = ins + r_ins, in_specs + r_specs
        r_out_specs, r_out_shape = _route_outputs(n, tm)
        out_specs, out_shape = out_specs + r_out_specs, out_shape + r_out_shape
    res = pl.pallas_call(
        functools.partial(_pool_ln_body, tm=tm, seq_len=seq_len, start_pos=start_pos), grid=(n // tm,),
        in_specs=in_specs, out_specs=out_specs, out_shape=out_shape,
        compiler_params=_params(("arbitrary",)), name="pool_ln",
    )(*ins)
    return res[0] if router is None else res


def _pad_lora_cols(w, width):
    return jnp.pad(w, ((0, 0), (0, LORA_PAD - width)))


def _even_layer_weights(prm, i):
    w_in = prm['w_in'][i]
    o = 3 * D_A
    cols = lambda w: jnp.concatenate(
        [w[:, :o], _pad_lora_cols(w[:, o:o + W_LORA], W_LORA),
         _pad_lora_cols(w[:, o + W_LORA:o + W_LORA + A_LORA], A_LORA),
         _pad_lora_cols(w[:, o + W_LORA + A_LORA:RWKV_COLS], G_LORA)], axis=1)
    pad_rows = lambda w: jnp.pad(w, ((0, LORA_PAD - w.shape[0]), (0, 0))).astype(BF16)
    vec = lambda a: a.reshape(1, -1)
    return dict(
        wr=cols(w_in[:, :RWKV_COLS]).astype(BF16), ws=w_in[:, RWKV_COLS:].astype(BF16),
        mu=cols(prm['tshift_mu'][i].reshape(1, -1)),
        w0=vec(prm['decay_w0'][i]), ww2=pad_rows(prm['decay_w2'][i]),
        a0=vec(prm['iclr_a0'][i]), wa2=pad_rows(prm['iclr_a2'][i]), wg2=pad_rows(prm['gate_w2'][i]),
        k_k=vec(prm['k_k'][i]), k_a=vec(prm['k_a'][i]), r_k=vec(prm['r_k'][i]),
    )


def _group_experts(w):
    return w.astype(BF16).reshape(DEPTH, N_GROUPS, EXPERTS_PER_GROUP, D_MODEL, D_FF)


MOE_SORT_TILE = 512
MOE_WINDOW = 160


def _moe_weights(prm, layer):
    return (prm['wg'], prm['wu'], prm['wd'], prm['ln_ffn_g'][layer], prm['ln_ffn_b'][layer])


def _moe_block(x2d, prm, layer, *, tm):
    meta, _, _ = _router(x2d, prm['w_router'], prm['b_router'], tm=min(tm, 512))
    return _moe_ln(x2d, meta, *_moe_weights(prm, layer), layer=layer, tm=tm)


PROMPT_CHUNKS_PER_STEP = 8
PROMPT_ATTN_BLOCKS_PER_STEP = 8
SAMPLE_CHUNK = 16
SAMPLE_QROWS = SUBLANES
SAMPLE_SEQS_PER_STEP = 8


def _prompt_trunk(x, prm):
    bsz, t, _ = x.shape
    n = bsz * t
    x0 = x.reshape(n, D_MODEL)
    wts = _even_layer_weights(prm, 0)
    rt, kt, bt, at, v, bonus, g, gl, swa = _inproj_prep(x0, None, wts, seq_len=t, chunk=CHUNK, tm=512)
    s0 = jnp.zeros((bsz, N_HEADS, HEAD, HEAD), F32)
    ya, s_new = _rwkv_scan(rt, kt, bt, at, v, bonus, g, gl, prm['lnx_g'][0].reshape(1, -1),
                           prm['lnx_b'][0].reshape(1, -1), s0, n_seq=bsz, seq_len=t, chunk=CHUNK,
                           nc=min(PROMPT_CHUNKS_PER_STEP, t // CHUNK))
    n_blk = t // WINDOW
    qb = min(PROMPT_ATTN_BLOCKS_PER_STEP, n_blk)
    yb = _swa(swa, (swa,), prm['rel_bias'], prm['attn_sinks'][0], n_seq=bsz, n_blk=n_blk, qr=WINDOW, fresh=True,
              per_step=qb)
    tm = MOE_SORT_TILE
    assert n % tm == 0
    router = (prm['w_router'], prm['b_router'])
    x1, *routing = _outproj_ln(ya, yb, x0, prm['w_out'][0], prm['ln_mix_g'][0], prm['ln_mix_b'][0], tm=tm,
                               router=router)
    x2 = _moe_sorted_ln(x1, *routing, *_moe_weights(prm, 0), layer=0, tm=tm, wn=MOE_WINDOW)
    x3, *routing = _pool_ln(x2, prm['w_pool'][0], prm['pool_scale'][0], prm['ln_mix_g'][1], prm['ln_mix_b'][1],
                            tm=tm, seq_len=t, start_pos=0, router=router)
    x4 = _moe_sorted_ln(x3, *routing, *_moe_weights(prm, 1), layer=1, tm=tm, wn=MOE_WINDOW)
    swa3 = swa.reshape(bsz, t, SWA_COLS)
    k_new = swa3[:, t - WINDOW:, D_B:D_B + KV_COLS].reshape(bsz, WINDOW, KV_HEADS, HEAD)
    v_new = swa3[:, t - WINDOW:, D_B + KV_COLS:].reshape(bsz, WINDOW, KV_HEADS, HEAD)
    pool_new = x2.reshape(bsz, t, D_MODEL)[:, t - POOL_KEEP:]
    return (x4.reshape(bsz, t, D_MODEL), s_new[None], x[:, -1][None], k_new[None], v_new[None], pool_new[None])


def _sample_trunk(x, st_rwkv, st_shift, c_k, c_v, st_pool, prm):
    bsz = x.shape[0]
    x0 = x.reshape(bsz, D_MODEL)
    wts = _even_layer_weights(prm, 0)
    outs = _inproj_prep(x0, st_shift[0], wts, seq_len=1, chunk=1, tm=bsz)
    gl, swa = outs[7], outs[8]
    ya, s_new = _rwkv_scan(*outs[:7], gl.reshape(bsz, 1, D_A), prm['lnx_g'][0].reshape(1, -1),
                           prm['lnx_b'][0].reshape(1, -1), st_rwkv[0], n_seq=bsz, seq_len=SAMPLE_CHUNK,
                           chunk=SAMPLE_CHUNK, nc=SAMPLE_SEQS_PER_STEP)
    yb = _swa(swa, (c_k[0], c_v[0]), prm['rel_bias'], prm['attn_sinks'][0], n_seq=bsz, n_blk=1,
              qr=SAMPLE_QROWS, fresh=False, per_step=SAMPLE_SEQS_PER_STEP)
    x1 = _outproj_ln(ya, yb, x0, prm['w_out'][0], prm['ln_mix_g'][0], prm['ln_mix_b'][0], tm=bsz)
    x2 = _moe_block(x1, prm, 0, tm=bsz)
    xcat = jnp.concatenate([st_pool[0], x2[:, None]], axis=1)
    x3 = _pool_ln(xcat.reshape(bsz * HALO, D_MODEL), prm['w_pool'][0], prm['pool_scale'][0],
                  prm['ln_mix_g'][1], prm['ln_mix_b'][1], tm=min(512, bsz * HALO), seq_len=HALO,
                  start_pos=PAST_LEN - POOL_KEEP)
    x3 = x3.reshape(bsz, HALO, D_MODEL)[:, -1]
    x4 = _moe_block(x3, prm, 1, tm=bsz)
    k_new = jnp.concatenate([c_k[0][:, 1:], swa[:, D_B:D_B + KV_COLS].reshape(bsz, 1, KV_HEADS, HEAD)], axis=1)
    v_new = jnp.concatenate([c_v[0][:, 1:], swa[:, D_B + KV_COLS:].reshape(bsz, 1, KV_HEADS, HEAD)], axis=1)
    return (x4.reshape(bsz, 1, D_MODEL), s_new[None], x[:, -1][None], k_new[None], v_new[None],
            xcat[:, 1:][None])


def _prepare_params(raw):
    prm = dict(raw)
    prm['r_k'] = raw['r_k'].reshape(raw['r_k'].shape[0], -1)
    prm['wg'] = _group_experts(raw['w_ex_gate'])
    prm['wu'] = _group_experts(raw['w_ex_up'])
    prm['wd'] = raw['w_ex_down'].astype(BF16).reshape(DEPTH, N_GROUPS, EXPERTS_PER_GROUP * D_FF, D_MODEL)
    return prm


def kernel(x_prompt, x_sample, state_rwkv, state_shift, cache_swa_k, cache_swa_v, state_pool, w_in, tshift_mu,
           decay_w0, decay_w2, iclr_a0, iclr_a2, gate_w2, k_k, k_a, r_k, lnx_g, lnx_b, attn_sinks, rel_bias, w_out,
           w_pool, pool_scale, ln_mix_g, ln_mix_b, ln_ffn_g, ln_ffn_b, w_router, b_router, w_ex_gate, w_ex_up,
           w_ex_down):
    prm = _prepare_params(dict(
        w_in=w_in, tshift_mu=tshift_mu, decay_w0=decay_w0, decay_w2=decay_w2, iclr_a0=iclr_a0, iclr_a2=iclr_a2,
        gate_w2=gate_w2, k_k=k_k, k_a=k_a, r_k=r_k, lnx_g=lnx_g, lnx_b=lnx_b, attn_sinks=attn_sinks,
        rel_bias=rel_bias, w_out=w_out, w_pool=w_pool, pool_scale=pool_scale, ln_mix_g=ln_mix_g, ln_mix_b=ln_mix_b,
        ln_ffn_g=ln_ffn_g, ln_ffn_b=ln_ffn_b, w_router=w_router, b_router=b_router, w_ex_gate=w_ex_gate,
        w_ex_up=w_ex_up, w_ex_down=w_ex_down))
    y_p, rwkv_p, shift_p, k_p, v_p, pool_p = _prompt_trunk(x_prompt, prm)
    y_s, rwkv_s, shift_s, k_s, v_s, pool_s = _sample_trunk(x_sample, state_rwkv, state_shift, cache_swa_k,
                                                            cache_swa_v, state_pool, prm)
    return (y_p, y_s, rwkv_p, rwkv_s, shift_p, shift_s, k_p, k_s, v_p, v_s, pool_p, pool_s)
```

```python
import functools
import math

import numpy as np
import jax
import jax.numpy as jnp
from jax import lax
from jax.experimental import pallas as pl
from jax.experimental.pallas import tpu as pltpu

F32 = jnp.float32
BF16 = jnp.bfloat16

D_MODEL = 1024
DEPTH = 2
PAST_LEN = 16384
D_A = 512
HEAD = 64
N_HEADS = 8
W_LORA, A_LORA, G_LORA = 32, 64, 96
RWKV_COLS = 3 * D_A + W_LORA + A_LORA + G_LORA
D_B = 512
KV_HEADS = 2
Q_PER_KV = 4
KV_COLS = KV_HEADS * HEAD
SWA_COLS = D_B + 2 * KV_COLS
WINDOW = 128
N_BUCKETS = 32
MAX_EXACT = 16
REL_MAX_DIST = 128
POOL_WINDOWS = (2, 4, 8, 16)
POOL_GROUP = 256
POOL_KEEP = 15
N_EXPERTS = 16
N_GROUPS = 4
EXPERTS_PER_GROUP = 4
D_FF = 256
ALPHA = (2.0 * DEPTH) ** 0.25
LN_EPS = 1e-5
LNX_EPS = 64e-5
NEG_INF = -1e30

LANES = 128
SUBLANES = 8
PACK = 16
PREP_ROWS = 128
EPILOGUE_ROWS = 256
VMEM_LIMIT_BYTES = 56 * 1024 * 1024

LORA_PAD = LANES
RWKV_PAD_COLS = 3 * D_A + 3 * LORA_PAD
CHUNK = 64

NN = ((1,), (0,))
NT = ((1,), (1,))
TN = ((0,), (0,))


def _dg(a, b, dims=NN):
    return lax.dot_general(a, b, (dims, ((), ())), preferred_element_type=F32)


def _bdot(a, b, dims=NN):
    return _dg(a.astype(BF16), b.astype(BF16), dims)


def _split(x, n):
    parts, rem = [], x
    for i in range(n):
        p = rem.astype(BF16)
        parts.append(p)
        if i + 1 < n:
            rem = rem - p.astype(F32)
    return parts


def _mm(a, b, dims=NN, passes=3):
    if passes == 1:
        return _bdot(a, b, dims)
    ah, al = _split(a, 2)
    bh, bl = _split(b, 2)
    return _dg(ah, bh, dims) + (_dg(ah, bl, dims) + _dg(al, bh, dims))


def _dot_exact_rhs(x, ones_bf16, n=3):
    out = None
    for p in _split(x, n):
        t = _dg(p, ones_bf16)
        out = t if out is None else out + t
    return out


def _dot_exact_lhs(ones_bf16, x, n=3):
    out = None
    for p in _split(x, n):
        t = _dg(ones_bf16, p)
        out = t if out is None else out + t
    return out


def _sigmoid(x):
    return 1.0 / (1.0 + jnp.exp(-x))


def _layer_norm(z, g, b):
    mu = jnp.mean(z, axis=-1, keepdims=True)
    d = z - mu
    var = jnp.mean(d * d, axis=-1, keepdims=True)
    return d * lax.rsqrt(var + LN_EPS) * g + b


def _params(sem):
    return pltpu.CompilerParams(dimension_semantics=sem, vmem_limit_bytes=VMEM_LIMIT_BYTES)


def _full(shape):
    nd = len(shape)
    return pl.BlockSpec(shape, lambda *_: (0,) * nd)


def _inproj_prep_body(*refs, tm, tiles_per_seq, chunk, has_prev):
    if has_prev:
        x_ref, xp_ref = refs[0], refs[1]
        refs = refs[2:]
    else:
        x_ref, xp_ref = refs[0], None
        refs = refs[1:]
    (wr_ref, ws_ref, mu_ref, w0_ref, ww2_ref, a0_ref, wa2_ref, wg2_ref, kk_ref, ka_ref, rk_ref,
     ones_ref, tri_ref,
     rt_ref, kt_ref, bt_ref, at_ref, v_ref, bonus_ref, g_ref, gl_ref, swa_ref, carry_ref) = refs

    i = pl.program_id(0)
    sub = tri_ref.shape[0]
    assert tm % sub == 0 and sub % chunk == 0
    blocks = [slice(r0, r0 + sub) for r0 in range(0, tm, sub)]
    st = [dict() for _ in blocks]
    ones_bd = ones_ref[...]
    half = D_A // 2

    if not has_prev:
        @pl.when(i == 0)
        def _():
            carry_ref[...] = jnp.zeros_like(carry_ref)

        first = (i % tiles_per_seq) == 0
        carried = jnp.where(first, 0.0, carry_ref[0:1, :])
        row = lax.broadcasted_iota(jnp.int32, (sub, 1), 0)

    def project(u):
        b_ = blocks[u]
        xb = x_ref[b_, :].astype(BF16)
        pr = _dg(xb, wr_ref[...])
        swa_ref[b_, :] = _dg(xb, ws_ref[...])
        if has_prev:
            shifted = _dg(xp_ref[b_, :].astype(BF16), wr_ref[...])
        else:
            before = carried if u == 0 else st[u - 1]['last']
            shifted = jnp.where(row == 0, before, pltpu.roll(pr, 1, axis=0))
            st[u]['last'] = pr[sub - 1:sub, :]
        st[u]['m'] = pr + mu_ref[...] * (shifted - pr)

    def low_rank(u):
        m = st[u].pop('m')
        s = st[u]
        s['r'], k, s['v'] = m[:, 0:D_A], m[:, D_A:2 * D_A], m[:, 2 * D_A:3 * D_A]
        wd, ad, gd = (m[:, 3 * D_A + j * LORA_PAD:3 * D_A + (j + 1) * LORA_PAD] for j in range(3))
        nz = -(w0_ref[...] + _bdot(jnp.tanh(wd), ww2_ref[...]))
        softplus = jnp.maximum(nz, 0.0) + jnp.log1p(jnp.exp(-jnp.abs(nz)))
        s['logw'] = -jnp.exp(-softplus - 0.5)
        s['a'] = _sigmoid(a0_ref[...] + _bdot(ad, wa2_ref[...]))
        s['g'] = _bdot(_sigmoid(gd), wg2_ref[...])
        s['kkr'] = k * kk_ref[...]
        s['k2'] = k * (1.0 + (s['a'] - 1.0) * ka_ref[...])

    def head_sums(z):
        return jnp.concatenate([_dot_exact_rhs(z[:, :half], ones_bd, 2), _dot_exact_rhs(z[:, half:], ones_bd, 2)],
                               axis=1)

    def sums(u):
        s = st[u]
        s['sq'] = head_sums(s['kkr'] * s['kkr'])
        s['rk'] = head_sums(s['r'] * s['k2'] * rk_ref[...])
        s['cum'] = _dot_exact_lhs(tri_ref[...], s['logw'], 2) if chunk > 1 else s['logw']

    def store(u):
        s, b_ = st[u], blocks[u]
        kk = s['kkr'] / jnp.maximum(jnp.sqrt(s['sq']), 1e-12)
        gam = jnp.exp(s['cum'])
        inv = jnp.exp(-s['cum'])
        bonus_ref[b_, :] = s['rk'] * s['v']
        rt_ref[b_, :] = s['r'] * gam
        kt_ref[b_, :] = s['k2'] * inv
        bt_ref[b_, :] = kk * s['a'] * inv
        at_ref[b_, :] = -kk * jnp.exp(s['cum'] - s['logw'])
        v_ref[b_, :] = s['v']
        g_ref[b_, :] = s['g']
        if chunk > 1:
            for c in range(sub // chunk):
                gl_ref[u * (sub // chunk) + c] = gam[(c + 1) * chunk - 1:(c + 1) * chunk, :]
        else:
            gl_ref[b_, :] = gam

    stages = (project, low_rank, sums, store)
    for t in range(len(blocks) + len(stages) - 1):
        for depth, stage in enumerate(stages):
            if 0 <= t - depth < len(blocks):
                stage(t - depth)
    if not has_prev:
        carry_ref[0:1, :] = st[-1]['last']


def _inproj_prep(x2d, x_prev, wts, *, seq_len, chunk, tm):
    n = x2d.shape[0]
    assert n % tm == 0 and (x_prev is not None or seq_len % tm == 0)
    assert tm % chunk == 0
    has_prev = x_prev is not None
    row = lambda w: pl.BlockSpec((tm, w), lambda i: (i, 0))
    ins = [x2d] + ([x_prev] if has_prev else [])
    in_specs = [row(D_MODEL)] + ([row(D_MODEL)] if has_prev else [])
    consts = [wts['wr'], wts['ws'], wts['mu'], wts['w0'], wts['ww2'], wts['a0'], wts['wa2'], wts['wg2'],
              wts['k_k'], wts['k_a'], wts['r_k'], _ones_block_diag(D_A // 2, HEAD), _tri_block_diag(min(tm, PREP_ROWS), chunk)]
    ins += consts
    in_specs += [_full(c.shape) for c in consts]
    if chunk > 1:
        gl_shape = jax.ShapeDtypeStruct((n // chunk, 1, D_A), F32)
        gl_spec = pl.BlockSpec((tm // chunk, 1, D_A), lambda i: (i, 0, 0))
    else:
        gl_shape = jax.ShapeDtypeStruct((n, D_A), F32)
        gl_spec = row(D_A)
    out_shape = [jax.ShapeDtypeStruct((n, D_A), F32)] * 7 + [gl_shape, jax.ShapeDtypeStruct((n, SWA_COLS), F32)]
    out_specs = [row(D_A)] * 7 + [gl_spec, row(SWA_COLS)]
    body = functools.partial(_inproj_prep_body, tm=tm, tiles_per_seq=max(seq_len // tm, 1), chunk=chunk,
                             has_prev=has_prev)
    return pl.pallas_call(
        body, grid=(n // tm,), in_specs=in_specs, out_specs=out_specs, out_shape=out_shape,
        scratch_shapes=[pltpu.VMEM((SUBLANES, RWKV_PAD_COLS), F32)],
        compiler_params=_params(("arbitrary",)), name="inproj_prep",
    )(*ins)


def _ones_block_diag(n, blk):
    idx = np.arange(n) // blk
    return jnp.asarray(idx[:, None] == idx[None, :], dtype=BF16)


def _tri_block_diag(n, blk):
    idx = np.arange(n)
    same = (idx[:, None] // blk) == (idx[None, :] // blk)
    return jnp.asarray(same & (idx[:, None] >= idx[None, :]), dtype=BF16)


PAIR = 2 * HEAD
N_PAIRS = N_HEADS // 2


def _scan_body(rt_ref, kt_ref, bt_ref, at_ref, v_ref, bonus_ref, g_ref, gl_ref, lng_ref, lnb_ref, s0_ref,
               y_ref, sout_ref, S_ref, *, C, nc, one_chunk_seqs):
    step = pl.program_id(1)

    def pair_state(ref, s, p):
        return jnp.concatenate([ref[s, 2 * p], ref[s, 2 * p + 1]], axis=1)

    if not one_chunk_seqs:
        @pl.when(step == 0)
        def _():
            for p in range(N_PAIRS):
                S_ref[p] = pair_state(s0_ref, 0, p)

    head_a = lax.broadcasted_iota(jnp.int32, (1, PAIR), 1) < HEAD
    rowc = lax.broadcasted_iota(jnp.int32, (C, 2 * C), 0)
    colc = lax.broadcasted_iota(jnp.int32, (C, 2 * C), 1)
    first_c = colc < C
    col_in = jnp.where(first_c, colc, colc - C)
    strict = rowc > col_in
    incl = rowc >= col_in
    eye_c = (rowc == col_in).astype(F32)
    r128 = lax.broadcasted_iota(jnp.int32, (PAIR, PAIR), 0)
    c128 = lax.broadcasted_iota(jnp.int32, (PAIR, PAIR), 1)
    same_head = (r128 < HEAD) == (c128 < HEAD)
    eye128 = (r128 == c128).astype(F32)

    def by_head(x):
        xb = x.astype(BF16)
        z = jnp.zeros_like(xb)
        return jnp.concatenate([jnp.where(head_a, xb, z), jnp.where(head_a, z, xb)], axis=0)

    def head_mean(z):
        za = jnp.sum(jnp.where(head_a, z, 0.0), axis=-1, keepdims=True)
        zb = jnp.sum(jnp.where(head_a, 0.0, z), axis=-1, keepdims=True)
        return jnp.where(head_a, za, zb) * (1.0 / HEAD)

    n_fold = int(math.log2(C)) - 1
    zero_blk = jnp.zeros((2 * C, PAIR), BF16)
    items = [(ci, p) for ci in range(nc) for p in range(N_PAIRS)]

    def load(ref, ci, p):
        if one_chunk_seqs:
            pad = jnp.zeros((C - 1, PAIR), F32)
            return jnp.concatenate([ref[ci:ci + 1, p * PAIR:(p + 1) * PAIR], pad], axis=0)
        return ref[ci * C:(ci + 1) * C, p * PAIR:(p + 1) * PAIR]

    def blocks(x):
        xb = x.astype(BF16)
        z = jnp.zeros_like(xb)
        return jnp.concatenate([jnp.where(first_c, xb, z), jnp.where(first_c, z, xb)], axis=0)

    st = [dict() for _ in items]

    def products(n):
        ci, p = items[n]
        s = st[n]
        R, K, Bm, A, V = (load(ref, ci, p) for ref in (rt_ref, kt_ref, bt_ref, at_ref, v_ref))
        big = _dg(jnp.concatenate([A, R], axis=0).astype(BF16),
                  jnp.concatenate([by_head(Bm), by_head(K)], axis=0), NT)
        s.update(R=R, K=K, Bm=Bm, A=A, V=V,
                 a_ab=jnp.where(strict, big[:C, :2 * C], 0.0), a_ak=jnp.where(strict, big[:C, 2 * C:], 0.0),
                 l_rb=jnp.where(incl, big[C:, :2 * C], 0.0), l_rk=jnp.where(incl, big[C:, 2 * C:], 0.0))

    def square(n):
        s = st[n]
        a_ab = s.pop('a_ab')
        s['X'] = _dg(a_ab.astype(BF16), blocks(a_ab))
        s['T'] = eye_c + a_ab
        s['akv'] = _dg(s.pop('a_ak').astype(BF16), by_head(s['V']))

    def fold(n):
        s = st[n]
        res = _dg(jnp.concatenate([s['X'], s['T']], axis=0).astype(BF16), blocks(s['X']))
        s['X'], s['T'] = res[:C], s['T'] + res[C:]

    def last_fold(n):
        s = st[n]
        s['T'] = s['T'] + _dg(s['T'].astype(BF16), blocks(s.pop('X')))

    def solve(n):
        s = st[n]
        pq = _dg(s.pop('T').astype(BF16), jnp.concatenate([by_head(s['A']), by_head(s.pop('akv'))], axis=1))
        s['P'], s['Q'] = pq[:, :PAIR], pq[:, PAIR:]

    def readout(n):
        s = st[n]
        P, Q, V = s.pop('P'), s.pop('Q'), s.pop('V')
        wz = _dg(jnp.concatenate([s.pop('l_rb'), s.pop('l_rk')], axis=1).astype(BF16),
                 jnp.concatenate([jnp.concatenate([by_head(P), by_head(Q)], axis=1),
                                  jnp.concatenate([zero_blk, by_head(V)], axis=1)], axis=0))
        s['W'], s['Z'] = s.pop('R') + wz[:, :PAIR], wz[:, PAIR:]
        Bm, K = s.pop('Bm'), s.pop('K')
        s['ptb'] = _bdot(P, Bm, TN)
        s['gfull'] = _bdot(jnp.concatenate([Q, V], axis=0), jnp.concatenate([Bm, K], axis=0), TN)
        del s['A']

    def put_state(s, p, val):
        sout_ref[s, 2 * p] = val[:, :HEAD]
        sout_ref[s, 2 * p + 1] = val[:, HEAD:]

    state = [None if one_chunk_seqs else S_ref[p] for p in range(N_PAIRS)]

    def advance(n):
        ci, p = items[n]
        s = st[n]
        rows = slice(ci, ci + 1) if one_chunk_seqs else slice(ci * C, (ci + 1) * C)
        ln = slice(p * PAIR, (p + 1) * PAIR)
        g_c = gl_ref[ci, :, ln]
        gfull = s.pop('gfull')
        M = (eye128 + jnp.where(same_head, s.pop('ptb'), 0.0)) * g_c
        G = jnp.where(head_a, gfull[:HEAD], gfull[HEAD:]) * g_c
        S0 = pair_state(s0_ref, ci, p) if one_chunk_seqs else state[p]
        Y = _dg(s.pop('W').astype(BF16), by_head(S0), NT) + s.pop('Z')
        S1 = _bdot(S0, M) + G
        if one_chunk_seqs:
            put_state(ci, p, S1)
        else:
            state[p] = S1

        if one_chunk_seqs:
            Y = Y[0:1]
        mu = head_mean(Y)
        d = Y - mu
        var = head_mean(d * d)
        yn = d * lax.rsqrt(var + LNX_EPS) * lng_ref[:, ln] + lnb_ref[:, ln]
        y_ref[rows, ln] = ((yn + bonus_ref[rows, ln]) * g_ref[rows, ln]).astype(y_ref.dtype)

    for stage in (products, square) + (fold,) * (n_fold - 1) + (last_fold, solve, readout, advance):
        for n in range(len(items)):
            stage(n)

    if not one_chunk_seqs:
        for p in range(N_PAIRS):
            S_ref[p] = state[p]

        @pl.when(step == pl.num_programs(1) - 1)
        def _():
            for p in range(N_PAIRS):
                put_state(0, p, state[p])


def _rwkv_scan(rt, kt, bt, at, v, bonus, g, gl, lnx_g, lnx_b, s0, *, n_seq, seq_len, chunk, nc):
    one_chunk_seqs = seq_len == chunk
    if one_chunk_seqs:
        assert n_seq % nc == 0
        grid = (n_seq // nc, 1)
        blk = lambda b, c: b
        st = pl.BlockSpec((nc, N_HEADS, HEAD, HEAD), lambda b, c: (b, 0, 0, 0))
        rows, n_rows = nc, n_seq
    else:
        rows, n_rows = chunk * nc, n_seq * seq_len
        assert seq_len % rows == 0
        nsteps = seq_len // rows
        grid = (n_seq, nsteps)
        blk = lambda b, c: b * nsteps + c
        st = pl.BlockSpec((1, N_HEADS, HEAD, HEAD), lambda b, c: (b, 0, 0, 0))
    row = pl.BlockSpec((rows, D_A), lambda b, c: (blk(b, c), 0))
    vec = pl.BlockSpec((1, D_A), lambda b, c: (0, 0))
    return pl.pallas_call(
        functools.partial(_scan_body, C=chunk, nc=nc, one_chunk_seqs=one_chunk_seqs),
        grid=grid,
        in_specs=[row] * 7 + [pl.BlockSpec((nc, 1, D_A), lambda b, c: (blk(b, c), 0, 0)), vec, vec, st],
        out_specs=[row, st],
        out_shape=[jax.ShapeDtypeStruct((n_rows, D_A), BF16 if n_rows % PACK == 0 and rows % PACK == 0 else F32),
                   jax.ShapeDtypeStruct((n_seq, N_HEADS, HEAD, HEAD), F32)],
        scratch_shapes=[pltpu.VMEM((N_PAIRS, HEAD, PAIR), F32)],
        compiler_params=_params(("arbitrary", "arbitrary")), name="rwkv_scan",
    )(rt, kt, bt, at, v, bonus, g, gl, lnx_g, lnx_b, s0)


def _bucket_ranges():
    d = np.arange(WINDOW + 1)
    scaled = np.log(np.maximum(d, MAX_EXACT).astype(np.float32) / MAX_EXACT) / math.log(REL_MAX_DIST / MAX_EXACT)
    large = np.minimum(MAX_EXACT + (scaled * (N_BUCKETS - MAX_EXACT)).astype(np.int32), N_BUCKETS - 1)
    bucket = np.where(d < MAX_EXACT, d, large)
    frac = scaled.astype(np.float64) * (N_BUCKETS - MAX_EXACT)
    near = np.abs(frac - np.round(frac)) < 1e-3
    assert all(int(x) in (MAX_EXACT, REL_MAX_DIST) for x in d[(d >= MAX_EXACT) & near])
    ranges = []
    for b in range(N_BUCKETS):
        hit = d[bucket == b]
        ranges.append((int(hit.min()), int(hit.max())) if hit.size else None)
    return tuple(ranges)


def _swa_body(rb_ref, sink_ref, cur_ref, *rest, qr, fresh, per_step, ranges):
    prevs, (y_ref, bias_ref, sinkcol_ref) = rest[:-3], rest[-3:]
    W = WINDOW
    nk = W + qr

    @pl.when((pl.program_id(0) == 0) & (pl.program_id(1) == 0))
    def _():
        qi = lax.broadcasted_iota(jnp.int32, (qr, nk), 0)
        kj = lax.broadcasted_iota(jnp.int32, (qr, nk), 1)
        d = qi + W - kj
        valid = (d >= 0) & (d <= W)
        for hd in range(N_HEADS):
            t = jnp.zeros((qr, nk), F32)
            for bkt, rng in enumerate(ranges):
                if rng is not None:
                    t = jnp.where((d >= rng[0]) & (d <= rng[1]), rb_ref[bkt * N_HEADS + hd], t)
            kvh, grp = hd // Q_PER_KV, hd % Q_PER_KV
            rows = slice(grp * qr, (grp + 1) * qr)
            bias_ref[kvh, rows, :] = jnp.where(valid, t, NEG_INF)
            bias_ref[KV_HEADS + kvh, rows, :] = jnp.where(valid & (kj >= W), t, NEG_INF)
            sinkcol_ref[kvh, rows, :] = jnp.full((qr, 1), sink_ref[hd], F32)

    first_table = jnp.where(pl.program_id(1) == 0, KV_HEADS, 0) if fresh else 0
    items = [(sub, kvh) for sub in range(per_step) for kvh in range(KV_HEADS)]
    each = lambda f, *lists: [f(*xs) for xs in zip(*lists)]

    def own_rows(sub, col):
        if fresh:
            return cur_ref[sub * qr:(sub + 1) * qr, col:col + HEAD]
        return jnp.concatenate([cur_ref[sub:sub + 1, col:col + HEAD], jnp.zeros((qr - 1, HEAD), F32)], axis=0)

    def kv_rows(sub, kvh, is_v):
        col = D_B + is_v * KV_COLS + kvh * HEAD
        if not fresh:
            before = prevs[is_v][sub, :, kvh, :]
        elif sub > 0:
            before = cur_ref[(sub - 1) * qr:sub * qr, col:col + HEAD]
        else:
            before = prevs[0][:, col - D_B:col - D_B + HEAD]
        return jnp.concatenate([before, own_rows(sub, col)], axis=0).astype(BF16)

    kb = [kv_rows(sub, kvh, 0) for sub, kvh in items]
    vb = [kv_rows(sub, kvh, 1) for sub, kvh in items]
    q4 = [(jnp.concatenate([own_rows(sub, (kvh * Q_PER_KV + gq) * HEAD) for gq in range(Q_PER_KV)], axis=0)
           * (HEAD ** -0.5)).astype(BF16) for sub, kvh in items]
    s = [_dg(q, k, NT) + bias_ref[(first_table if sub == 0 else 0) + kvh] for q, k, (sub, kvh) in zip(q4, kb, items)]
    sink = [sinkcol_ref[kvh] for _, kvh in items]
    m = each(lambda s_, sk: jnp.maximum(jnp.max(s_, axis=-1, keepdims=True), sk), s, sink)
    p = each(lambda s_, m_: jnp.exp(s_ - m_).astype(BF16), s, m)
    pad = jnp.concatenate([jnp.zeros((nk, LANES - HEAD), BF16), jnp.ones((nk, LANES), BF16)], axis=1)
    ov = each(lambda p_, v_: _dg(p_, jnp.concatenate([v_, pad], axis=1)), p, vb)
    o = each(lambda ov_, sk, m_: ov_[:, :LANES] / (ov_[:, LANES:] + jnp.exp(sk - m_)), ov, sink, m)
    for sub in range(per_step):
        outs = [o[sub * KV_HEADS + kvh][gq * qr:(gq + 1) * qr, :HEAD]
                for kvh in range(KV_HEADS) for gq in range(Q_PER_KV)]
        y = jnp.concatenate(outs, axis=1).astype(y_ref.dtype)
        if fresh:
            y_ref[sub * qr:(sub + 1) * qr, :] = y
        else:
            y_ref[sub:sub + 1, :] = y[0:1]


def _swa(cur2d, prevs, rel_bias, sinks, *, n_seq, n_blk, qr, fresh, per_step=1):
    smem = pl.BlockSpec(memory_space=pltpu.SMEM)
    if fresh:
        assert qr == WINDOW and n_blk % per_step == 0
        blocks = n_blk
        n_blk //= per_step
        prev_specs = [pl.BlockSpec((WINDOW, 2 * KV_COLS),
                                   lambda b, j: (b * blocks + jnp.maximum(j * per_step - 1, 0), 2))]
    else:
        assert n_blk == 1 and n_seq % per_step == 0
        n_seq //= per_step
        prev_specs = [pl.BlockSpec((per_step, WINDOW, KV_HEADS, HEAD), lambda b, j: (b, 0, 0, 0))] * 2
    qrows = per_step * qr if fresh else per_step
    out_dtype = BF16 if fresh else F32
    return pl.pallas_call(
        functools.partial(_swa_body, qr=qr, fresh=fresh, per_step=per_step, ranges=_bucket_ranges()),
        grid=(n_seq, n_blk),
        in_specs=[smem, smem, pl.BlockSpec((qrows, SWA_COLS), lambda b, j: (b * n_blk + j, 0))] + prev_specs,
        out_specs=pl.BlockSpec((qrows, D_B), lambda b, j: (b * n_blk + j, 0)),
        out_shape=jax.ShapeDtypeStruct((n_seq * n_blk * qrows, D_B), out_dtype),
        scratch_shapes=[pltpu.VMEM((2 * KV_HEADS, Q_PER_KV * qr, WINDOW + qr), F32),
                        pltpu.VMEM((KV_HEADS, Q_PER_KV * qr, 1), F32)],
        compiler_params=_params(("arbitrary", "arbitrary")), name="swa",
    )(rel_bias.reshape(-1), sinks, cur2d, *prevs)


def _outproj_ln_body(ya_ref, yb_ref, x_ref, wa_ref, wb_ref, g_ref, b_ref, *rest):
    o_ref = rest[0] if len(rest) == 1 else rest[N_ROUTE_INPUTS]
    tm = x_ref.shape[0]
    step = min(tm, EPILOGUE_ROWS)
    logits = []
    for r0 in range(0, tm, step):
        rows = slice(r0, r0 + step)
        h = _dg(ya_ref[rows, :].astype(BF16), wa_ref[...]) + _dg(yb_ref[rows, :].astype(BF16), wb_ref[...])
        out = _layer_norm(ALPHA * x_ref[rows, :] + h, g_ref[...], b_ref[...])
        o_ref[rows, :] = out
        if len(rest) > 1:
            logits.append(_dg(out.astype(BF16), rest[0][...]))
    if len(rest) > 1:
        _route(logits, *rest[:N_ROUTE_INPUTS], *rest[N_ROUTE_INPUTS + 1:])


def _outproj_ln(ya, yb, x2d, w_out, ln_g, ln_b, *, tm, router=None):
    n = x2d.shape[0]
    half = pl.BlockSpec((tm, D_A), lambda i: (i, 0))
    full = pl.BlockSpec((tm, D_MODEL), lambda i: (i, 0))
    wa, wb = w_out[:D_A].astype(BF16), w_out[D_A:].astype(BF16)
    ins = [ya, yb, x2d, wa, wb, ln_g.reshape(1, -1), ln_b.reshape(1, -1)]
    in_specs = [half, half, full, _full(wa.shape), _full(wb.shape), _full((1, D_MODEL)), _full((1, D_MODEL))]
    out_specs, out_shape = [full], [jax.ShapeDtypeStruct((n, D_MODEL), F32)]
    if router is not None:
        r_ins, r_specs = _route_operands(*router, tm)
        ins, in_specs = ins + r_ins, in_specs + r_specs
        r_out_specs, r_out_shape = _route_outputs(n, tm)
        out_specs, out_shape = out_specs + r_out_specs, out_shape + r_out_shape
    res = pl.pallas_call(
        _outproj_ln_body, grid=(n // tm,), in_specs=in_specs, out_specs=out_specs, out_shape=out_shape,
        compiler_params=_params(("arbitrary",)), name="outproj_ln",
    )(*ins)
    return res[0] if router is None else res


def _route(x, wr_ref, br_ref, tri_ref, meta_ref, slot_ref, cnt_ref):
    logits_tm = jnp.concatenate(x, axis=0) if isinstance(x, list) else _dg(x.astype(BF16), wr_ref[...])
    x = logits_tm
    logits = jnp.transpose(logits_tm)[:N_EXPERTS] + br_ref[...]
    z = jnp.exp(logits - jnp.max(logits, axis=0, keepdims=True))
    probs = z / jnp.sum(z, axis=0, keepdims=True)
    pa = [probs[a_ * N_GROUPS:(a_ + 1) * N_GROUPS] for a_ in range(EXPERTS_PER_GROUP)]

    sel = []
    for a_ in range(EXPERTS_PER_GROUP):
        rank = None
        for j in range(EXPERTS_PER_GROUP):
            if j == a_:
                continue
            ahead = ((pa[j] >= pa[a_]) if j < a_ else (pa[j] > pa[a_])).astype(jnp.int32)
            rank = ahead if rank is None else rank + ahead
        sel.append(rank < 2)
    score = None
    for a_ in range(EXPERTS_PER_GROUP):
        t = jnp.where(sel[a_], pa[a_], 0.0)
        score = t if score is None else score + t
    srow = [score[gi:gi + 1] for gi in range(N_GROUPS)]
    best = []
    for gi in range(N_GROUPS):
        ok = None
        for j in range(N_GROUPS):
            if j == gi:
                continue
            c_ = (srow[gi] > srow[j]) if j < gi else (srow[gi] >= srow[j])
            ok = c_ if ok is None else (ok & c_)
        best.append(ok.astype(F32))
    tm = x.shape[0]
    best4 = jnp.concatenate(best, axis=0)
    chosen = best4 > 0.5
    kept = [jnp.where(sel[a_] & chosen, pa[a_], 0.0) for a_ in range(EXPERTS_PER_GROUP)]
    tot = None
    for a_ in range(EXPERTS_PER_GROUP):
        tot = kept[a_] if tot is None else tot + kept[a_]
    tot = jnp.sum(tot, axis=0, keepdims=True)
    gates = [jnp.where(sel[a_] & chosen, pa[a_] / tot, 0.0) for a_ in range(EXPERTS_PER_GROUP)]

    onehot = jnp.concatenate([best4, jnp.zeros((SUBLANES - N_GROUPS, tm), F32)], axis=0)
    oh_b = jnp.concatenate([onehot, jnp.zeros((LANES - SUBLANES, tm), F32)], axis=0).astype(BF16)
    incl = _dg(onehot.astype(BF16), tri_ref[...])
    slot = None
    offset = jnp.zeros((1, 1), F32)
    for gi in range(N_GROUPS):
        t = onehot[gi:gi + 1] * (offset + incl[gi:gi + 1] - 1.0)
        slot = t if slot is None else slot + t
        offset = offset + incl[gi:gi + 1, tm - 1:tm]
    slot_ref[0] = slot.astype(jnp.int32)
    cnt_ref[0] = _dg(jnp.ones((SUBLANES, tm), BF16), oh_b, NT)[0:1]
    rows = jnp.concatenate(gates + [slot, jnp.zeros((LANES - N_EXPERTS - 1, tm), F32)], axis=0)
    meta_ref[...] = jnp.transpose(rows)


META_SLOT = N_EXPERTS
N_ROUTE_INPUTS = 3


def _gate_lane(group, j):
    return j * N_GROUPS + group


def _route_operands(w_router, b_router, tm):
    idx = np.arange(tm)
    tri = jnp.asarray(idx[:, None] <= idx[None, :], dtype=BF16)
    by_member = lambda a: a.reshape(N_GROUPS, EXPERTS_PER_GROUP, -1).transpose(1, 0, 2).reshape(N_EXPERTS, -1)
    wr = jnp.pad(by_member(w_router.T).T, ((0, 0), (0, LANES - N_EXPERTS))).astype(BF16)
    ins = [wr, by_member(b_router), tri]
    return ins, [_full(a.shape) for a in ins]


def _route_outputs(n, tm):
    nt = n // tm
    specs = [pl.BlockSpec((tm, LANES), lambda i: (i, 0)), pl.BlockSpec((1, 1, tm), lambda i: (i, 0, 0)),
             pl.BlockSpec((1, 1, LANES), lambda i: (i, 0, 0))]
    shapes = [jax.ShapeDtypeStruct((n, LANES), F32), jax.ShapeDtypeStruct((nt, 1, tm), jnp.int32),
              jax.ShapeDtypeStruct((nt, 1, LANES), F32)]
    return specs, shapes


def _router_body(x_ref, *refs):
    _route(x_ref[...], *refs)


def _router(x2d, w_router, b_router, *, tm):
    n = x2d.shape[0]
    ins, in_specs = _route_operands(w_router, b_router, tm)
    out_specs, out_shape = _route_outputs(n, tm)
    return pl.pallas_call(
        _router_body, grid=(n // tm,),
        in_specs=[pl.BlockSpec((tm, D_MODEL), lambda i: (i, 0))] + in_specs,
        out_specs=out_specs, out_shape=out_shape,
        compiler_params=_params(("arbitrary",)), name="router",
    )(x2d, *ins)


def _moe_ln_body(x_ref, gates_ref, wg_ref, wu_ref, wd_ref, g_ref, b_ref, o_ref, acc_ref, xb_ref):
    gi = pl.program_id(1)

    @pl.when(gi == 0)
    def _():
        xb_ref[...] = x_ref[...].astype(BF16)
        acc_ref[...] = jnp.zeros_like(acc_ref)

    xb = xb_ref[...]
    lane = lax.broadcasted_iota(jnp.int32, gates_ref.shape, 1)
    hs = []
    for j in range(EXPERTS_PER_GROUP):
        hg = _dg(xb, wg_ref[0, j])
        hu = _dg(xb, wu_ref[0, j])
        gate = jnp.sum(jnp.where(lane == _gate_lane(gi, j), gates_ref[...], 0.0), axis=1, keepdims=True)
        hs.append((hg * _sigmoid(hg) * hu * gate).astype(BF16))
    acc_ref[...] += _dg(jnp.concatenate(hs, axis=1), wd_ref[0])

    @pl.when(gi == N_GROUPS - 1)
    def _():
        o_ref[...] = _layer_norm(ALPHA * x_ref[...] + acc_ref[...], g_ref[...], b_ref[...])


def _moe_ln(x2d, gates, wg, wu, wd, ln_g, ln_b, *, layer, tm):
    n = x2d.shape[0]
    tok = pl.BlockSpec((tm, D_MODEL), lambda i, e: (i, 0))
    wspec = pl.BlockSpec((None, 1, EXPERTS_PER_GROUP, D_MODEL, D_FF), lambda i, e: (layer, e, 0, 0, 0))
    return pl.pallas_call(
        _moe_ln_body, grid=(n // tm, N_GROUPS),
        in_specs=[tok, pl.BlockSpec((tm, LANES), lambda i, e: (i, 0)), wspec, wspec,
                  pl.BlockSpec((None, 1, EXPERTS_PER_GROUP * D_FF, D_MODEL), lambda i, e: (layer, e, 0, 0)),
                  pl.BlockSpec((1, D_MODEL), lambda i, e: (0, 0)), pl.BlockSpec((1, D_MODEL), lambda i, e: (0, 0))],
        out_specs=tok, out_shape=jax.ShapeDtypeStruct((n, D_MODEL), F32),
        scratch_shapes=[pltpu.VMEM((tm, D_MODEL), F32), pltpu.VMEM((tm, D_MODEL), BF16)],
        compiler_params=_params(("arbitrary", "arbitrary")), name="moe_ln",
    )(x2d, gates, wg, wu, wd, ln_g.reshape(1, -1), ln_b.reshape(1, -1))


UNSORT_ROWS = EPILOGUE_ROWS


def _moe_sorted_body(start_ref, nwin_ref, x_ref, meta_ref, slot_ref, wg_ref, wu_ref, wd_ref, g_ref, b_ref, o_ref,
                     xs_ref, gs_ref, ys_ref, unperm_ref, *, tm, wn):
    i = pl.program_id(0)

    @pl.when(i == 0)
    def _():
        xs_ref[tm:, :] = jnp.zeros((wn, D_MODEL), BF16)
        gs_ref[tm:, :] = jnp.zeros((wn, LANES), F32)

    x = x_ref[...]
    meta = meta_ref[...]
    slot_iota = lax.broadcasted_iota(jnp.int32, (tm, tm), 0)
    perm = jnp.where(slot_iota == slot_ref[0], 1.0, 0.0).astype(BF16)
    m_hi, m_lo = _split(meta, 2)
    moved = _dg(perm, jnp.concatenate([x.astype(BF16), m_hi, m_lo], axis=1))
    xs_ref[:tm, :] = moved[:, :D_MODEL].astype(BF16)
    gs_ref[:tm, :] = moved[:, D_MODEL:D_MODEL + LANES] + moved[:, D_MODEL + LANES:]
    ys_ref[...] = jnp.zeros_like(ys_ref)
    tok_slot = meta[:, META_SLOT:META_SLOT + 1]
    lane_slot = lax.broadcasted_iota(jnp.int32, (tm, tm), 1).astype(F32)
    unperm_ref[...] = jnp.where(tok_slot == lane_slot, 1.0, 0.0).astype(BF16)

    for gi in range(N_GROUPS):
        first = start_ref[i * N_GROUPS + gi]

        def window(w, carry, gi=gi, first=first):
            st = pl.multiple_of(first + w * wn, PACK)
            xw = xs_ref[pl.ds(st, wn), :]
            gw = gs_ref[pl.ds(st, wn), :]
            hs = []
            for j in range(EXPERTS_PER_GROUP):
                e = _gate_lane(gi, j)
                hg = _dg(xw, wg_ref[gi, j])
                hu = _dg(xw, wu_ref[gi, j])
                hs.append((hg * _sigmoid(hg) * hu * gw[:, e:e + 1]).astype(BF16))
            ys_ref[pl.ds(st, wn), :] += _dg(jnp.concatenate(hs, axis=1), wd_ref[gi])
            return carry

        lax.fori_loop(0, nwin_ref[i * N_GROUPS + gi], window, 0)

    ys = ys_ref[:tm, :].astype(BF16)
    for r0 in range(0, tm, UNSORT_ROWS):
        rows = slice(r0, r0 + UNSORT_ROWS)
        y = _dg(unperm_ref[rows, :], ys)
        o_ref[rows, :] = _layer_norm(ALPHA * x_ref[rows, :] + y, g_ref[...], b_ref[...])


def _moe_sorted_ln(x2d, meta, slot_rows, counts, wg, wu, wd, ln_g, ln_b, *, layer, tm, wn):
    n = x2d.shape[0]
    nt = n // tm
    assert wn % PACK == 0 and tm % PACK == 0
    cnt = counts[:, 0, :N_GROUPS].astype(jnp.int32)
    offs = jnp.cumsum(cnt, axis=1) - cnt
    first = (offs // PACK) * PACK
    nwin = jnp.where(cnt > 0, (offs - first + cnt + wn - 1) // wn, 0)
    tok = lambda w: pl.BlockSpec((tm, w), lambda i, *_: (i, 0))
    resident = lambda a: pl.BlockSpec((None,) + a.shape[1:], lambda i, *_: (layer,) + (0,) * (a.ndim - 1),
                                      pipeline_mode=pl.Buffered(1))
    vec = pl.BlockSpec((1, D_MODEL), lambda i, *_: (0, 0))
    grid_spec = pltpu.PrefetchScalarGridSpec(
        num_scalar_prefetch=2, grid=(nt,),
        in_specs=[tok(D_MODEL), tok(LANES), pl.BlockSpec((1, 1, tm), lambda i, *_: (i, 0, 0)),
                  resident(wg), resident(wu), resident(wd), vec, vec],
        out_specs=tok(D_MODEL),
        scratch_shapes=[pltpu.VMEM((tm + wn, D_MODEL), BF16), pltpu.VMEM((tm + wn, LANES), F32),
                        pltpu.VMEM((tm + wn, D_MODEL), F32), pltpu.VMEM((tm, tm), BF16)])
    return pl.pallas_call(
        functools.partial(_moe_sorted_body, tm=tm, wn=wn), grid_spec=grid_spec,
        out_shape=jax.ShapeDtypeStruct((n, D_MODEL), F32),
        compiler_params=_params(("arbitrary",)), name="moe_sorted_ln",
    )(first.reshape(-1), nwin.reshape(-1), x2d, meta, slot_rows, wg, wu, wd, ln_g.reshape(1, -1), ln_b.reshape(1, -1))


HALO = 16


def _pool_ln_body(x_ref, halo_ref, wp_ref, ps_ref, g_ref, b_ref, *rest, tm, seq_len, start_pos):
    o_ref = rest[0] if len(rest) == 1 else rest[N_ROUTE_INPUTS]
    i = pl.program_id(0)
    nrow = min(tm, EPILOGUE_ROWS)
    logits = []
    for r0 in range(0, tm, nrow):
        rows = slice(r0, r0 + nrow)
        x = x_ref[rows, :]
        before = halo_ref if r0 == 0 else x_ref.at[r0 - HALO:r0, :]
        idx = lax.broadcasted_iota(jnp.int32, (nrow + HALO, 1), 0)
        pos_e = (i * tm + r0 - HALO + idx) & (seq_len - 1)
        cnt_pos = (pos_e[HALO:] + (start_pos + 1)).astype(F32)
        mixed = []
        for gi, w in enumerate(POOL_WINDOWS):
            cols = slice(gi * POOL_GROUP, (gi + 1) * POOL_GROUP)
            xg = x[:, cols]
            s = jnp.concatenate([before[:, cols], xg], axis=0)
            step = 1
            while step < w:
                s = s + jnp.where(pos_e >= step, pltpu.roll(s, step, axis=0), 0.0)
                step *= 2
            pooled = s[HALO:] / jnp.minimum(float(w), cnt_pos) - xg
            mixed.append(_bdot(pooled, wp_ref[gi]))
        h = jnp.concatenate(mixed, axis=1) * ps_ref[...]
        out = _layer_norm(ALPHA * x + h, g_ref[...], b_ref[...])
        o_ref[rows, :] = out
        if len(rest) > 1:
            logits.append(_dg(out.astype(BF16), rest[0][...]))
    if len(rest) > 1:
        _route(logits, *rest[:N_ROUTE_INPUTS], *rest[N_ROUTE_INPUTS + 1:])


def _pool_ln(x2d, w_pool, pool_scale, ln_g, ln_b, *, tm, seq_len, start_pos, router=None):
    n = x2d.shape[0]
    assert seq_len & (seq_len - 1) == 0 and seq_len >= HALO and tm % HALO == 0
    tok = pl.BlockSpec((tm, D_MODEL), lambda i: (i, 0))
    ins = [x2d, x2d, w_pool.astype(BF16), pool_scale.reshape(1, -1), ln_g.reshape(1, -1), ln_b.reshape(1, -1)]
    in_specs = [tok, pl.BlockSpec((HALO, D_MODEL), lambda i: (jnp.maximum(i * (tm // HALO) - 1, 0), 0)),
                _full(w_pool.shape), _full((1, D_MODEL)), _full((1, D_MODEL)), _full((1, D_MODEL))]
    out_specs, out_shape = [tok], [jax.ShapeDtypeStruct((n, D_MODEL), F32)]
    if router is not None:
        r_ins, r_specs = _route_operands(*router, tm)
        ins, in_specs = ins + r_ins, in_specs + r_specs
        r_out_specs, r_out_shape = _route_outputs(n, tm)
        out_specs, out_shape = out_specs + r_out_specs, out_shape + r_out_shape
    res = pl.pallas_call(
        functools.partial(_pool_ln_body, tm=tm, seq_len=seq_len, start_pos=start_pos), grid=(n // tm,),
        in_specs=in_specs, out_specs=out_specs, out_shape=out_shape,
        compiler_params=_params(("arbitrary",)), name="pool_ln",
    )(*ins)
    return res[0] if router is None else res


def _pad_lora_cols(w, width):
    return jnp.pad(w, ((0, 0), (0, LORA_PAD - width)))


def _even_layer_weights(prm, i):
    w_in = prm['w_in'][i]
    o = 3 * D_A
    cols = lambda w: jnp.concatenate(
        [w[:, :o], _pad_lora_cols(w[:, o:o + W_LORA], W_LORA),
         _pad_lora_cols(w[:, o + W_LORA:o + W_LORA + A_LORA], A_LORA),
         _pad_lora_cols(w[:, o + W_LORA + A_LORA:RWKV_COLS], G_LORA)], axis=1)
    pad_rows = lambda w: jnp.pad(w, ((0, LORA_PAD - w.shape[0]), (0, 0))).astype(BF16)
    vec = lambda a: a.reshape(1, -1)
    return dict(
        wr=cols(w_in[:, :RWKV_COLS]).astype(BF16), ws=w_in[:, RWKV_COLS:].astype(BF16),
        mu=cols(prm['tshift_mu'][i].reshape(1, -1)),
        w0=vec(prm['decay_w0'][i]), ww2=pad_rows(prm['decay_w2'][i]),
        a0=vec(prm['iclr_a0'][i]), wa2=pad_rows(prm['iclr_a2'][i]), wg2=pad_rows(prm['gate_w2'][i]),
        k_k=vec(prm['k_k'][i]), k_a=vec(prm['k_a'][i]), r_k=vec(prm['r_k'][i]),
    )


def _group_experts(w):
    return w.astype(BF16).reshape(DEPTH, N_GROUPS, EXPERTS_PER_GROUP, D_MODEL, D_FF)


MOE_SORT_TILE = 512
MOE_WINDOW = 160


def _moe_weights(prm, layer):
    return (prm['wg'], prm['wu'], prm['wd'], prm['ln_ffn_g'][layer], prm['ln_ffn_b'][layer])


def _moe_block(x2d, prm, layer, *, tm):
    meta, _, _ = _router(x2d, prm['w_router'], prm['b_router'], tm=min(tm, 512))
    return _moe_ln(x2d, meta, *_moe_weights(prm, layer), layer=layer, tm=tm)


PROMPT_CHUNKS_PER_STEP = 16
PROMPT_ATTN_BLOCKS_PER_STEP = 8
SAMPLE_CHUNK = 16
SAMPLE_QROWS = SUBLANES
SAMPLE_SEQS_PER_STEP = 8


def _prompt_trunk(x, prm):
    bsz, t, _ = x.shape
    n = bsz * t
    x0 = x.reshape(n, D_MODEL)
    wts = _even_layer_weights(prm, 0)
    rt, kt, bt, at, v, bonus, g, gl, swa = _inproj_prep(x0, None, wts, seq_len=t, chunk=CHUNK, tm=512)
    s0 = jnp.zeros((bsz, N_HEADS, HEAD, HEAD), F32)
    ya, s_new = _rwkv_scan(rt, kt, bt, at, v, bonus, g, gl, prm['lnx_g'][0].reshape(1, -1),
                           prm['lnx_b'][0].reshape(1, -1), s0, n_seq=bsz, seq_len=t, chunk=CHUNK,
                           nc=min(PROMPT_CHUNKS_PER_STEP, t // CHUNK))
    n_blk = t // WINDOW
    qb = min(PROMPT_ATTN_BLOCKS_PER_STEP, n_blk)
    yb = _swa(swa, (swa,), prm['rel_bias'], prm['attn_sinks'][0], n_seq=bsz, n_blk=n_blk, qr=WINDOW, fresh=True,
              per_step=qb)
    tm = MOE_SORT_TILE
    assert n % tm == 0
    router = (prm['w_router'], prm['b_router'])
    x1, *routing = _outproj_ln(ya, yb, x0, prm['w_out'][0], prm['ln_mix_g'][0], prm['ln_mix_b'][0], tm=tm,
                               router=router)
    x2 = _moe_sorted_ln(x1, *routing, *_moe_weights(prm, 0), layer=0, tm=tm, wn=MOE_WINDOW)
    x3, *routing = _pool_ln(x2, prm['w_pool'][0], prm['pool_scale'][0], prm['ln_mix_g'][1], prm['ln_mix_b'][1],
                            tm=tm, seq_len=t, start_pos=0, router=router)
    x4 = _moe_sorted_ln(x3, *routing, *_moe_weights(prm, 1), layer=1, tm=tm, wn=MOE_WINDOW)
    swa3 = swa.reshape(bsz, t, SWA_COLS)
    k_new = swa3[:, t - WINDOW:, D_B:D_B + KV_COLS].reshape(bsz, WINDOW, KV_HEADS, HEAD)
    v_new = swa3[:, t - WINDOW:, D_B + KV_COLS:].reshape(bsz, WINDOW, KV_HEADS, HEAD)
    pool_new = x2.reshape(bsz, t, D_MODEL)[:, t - POOL_KEEP:]
    return (x4.reshape(bsz, t, D_MODEL), s_new[None], x[:, -1][None], k_new[None], v_new[None], pool_new[None])


def _sample_trunk(x, st_rwkv, st_shift, c_k, c_v, st_pool, prm):
    bsz = x.shape[0]
    x0 = x.reshape(bsz, D_MODEL)
    wts = _even_layer_weights(prm, 0)
    outs = _inproj_prep(x0, st_shift[0], wts, seq_len=1, chunk=1, tm=bsz)
    gl, swa = outs[7], outs[8]
    ya, s_new = _rwkv_scan(*outs[:7], gl.reshape(bsz, 1, D_A), prm['lnx_g'][0].reshape(1, -1),
                           prm['lnx_b'][0].reshape(1, -1), st_rwkv[0], n_seq=bsz, seq_len=SAMPLE_CHUNK,
                           chunk=SAMPLE_CHUNK, nc=SAMPLE_SEQS_PER_STEP)
    yb = _swa(swa, (c_k[0], c_v[0]), prm['rel_bias'], prm['attn_sinks'][0], n_seq=bsz, n_blk=1,
              qr=SAMPLE_QROWS, fresh=False, per_step=SAMPLE_SEQS_PER_STEP)
    x1 = _outproj_ln(ya, yb, x0, prm['w_out'][0], prm['ln_mix_g'][0], prm['ln_mix_b'][0], tm=bsz)
    x2 = _moe_block(x1, prm, 0, tm=bsz)
    xcat = jnp.concatenate([st_pool[0], x2[:, None]], axis=1)
    x3 = _pool_ln(xcat.reshape(bsz * HALO, D_MODEL), prm['w_pool'][0], prm['pool_scale'][0],
                  prm['ln_mix_g'][1], prm['ln_mix_b'][1], tm=min(512, bsz * HALO), seq_len=HALO,
                  start_pos=PAST_LEN - POOL_KEEP)
    x3 = x3.reshape(bsz, HALO, D_MODEL)[:, -1]
    x4 = _moe_block(x3, prm, 1, tm=bsz)
    k_new = jnp.concatenate([c_k[0][:, 1:], swa[:, D_B:D_B + KV_COLS].reshape(bsz, 1, KV_HEADS, HEAD)], axis=1)
    v_new = jnp.concatenate([c_v[0][:, 1:], swa[:, D_B + KV_COLS:].reshape(bsz, 1, KV_HEADS, HEAD)], axis=1)
    return (x4.reshape(bsz, 1, D_MODEL), s_new[None], x[:, -1][None], k_new[None], v_new[None],
            xcat[:, 1:][None])


def _prepare_params(raw):
    prm = dict(raw)
    prm['r_k'] = raw['r_k'].reshape(raw['r_k'].shape[0], -1)
    prm['wg'] = _group_experts(raw['w_ex_gate'])
    prm['wu'] = _group_experts(raw['w_ex_up'])
    prm['wd'] = raw['w_ex_down'].astype(BF16).reshape(DEPTH, N_GROUPS, EXPERTS_PER_GROUP * D_FF, D_MODEL)
    return prm


def kernel(x_prompt, x_sample, state_rwkv, state_shift, cache_swa_k, cache_swa_v, state_pool, w_in, tshift_mu,
           decay_w0, decay_w2, iclr_a0, iclr_a2, gate_w2, k_k, k_a, r_k, lnx_g, lnx_b, attn_sinks, rel_bias, w_out,
           w_pool, pool_scale, ln_mix_g, ln_mix_b, ln_ffn_g, ln_ffn_b, w_router, b_router, w_ex_gate, w_ex_up,
           w_ex_down):
    prm = _prepare_params(dict(
        w_in=w_in, tshift_mu=tshift_mu, decay_w0=decay_w0, decay_w2=decay_w2, iclr_a0=iclr_a0, iclr_a2=iclr_a2,
        gate_w2=gate_w2, k_k=k_k, k_a=k_a, r_k=r_k, lnx_g=lnx_g, lnx_b=lnx_b, attn_sinks=attn_sinks,
        rel_bias=rel_bias, w_out=w_out, w_pool=w_pool, pool_scale=pool_scale, ln_mix_g=ln_mix_g, ln_mix_b=ln_mix_b,
        ln_ffn_g=ln_ffn_g, ln_ffn_b=ln_ffn_b, w_router=w_router, b_router=b_router, w_ex_gate=w_ex_gate,
        w_ex_up=w_ex_up, w_ex_down=w_ex_down))
    y_p, rwkv_p, shift_p, k_p, v_p, pool_p = _prompt_trunk(x_prompt, prm)
    y_s, rwkv_s, shift_s, k_s, v_s, pool_s = _sample_trunk(x_sample, state_rwkv, state_shift, cache_swa_k,
                                                            cache_swa_v, state_pool, prm)
    return (y_p, y_s, rwkv_p, rwkv_s, shift_p, shift_s, k_p, k_s, v_p, v_s, pool_p, pool_s)
```

```python
import functools
import math

import numpy as np
import jax
import jax.numpy as jnp
from jax import lax
from jax.experimental import pallas as pl
from jax.experimental.pallas import tpu as pltpu

F32 = jnp.float32
BF16 = jnp.bfloat16

D_MODEL = 1024
DEPTH = 2
PAST_LEN = 16384
D_A = 512
HEAD = 64
N_HEADS = 8
W_LORA, A_LORA, G_LORA = 32, 64, 96
RWKV_COLS = 3 * D_A + W_LORA + A_LORA + G_LORA
D_B = 512
KV_HEADS = 2
Q_PER_KV = 4
KV_COLS = KV_HEADS * HEAD
SWA_COLS = D_B + 2 * KV_COLS
WINDOW = 128
N_BUCKETS = 32
MAX_EXACT = 16
REL_MAX_DIST = 128
POOL_WINDOWS = (2, 4, 8, 16)
POOL_GROUP = 256
POOL_KEEP = 15
N_EXPERTS = 16
N_GROUPS = 4
EXPERTS_PER_GROUP = 4
D_FF = 256
ALPHA = (2.0 * DEPTH) ** 0.25
LN_EPS = 1e-5
LNX_EPS = 64e-5
NEG_INF = -1e30

LANES = 128
SUBLANES = 8
PACK = 16
PREP_ROWS = 128
EPILOGUE_ROWS = 256
VMEM_LIMIT_BYTES = 56 * 1024 * 1024

LORA_PAD = LANES
RWKV_PAD_COLS = 3 * D_A + 3 * LORA_PAD
CHUNK = 64

NN = ((1,), (0,))
NT = ((1,), (1,))
TN = ((0,), (0,))


def _dg(a, b, dims=NN):
    return lax.dot_general(a, b, (dims, ((), ())), preferred_element_type=F32)


def _bdot(a, b, dims=NN):
    return _dg(a.astype(BF16), b.astype(BF16), dims)


def _split(x, n):
    parts, rem = [], x
    for i in range(n):
        p = rem.astype(BF16)
        parts.append(p)
        if i + 1 < n:
            rem = rem - p.astype(F32)
    return parts


def _dot_exact_rhs(x, ones_bf16, n=3):
    out = None
    for p in _split(x, n):
        t = _dg(p, ones_bf16)
        out = t if out is None else out + t
    return out


def _dot_exact_lhs(ones_bf16, x, n=3):
    out = None
    for p in _split(x, n):
        t = _dg(ones_bf16, p)
        out = t if out is None else out + t
    return out


def _sigmoid(x):
    return 1.0 / (1.0 + jnp.exp(-x))


def _layer_norm(z, g, b):
    mu = jnp.mean(z, axis=-1, keepdims=True)
    d = z - mu
    var = jnp.mean(d * d, axis=-1, keepdims=True)
    return d * lax.rsqrt(var + LN_EPS) * g + b


def _params(sem):
    return pltpu.CompilerParams(dimension_semantics=sem, vmem_limit_bytes=VMEM_LIMIT_BYTES)


def _full(shape):
    nd = len(shape)
    return pl.BlockSpec(shape, lambda *_: (0,) * nd)


def _inproj_prep_body(*refs, tm, tiles_per_seq, chunk, has_prev):
    if has_prev:
        x_ref, xp_ref = refs[0], refs[1]
        refs = refs[2:]
    else:
        x_ref, xp_ref = refs[0], None
        refs = refs[1:]
    (wr_ref, ws_ref, mu_ref, w0_ref, ww2_ref, a0_ref, wa2_ref, wg2_ref, kk_ref, ka_ref, rk_ref,
     ones_ref, tri_ref,
     rt_ref, kt_ref, bt_ref, at_ref, v_ref, bonus_ref, g_ref, gl_ref, swa_ref, carry_ref) = refs

    i = pl.program_id(0)
    sub = tri_ref.shape[0]
    assert tm % sub == 0 and sub % chunk == 0
    blocks = [slice(r0, r0 + sub) for r0 in range(0, tm, sub)]
    st = [dict() for _ in blocks]
    ones_bd = ones_ref[...]
    half = D_A // 2

    if not has_prev:
        @pl.when(i == 0)
        def _():
            carry_ref[...] = jnp.zeros_like(carry_ref)

        first = (i % tiles_per_seq) == 0
        carried = jnp.where(first, 0.0, carry_ref[0:1, :])
        row = lax.broadcasted_iota(jnp.int32, (sub, 1), 0)

    def project(u):
        b_ = blocks[u]
        xb = x_ref[b_, :].astype(BF16)
        pr = _dg(xb, wr_ref[...])
        swa_ref[b_, :] = _dg(xb, ws_ref[...])
        if has_prev:
            shifted = _dg(xp_ref[b_, :].astype(BF16), wr_ref[...])
        else:
            before = carried if u == 0 else st[u - 1]['last']
            shifted = jnp.where(row == 0, before, pltpu.roll(pr, 1, axis=0))
            st[u]['last'] = pr[sub - 1:sub, :]
        st[u]['m'] = pr + mu_ref[...] * (shifted - pr)

    def low_rank(u):
        m = st[u].pop('m')
        s = st[u]
        s['r'], k, s['v'] = m[:, 0:D_A], m[:, D_A:2 * D_A], m[:, 2 * D_A:3 * D_A]
        wd, ad, gd = (m[:, 3 * D_A + j * LORA_PAD:3 * D_A + (j + 1) * LORA_PAD] for j in range(3))
        nz = -(w0_ref[...] + _bdot(jnp.tanh(wd), ww2_ref[...]))
        softplus = jnp.maximum(nz, 0.0) + jnp.log1p(jnp.exp(-jnp.abs(nz)))
        s['logw'] = -jnp.exp(-softplus - 0.5)
        s['a'] = _sigmoid(a0_ref[...] + _bdot(ad, wa2_ref[...]))
        s['g'] = _bdot(_sigmoid(gd), wg2_ref[...])
        s['kkr'] = k * kk_ref[...]
        s['k2'] = k * (1.0 + (s['a'] - 1.0) * ka_ref[...])

    def head_sums(z):
        return jnp.concatenate([_dot_exact_rhs(z[:, :half], ones_bd, 2), _dot_exact_rhs(z[:, half:], ones_bd, 2)],
                               axis=1)

    def sums(u):
        s = st[u]
        s['sq'] = head_sums(s['kkr'] * s['kkr'])
        s['rk'] = head_sums(s['r'] * s['k2'] * rk_ref[...])
        s['cum'] = _dot_exact_lhs(tri_ref[...], s['logw'], 2) if chunk > 1 else s['logw']

    def store(u):
        s, b_ = st[u], blocks[u]
        kk = s['kkr'] / jnp.maximum(jnp.sqrt(s['sq']), 1e-12)
        gam = jnp.exp(s['cum'])
        inv = jnp.exp(-s['cum'])
        bonus_ref[b_, :] = s['rk'] * s['v']
        rt_ref[b_, :] = s['r'] * gam
        kt_ref[b_, :] = s['k2'] * inv
        bt_ref[b_, :] = kk * s['a'] * inv
        at_ref[b_, :] = -kk * jnp.exp(s['cum'] - s['logw'])
        v_ref[b_, :] = s['v']
        g_ref[b_, :] = s['g']
        if chunk > 1:
            for c in range(sub // chunk):
                gl_ref[u * (sub // chunk) + c] = gam[(c + 1) * chunk - 1:(c + 1) * chunk, :]
        else:
            gl_ref[b_, :] = gam

    stages = (project, low_rank, sums, store)
    for t in range(len(blocks) + len(stages) - 1):
        for depth, stage in enumerate(stages):
            if 0 <= t - depth < len(blocks):
                stage(t - depth)
    if not has_prev:
        carry_ref[0:1, :] = st[-1]['last']


def _inproj_prep(x2d, x_prev, wts, *, seq_len, chunk, tm):
    n = x2d.shape[0]
    assert n % tm == 0 and (x_prev is not None or seq_len % tm == 0)
    assert tm % chunk == 0
    has_prev = x_prev is not None
    row = lambda w: pl.BlockSpec((tm, w), lambda i: (i, 0))
    ins = [x2d] + ([x_prev] if has_prev else [])
    in_specs = [row(D_MODEL)] + ([row(D_MODEL)] if has_prev else [])
    consts = [wts['wr'], wts['ws'], wts['mu'], wts['w0'], wts['ww2'], wts['a0'], wts['wa2'], wts['wg2'],
              wts['k_k'], wts['k_a'], wts['r_k'], _ones_block_diag(D_A // 2, HEAD), _tri_block_diag(min(tm, PREP_ROWS), chunk)]
    ins += consts
    in_specs += [_full(c.shape) for c in consts]
    if chunk > 1:
        gl_shape = jax.ShapeDtypeStruct((n // chunk, 1, D_A), F32)
        gl_spec = pl.BlockSpec((tm // chunk, 1, D_A), lambda i: (i, 0, 0))
    else:
        gl_shape = jax.ShapeDtypeStruct((n, D_A), F32)
        gl_spec = row(D_A)
    out_shape = [jax.ShapeDtypeStruct((n, D_A), F32)] * 7 + [gl_shape, jax.ShapeDtypeStruct((n, SWA_COLS), F32)]
    out_specs = [row(D_A)] * 7 + [gl_spec, row(SWA_COLS)]
    body = functools.partial(_inproj_prep_body, tm=tm, tiles_per_seq=max(seq_len // tm, 1), chunk=chunk,
                             has_prev=has_prev)
    return pl.pallas_call(
        body, grid=(n // tm,), in_specs=in_specs, out_specs=out_specs, out_shape=out_shape,
        scratch_shapes=[pltpu.VMEM((SUBLANES, RWKV_PAD_COLS), F32)],
        compiler_params=_params(("arbitrary",)), name="inproj_prep",
    )(*ins)


def _ones_block_diag(n, blk):
    idx = np.arange(n) // blk
    return jnp.asarray(idx[:, None] == idx[None, :], dtype=BF16)


def _tri_block_diag(n, blk):
    idx = np.arange(n)
    same = (idx[:, None] // blk) == (idx[None, :] // blk)
    return jnp.asarray(same & (idx[:, None] >= idx[None, :]), dtype=BF16)


PAIR = 2 * HEAD
N_PAIRS = N_HEADS // 2


def _scan_body(rt_ref, kt_ref, bt_ref, at_ref, v_ref, bonus_ref, g_ref, gl_ref, lng_ref, lnb_ref, s0_ref,
               y_ref, sout_ref, S_ref, *, C, nc, one_chunk_seqs):
    step = pl.program_id(1)

    def pair_state(ref, s, p):
        return jnp.concatenate([ref[s, 2 * p], ref[s, 2 * p + 1]], axis=1)

    if not one_chunk_seqs:
        @pl.when(step == 0)
        def _():
            for p in range(N_PAIRS):
                S_ref[p] = pair_state(s0_ref, 0, p)

    head_a = lax.broadcasted_iota(jnp.int32, (1, PAIR), 1) < HEAD
    rowc = lax.broadcasted_iota(jnp.int32, (C, 2 * C), 0)
    colc = lax.broadcasted_iota(jnp.int32, (C, 2 * C), 1)
    first_c = colc < C
    col_in = jnp.where(first_c, colc, colc - C)
    strict = rowc > col_in
    incl = rowc >= col_in
    eye_c = (rowc == col_in).astype(F32)
    r128 = lax.broadcasted_iota(jnp.int32, (PAIR, PAIR), 0)
    c128 = lax.broadcasted_iota(jnp.int32, (PAIR, PAIR), 1)
    same_head = (r128 < HEAD) == (c128 < HEAD)
    eye128 = (r128 == c128).astype(F32)

    def by_head(x):
        xb = x.astype(BF16)
        z = jnp.zeros_like(xb)
        return jnp.concatenate([jnp.where(head_a, xb, z), jnp.where(head_a, z, xb)], axis=0)

    def head_mean(z):
        za = jnp.sum(jnp.where(head_a, z, 0.0), axis=-1, keepdims=True)
        zb = jnp.sum(jnp.where(head_a, 0.0, z), axis=-1, keepdims=True)
        return jnp.where(head_a, za, zb) * (1.0 / HEAD)

    n_fold = int(math.log2(C)) - 1
    zero_blk = jnp.zeros((2 * C, PAIR), BF16)
    items = [(ci, p) for ci in range(nc) for p in range(N_PAIRS)]

    def load(ref, ci, p):
        if one_chunk_seqs:
            pad = jnp.zeros((C - 1, PAIR), F32)
            return jnp.concatenate([ref[ci:ci + 1, p * PAIR:(p + 1) * PAIR], pad], axis=0)
        return ref[ci * C:(ci + 1) * C, p * PAIR:(p + 1) * PAIR]

    def blocks(x):
        xb = x.astype(BF16)
        z = jnp.zeros_like(xb)
        return jnp.concatenate([jnp.where(first_c, xb, z), jnp.where(first_c, z, xb)], axis=0)

    st = [dict() for _ in items]

    def products(n):
        ci, p = items[n]
        s = st[n]
        R, K, Bm, A, V = (load(ref, ci, p) for ref in (rt_ref, kt_ref, bt_ref, at_ref, v_ref))
        big = _dg(jnp.concatenate([A, R], axis=0).astype(BF16),
                  jnp.concatenate([by_head(Bm), by_head(K)], axis=0), NT)
        s.update(R=R, K=K, Bm=Bm, A=A, V=V,
                 a_ab=jnp.where(strict, big[:C, :2 * C], 0.0), a_ak=jnp.where(strict, big[:C, 2 * C:], 0.0),
                 l_rb=jnp.where(incl, big[C:, :2 * C], 0.0), l_rk=jnp.where(incl, big[C:, 2 * C:], 0.0))

    def square(n):
        s = st[n]
        a_ab = s.pop('a_ab')
        s['X'] = _dg(a_ab.astype(BF16), blocks(a_ab))
        s['T'] = eye_c + a_ab
        s['akv'] = _dg(s.pop('a_ak').astype(BF16), by_head(s['V']))

    def fold(n):
        s = st[n]
        res = _dg(jnp.concatenate([s['X'], s['T']], axis=0).astype(BF16), blocks(s['X']))
        s['X'], s['T'] = res[:C], s['T'] + res[C:]

    def last_fold(n):
        s = st[n]
        s['T'] = s['T'] + _dg(s['T'].astype(BF16), blocks(s.pop('X')))

    def solve(n):
        s = st[n]
        pq = _dg(s.pop('T').astype(BF16), jnp.concatenate([by_head(s['A']), by_head(s.pop('akv'))], axis=1))
        s['P'], s['Q'] = pq[:, :PAIR], pq[:, PAIR:]

    def readout(n):
        s = st[n]
        P, Q, V = s.pop('P'), s.pop('Q'), s.pop('V')
        wz = _dg(jnp.concatenate([s.pop('l_rb'), s.pop('l_rk')], axis=1).astype(BF16),
                 jnp.concatenate([jnp.concatenate([by_head(P), by_head(Q)], axis=1),
                                  jnp.concatenate([zero_blk, by_head(V)], axis=1)], axis=0))
        s['W'], s['Z'] = s.pop('R') + wz[:, :PAIR], wz[:, PAIR:]
        Bm, K = s.pop('Bm'), s.pop('K')
        s['ptb'] = _bdot(P, Bm, TN)
        s['gfull'] = _bdot(jnp.concatenate([Q, V], axis=0), jnp.concatenate([Bm, K], axis=0), TN)
        del s['A']

    def put_state(s, p, val):
        sout_ref[s, 2 * p] = val[:, :HEAD]
        sout_ref[s, 2 * p + 1] = val[:, HEAD:]

    state = [None if one_chunk_seqs else S_ref[p] for p in range(N_PAIRS)]

    def advance(n):
        ci, p = items[n]
        s = st[n]
        rows = slice(ci, ci + 1) if one_chunk_seqs else slice(ci * C, (ci + 1) * C)
        ln = slice(p * PAIR, (p + 1) * PAIR)
        g_c = gl_ref[ci, :, ln]
        gfull = s.pop('gfull')
        M = (eye128 + jnp.where(same_head, s.pop('ptb'), 0.0)) * g_c
        G = jnp.where(head_a, gfull[:HEAD], gfull[HEAD:]) * g_c
        S0 = pair_state(s0_ref, ci, p) if one_chunk_seqs else state[p]
        Y = _dg(s.pop('W').astype(BF16), by_head(S0), NT) + s.pop('Z')
        S1 = _bdot(S0, M) + G
        if one_chunk_seqs:
            put_state(ci, p, S1)
        else:
            state[p] = S1

        if one_chunk_seqs:
            Y = Y[0:1]
        mu = head_mean(Y)
        d = Y - mu
        var = head_mean(d * d)
        yn = d * lax.rsqrt(var + LNX_EPS) * lng_ref[:, ln] + lnb_ref[:, ln]
        y_ref[rows, ln] = ((yn + bonus_ref[rows, ln]) * g_ref[rows, ln]).astype(y_ref.dtype)

    for stage in (products, square) + (fold,) * (n_fold - 1) + (last_fold, solve, readout, advance):
        for n in range(len(items)):
            stage(n)

    if not one_chunk_seqs:
        for p in range(N_PAIRS):
            S_ref[p] = state[p]

        @pl.when(step == pl.num_programs(1) - 1)
        def _():
            for p in range(N_PAIRS):
                put_state(0, p, state[p])


def _rwkv_scan(rt, kt, bt, at, v, bonus, g, gl, lnx_g, lnx_b, s0, *, n_seq, seq_len, chunk, nc):
    one_chunk_seqs = seq_len == chunk
    if one_chunk_seqs:
        assert n_seq % nc == 0
        grid = (n_seq // nc, 1)
        blk = lambda b, c: b
        st = pl.BlockSpec((nc, N_HEADS, HEAD, HEAD), lambda b, c: (b, 0, 0, 0))
        rows, n_rows = nc, n_seq
    else:
        rows, n_rows = chunk * nc, n_seq * seq_len
        assert seq_len % rows == 0
        nsteps = seq_len // rows
        grid = (n_seq, nsteps)
        blk = lambda b, c: b * nsteps + c
        st = pl.BlockSpec((1, N_HEADS, HEAD, HEAD), lambda b, c: (b, 0, 0, 0))
    row = pl.BlockSpec((rows, D_A), lambda b, c: (blk(b, c), 0))
    vec = pl.BlockSpec((1, D_A), lambda b, c: (0, 0))
    return pl.pallas_call(
        functools.partial(_scan_body, C=chunk, nc=nc, one_chunk_seqs=one_chunk_seqs),
        grid=grid,
        in_specs=[row] * 7 + [pl.BlockSpec((nc, 1, D_A), lambda b, c: (blk(b, c), 0, 0)), vec, vec, st],
        out_specs=[row, st],
        out_shape=[jax.ShapeDtypeStruct((n_rows, D_A), BF16 if n_rows % PACK == 0 and rows % PACK == 0 else F32),
                   jax.ShapeDtypeStruct((n_seq, N_HEADS, HEAD, HEAD), F32)],
        scratch_shapes=[pltpu.VMEM((N_PAIRS, HEAD, PAIR), F32)],
        compiler_params=_params(("arbitrary", "arbitrary")), name="rwkv_scan",
    )(rt, kt, bt, at, v, bonus, g, gl, lnx_g, lnx_b, s0)


def _bucket_ranges():
    d = np.arange(WINDOW + 1)
    scaled = np.log(np.maximum(d, MAX_EXACT).astype(np.float32) / MAX_EXACT) / math.log(REL_MAX_DIST / MAX_EXACT)
    large = np.minimum(MAX_EXACT + (scaled * (N_BUCKETS - MAX_EXACT)).astype(np.int32), N_BUCKETS - 1)
    bucket = np.where(d < MAX_EXACT, d, large)
    frac = scaled.astype(np.float64) * (N_BUCKETS - MAX_EXACT)
    near = np.abs(frac - np.round(frac)) < 1e-3
    assert all(int(x) in (MAX_EXACT, REL_MAX_DIST) for x in d[(d >= MAX_EXACT) & near])
    ranges = []
    for b in range(N_BUCKETS):
        hit = d[bucket == b]
        ranges.append((int(hit.min()), int(hit.max())) if hit.size else None)
    return tuple(ranges)


def _swa_body(rb_ref, sink_ref, cur_ref, *rest, qr, fresh, per_step, ranges):
    prevs, (y_ref, bias_ref, sinkcol_ref) = rest[:-3], rest[-3:]
    W = WINDOW
    nk = W + qr

    @pl.when((pl.program_id(0) == 0) & (pl.program_id(1) == 0))
    def _():
        qi = lax.broadcasted_iota(jnp.int32, (qr, nk), 0)
        kj = lax.broadcasted_iota(jnp.int32, (qr, nk), 1)
        d = qi + W - kj
        valid = (d >= 0) & (d <= W)
        for hd in range(N_HEADS):
            t = jnp.zeros((qr, nk), F32)
            for bkt, rng in enumerate(ranges):
                if rng is not None:
                    t = jnp.where((d >= rng[0]) & (d <= rng[1]), rb_ref[bkt * N_HEADS + hd], t)
            kvh, grp = hd // Q_PER_KV, hd % Q_PER_KV
            rows = slice(grp * qr, (grp + 1) * qr)
            bias_ref[kvh, rows, :] = jnp.where(valid, t, NEG_INF)
            bias_ref[KV_HEADS + kvh, rows, :] = jnp.where(valid & (kj >= W), t, NEG_INF)
            sinkcol_ref[kvh, rows, :] = jnp.full((qr, 1), sink_ref[hd], F32)

    first_table = jnp.where(pl.program_id(1) == 0, KV_HEADS, 0) if fresh else 0
    items = [(sub, kvh) for sub in range(per_step) for kvh in range(KV_HEADS)]
    each = lambda f, *lists: [f(*xs) for xs in zip(*lists)]

    def own_rows(sub, col):
        if fresh:
            return cur_ref[sub * qr:(sub + 1) * qr, col:col + HEAD]
        return jnp.concatenate([cur_ref[sub:sub + 1, col:col + HEAD], jnp.zeros((qr - 1, HEAD), F32)], axis=0)

    def kv_rows(sub, kvh, is_v):
        col = D_B + is_v * KV_COLS + kvh * HEAD
        if not fresh:
            before = prevs[is_v][sub, :, kvh, :]
        elif sub > 0:
            before = cur_ref[(sub - 1) * qr:sub * qr, col:col + HEAD]
        else:
            before = prevs[0][:, col - D_B:col - D_B + HEAD]
        return jnp.concatenate([before, own_rows(sub, col)], axis=0).astype(BF16)

    kb = [kv_rows(sub, kvh, 0) for sub, kvh in items]
    vb = [kv_rows(sub, kvh, 1) for sub, kvh in items]
    q4 = [(jnp.concatenate([own_rows(sub, (kvh * Q_PER_KV + gq) * HEAD) for gq in range(Q_PER_KV)], axis=0)
           * (HEAD ** -0.5)).astype(BF16) for sub, kvh in items]
    s = [_dg(q, k, NT) + bias_ref[(first_table if sub == 0 else 0) + kvh] for q, k, (sub, kvh) in zip(q4, kb, items)]
    sink = [sinkcol_ref[kvh] for _, kvh in items]
    m = each(lambda s_, sk: jnp.maximum(jnp.max(s_, axis=-1, keepdims=True), sk), s, sink)
    p = each(lambda s_, m_: jnp.exp(s_ - m_).astype(BF16), s, m)
    pad = jnp.concatenate([jnp.zeros((nk, LANES - HEAD), BF16), jnp.ones((nk, LANES), BF16)], axis=1)
    ov = each(lambda p_, v_: _dg(p_, jnp.concatenate([v_, pad], axis=1)), p, vb)
    o = each(lambda ov_, sk, m_: ov_[:, :LANES] / (ov_[:, LANES:] + jnp.exp(sk - m_)), ov, sink, m)
    for sub in range(per_step):
        outs = [o[sub * KV_HEADS + kvh][gq * qr:(gq + 1) * qr, :HEAD]
                for kvh in range(KV_HEADS) for gq in range(Q_PER_KV)]
        y = jnp.concatenate(outs, axis=1).astype(y_ref.dtype)
        if fresh:
            y_ref[sub * qr:(sub + 1) * qr, :] = y
        else:
            y_ref[sub:sub + 1, :] = y[0:1]


def _swa(cur2d, prevs, rel_bias, sinks, *, n_seq, n_blk, qr, fresh, per_step=1):
    smem = pl.BlockSpec(memory_space=pltpu.SMEM)
    if fresh:
        assert qr == WINDOW and n_blk % per_step == 0
        blocks = n_blk
        n_blk //= per_step
        prev_specs = [pl.BlockSpec((WINDOW, 2 * KV_COLS),
                                   lambda b, j: (b * blocks + jnp.maximum(j * per_step - 1, 0), 2))]
    else:
        assert n_blk == 1 and n_seq % per_step == 0
        n_seq //= per_step
        prev_specs = [pl.BlockSpec((per_step, WINDOW, KV_HEADS, HEAD), lambda b, j: (b, 0, 0, 0))] * 2
    qrows = per_step * qr if fresh else per_step
    out_dtype = BF16 if fresh else F32
    return pl.pallas_call(
        functools.partial(_swa_body, qr=qr, fresh=fresh, per_step=per_step, ranges=_bucket_ranges()),
        grid=(n_seq, n_blk),
        in_specs=[smem, smem, pl.BlockSpec((qrows, SWA_COLS), lambda b, j: (b * n_blk + j, 0))] + prev_specs,
        out_specs=pl.BlockSpec((qrows, D_B), lambda b, j: (b * n_blk + j, 0)),
        out_shape=jax.ShapeDtypeStruct((n_seq * n_blk * qrows, D_B), out_dtype),
        scratch_shapes=[pltpu.VMEM((2 * KV_HEADS, Q_PER_KV * qr, WINDOW + qr), F32),
                        pltpu.VMEM((KV_HEADS, Q_PER_KV * qr, 1), F32)],
        compiler_params=_params(("arbitrary", "arbitrary")), name="swa",
    )(rel_bias.reshape(-1), sinks, cur2d, *prevs)


def _outproj_ln_body(ya_ref, yb_ref, x_ref, wa_ref, wb_ref, g_ref, b_ref, *rest):
    o_ref = rest[0] if len(rest) == 1 else rest[N_ROUTE_INPUTS]
    tm = x_ref.shape[0]
    step = min(tm, EPILOGUE_ROWS)
    logits = []
    for r0 in range(0, tm, step):
        rows = slice(r0, r0 + step)
        h = _dg(ya_ref[rows, :].astype(BF16), wa_ref[...]) + _dg(yb_ref[rows, :].astype(BF16), wb_ref[...])
        out = _layer_norm(ALPHA * x_ref[rows, :] + h, g_ref[...], b_ref[...])
        o_ref[rows, :] = out
        if len(rest) > 1:
            logits.append(_dg(out.astype(BF16), rest[0][...]))
    if len(rest) > 1:
        _route(logits, *rest[:N_ROUTE_INPUTS], *rest[N_ROUTE_INPUTS + 1:])


def _outproj_ln(ya, yb, x2d, w_out, ln_g, ln_b, *, tm, router=None):
    n = x2d.shape[0]
    half = pl.BlockSpec((tm, D_A), lambda i: (i, 0))
    full = pl.BlockSpec((tm, D_MODEL), lambda i: (i, 0))
    wa, wb = w_out[:D_A].astype(BF16), w_out[D_A:].astype(BF16)
    ins = [ya, yb, x2d, wa, wb, ln_g.reshape(1, -1), ln_b.reshape(1, -1)]
    in_specs = [half, half, full, _full(wa.shape), _full(wb.shape), _full((1, D_MODEL)), _full((1, D_MODEL))]
    out_specs, out_shape = [full], [jax.ShapeDtypeStruct((n, D_MODEL), F32)]
    if router is not None:
        r_ins, r_specs = _route_operands(*router, tm)
        ins, in_specs = ins + r_ins, in_specs + r_specs
        r_out_specs, r_out_shape = _route_outputs(n, tm)
        out_specs, out_shape = out_specs + r_out_specs, out_shape + r_out_shape
    res = pl.pallas_call(
        _outproj_ln_body, grid=(n // tm,), in_specs=in_specs, out_specs=out_specs, out_shape=out_shape,
        compiler_params=_params(("arbitrary",)), name="outproj_ln",
    )(*ins)
    return res[0] if router is None else res


def _route(x, wr_ref, br_ref, tri_ref, meta_ref, slot_ref, cnt_ref):
    logits_tm = jnp.concatenate(x, axis=0) if isinstance(x, list) else _dg(x.astype(BF16), wr_ref[...])
    x = logits_tm
    logits = jnp.transpose(logits_tm)[:N_EXPERTS] + br_ref[...]
    z = jnp.exp(logits - jnp.max(logits, axis=0, keepdims=True))
    probs = z / jnp.sum(z, axis=0, keepdims=True)
    pa = [probs[a_ * N_GROUPS:(a_ + 1) * N_GROUPS] for a_ in range(EXPERTS_PER_GROUP)]

    sel = []
    for a_ in range(EXPERTS_PER_GROUP):
        rank = None
        for j in range(EXPERTS_PER_GROUP):
            if j == a_:
                continue
            ahead = ((pa[j] >= pa[a_]) if j < a_ else (pa[j] > pa[a_])).astype(jnp.int32)
            rank = ahead if rank is None else rank + ahead
        sel.append(rank < 2)
    score = None
    for a_ in range(EXPERTS_PER_GROUP):
        t = jnp.where(sel[a_], pa[a_], 0.0)
        score = t if score is None else score + t
    srow = [score[gi:gi + 1] for gi in range(N_GROUPS)]
    best = []
    for gi in range(N_GROUPS):
        ok = None
        for j in range(N_GROUPS):
            if j == gi:
                continue
            c_ = (srow[gi] > srow[j]) if j < gi else (srow[gi] >= srow[j])
            ok = c_ if ok is None else (ok & c_)
        best.append(ok.astype(F32))
    tm = x.shape[0]
    best4 = jnp.concatenate(best, axis=0)
    chosen = best4 > 0.5
    kept = [jnp.where(sel[a_] & chosen, pa[a_], 0.0) for a_ in range(EXPERTS_PER_GROUP)]
    tot = None
    for a_ in range(EXPERTS_PER_GROUP):
        tot = kept[a_] if tot is None else tot + kept[a_]
    tot = jnp.sum(tot, axis=0, keepdims=True)
    gates = [jnp.where(sel[a_] & chosen, pa[a_] / tot, 0.0) for a_ in range(EXPERTS_PER_GROUP)]

    onehot = jnp.concatenate([best4, jnp.zeros((SUBLANES - N_GROUPS, tm), F32)], axis=0)
    oh_b = jnp.concatenate([onehot, jnp.zeros((LANES - SUBLANES, tm), F32)], axis=0).astype(BF16)
    incl = _dg(onehot.astype(BF16), tri_ref[...])
    slot = None
    offset = jnp.zeros((1, 1), F32)
    for gi in range(N_GROUPS):
        t = onehot[gi:gi + 1] * (offset + incl[gi:gi + 1] - 1.0)
        slot = t if slot is None else slot + t
        offset = offset + incl[gi:gi + 1, tm - 1:tm]
    slot_ref[0] = slot.astype(jnp.int32)
    cnt_ref[0] = _dg(jnp.ones((SUBLANES, tm), BF16), oh_b, NT)[0:1]
    rows = jnp.concatenate(gates + [slot, jnp.zeros((LANES - N_EXPERTS - 1, tm), F32)], axis=0)
    meta_ref[...] = jnp.transpose(rows)


META_SLOT = N_EXPERTS
N_ROUTE_INPUTS = 3


def _gate_lane(group, j):
    return j * N_GROUPS + group


def _route_operands(w_router, b_router, tm):
    idx = np.arange(tm)
    tri = jnp.asarray(idx[:, None] <= idx[None, :], dtype=BF16)
    by_member = lambda a: a.reshape(N_GROUPS, EXPERTS_PER_GROUP, -1).transpose(1, 0, 2).reshape(N_EXPERTS, -1)
    wr = jnp.pad(by_member(w_router.T).T, ((0, 0), (0, LANES - N_EXPERTS))).astype(BF16)
    ins = [wr, by_member(b_router), tri]
    return ins, [_full(a.shape) for a in ins]


def _route_outputs(n, tm):
    nt = n // tm
    specs = [pl.BlockSpec((tm, LANES), lambda i: (i, 0)), pl.BlockSpec((1, 1, tm), lambda i: (i, 0, 0)),
             pl.BlockSpec((1, 1, LANES), lambda i: (i, 0, 0))]
    shapes = [jax.ShapeDtypeStruct((n, LANES), F32), jax.ShapeDtypeStruct((nt, 1, tm), jnp.int32),
              jax.ShapeDtypeStruct((nt, 1, LANES), F32)]
    return specs, shapes


def _router_body(x_ref, *refs):
    _route(x_ref[...], *refs)


def _router(x2d, w_router, b_router, *, tm):
    n = x2d.shape[0]
    ins, in_specs = _route_operands(w_router, b_router, tm)
    out_specs, out_shape = _route_outputs(n, tm)
    return pl.pallas_call(
        _router_body, grid=(n // tm,),
        in_specs=[pl.BlockSpec((tm, D_MODEL), lambda i: (i, 0))] + in_specs,
        out_specs=out_specs, out_shape=out_shape,
        compiler_params=_params(("arbitrary",)), name="router",
    )(x2d, *ins)


def _moe_sorted_body(start_ref, nwin_ref, x_ref, meta_ref, slot_ref, wg_ref, wu_ref, wd_ref, g_ref, b_ref, o_ref,
                     xs_ref, gs_ref, ys_ref, unperm_ref, *, tm, wn):
    i = pl.program_id(0)

    @pl.when(i == 0)
    def _():
        xs_ref[tm:, :] = jnp.zeros((wn, D_MODEL), BF16)
        gs_ref[tm:, :] = jnp.zeros((wn, LANES), F32)

    x = x_ref[...]
    meta = meta_ref[...]
    slot_iota = lax.broadcasted_iota(jnp.int32, (tm, tm), 0)
    perm = jnp.where(slot_iota == slot_ref[0], 1.0, 0.0).astype(BF16)
    m_hi, m_lo = _split(meta, 2)
    moved = _dg(perm, jnp.concatenate([x.astype(BF16), m_hi, m_lo], axis=1))
    xs_ref[:tm, :] = moved[:, :D_MODEL].astype(BF16)
    gs_ref[:tm, :] = moved[:, D_MODEL:D_MODEL + LANES] + moved[:, D_MODEL + LANES:]
    ys_ref[...] = jnp.zeros_like(ys_ref)
    tok_slot = meta[:, META_SLOT:META_SLOT + 1]
    lane_slot = lax.broadcasted_iota(jnp.int32, (tm, tm), 1).astype(F32)
    unperm_ref[...] = jnp.where(tok_slot == lane_slot, 1.0, 0.0).astype(BF16)

    for gi in range(N_GROUPS):
        first = start_ref[i * N_GROUPS + gi]

        def window(w, carry, gi=gi, first=first):
            st = pl.multiple_of(first + w * wn, PACK)
            xw = xs_ref[pl.ds(st, wn), :]
            gw = gs_ref[pl.ds(st, wn), :]
            hs = []
            for j in range(EXPERTS_PER_GROUP):
                e = _gate_lane(gi, j)
                hg = _dg(xw, wg_ref[gi, j])
                hu = _dg(xw, wu_ref[gi, j])
                hs.append((hg * _sigmoid(hg) * hu * gw[:, e:e + 1]).astype(BF16))
            ys_ref[pl.ds(st, wn), :] += _dg(jnp.concatenate(hs, axis=1), wd_ref[gi])
            return carry

        lax.fori_loop(0, nwin_ref[i * N_GROUPS + gi], window, 0)

    ys = ys_ref[:tm, :].astype(BF16)
    nrow = min(tm, EPILOGUE_ROWS)
    for r0 in range(0, tm, nrow):
        rows = slice(r0, r0 + nrow)
        y = _dg(unperm_ref[rows, :], ys)
        o_ref[rows, :] = _layer_norm(ALPHA * x_ref[rows, :] + y, g_ref[...], b_ref[...])


def _moe_sorted_ln(x2d, meta, slot_rows, counts, wg, wu, wd, ln_g, ln_b, *, layer, tm, wn):
    n = x2d.shape[0]
    nt = n // tm
    assert wn % PACK == 0 and tm % PACK == 0
    cnt = counts[:, 0, :N_GROUPS].astype(jnp.int32)
    offs = jnp.cumsum(cnt, axis=1) - cnt
    first = (offs // PACK) * PACK
    nwin = jnp.where(cnt > 0, (offs - first + cnt + wn - 1) // wn, 0)
    tok = lambda w: pl.BlockSpec((tm, w), lambda i, *_: (i, 0))
    resident = lambda a: pl.BlockSpec((None,) + a.shape[1:], lambda i, *_: (layer,) + (0,) * (a.ndim - 1),
                                      pipeline_mode=pl.Buffered(1))
    vec = pl.BlockSpec((1, D_MODEL), lambda i, *_: (0, 0))
    grid_spec = pltpu.PrefetchScalarGridSpec(
        num_scalar_prefetch=2, grid=(nt,),
        in_specs=[tok(D_MODEL), tok(LANES), pl.BlockSpec((1, 1, tm), lambda i, *_: (i, 0, 0)),
                  resident(wg), resident(wu), resident(wd), vec, vec],
        out_specs=tok(D_MODEL),
        scratch_shapes=[pltpu.VMEM((tm + wn, D_MODEL), BF16), pltpu.VMEM((tm + wn, LANES), F32),
                        pltpu.VMEM((tm + wn, D_MODEL), F32), pltpu.VMEM((tm, tm), BF16)])
    return pl.pallas_call(
        functools.partial(_moe_sorted_body, tm=tm, wn=wn), grid_spec=grid_spec,
        out_shape=jax.ShapeDtypeStruct((n, D_MODEL), F32),
        compiler_params=_params(("arbitrary",)), name="moe_sorted_ln",
    )(first.reshape(-1), nwin.reshape(-1), x2d, meta, slot_rows, wg, wu, wd, ln_g.reshape(1, -1), ln_b.reshape(1, -1))


HALO = 16


def _pool_ln_body(x_ref, halo_ref, wp_ref, ps_ref, g_ref, b_ref, *rest, tm, seq_len, start_pos):
    o_ref = rest[0] if len(rest) == 1 else rest[N_ROUTE_INPUTS]
    i = pl.program_id(0)
    nrow = min(tm, EPILOGUE_ROWS)
    logits = []
    for r0 in range(0, tm, nrow):
        rows = slice(r0, r0 + nrow)
        x = x_ref[rows, :]
        before = halo_ref if r0 == 0 else x_ref.at[r0 - HALO:r0, :]
        idx = lax.broadcasted_iota(jnp.int32, (nrow + HALO, 1), 0)
        pos_e = (i * tm + r0 - HALO + idx) & (seq_len - 1)
        cnt_pos = (pos_e[HALO:] + (start_pos + 1)).astype(F32)
        mixed = []
        for gi, w in enumerate(POOL_WINDOWS):
            cols = slice(gi * POOL_GROUP, (gi + 1) * POOL_GROUP)
            xg = x[:, cols]
            s = jnp.concatenate([before[:, cols], xg], axis=0)
            step = 1
            while step < w:
                s = s + jnp.where(pos_e >= step, pltpu.roll(s, step, axis=0), 0.0)
                step *= 2
            pooled = s[HALO:] / jnp.minimum(float(w), cnt_pos) - xg
            mixed.append(_bdot(pooled, wp_ref[gi]))
        h = jnp.concatenate(mixed, axis=1) * ps_ref[...]
        out = _layer_norm(ALPHA * x + h, g_ref[...], b_ref[...])
        o_ref[rows, :] = out
        if len(rest) > 1:
            logits.append(_dg(out.astype(BF16), rest[0][...]))
    if len(rest) > 1:
        _route(logits, *rest[:N_ROUTE_INPUTS], *rest[N_ROUTE_INPUTS + 1:])


def _pool_ln(x2d, w_pool, pool_scale, ln_g, ln_b, *, tm, seq_len, start_pos, router=None):
    n = x2d.shape[0]
    assert seq_len & (seq_len - 1) == 0 and seq_len >= HALO and tm % HALO == 0
    tok = pl.BlockSpec((tm, D_MODEL), lambda i: (i, 0))
    ins = [x2d, x2d, w_pool.astype(BF16), pool_scale.reshape(1, -1), ln_g.reshape(1, -1), ln_b.reshape(1, -1)]
    in_specs = [tok, pl.BlockSpec((HALO, D_MODEL), lambda i: (jnp.maximum(i * (tm // HALO) - 1, 0), 0)),
                _full(w_pool.shape), _full((1, D_MODEL)), _full((1, D_MODEL)), _full((1, D_MODEL))]
    out_specs, out_shape = [tok], [jax.ShapeDtypeStruct((n, D_MODEL), F32)]
    if router is not None:
        r_ins, r_specs = _route_operands(*router, tm)
        ins, in_specs = ins + r_ins, in_specs + r_specs
        r_out_specs, r_out_shape = _route_outputs(n, tm)
        out_specs, out_shape = out_specs + r_out_specs, out_shape + r_out_shape
    res = pl.pallas_call(
        functools.partial(_pool_ln_body, tm=tm, seq_len=seq_len, start_pos=start_pos), grid=(n // tm,),
        in_specs=in_specs, out_specs=out_specs, out_shape=out_shape,
        compiler_params=_params(("arbitrary",)), name="pool_ln",
    )(*ins)
    return res[0] if router is None else res


def _pad_lora_cols(w, width):
    return jnp.pad(w, ((0, 0), (0, LORA_PAD - width)))


def _even_layer_weights(prm, i):
    w_in = prm['w_in'][i]
    o = 3 * D_A
    cols = lambda w: jnp.concatenate(
        [w[:, :o], _pad_lora_cols(w[:, o:o + W_LORA], W_LORA),
         _pad_lora_cols(w[:, o + W_LORA:o + W_LORA + A_LORA], A_LORA),
         _pad_lora_cols(w[:, o + W_LORA + A_LORA:RWKV_COLS], G_LORA)], axis=1)
    pad_rows = lambda w: jnp.pad(w, ((0, LORA_PAD - w.shape[0]), (0, 0))).astype(BF16)
    vec = lambda a: a.reshape(1, -1)
    return dict(
        wr=cols(w_in[:, :RWKV_COLS]).astype(BF16), ws=w_in[:, RWKV_COLS:].astype(BF16),
        mu=cols(prm['tshift_mu'][i].reshape(1, -1)),
        w0=vec(prm['decay_w0'][i]), ww2=pad_rows(prm['decay_w2'][i]),
        a0=vec(prm['iclr_a0'][i]), wa2=pad_rows(prm['iclr_a2'][i]), wg2=pad_rows(prm['gate_w2'][i]),
        k_k=vec(prm['k_k'][i]), k_a=vec(prm['k_a'][i]), r_k=vec(prm['r_k'][i]),
    )


def _group_experts(w):
    return w.astype(BF16).reshape(DEPTH, N_GROUPS, EXPERTS_PER_GROUP, D_MODEL, D_FF)


TOKEN_TILE = 512
PROMPT_CHUNKS_PER_STEP = 16
PROMPT_ATTN_BLOCKS_PER_STEP = 8
SAMPLE_CHUNK = 16
SAMPLE_QROWS = SUBLANES
SAMPLE_SEQS_PER_STEP = 8


def _moe_weights(prm, layer):
    return (prm['wg'], prm['wu'], prm['wd'], prm['ln_ffn_g'][layer], prm['ln_ffn_b'][layer])


def _moe_window(tm):
    return -(-(tm // N_GROUPS) // PACK) * PACK + 2 * PACK


def _moe(x2d, routing, prm, layer, *, tm):
    return _moe_sorted_ln(x2d, *routing, *_moe_weights(prm, layer), layer=layer, tm=tm, wn=_moe_window(tm))


def _prompt_trunk(x, prm):
    bsz, t, _ = x.shape
    n = bsz * t
    x0 = x.reshape(n, D_MODEL)
    wts = _even_layer_weights(prm, 0)
    rt, kt, bt, at, v, bonus, g, gl, swa = _inproj_prep(x0, None, wts, seq_len=t, chunk=CHUNK, tm=TOKEN_TILE)
    s0 = jnp.zeros((bsz, N_HEADS, HEAD, HEAD), F32)
    ya, s_new = _rwkv_scan(rt, kt, bt, at, v, bonus, g, gl, prm['lnx_g'][0].reshape(1, -1),
                           prm['lnx_b'][0].reshape(1, -1), s0, n_seq=bsz, seq_len=t, chunk=CHUNK,
                           nc=min(PROMPT_CHUNKS_PER_STEP, t // CHUNK))
    n_blk = t // WINDOW
    qb = min(PROMPT_ATTN_BLOCKS_PER_STEP, n_blk)
    yb = _swa(swa, (swa,), prm['rel_bias'], prm['attn_sinks'][0], n_seq=bsz, n_blk=n_blk, qr=WINDOW, fresh=True,
              per_step=qb)
    tm = TOKEN_TILE
    assert n % tm == 0
    router = (prm['w_router'], prm['b_router'])
    x1, *routing = _outproj_ln(ya, yb, x0, prm['w_out'][0], prm['ln_mix_g'][0], prm['ln_mix_b'][0], tm=tm,
                               router=router)
    x2 = _moe(x1, routing, prm, 0, tm=tm)
    x3, *routing = _pool_ln(x2, prm['w_pool'][0], prm['pool_scale'][0], prm['ln_mix_g'][1], prm['ln_mix_b'][1],
                            tm=tm, seq_len=t, start_pos=0, router=router)
    x4 = _moe(x3, routing, prm, 1, tm=tm)
    swa3 = swa.reshape(bsz, t, SWA_COLS)
    k_new = swa3[:, t - WINDOW:, D_B:D_B + KV_COLS].reshape(bsz, WINDOW, KV_HEADS, HEAD)
    v_new = swa3[:, t - WINDOW:, D_B + KV_COLS:].reshape(bsz, WINDOW, KV_HEADS, HEAD)
    pool_new = x2.reshape(bsz, t, D_MODEL)[:, t - POOL_KEEP:]
    return (x4.reshape(bsz, t, D_MODEL), s_new[None], x[:, -1][None], k_new[None], v_new[None], pool_new[None])


def _sample_trunk(x, st_rwkv, st_shift, c_k, c_v, st_pool, prm):
    bsz = x.shape[0]
    x0 = x.reshape(bsz, D_MODEL)
    wts = _even_layer_weights(prm, 0)
    outs = _inproj_prep(x0, st_shift[0], wts, seq_len=1, chunk=1, tm=bsz)
    gl, swa = outs[7], outs[8]
    ya, s_new = _rwkv_scan(*outs[:7], gl.reshape(bsz, 1, D_A), prm['lnx_g'][0].reshape(1, -1),
                           prm['lnx_b'][0].reshape(1, -1), st_rwkv[0], n_seq=bsz, seq_len=SAMPLE_CHUNK,
                           chunk=SAMPLE_CHUNK, nc=SAMPLE_SEQS_PER_STEP)
    yb = _swa(swa, (c_k[0], c_v[0]), prm['rel_bias'], prm['attn_sinks'][0], n_seq=bsz, n_blk=1,
              qr=SAMPLE_QROWS, fresh=False, per_step=SAMPLE_SEQS_PER_STEP)
    router = (prm['w_router'], prm['b_router'])
    x1, *routing = _outproj_ln(ya, yb, x0, prm['w_out'][0], prm['ln_mix_g'][0], prm['ln_mix_b'][0], tm=bsz,
                               router=router)
    x2 = _moe(x1, routing, prm, 0, tm=bsz)
    xcat = jnp.concatenate([st_pool[0], x2[:, None]], axis=1)
    x3 = _pool_ln(xcat.reshape(bsz * HALO, D_MODEL), prm['w_pool'][0], prm['pool_scale'][0],
                  prm['ln_mix_g'][1], prm['ln_mix_b'][1], tm=min(TOKEN_TILE, bsz * HALO), seq_len=HALO,
                  start_pos=PAST_LEN - POOL_KEEP)
    x3 = x3.reshape(bsz, HALO, D_MODEL)[:, -1]
    x4 = _moe(x3, _router(x3, *router, tm=bsz), prm, 1, tm=bsz)
    k_new = jnp.concatenate([c_k[0][:, 1:], swa[:, D_B:D_B + KV_COLS].reshape(bsz, 1, KV_HEADS, HEAD)], axis=1)
    v_new = jnp.concatenate([c_v[0][:, 1:], swa[:, D_B + KV_COLS:].reshape(bsz, 1, KV_HEADS, HEAD)], axis=1)
    return (x4.reshape(bsz, 1, D_MODEL), s_new[None], x[:, -1][None], k_new[None], v_new[None],
            xcat[:, 1:][None])


def _prepare_params(raw):
    prm = dict(raw)
    prm['r_k'] = raw['r_k'].reshape(raw['r_k'].shape[0], -1)
    prm['wg'] = _group_experts(raw['w_ex_gate'])
    prm['wu'] = _group_experts(raw['w_ex_up'])
    prm['wd'] = raw['w_ex_down'].astype(BF16).reshape(DEPTH, N_GROUPS, EXPERTS_PER_GROUP * D_FF, D_MODEL)
    return prm


def kernel(x_prompt, x_sample, state_rwkv, state_shift, cache_swa_k, cache_swa_v, state_pool, w_in, tshift_mu,
           decay_w0, decay_w2, iclr_a0, iclr_a2, gate_w2, k_k, k_a, r_k, lnx_g, lnx_b, attn_sinks, rel_bias, w_out,
           w_pool, pool_scale, ln_mix_g, ln_mix_b, ln_ffn_g, ln_ffn_b, w_router, b_router, w_ex_gate, w_ex_up,
           w_ex_down):
    prm = _prepare_params(dict(
        w_in=w_in, tshift_mu=tshift_mu, decay_w0=decay_w0, decay_w2=decay_w2, iclr_a0=iclr_a0, iclr_a2=iclr_a2,
        gate_w2=gate_w2, k_k=k_k, k_a=k_a, r_k=r_k, lnx_g=lnx_g, lnx_b=lnx_b, attn_sinks=attn_sinks,
        rel_bias=rel_bias, w_out=w_out, w_pool=w_pool, pool_scale=pool_scale, ln_mix_g=ln_mix_g, ln_mix_b=ln_mix_b,
        ln_ffn_g=ln_ffn_g, ln_ffn_b=ln_ffn_b, w_router=w_router, b_router=b_router, w_ex_gate=w_ex_gate,
        w_ex_up=w_ex_up, w_ex_down=w_ex_down))
    y_p, rwkv_p, shift_p, k_p, v_p, pool_p = _prompt_trunk(x_prompt, prm)
    y_s, rwkv_s, shift_s, k_s, v_s, pool_s = _sample_trunk(x_sample, state_rwkv, state_shift, cache_swa_k,
                                                            cache_swa_v, state_pool, prm)
    return (y_p, y_s, rwkv_p, rwkv_s, shift_p, shift_s, k_p, k_s, v_p, v_s, pool_p, pool_s)
```

```python
import functools
import math

import numpy as np
import jax
import jax.numpy as jnp
from jax import lax
from jax.experimental import pallas as pl
from jax.experimental.pallas import tpu as pltpu

F32 = jnp.float32
BF16 = jnp.bfloat16

D_MODEL = 1024
DEPTH = 2
PAST_LEN = 16384
D_A = 512
HEAD = 64
N_HEADS = 8
W_LORA, A_LORA, G_LORA = 32, 64, 96
RWKV_COLS = 3 * D_A + W_LORA + A_LORA + G_LORA
D_B = 512
KV_HEADS = 2
Q_PER_KV = 4
KV_COLS = KV_HEADS * HEAD
SWA_COLS = D_B + 2 * KV_COLS
WINDOW = 128
N_BUCKETS = 32
MAX_EXACT = 16
REL_MAX_DIST = 128
POOL_WINDOWS = (2, 4, 8, 16)
POOL_GROUP = 256
POOL_KEEP = 15
N_EXPERTS = 16
N_GROUPS = 4
EXPERTS_PER_GROUP = 4
D_FF = 256
ALPHA = (2.0 * DEPTH) ** 0.25
LN_EPS = 1e-5
LNX_EPS = 64e-5
NEG_INF = -1e30

LANES = 128
SUBLANES = 8
PACK = 16
PREP_ROWS = 128
EPILOGUE_ROWS = 256
VMEM_LIMIT_BYTES = 56 * 1024 * 1024

LORA_PAD = LANES
RWKV_PAD_COLS = 3 * D_A + 3 * LORA_PAD
CHUNK = 64

NN = ((1,), (0,))
NT = ((1,), (1,))
TN = ((0,), (0,))


def _dg(a, b, dims=NN):
    return lax.dot_general(a, b, (dims, ((), ())), preferred_element_type=F32)


def _bdot(a, b, dims=NN):
    return _dg(a.astype(BF16), b.astype(BF16), dims)


def _split(x, n):
    parts, rem = [], x
    for i in range(n):
        p = rem.astype(BF16)
        parts.append(p)
        if i + 1 < n:
            rem = rem - p.astype(F32)
    return parts


def _dot_exact_rhs(x, ones_bf16, n=3):
    out = None
    for p in _split(x, n):
        t = _dg(p, ones_bf16)
        out = t if out is None else out + t
    return out


def _dot_exact_lhs(ones_bf16, x, n=3):
    out = None
    for p in _split(x, n):
        t = _dg(ones_bf16, p)
        out = t if out is None else out + t
    return out


def _sigmoid(x):
    return 1.0 / (1.0 + jnp.exp(-x))


def _layer_norm(z, g, b):
    mu = jnp.mean(z, axis=-1, keepdims=True)
    d = z - mu
    var = jnp.mean(d * d, axis=-1, keepdims=True)
    return d * lax.rsqrt(var + LN_EPS) * g + b


def _params(sem):
    return pltpu.CompilerParams(dimension_semantics=sem, vmem_limit_bytes=VMEM_LIMIT_BYTES)


def _full(shape):
    nd = len(shape)
    return pl.BlockSpec(shape, lambda *_: (0,) * nd)


def _inproj_prep_body(*refs, tm, tiles_per_seq, chunk, has_prev):
    if has_prev:
        x_ref, xp_ref = refs[0], refs[1]
        refs = refs[2:]
    else:
        x_ref, xp_ref = refs[0], None
        refs = refs[1:]
    (wr_ref, ws_ref, mu_ref, w0_ref, ww2_ref, a0_ref, wa2_ref, wg2_ref, kk_ref, ka_ref, rk_ref,
     ones_ref, tri_ref,
     rt_ref, kt_ref, bt_ref, at_ref, v_ref, bonus_ref, g_ref, gl_ref, swa_ref, carry_ref) = refs

    i = pl.program_id(0)
    sub = tri_ref.shape[0]
    assert tm % sub == 0 and sub % chunk == 0
    blocks = [slice(r0, r0 + sub) for r0 in range(0, tm, sub)]
    st = [dict() for _ in blocks]
    ones_bd = ones_ref[...]
    half = D_A // 2

    if not has_prev:
        @pl.when(i == 0)
        def _():
            carry_ref[...] = jnp.zeros_like(carry_ref)

        first = (i % tiles_per_seq) == 0
        carried = jnp.where(first, 0.0, carry_ref[0:1, :])
        row = lax.broadcasted_iota(jnp.int32, (sub, 1), 0)

    def project(u):
        b_ = blocks[u]
        xb = x_ref[b_, :].astype(BF16)
        pr = _dg(xb, wr_ref[...])
        swa_ref[b_, :] = _dg(xb, ws_ref[...])
        if has_prev:
            shifted = _dg(xp_ref[b_, :].astype(BF16), wr_ref[...])
        else:
            before = carried if u == 0 else st[u - 1]['last']
            shifted = jnp.where(row == 0, before, pltpu.roll(pr, 1, axis=0))
            st[u]['last'] = pr[sub - 1:sub, :]
        st[u]['m'] = pr + mu_ref[...] * (shifted - pr)

    def low_rank(u):
        m = st[u].pop('m')
        s = st[u]
        s['r'], k, s['v'] = m[:, 0:D_A], m[:, D_A:2 * D_A], m[:, 2 * D_A:3 * D_A]
        wd, ad, gd = (m[:, 3 * D_A + j * LORA_PAD:3 * D_A + (j + 1) * LORA_PAD] for j in range(3))
        nz = -(w0_ref[...] + _bdot(jnp.tanh(wd), ww2_ref[...]))
        softplus = jnp.maximum(nz, 0.0) + jnp.log1p(jnp.exp(-jnp.abs(nz)))
        s['logw'] = -jnp.exp(-softplus - 0.5)
        s['a'] = _sigmoid(a0_ref[...] + _bdot(ad, wa2_ref[...]))
        s['g'] = _bdot(_sigmoid(gd), wg2_ref[...])
        s['kkr'] = k * kk_ref[...]
        s['k2'] = k * (1.0 + (s['a'] - 1.0) * ka_ref[...])

    def head_sums(z):
        return jnp.concatenate([_dot_exact_rhs(z[:, :half], ones_bd, 2), _dot_exact_rhs(z[:, half:], ones_bd, 2)],
                               axis=1)

    def sums(u):
        s = st[u]
        s['sq'] = head_sums(s['kkr'] * s['kkr'])
        s['rk'] = head_sums(s['r'] * s['k2'] * rk_ref[...])
        s['cum'] = _dot_exact_lhs(tri_ref[...], s['logw'], 2) if chunk > 1 else s['logw']

    def store(u):
        s, b_ = st[u], blocks[u]
        kk = s['kkr'] / jnp.maximum(jnp.sqrt(s['sq']), 1e-12)
        gam = jnp.exp(s['cum'])
        inv = jnp.exp(-s['cum'])
        bonus_ref[b_, :] = s['rk'] * s['v']
        rt_ref[b_, :] = s['r'] * gam
        kt_ref[b_, :] = s['k2'] * inv
        bt_ref[b_, :] = kk * s['a'] * inv
        at_ref[b_, :] = -kk * jnp.exp(s['cum'] - s['logw'])
        v_ref[b_, :] = s['v']
        g_ref[b_, :] = s['g']
        if chunk > 1:
            for c in range(sub // chunk):
                gl_ref[u * (sub // chunk) + c] = gam[(c + 1) * chunk - 1:(c + 1) * chunk, :]
        else:
            gl_ref[b_, :] = gam

    stages = (project, low_rank, sums, store)
    for t in range(len(blocks) + len(stages) - 1):
        for depth, stage in enumerate(stages):
            if 0 <= t - depth < len(blocks):
                stage(t - depth)
    if not has_prev:
        carry_ref[0:1, :] = st[-1]['last']


def _inproj_prep(x2d, x_prev, wts, *, seq_len, chunk, tm):
    n = x2d.shape[0]
    assert n % tm == 0 and (x_prev is not None or seq_len % tm == 0)
    assert tm % chunk == 0
    has_prev = x_prev is not None
    row = lambda w: pl.BlockSpec((tm, w), lambda i: (i, 0))
    ins = [x2d] + ([x_prev] if has_prev else [])
    in_specs = [row(D_MODEL)] + ([row(D_MODEL)] if has_prev else [])
    consts = [wts['wr'], wts['ws'], wts['mu'], wts['w0'], wts['ww2'], wts['a0'], wts['wa2'], wts['wg2'],
              wts['k_k'], wts['k_a'], wts['r_k'], _ones_block_diag(D_A // 2, HEAD), _tri_block_diag(min(tm, PREP_ROWS), chunk)]
    ins += consts
    in_specs += [_full(c.shape) for c in consts]
    if chunk > 1:
        gl_shape = jax.ShapeDtypeStruct((n // chunk, 1, D_A), F32)
        gl_spec = pl.BlockSpec((tm // chunk, 1, D_A), lambda i: (i, 0, 0))
    else:
        gl_shape = jax.ShapeDtypeStruct((n, D_A), F32)
        gl_spec = row(D_A)
    out_shape = [jax.ShapeDtypeStruct((n, D_A), F32)] * 7 + [gl_shape, jax.ShapeDtypeStruct((n, SWA_COLS), F32)]
    out_specs = [row(D_A)] * 7 + [gl_spec, row(SWA_COLS)]
    body = functools.partial(_inproj_prep_body, tm=tm, tiles_per_seq=max(seq_len // tm, 1), chunk=chunk,
                             has_prev=has_prev)
    return pl.pallas_call(
        body, grid=(n // tm,), in_specs=in_specs, out_specs=out_specs, out_shape=out_shape,
        scratch_shapes=[pltpu.VMEM((SUBLANES, RWKV_PAD_COLS), F32)],
        compiler_params=_params(("arbitrary",)), name="inproj_prep",
    )(*ins)


def _ones_block_diag(n, blk):
    idx = np.arange(n) // blk
    return jnp.asarray(idx[:, None] == idx[None, :], dtype=BF16)


def _tri_block_diag(n, blk):
    idx = np.arange(n)
    same = (idx[:, None] // blk) == (idx[None, :] // blk)
    return jnp.asarray(same & (idx[:, None] >= idx[None, :]), dtype=BF16)


PAIR = 2 * HEAD
N_PAIRS = N_HEADS // 2


def _scan_body(rt_ref, kt_ref, bt_ref, at_ref, v_ref, bonus_ref, g_ref, gl_ref, lng_ref, lnb_ref, s0_ref,
               y_ref, sout_ref, S_ref, *, C, nc, one_chunk_seqs):
    step = pl.program_id(1)

    def pair_state(ref, s, p):
        return jnp.concatenate([ref[s, 2 * p], ref[s, 2 * p + 1]], axis=1)

    if not one_chunk_seqs:
        @pl.when(step == 0)
        def _():
            for p in range(N_PAIRS):
                S_ref[p] = pair_state(s0_ref, 0, p)

    head_a = lax.broadcasted_iota(jnp.int32, (1, PAIR), 1) < HEAD
    rowc = lax.broadcasted_iota(jnp.int32, (C, 2 * C), 0)
    colc = lax.broadcasted_iota(jnp.int32, (C, 2 * C), 1)
    first_c = colc < C
    col_in = jnp.where(first_c, colc, colc - C)
    strict = rowc > col_in
    incl = rowc >= col_in
    eye_c = (rowc == col_in).astype(F32)
    r128 = lax.broadcasted_iota(jnp.int32, (PAIR, PAIR), 0)
    c128 = lax.broadcasted_iota(jnp.int32, (PAIR, PAIR), 1)
    same_head = (r128 < HEAD) == (c128 < HEAD)
    eye128 = (r128 == c128).astype(F32)

    def by_head(x):
        xb = x.astype(BF16)
        z = jnp.zeros_like(xb)
        return jnp.concatenate([jnp.where(head_a, xb, z), jnp.where(head_a, z, xb)], axis=0)

    def head_mean(z):
        za = jnp.sum(jnp.where(head_a, z, 0.0), axis=-1, keepdims=True)
        zb = jnp.sum(jnp.where(head_a, 0.0, z), axis=-1, keepdims=True)
        return jnp.where(head_a, za, zb) * (1.0 / HEAD)

    n_fold = int(math.log2(C)) - 1
    zero_blk = jnp.zeros((2 * C, PAIR), BF16)
    items = [(ci, p) for ci in range(nc) for p in range(N_PAIRS)]

    def load(ref, ci, p):
        if one_chunk_seqs:
            pad = jnp.zeros((C - 1, PAIR), F32)
            return jnp.concatenate([ref[ci:ci + 1, p * PAIR:(p + 1) * PAIR], pad], axis=0)
        return ref[ci * C:(ci + 1) * C, p * PAIR:(p + 1) * PAIR]

    def blocks(x):
        xb = x.astype(BF16)
        z = jnp.zeros_like(xb)
        return jnp.concatenate([jnp.where(first_c, xb, z), jnp.where(first_c, z, xb)], axis=0)

    st = [dict() for _ in items]

    def products(n):
        ci, p = items[n]
        s = st[n]
        R, K, Bm, A, V = (load(ref, ci, p) for ref in (rt_ref, kt_ref, bt_ref, at_ref, v_ref))
        big = _dg(jnp.concatenate([A, R], axis=0).astype(BF16),
                  jnp.concatenate([by_head(Bm), by_head(K)], axis=0), NT)
        s.update(R=R, K=K, Bm=Bm, A=A, V=V,
                 a_ab=jnp.where(strict, big[:C, :2 * C], 0.0), a_ak=jnp.where(strict, big[:C, 2 * C:], 0.0),
                 l_rb=jnp.where(incl, big[C:, :2 * C], 0.0), l_rk=jnp.where(incl, big[C:, 2 * C:], 0.0))

    def square(n):
        s = st[n]
        a_ab = s.pop('a_ab')
        s['X'] = _dg(a_ab.astype(BF16), blocks(a_ab))
        s['T'] = eye_c + a_ab
        s['akv'] = _dg(s.pop('a_ak').astype(BF16), by_head(s['V']))

    def fold(n):
        s = st[n]
        res = _dg(jnp.concatenate([s['X'], s['T']], axis=0).astype(BF16), blocks(s['X']))
        s['X'], s['T'] = res[:C], s['T'] + res[C:]

    def last_fold(n):
        s = st[n]
        s['T'] = s['T'] + _dg(s['T'].astype(BF16), blocks(s.pop('X')))

    def solve(n):
        s = st[n]
        pq = _dg(s.pop('T').astype(BF16), jnp.concatenate([by_head(s['A']), by_head(s.pop('akv'))], axis=1))
        s['P'], s['Q'] = pq[:, :PAIR], pq[:, PAIR:]

    def readout(n):
        s = st[n]
        P, Q, V = s.pop('P'), s.pop('Q'), s.pop('V')
        wz = _dg(jnp.concatenate([s.pop('l_rb'), s.pop('l_rk')], axis=1).astype(BF16),
                 jnp.concatenate([jnp.concatenate([by_head(P), by_head(Q)], axis=1),
                                  jnp.concatenate([zero_blk, by_head(V)], axis=1)], axis=0))
        s['W'], s['Z'] = s.pop('R') + wz[:, :PAIR], wz[:, PAIR:]
        Bm, K = s.pop('Bm'), s.pop('K')
        s['ptb'] = _bdot(P, Bm, TN)
        s['gfull'] = _bdot(jnp.concatenate([Q, V], axis=0), jnp.concatenate([Bm, K], axis=0), TN)
        del s['A']

    def put_state(s, p, val):
        sout_ref[s, 2 * p] = val[:, :HEAD]
        sout_ref[s, 2 * p + 1] = val[:, HEAD:]

    state = [None if one_chunk_seqs else S_ref[p] for p in range(N_PAIRS)]

    def advance(n):
        ci, p = items[n]
        s = st[n]
        rows = slice(ci, ci + 1) if one_chunk_seqs else slice(ci * C, (ci + 1) * C)
        ln = slice(p * PAIR, (p + 1) * PAIR)
        g_c = gl_ref[ci, :, ln]
        gfull = s.pop('gfull')
        M = (eye128 + jnp.where(same_head, s.pop('ptb'), 0.0)) * g_c
        G = jnp.where(head_a, gfull[:HEAD], gfull[HEAD:]) * g_c
        S0 = pair_state(s0_ref, ci, p) if one_chunk_seqs else state[p]
        Y = _dg(s.pop('W').astype(BF16), by_head(S0), NT) + s.pop('Z')
        S1 = _bdot(S0, M) + G
        if one_chunk_seqs:
            put_state(ci, p, S1)
        else:
            state[p] = S1

        if one_chunk_seqs:
            Y = Y[0:1]
        mu = head_mean(Y)
        d = Y - mu
        var = head_mean(d * d)
        yn = d * lax.rsqrt(var + LNX_EPS) * lng_ref[:, ln] + lnb_ref[:, ln]
        y_ref[rows, ln] = ((yn + bonus_ref[rows, ln]) * g_ref[rows, ln]).astype(y_ref.dtype)

    for stage in (products, square) + (fold,) * (n_fold - 1) + (last_fold, solve, readout, advance):
        for n in range(len(items)):
            stage(n)

    if not one_chunk_seqs:
        for p in range(N_PAIRS):
            S_ref[p] = state[p]

        @pl.when(step == pl.num_programs(1) - 1)
        def _():
            for p in range(N_PAIRS):
                put_state(0, p, state[p])


def _rwkv_scan(rt, kt, bt, at, v, bonus, g, gl, lnx_g, lnx_b, s0, *, n_seq, seq_len, chunk, nc):
    one_chunk_seqs = seq_len == chunk
    if one_chunk_seqs:
        assert n_seq % nc == 0
        grid = (n_seq // nc, 1)
        blk = lambda b, c: b
        st = pl.BlockSpec((nc, N_HEADS, HEAD, HEAD), lambda b, c: (b, 0, 0, 0))
        rows, n_rows = nc, n_seq
    else:
        rows, n_rows = chunk * nc, n_seq * seq_len
        assert seq_len % rows == 0
        nsteps = seq_len // rows
        grid = (n_seq, nsteps)
        blk = lambda b, c: b * nsteps + c
        st = pl.BlockSpec((1, N_HEADS, HEAD, HEAD), lambda b, c: (b, 0, 0, 0))
    row = pl.BlockSpec((rows, D_A), lambda b, c: (blk(b, c), 0))
    vec = pl.BlockSpec((1, D_A), lambda b, c: (0, 0))
    return pl.pallas_call(
        functools.partial(_scan_body, C=chunk, nc=nc, one_chunk_seqs=one_chunk_seqs),
        grid=grid,
        in_specs=[row] * 7 + [pl.BlockSpec((nc, 1, D_A), lambda b, c: (blk(b, c), 0, 0)), vec, vec, st],
        out_specs=[row, st],
        out_shape=[jax.ShapeDtypeStruct((n_rows, D_A), BF16 if n_rows % PACK == 0 and rows % PACK == 0 else F32),
                   jax.ShapeDtypeStruct((n_seq, N_HEADS, HEAD, HEAD), F32)],
        scratch_shapes=[pltpu.VMEM((N_PAIRS, HEAD, PAIR), F32)],
        compiler_params=_params(("arbitrary", "arbitrary")), name="rwkv_scan",
    )(rt, kt, bt, at, v, bonus, g, gl, lnx_g, lnx_b, s0)


def _bucket_ranges():
    d = np.arange(WINDOW + 1)
    scaled = np.log(np.maximum(d, MAX_EXACT).astype(np.float32) / MAX_EXACT) / math.log(REL_MAX_DIST / MAX_EXACT)
    large = np.minimum(MAX_EXACT + (scaled * (N_BUCKETS - MAX_EXACT)).astype(np.int32), N_BUCKETS - 1)
    bucket = np.where(d < MAX_EXACT, d, large)
    frac = scaled.astype(np.float64) * (N_BUCKETS - MAX_EXACT)
    near = np.abs(frac - np.round(frac)) < 1e-3
    assert all(int(x) in (MAX_EXACT, REL_MAX_DIST) for x in d[(d >= MAX_EXACT) & near])
    ranges = []
    for b in range(N_BUCKETS):
        hit = d[bucket == b]
        ranges.append((int(hit.min()), int(hit.max())) if hit.size else None)
    return tuple(ranges)


def _swa_body(rb_ref, sink_ref, cur_ref, *rest, qr, fresh, per_step, ranges):
    prevs, (y_ref, bias_ref, sinkcol_ref) = rest[:-3], rest[-3:]
    W = WINDOW
    nk = W + qr

    @pl.when((pl.program_id(0) == 0) & (pl.program_id(1) == 0))
    def _():
        qi = lax.broadcasted_iota(jnp.int32, (qr, nk), 0)
        kj = lax.broadcasted_iota(jnp.int32, (qr, nk), 1)
        d = qi + W - kj
        valid = (d >= 0) & (d <= W)
        for hd in range(N_HEADS):
            t = jnp.zeros((qr, nk), F32)
            for bkt, rng in enumerate(ranges):
                if rng is not None:
                    t = jnp.where((d >= rng[0]) & (d <= rng[1]), rb_ref[bkt * N_HEADS + hd], t)
            kvh, grp = hd // Q_PER_KV, hd % Q_PER_KV
            rows = slice(grp * qr, (grp + 1) * qr)
            bias_ref[kvh, rows, :] = jnp.where(valid, t, NEG_INF)
            bias_ref[KV_HEADS + kvh, rows, :] = jnp.where(valid & (kj >= W), t, NEG_INF)
            sinkcol_ref[kvh, rows, :] = jnp.full((qr, 1), sink_ref[hd], F32)

    first_table = jnp.where(pl.program_id(1) == 0, KV_HEADS, 0) if fresh else 0
    items = [(sub, kvh) for sub in range(per_step) for kvh in range(KV_HEADS)]
    each = lambda f, *lists: [f(*xs) for xs in zip(*lists)]

    def own_rows(sub, col):
        if fresh:
            return cur_ref[sub * qr:(sub + 1) * qr, col:col + HEAD]
        return jnp.concatenate([cur_ref[sub:sub + 1, col:col + HEAD], jnp.zeros((qr - 1, HEAD), F32)], axis=0)

    def kv_rows(sub, kvh, is_v):
        col = D_B + is_v * KV_COLS + kvh * HEAD
        if not fresh:
            before = prevs[is_v][sub, :, kvh, :]
        elif sub > 0:
            before = cur_ref[(sub - 1) * qr:sub * qr, col:col + HEAD]
        else:
            before = prevs[0][:, col - D_B:col - D_B + HEAD]
        return jnp.concatenate([before, own_rows(sub, col)], axis=0).astype(BF16)

    kb = [kv_rows(sub, kvh, 0) for sub, kvh in items]
    vb = [kv_rows(sub, kvh, 1) for sub, kvh in items]
    q4 = [(jnp.concatenate([own_rows(sub, (kvh * Q_PER_KV + gq) * HEAD) for gq in range(Q_PER_KV)], axis=0)
           * (HEAD ** -0.5)).astype(BF16) for sub, kvh in items]
    s = [_dg(q, k, NT) + bias_ref[(first_table if sub == 0 else 0) + kvh] for q, k, (sub, kvh) in zip(q4, kb, items)]
    sink = [sinkcol_ref[kvh] for _, kvh in items]
    m = each(lambda s_, sk: jnp.maximum(jnp.max(s_, axis=-1, keepdims=True), sk), s, sink)
    p = each(lambda s_, m_: jnp.exp(s_ - m_).astype(BF16), s, m)
    pad = jnp.concatenate([jnp.zeros((nk, LANES - HEAD), BF16), jnp.ones((nk, LANES), BF16)], axis=1)
    ov = each(lambda p_, v_: _dg(p_, jnp.concatenate([v_, pad], axis=1)), p, vb)
    o = each(lambda ov_, sk, m_: ov_[:, :LANES] / (ov_[:, LANES:] + jnp.exp(sk - m_)), ov, sink, m)
    for sub in range(per_step):
        outs = [o[sub * KV_HEADS + kvh][gq * qr:(gq + 1) * qr, :HEAD]
                for kvh in range(KV_HEADS) for gq in range(Q_PER_KV)]
        y = jnp.concatenate(outs, axis=1).astype(y_ref.dtype)
        if fresh:
            y_ref[sub * qr:(sub + 1) * qr, :] = y
        else:
            y_ref[sub:sub + 1, :] = y[0:1]


def _swa(cur2d, prevs, rel_bias, sinks, *, n_seq, n_blk, qr, fresh, per_step=1):
    smem = pl.BlockSpec(memory_space=pltpu.SMEM)
    if fresh:
        assert qr == WINDOW and n_blk % per_step == 0
        blocks = n_blk
        n_blk //= per_step
        prev_specs = [pl.BlockSpec((WINDOW, 2 * KV_COLS),
                                   lambda b, j: (b * blocks + jnp.maximum(j * per_step - 1, 0), 2))]
    else:
        assert n_blk == 1 and n_seq % per_step == 0
        n_seq //= per_step
        prev_specs = [pl.BlockSpec((per_step, WINDOW, KV_HEADS, HEAD), lambda b, j: (b, 0, 0, 0))] * 2
    qrows = per_step * qr if fresh else per_step
    out_dtype = BF16 if fresh else F32
    return pl.pallas_call(
        functools.partial(_swa_body, qr=qr, fresh=fresh, per_step=per_step, ranges=_bucket_ranges()),
        grid=(n_seq, n_blk),
        in_specs=[smem, smem, pl.BlockSpec((qrows, SWA_COLS), lambda b, j: (b * n_blk + j, 0))] + prev_specs,
        out_specs=pl.BlockSpec((qrows, D_B), lambda b, j: (b * n_blk + j, 0)),
        out_shape=jax.ShapeDtypeStruct((n_seq * n_blk * qrows, D_B), out_dtype),
        scratch_shapes=[pltpu.VMEM((2 * KV_HEADS, Q_PER_KV * qr, WINDOW + qr), F32),
                        pltpu.VMEM((KV_HEADS, Q_PER_KV * qr, 1), F32)],
        compiler_params=_params(("arbitrary", "arbitrary")), name="swa",
    )(rel_bias.reshape(-1), sinks, cur2d, *prevs)


def _outproj_ln_body(ya_ref, yb_ref, x_ref, wa_ref, wb_ref, g_ref, b_ref, *rest):
    o_ref = rest[0] if len(rest) == 1 else rest[N_ROUTE_INPUTS]
    tm = x_ref.shape[0]
    step = min(tm, EPILOGUE_ROWS)
    parts = []
    for r0 in range(0, tm, step):
        rows = slice(r0, r0 + step)
        h = _dg(ya_ref[rows, :].astype(BF16), wa_ref[...]) + _dg(yb_ref[rows, :].astype(BF16), wb_ref[...])
        out = _layer_norm(ALPHA * x_ref[rows, :] + h, g_ref[...], b_ref[...])
        o_ref[rows, :] = out
        if len(rest) > 1:
            parts.append(_route_tokens(out, rest[0], rest[1]))
    if len(rest) > 1:
        _route_tile(parts, rest[2], *rest[N_ROUTE_INPUTS + 1:])


def _outproj_ln(ya, yb, x2d, w_out, ln_g, ln_b, *, tm, router=None):
    n = x2d.shape[0]
    half = pl.BlockSpec((tm, D_A), lambda i: (i, 0))
    full = pl.BlockSpec((tm, D_MODEL), lambda i: (i, 0))
    wa, wb = w_out[:D_A].astype(BF16), w_out[D_A:].astype(BF16)
    ins = [ya, yb, x2d, wa, wb, ln_g.reshape(1, -1), ln_b.reshape(1, -1)]
    in_specs = [half, half, full, _full(wa.shape), _full(wb.shape), _full((1, D_MODEL)), _full((1, D_MODEL))]
    out_specs, out_shape = [full], [jax.ShapeDtypeStruct((n, D_MODEL), F32)]
    if router is not None:
        r_ins, r_specs = _route_operands(*router, tm)
        ins, in_specs = ins + r_ins, in_specs + r_specs
        r_out_specs, r_out_shape = _route_outputs(n, tm)
        out_specs, out_shape = out_specs + r_out_specs, out_shape + r_out_shape
    res = pl.pallas_call(
        _outproj_ln_body, grid=(n // tm,), in_specs=in_specs, out_specs=out_specs, out_shape=out_shape,
        compiler_params=_params(("arbitrary",)), name="outproj_ln",
    )(*ins)
    return res[0] if router is None else res


def _route_tokens(x, wr_ref, br_ref):
    logits = jnp.transpose(_dg(x.astype(BF16), wr_ref[...]))[:N_EXPERTS] + br_ref[...]
    z = jnp.exp(logits - jnp.max(logits, axis=0, keepdims=True))
    probs = z / jnp.sum(z, axis=0, keepdims=True)
    pa = [probs[a_ * N_GROUPS:(a_ + 1) * N_GROUPS] for a_ in range(EXPERTS_PER_GROUP)]

    sel = []
    for a_ in range(EXPERTS_PER_GROUP):
        rank = None
        for j in range(EXPERTS_PER_GROUP):
            if j == a_:
                continue
            ahead = ((pa[j] >= pa[a_]) if j < a_ else (pa[j] > pa[a_])).astype(jnp.int32)
            rank = ahead if rank is None else rank + ahead
        sel.append(rank < 2)
    score = None
    for a_ in range(EXPERTS_PER_GROUP):
        t = jnp.where(sel[a_], pa[a_], 0.0)
        score = t if score is None else score + t
    srow = [score[gi:gi + 1] for gi in range(N_GROUPS)]
    best = []
    for gi in range(N_GROUPS):
        ok = None
        for j in range(N_GROUPS):
            if j == gi:
                continue
            c_ = (srow[gi] > srow[j]) if j < gi else (srow[gi] >= srow[j])
            ok = c_ if ok is None else (ok & c_)
        best.append(ok.astype(F32))
    best4 = jnp.concatenate(best, axis=0)
    chosen = best4 > 0.5
    kept = [jnp.where(sel[a_] & chosen, pa[a_], 0.0) for a_ in range(EXPERTS_PER_GROUP)]
    tot = None
    for a_ in range(EXPERTS_PER_GROUP):
        tot = kept[a_] if tot is None else tot + kept[a_]
    tot = jnp.sum(tot, axis=0, keepdims=True)
    return [jnp.where(sel[a_] & chosen, pa[a_] / tot, 0.0) for a_ in range(EXPERTS_PER_GROUP)], best4


def _route_tile(parts, tri_ref, meta_ref, slot_ref, cnt_ref):
    gates = [jnp.concatenate([g_[a_] for g_, _ in parts], axis=1) for a_ in range(EXPERTS_PER_GROUP)]
    best4 = jnp.concatenate([b_ for _, b_ in parts], axis=1)
    tm = best4.shape[1]
    onehot = jnp.concatenate([best4, jnp.zeros((SUBLANES - N_GROUPS, tm), F32)], axis=0)
    incl = _dg(onehot.astype(BF16), tri_ref[...])
    slot = None
    offset = jnp.zeros((1, 1), F32)
    lane = lax.broadcasted_iota(jnp.int32, (1, LANES), 1)
    counts = jnp.zeros((1, LANES), F32)
    for gi in range(N_GROUPS):
        t = onehot[gi:gi + 1] * (offset + incl[gi:gi + 1] - 1.0)
        slot = t if slot is None else slot + t
        total = incl[gi:gi + 1, tm - 1:tm]
        counts = jnp.where(lane == gi, total, counts)
        offset = offset + total
    slot_ref[0] = slot.astype(jnp.int32)
    cnt_ref[0] = counts
    rows = jnp.concatenate(gates + [slot, jnp.zeros((LANES - N_EXPERTS - 1, tm), F32)], axis=0)
    meta_ref[...] = jnp.transpose(rows)


META_SLOT = N_EXPERTS
N_ROUTE_INPUTS = 3


def _gate_lane(group, j):
    return j * N_GROUPS + group


def _route_operands(w_router, b_router, tm):
    idx = np.arange(tm)
    tri = jnp.asarray(idx[:, None] <= idx[None, :], dtype=BF16)
    by_member = lambda a: a.reshape(N_GROUPS, EXPERTS_PER_GROUP, -1).transpose(1, 0, 2).reshape(N_EXPERTS, -1)
    wr = jnp.pad(by_member(w_router.T).T, ((0, 0), (0, LANES - N_EXPERTS))).astype(BF16)
    ins = [wr, by_member(b_router), tri]
    return ins, [_full(a.shape) for a in ins]


def _route_outputs(n, tm):
    nt = n // tm
    specs = [pl.BlockSpec((tm, LANES), lambda i: (i, 0)), pl.BlockSpec((1, 1, tm), lambda i: (i, 0, 0)),
             pl.BlockSpec((1, 1, LANES), lambda i: (i, 0, 0))]
    shapes = [jax.ShapeDtypeStruct((n, LANES), F32), jax.ShapeDtypeStruct((nt, 1, tm), jnp.int32),
              jax.ShapeDtypeStruct((nt, 1, LANES), F32)]
    return specs, shapes


def _router_body(x_ref, wr_ref, br_ref, tri_ref, meta_ref, slot_ref, cnt_ref):
    _route_tile([_route_tokens(x_ref[...], wr_ref, br_ref)], tri_ref, meta_ref, slot_ref, cnt_ref)


def _router(x2d, w_router, b_router, *, tm):
    n = x2d.shape[0]
    ins, in_specs = _route_operands(w_router, b_router, tm)
    out_specs, out_shape = _route_outputs(n, tm)
    return pl.pallas_call(
        _router_body, grid=(n // tm,),
        in_specs=[pl.BlockSpec((tm, D_MODEL), lambda i: (i, 0))] + in_specs,
        out_specs=out_specs, out_shape=out_shape,
        compiler_params=_params(("arbitrary",)), name="router",
    )(x2d, *ins)


def _moe_sorted_body(start_ref, nwin_ref, x_ref, meta_ref, slot_ref, wg_ref, wu_ref, wd_ref, g_ref, b_ref, o_ref,
                     xs_ref, gs_ref, ys_ref, unperm_ref, *, tm, wn):
    i = pl.program_id(0)

    @pl.when(i == 0)
    def _():
        xs_ref[tm:, :] = jnp.zeros((wn, D_MODEL), BF16)
        gs_ref[tm:, :] = jnp.zeros((wn, LANES), F32)

    x = x_ref[...]
    meta = meta_ref[...]
    slot_iota = lax.broadcasted_iota(jnp.int32, (tm, tm), 0)
    perm = jnp.where(slot_iota == slot_ref[0], 1.0, 0.0).astype(BF16)
    m_hi, m_lo = _split(meta, 2)
    moved = _dg(perm, jnp.concatenate([x.astype(BF16), m_hi, m_lo], axis=1))
    xs_ref[:tm, :] = moved[:, :D_MODEL].astype(BF16)
    gs_ref[:tm, :] = moved[:, D_MODEL:D_MODEL + LANES] + moved[:, D_MODEL + LANES:]
    ys_ref[...] = jnp.zeros_like(ys_ref)
    tok_slot = meta[:, META_SLOT:META_SLOT + 1]
    lane_slot = lax.broadcasted_iota(jnp.int32, (tm, tm), 1).astype(F32)
    unperm_ref[...] = jnp.where(tok_slot == lane_slot, 1.0, 0.0).astype(BF16)

    for gi in range(N_GROUPS):
        first = start_ref[i * N_GROUPS + gi]

        def window(w, carry, gi=gi, first=first):
            st = pl.multiple_of(first + w * wn, PACK)
            xw = xs_ref[pl.ds(st, wn), :]
            gw = gs_ref[pl.ds(st, wn), :]
            hs = []
            for j in range(EXPERTS_PER_GROUP):
                e = _gate_lane(gi, j)
                hg = _dg(xw, wg_ref[gi, j])
                hu = _dg(xw, wu_ref[gi, j])
                hs.append((hg * _sigmoid(hg) * hu * gw[:, e:e + 1]).astype(BF16))
            ys_ref[pl.ds(st, wn), :] += _dg(jnp.concatenate(hs, axis=1), wd_ref[gi])
            return carry

        lax.fori_loop(0, nwin_ref[i * N_GROUPS + gi], window, 0)

    ys = ys_ref[:tm, :].astype(BF16)
    nrow = min(tm, EPILOGUE_ROWS)
    for r0 in range(0, tm, nrow):
        rows = slice(r0, r0 + nrow)
        y = _dg(unperm_ref[rows, :], ys)
        o_ref[rows, :] = _layer_norm(ALPHA * x_ref[rows, :] + y, g_ref[...], b_ref[...])


def _moe_sorted_ln(x2d, meta, slot_rows, counts, wg, wu, wd, ln_g, ln_b, *, layer, tm, wn):
    n = x2d.shape[0]
    nt = n // tm
    assert wn % PACK == 0 and tm % PACK == 0
    cnt = counts[:, 0, :N_GROUPS].astype(jnp.int32)
    offs = jnp.cumsum(cnt, axis=1) - cnt
    first = (offs // PACK) * PACK
    nwin = jnp.where(cnt > 0, (offs - first + cnt + wn - 1) // wn, 0)
    tok = lambda w: pl.BlockSpec((tm, w), lambda i, *_: (i, 0))
    resident = lambda a: pl.BlockSpec((None,) + a.shape[1:], lambda i, *_: (layer,) + (0,) * (a.ndim - 1),
                                      pipeline_mode=pl.Buffered(1))
    vec = pl.BlockSpec((1, D_MODEL), lambda i, *_: (0, 0))
    grid_spec = pltpu.PrefetchScalarGridSpec(
        num_scalar_prefetch=2, grid=(nt,),
        in_specs=[tok(D_MODEL), tok(LANES), pl.BlockSpec((1, 1, tm), lambda i, *_: (i, 0, 0)),
                  resident(wg), resident(wu), resident(wd), vec, vec],
        out_specs=tok(D_MODEL),
        scratch_shapes=[pltpu.VMEM((tm + wn, D_MODEL), BF16), pltpu.VMEM((tm + wn, LANES), F32),
                        pltpu.VMEM((tm + wn, D_MODEL), F32), pltpu.VMEM((tm, tm), BF16)])
    return pl.pallas_call(
        functools.partial(_moe_sorted_body, tm=tm, wn=wn), grid_spec=grid_spec,
        out_shape=jax.ShapeDtypeStruct((n, D_MODEL), F32),
        compiler_params=_params(("arbitrary",)), name="moe_sorted_ln",
    )(first.reshape(-1), nwin.reshape(-1), x2d, meta, slot_rows, wg, wu, wd, ln_g.reshape(1, -1), ln_b.reshape(1, -1))


HALO = 16


def _pool_ln_body(x_ref, halo_ref, wp_ref, ps_ref, g_ref, b_ref, *rest, tm, seq_len, start_pos):
    o_ref = rest[0] if len(rest) == 1 else rest[N_ROUTE_INPUTS]
    i = pl.program_id(0)
    nrow = min(tm, EPILOGUE_ROWS)
    parts = []
    for r0 in range(0, tm, nrow):
        rows = slice(r0, r0 + nrow)
        x = x_ref[rows, :]
        before = halo_ref if r0 == 0 else x_ref.at[r0 - HALO:r0, :]
        idx = lax.broadcasted_iota(jnp.int32, (nrow + HALO, 1), 0)
        pos_e = (i * tm + r0 - HALO + idx) & (seq_len - 1)
        cnt_pos = (pos_e[HALO:] + (start_pos + 1)).astype(F32)
        mixed = []
        for gi, w in enumerate(POOL_WINDOWS):
            cols = slice(gi * POOL_GROUP, (gi + 1) * POOL_GROUP)
            xg = x[:, cols]
            s = jnp.concatenate([before[:, cols], xg], axis=0)
            step = 1
            while step < w:
                s = s + jnp.where(pos_e >= step, pltpu.roll(s, step, axis=0), 0.0)
                step *= 2
            pooled = s[HALO:] / jnp.minimum(float(w), cnt_pos) - xg
            mixed.append(_bdot(pooled, wp_ref[gi]))
        h = jnp.concatenate(mixed, axis=1) * ps_ref[...]
        out = _layer_norm(ALPHA * x + h, g_ref[...], b_ref[...])
        o_ref[rows, :] = out
        if len(rest) > 1:
            parts.append(_route_tokens(out, rest[0], rest[1]))
    if len(rest) > 1:
        _route_tile(parts, rest[2], *rest[N_ROUTE_INPUTS + 1:])


def _pool_ln(x2d, w_pool, pool_scale, ln_g, ln_b, *, tm, seq_len, start_pos, router=None):
    n = x2d.shape[0]
    assert seq_len & (seq_len - 1) == 0 and seq_len >= HALO and tm % HALO == 0
    tok = pl.BlockSpec((tm, D_MODEL), lambda i: (i, 0))
    ins = [x2d, x2d, w_pool.astype(BF16), pool_scale.reshape(1, -1), ln_g.reshape(1, -1), ln_b.reshape(1, -1)]
    in_specs = [tok, pl.BlockSpec((HALO, D_MODEL), lambda i: (jnp.maximum(i * (tm // HALO) - 1, 0), 0)),
                _full(w_pool.shape), _full((1, D_MODEL)), _full((1, D_MODEL)), _full((1, D_MODEL))]
    out_specs, out_shape = [tok], [jax.ShapeDtypeStruct((n, D_MODEL), F32)]
    if router is not None:
        r_ins, r_specs = _route_operands(*router, tm)
        ins, in_specs = ins + r_ins, in_specs + r_specs
        r_out_specs, r_out_shape = _route_outputs(n, tm)
        out_specs, out_shape = out_specs + r_out_specs, out_shape + r_out_shape
    res = pl.pallas_call(
        functools.partial(_pool_ln_body, tm=tm, seq_len=seq_len, start_pos=start_pos), grid=(n // tm,),
        in_specs=in_specs, out_specs=out_specs, out_shape=out_shape,
        compiler_params=_params(("arbitrary",)), name="pool_ln",
    )(*ins)
    return res[0] if router is None else res


def _pad_lora_cols(w, width):
    return jnp.pad(w, ((0, 0), (0, LORA_PAD - width)))


def _even_layer_weights(prm, i):
    w_in = prm['w_in'][i]
    o = 3 * D_A
    cols = lambda w: jnp.concatenate(
        [w[:, :o], _pad_lora_cols(w[:, o:o + W_LORA], W_LORA),
         _pad_lora_cols(w[:, o + W_LORA:o + W_LORA + A_LORA], A_LORA),
         _pad_lora_cols(w[:, o + W_LORA + A_LORA:RWKV_COLS], G_LORA)], axis=1)
    pad_rows = lambda w: jnp.pad(w, ((0, LORA_PAD - w.shape[0]), (0, 0))).astype(BF16)
    vec = lambda a: a.reshape(1, -1)
    return dict(
        wr=cols(w_in[:, :RWKV_COLS]).astype(BF16), ws=w_in[:, RWKV_COLS:].astype(BF16),
        mu=cols(prm['tshift_mu'][i].reshape(1, -1)),
        w0=vec(prm['decay_w0'][i]), ww2=pad_rows(prm['decay_w2'][i]),
        a0=vec(prm['iclr_a0'][i]), wa2=pad_rows(prm['iclr_a2'][i]), wg2=pad_rows(prm['gate_w2'][i]),
        k_k=vec(prm['k_k'][i]), k_a=vec(prm['k_a'][i]), r_k=vec(prm['r_k'][i]),
    )


def _group_experts(w):
    return w.astype(BF16).reshape(DEPTH, N_GROUPS, EXPERTS_PER_GROUP, D_MODEL, D_FF)


TOKEN_TILE = 512
PROMPT_CHUNKS_PER_STEP = 16
PROMPT_ATTN_BLOCKS_PER_STEP = 8
SAMPLE_CHUNK = 16
SAMPLE_QROWS = SUBLANES
SAMPLE_SEQS_PER_STEP = 8


def _moe_weights(prm, layer):
    return (prm['wg'], prm['wu'], prm['wd'], prm['ln_ffn_g'][layer], prm['ln_ffn_b'][layer])


def _moe_window(tm):
    return -(-(tm // N_GROUPS) // PACK) * PACK + 2 * PACK


def _moe(x2d, routing, prm, layer, *, tm):
    return _moe_sorted_ln(x2d, *routing, *_moe_weights(prm, layer), layer=layer, tm=tm, wn=_moe_window(tm))


def _prompt_trunk(x, prm):
    bsz, t, _ = x.shape
    n = bsz * t
    x0 = x.reshape(n, D_MODEL)
    wts = _even_layer_weights(prm, 0)
    rt, kt, bt, at, v, bonus, g, gl, swa = _inproj_prep(x0, None, wts, seq_len=t, chunk=CHUNK, tm=TOKEN_TILE)
    s0 = jnp.zeros((bsz, N_HEADS, HEAD, HEAD), F32)
    ya, s_new = _rwkv_scan(rt, kt, bt, at, v, bonus, g, gl, prm['lnx_g'][0].reshape(1, -1),
                           prm['lnx_b'][0].reshape(1, -1), s0, n_seq=bsz, seq_len=t, chunk=CHUNK,
                           nc=min(PROMPT_CHUNKS_PER_STEP, t // CHUNK))
    n_blk = t // WINDOW
    qb = min(PROMPT_ATTN_BLOCKS_PER_STEP, n_blk)
    yb = _swa(swa, (swa,), prm['rel_bias'], prm['attn_sinks'][0], n_seq=bsz, n_blk=n_blk, qr=WINDOW, fresh=True,
              per_step=qb)
    tm = TOKEN_TILE
    assert n % tm == 0
    router = (prm['w_router'], prm['b_router'])
    x1, *routing = _outproj_ln(ya, yb, x0, prm['w_out'][0], prm['ln_mix_g'][0], prm['ln_mix_b'][0], tm=tm,
                               router=router)
    x2 = _moe(x1, routing, prm, 0, tm=tm)
    x3, *routing = _pool_ln(x2, prm['w_pool'][0], prm['pool_scale'][0], prm['ln_mix_g'][1], prm['ln_mix_b'][1],
                            tm=tm, seq_len=t, start_pos=0, router=router)
    x4 = _moe(x3, routing, prm, 1, tm=tm)
    swa3 = swa.reshape(bsz, t, SWA_COLS)
    k_new = swa3[:, t - WINDOW:, D_B:D_B + KV_COLS].reshape(bsz, WINDOW, KV_HEADS, HEAD)
    v_new = swa3[:, t - WINDOW:, D_B + KV_COLS:].reshape(bsz, WINDOW, KV_HEADS, HEAD)
    pool_new = x2.reshape(bsz, t, D_MODEL)[:, t - POOL_KEEP:]
    return (x4.reshape(bsz, t, D_MODEL), s_new[None], x[:, -1][None], k_new[None], v_new[None], pool_new[None])


def _sample_trunk(x, st_rwkv, st_shift, c_k, c_v, st_pool, prm):
    bsz = x.shape[0]
    x0 = x.reshape(bsz, D_MODEL)
    wts = _even_layer_weights(prm, 0)
    outs = _inproj_prep(x0, st_shift[0], wts, seq_len=1, chunk=1, tm=bsz)
    gl, swa = outs[7], outs[8]
    ya, s_new = _rwkv_scan(*outs[:7], gl.reshape(bsz, 1, D_A), prm['lnx_g'][0].reshape(1, -1),
                           prm['lnx_b'][0].reshape(1, -1), st_rwkv[0], n_seq=bsz, seq_len=SAMPLE_CHUNK,
                           chunk=SAMPLE_CHUNK, nc=SAMPLE_SEQS_PER_STEP)
    yb = _swa(swa, (c_k[0], c_v[0]), prm['rel_bias'], prm['attn_sinks'][0], n_seq=bsz, n_blk=1,
              qr=SAMPLE_QROWS, fresh=False, per_step=SAMPLE_SEQS_PER_STEP)
    router = (prm['w_router'], prm['b_router'])
    x1, *routing = _outproj_ln(ya, yb, x0, prm['w_out'][0], prm['ln_mix_g'][0], prm['ln_mix_b'][0], tm=bsz,
                               router=router)
    x2 = _moe(x1, routing, prm, 0, tm=bsz)
    xcat = jnp.concatenate([st_pool[0], x2[:, None]], axis=1)
    x3 = _pool_ln(xcat.reshape(bsz * HALO, D_MODEL), prm['w_pool'][0], prm['pool_scale'][0],
                  prm['ln_mix_g'][1], prm['ln_mix_b'][1], tm=min(TOKEN_TILE, bsz * HALO), seq_len=HALO,
                  start_pos=PAST_LEN - POOL_KEEP)
    x3 = x3.reshape(bsz, HALO, D_MODEL)[:, -1]
    x4 = _moe(x3, _router(x3, *router, tm=bsz), prm, 1, tm=bsz)
    k_new = jnp.concatenate([c_k[0][:, 1:], swa[:, D_B:D_B + KV_COLS].reshape(bsz, 1, KV_HEADS, HEAD)], axis=1)
    v_new = jnp.concatenate([c_v[0][:, 1:], swa[:, D_B + KV_COLS:].reshape(bsz, 1, KV_HEADS, HEAD)], axis=1)
    return (x4.reshape(bsz, 1, D_MODEL), s_new[None], x[:, -1][None], k_new[None], v_new[None],
            xcat[:, 1:][None])


def _prepare_params(raw):
    prm = dict(raw)
    prm['r_k'] = raw['r_k'].reshape(raw['r_k'].shape[0], -1)
    prm['wg'] = _group_experts(raw['w_ex_gate'])
    prm['wu'] = _group_experts(raw['w_ex_up'])
    prm['wd'] = raw['w_ex_down'].astype(BF16).reshape(DEPTH, N_GROUPS, EXPERTS_PER_GROUP * D_FF, D_MODEL)
    return prm


def kernel(x_prompt, x_sample, state_rwkv, state_shift, cache_swa_k, cache_swa_v, state_pool, w_in, tshift_mu,
           decay_w0, decay_w2, iclr_a0, iclr_a2, gate_w2, k_k, k_a, r_k, lnx_g, lnx_b, attn_sinks, rel_bias, w_out,
           w_pool, pool_scale, ln_mix_g, ln_mix_b, ln_ffn_g, ln_ffn_b, w_router, b_router, w_ex_gate, w_ex_up,
           w_ex_down):
    prm = _prepare_params(dict(
        w_in=w_in, tshift_mu=tshift_mu, decay_w0=decay_w0, decay_w2=decay_w2, iclr_a0=iclr_a0, iclr_a2=iclr_a2,
        gate_w2=gate_w2, k_k=k_k, k_a=k_a, r_k=r_k, lnx_g=lnx_g, lnx_b=lnx_b, attn_sinks=attn_sinks,
        rel_bias=rel_bias, w_out=w_out, w_pool=w_pool, pool_scale=pool_scale, ln_mix_g=ln_mix_g, ln_mix_b=ln_mix_b,
        ln_ffn_g=ln_ffn_g, ln_ffn_b=ln_ffn_b, w_router=w_router, b_router=b_router, w_ex_gate=w_ex_gate,
        w_ex_up=w_ex_up, w_ex_down=w_ex_down))
    y_p, rwkv_p, shift_p, k_p, v_p, pool_p = _prompt_trunk(x_prompt, prm)
    y_s, rwkv_s, shift_s, k_s, v_s, pool_s = _sample_trunk(x_sample, state_rwkv, state_shift, cache_swa_k,
                                                            cache_swa_v, state_pool, prm)
    return (y_p, y_s, rwkv_p, rwkv_s, shift_p, shift_s, k_p, k_s, v_p, v_s, pool_p, pool_s)
```
